```python
import math
import jax, jax.numpy as jnp
from jax import lax
import numpy as np

D_MODEL = 2048
BATCH = 4
SEQ = 4096
DEPTH = 1

MEM_LEN = 256
EPS = 1e-6
Q_BLOCK = 128
NEG_BIG = 1e9

MLA_HEADS = 8
MLA_NOPE = 128
MLA_ROPE = 64
MLA_V = 128
MLA_Q_RANK = 512
MLA_KV_RANK = 512
ROPE_THETA = 10000.0

NSA_HEADS = 4
NSA_DK = 192
NSA_DV = 128
CMP_LEN = 32
CMP_STRIDE = 16
SLC_LEN = 64
SLC_TOPN = 16
WIN = 512

MEM_HEADS = 4
MEM_DH = 128

MLA_WIDTH = MLA_HEADS * MLA_V
NSA_WIDTH = NSA_HEADS * NSA_DV
MEM_WIDTH = MEM_HEADS * MEM_DH
MIX_WIDTH = MLA_WIDTH + NSA_WIDTH + MEM_WIDTH

IN_SPLITS = (
    MLA_Q_RANK, MLA_KV_RANK, MLA_ROPE, MLA_WIDTH,
    NSA_HEADS * NSA_DK, NSA_DK, NSA_DV, NSA_DK, NSA_DV,
    NSA_DK, NSA_DV, 3 * NSA_HEADS, NSA_WIDTH,
    MEM_WIDTH, MEM_WIDTH,
)
D_IN = sum(IN_SPLITS)

kernel_name = "hybrid_mla_nsa_memory_block"


def rmsnorm(x, g):
    xf = x.astype(jnp.float32)
    y = xf * lax.rsqrt(jnp.mean(xf * xf, axis=-1, keepdims=True) + EPS)
    return (y * g.astype(jnp.float32)).astype(x.dtype)


def masked_softmax(s, mask):
    s = jnp.where(mask, s.astype(jnp.float32), -1e30)
    m = jnp.max(s, axis=-1, keepdims=True)
    p = jnp.exp(s - m) * mask
    return p / (jnp.sum(p, axis=-1, keepdims=True) + 1e-20)


def alibi_slopes(n):
    return 2.0 ** (-8.0 * jnp.arange(1, n + 1, dtype=jnp.float32) / n)


def apply_rope(x, cos, sin):
    x1, x2 = jnp.split(x.astype(jnp.float32), 2, axis=-1)
    return jnp.concatenate([x1 * cos - x2 * sin, x1 * sin + x2 * cos], axis=-1).astype(x.dtype)


def to_blocks(a):
    b, s = a.shape[:2]
    return jnp.moveaxis(a.reshape(b, s // Q_BLOCK, Q_BLOCK, *a.shape[2:]), 1, 0)


def from_blocks(a):
    a = jnp.moveaxis(a, 0, 1)
    return a.reshape(a.shape[0], -1, *a.shape[3:])


def mla_mixer(c_q, c_kv, k_rope, cos, sin, q_norm_g, w_uq, kv_norm_g, w_ukv):
    b, s, _ = c_q.shape
    q = (rmsnorm(c_q, q_norm_g) @ w_uq).reshape(b, s, MLA_HEADS, MLA_NOPE + MLA_ROPE)
    q = jnp.concatenate([q[..., :MLA_NOPE],
                         apply_rope(q[..., MLA_NOPE:], cos[:, None], sin[:, None])], axis=-1)
    kv = (rmsnorm(c_kv, kv_norm_g) @ w_ukv).reshape(b, s, MLA_HEADS, MLA_NOPE + MLA_V)
    k_pe = apply_rope(k_rope, cos, sin)
    k = jnp.concatenate([kv[..., :MLA_NOPE],
                         jnp.broadcast_to(k_pe[:, :, None], (b, s, MLA_HEADS, MLA_ROPE))], axis=-1)
    v = kv[..., MLA_NOPE:]
    scale = (MLA_NOPE + MLA_ROPE) ** -0.5
    kpos = jnp.arange(s)

    def block(args):
        qb, i = args
        qpos = i * Q_BLOCK + jnp.arange(Q_BLOCK)
        sc = jnp.einsum('bqhd,bkhd->bhqk', qb, k, preferred_element_type=jnp.float32) * scale
        p = masked_softmax(sc, kpos[None, :] <= qpos[:, None])
        return jnp.einsum('bhqk,bkhd->bqhd', p.astype(v.dtype), v)

    o = lax.map(block, (to_blocks(q), jnp.arange(s // Q_BLOCK)))
    return from_blocks(o).reshape(b, s, MLA_WIDTH)


def compress(a, pe, w1, w2):
    b, s, d = a.shape
    ch = a.reshape(b, s // CMP_STRIDE, CMP_STRIDE, d)
    blocks = jnp.concatenate([ch[:, :-1], ch[:, 1:]], axis=2) + pe
    return jax.nn.silu(blocks.reshape(b, -1, CMP_LEN * d) @ w1) @ w2


def nsa_mixer(q, k_c, v_c, k_s, v_s, k_w, v_w, gate_logits,
              cmp_pe_k, cmp_pe_v, cmp_w1k, cmp_w2k, cmp_w1v, cmp_w2v):
    b, s, _ = q.shape
    q = q.reshape(b, s, NSA_HEADS, NSA_DK)
    gates = jax.nn.sigmoid(gate_logits.astype(jnp.float32)).reshape(b, s, NSA_HEADS, 3)
    scale = NSA_DK ** -0.5
    slopes = alibi_slopes(NSA_HEADS)[None, :, None, None]

    k_cmp = compress(k_c, cmp_pe_k, cmp_w1k, cmp_w2k)
    v_cmp = compress(v_c, cmp_pe_v, cmp_w1v, cmp_w2v)
    n_c = k_cmp.shape[1]
    c_start = jnp.arange(n_c) * CMP_STRIDE
    cmp_end = c_start + CMP_LEN - 1
    cmp_pos = c_start.astype(jnp.float32) + (CMP_LEN - 1) / 2.0

    n_s = s // SLC_LEN
    top_n = min(SLC_TOPN, n_s)
    k_blk = k_s.reshape(b, n_s, SLC_LEN, NSA_DK)
    v_blk = v_s.reshape(b, n_s, SLC_LEN, NSA_DV)
    s_start = jnp.arange(n_s) * SLC_LEN
    overlap = ((c_start[:, None] < s_start[None, :] + SLC_LEN) &
               (c_start[:, None] + CMP_LEN > s_start[None, :])).astype(jnp.float32)
    j = jnp.arange(n_s)

    k_wp = jnp.pad(k_w, ((0, 0), (WIN, 0), (0, 0)))
    v_wp = jnp.pad(v_w, ((0, 0), (WIN, 0), (0, 0)))

    def block(args):
        qb, gb, i = args
        t = i * Q_BLOCK + jnp.arange(Q_BLOCK)
        tf = t.astype(jnp.float32)

        sc = jnp.einsum('bqhd,bnd->bhqn', qb, k_cmp, preferred_element_type=jnp.float32) * scale
        sc = sc - slopes * (tf[:, None] - cmp_pos[None, :])
        p_cmp = masked_softmax(sc, cmp_end[None, :] <= t[:, None])
        o_cmp = jnp.einsum('bhqn,bnd->bqhd', p_cmp.astype(v_cmp.dtype), v_cmp)

        imp = jnp.einsum('bhqn,nm->bqm', p_cmp, overlap)
        cur = t // SLC_LEN
        forced = (j[None, :] == 0) | (j[None, :] == cur[:, None]) | (j[None, :] == cur[:, None] - 1)
        imp = jnp.where(forced, NEG_BIG, imp)
        imp = jnp.where(j[None, :] > cur[:, None], -NEG_BIG, imp)
        _, idx = lax.top_k(imp, top_n)
        ks = jax.vmap(lambda kb, ix: kb[ix])(k_blk, idx)
        vs = jax.vmap(lambda vb, ix: vb[ix])(v_blk, idx).reshape(b, Q_BLOCK, top_n * SLC_LEN, NSA_DV)
        spos = (idx[..., None] * SLC_LEN + jnp.arange(SLC_LEN)).reshape(b, Q_BLOCK, top_n * SLC_LEN)
        ss = jnp.einsum('bqhd,bqnld->bhqnl', qb, ks, preferred_element_type=jnp.float32)
        ss = ss.reshape(b, NSA_HEADS, Q_BLOCK, top_n * SLC_LEN) * scale
        ss = ss - slopes * (tf[None, None, :, None] - spos[:, None].astype(jnp.float32))
        p_s = masked_softmax(ss, (spos <= t[None, :, None])[:, None])
        o_slc = jnp.einsum('bhqk,bqkd->bqhd', p_s.astype(vs.dtype), vs)

        kw = lax.dynamic_slice_in_dim(k_wp, i * Q_BLOCK, WIN + Q_BLOCK, axis=1)
        vw = lax.dynamic_slice_in_dim(v_wp, i * Q_BLOCK, WIN + Q_BLOCK, axis=1)
        wpos = i * Q_BLOCK - WIN + jnp.arange(WIN + Q_BLOCK)
        rel = t[:, None] - wpos[None, :]
        sw = jnp.einsum('bqhd,bkd->bhqk', qb, kw, preferred_element_type=jnp.float32) * scale
        sw = sw - slopes * rel.astype(jnp.float32)
        p_w = masked_softmax(sw, (rel >= 0) & (rel < WIN) & (wpos[None, :] >= 0))
        o_win = jnp.einsum('bhqk,bkd->bqhd', p_w.astype(vw.dtype), vw)

        o = gb[..., 0:1] * o_cmp + gb[..., 1:2] * o_slc + gb[..., 2:3] * o_win
        return o.astype(qb.dtype)

    o = lax.map(block, (to_blocks(q), to_blocks(gates), jnp.arange(s // Q_BLOCK)))
    return from_blocks(o).reshape(b, s, NSA_WIDTH)


def memory_mixer(q, mem, mem_norm_g, w_mem_kv):
    b, s, _ = q.shape
    q = q.reshape(b, s, MEM_HEADS, MEM_DH)
    kv = (rmsnorm(mem, mem_norm_g) @ w_mem_kv).reshape(b, mem.shape[1], 2, MEM_HEADS, MEM_DH)
    k, v = kv[:, :, 0], kv[:, :, 1]
    sc = jnp.einsum('bshd,bmhd->bhsm', q, k, preferred_element_type=jnp.float32) * MEM_DH ** -0.5
    p = jax.nn.softmax(sc, axis=-1)
    return jnp.einsum('bhsm,bmhd->bshd', p.astype(v.dtype), v).reshape(b, s, MEM_WIDTH)


def hybrid_layer(x, mem, cos, sin, norm_g, w_in, q_norm_g, w_uq, kv_norm_g, w_ukv,
                 cmp_pe_k, cmp_pe_v, cmp_w1k, cmp_w2k, cmp_w1v, cmp_w2v,
                 mem_norm_g, w_mem_kv, w_out):
    h = rmsnorm(x, norm_g) @ w_in
    offsets = np.cumsum(IN_SPLITS)[:-1].tolist()
    (c_q, c_kv, k_rope, z_mla, q_nsa, k_c, v_c, k_s, v_s, k_w, v_w, g_nsa, z_nsa,
     q_mem, z_mem) = jnp.split(h, offsets, axis=-1)
    o_mla = mla_mixer(c_q, c_kv, k_rope, cos, sin, q_norm_g, w_uq, kv_norm_g, w_ukv) * jax.nn.silu(z_mla)
    o_nsa = nsa_mixer(q_nsa, k_c, v_c, k_s, v_s, k_w, v_w, g_nsa,
                      cmp_pe_k, cmp_pe_v, cmp_w1k, cmp_w2k, cmp_w1v, cmp_w2v) * jax.nn.silu(z_nsa)
    o_mem = memory_mixer(q_mem, mem, mem_norm_g, w_mem_kv) * jax.nn.silu(z_mem)
    o = jnp.concatenate([o_mla, o_nsa, o_mem], axis=-1) @ w_out
    return x + o.astype(x.dtype)


def setup_inputs(seed: int = 0) -> dict:
    key = jax.random.key(seed)
    ks = jax.random.split(key, 20)

    def nrm(k, shape, scale):
        return jax.random.normal(k, shape, jnp.float32) * scale

    def gain(k, shape):
        return 1.0 + 0.01 * jax.random.normal(k, shape, jnp.float32)

    return {
        "x": nrm(ks[0], (BATCH, SEQ, D_MODEL), 1.0),
        "mem": nrm(ks[1], (BATCH, MEM_LEN, D_MODEL), 1.0),
        "norm_g": gain(ks[2], (DEPTH, D_MODEL)),
        "w_in": nrm(ks[3], (DEPTH, D_MODEL, D_IN), D_MODEL ** -0.5),
        "q_norm_g": gain(ks[4], (DEPTH, MLA_Q_RANK)),
        "w_uq": nrm(ks[5], (DEPTH, MLA_Q_RANK, MLA_HEADS * (MLA_NOPE + MLA_ROPE)), MLA_Q_RANK ** -0.5),
        "kv_norm_g": gain(ks[6], (DEPTH, MLA_KV_RANK)),
        "w_ukv": nrm(ks[7], (DEPTH, MLA_KV_RANK, MLA_HEADS * (MLA_NOPE + MLA_V)), MLA_KV_RANK ** -0.5),
        "cmp_pe_k": nrm(ks[8], (DEPTH, CMP_LEN, NSA_DK), 0.02),
        "cmp_pe_v": nrm(ks[9], (DEPTH, CMP_LEN, NSA_DV), 0.02),
        "cmp_w1k": nrm(ks[10], (DEPTH, CMP_LEN * NSA_DK, NSA_DK), (CMP_LEN * NSA_DK) ** -0.5),
        "cmp_w2k": nrm(ks[11], (DEPTH, NSA_DK, NSA_DK), NSA_DK ** -0.5),
        "cmp_w1v": nrm(ks[12], (DEPTH, CMP_LEN * NSA_DV, NSA_DV), (CMP_LEN * NSA_DV) ** -0.5),
        "cmp_w2v": nrm(ks[13], (DEPTH, NSA_DV, NSA_DV), NSA_DV ** -0.5),
        "mem_norm_g": gain(ks[14], (DEPTH, D_MODEL)),
        "w_mem_kv": nrm(ks[15], (DEPTH, D_MODEL, 2 * MEM_WIDTH), D_MODEL ** -0.5),
        "w_out": nrm(ks[16], (DEPTH, MIX_WIDTH, D_MODEL), MIX_WIDTH ** -0.5),
        "final_norm_g": gain(ks[17], (D_MODEL,)),
    }


def reference(x, mem, norm_g, w_in, q_norm_g, w_uq, kv_norm_g, w_ukv,
              cmp_pe_k, cmp_pe_v, cmp_w1k, cmp_w2k, cmp_w1v, cmp_w2v,
              mem_norm_g, w_mem_kv, w_out, final_norm_g):
    s = x.shape[1]
    pos = jnp.arange(s, dtype=jnp.float32)
    inv_freq = ROPE_THETA ** (-jnp.arange(0, MLA_ROPE, 2, dtype=jnp.float32) / MLA_ROPE)
    ang = pos[:, None] * inv_freq[None, :]
    cos, sin = jnp.cos(ang), jnp.sin(ang)
    for l in range(DEPTH):
        x = hybrid_layer(x, mem, cos, sin, norm_g[l], w_in[l], q_norm_g[l], w_uq[l],
                         kv_norm_g[l], w_ukv[l], cmp_pe_k[l], cmp_pe_v[l], cmp_w1k[l],
                         cmp_w2k[l], cmp_w1v[l], cmp_w2v[l], mem_norm_g[l], w_mem_kv[l], w_out[l])
    return rmsnorm(x, final_norm_g)
```

```python
import functools

import jax
import jax.numpy as jnp
from jax import lax
from jax.experimental import pallas as pl
from jax.experimental.pallas import tpu as pltpu

F32 = jnp.float32
BF16 = jnp.bfloat16

EPS = 1e-6
NEG_BIG = 1e9
MASK_VALUE = -1e30
SOFTMAX_EPS = 1e-20

MLA_HEADS = 8
MLA_NOPE = 128
MLA_ROPE = 64
MLA_V = 128
MLA_Q_RANK = 512
MLA_KV_RANK = 512
ROPE_THETA = 10000.0

NSA_HEADS = 4
NSA_DK = 192
NSA_DV = 128
CMP_LEN = 32
CMP_STRIDE = 16
SLC_LEN = 64
SLC_TOPN = 16
WIN = 512

MEM_HEADS = 4
MEM_DH = 128

MLA_WIDTH = MLA_HEADS * MLA_V
NSA_WIDTH = NSA_HEADS * NSA_DV
MEM_WIDTH = MEM_HEADS * MEM_DH

IN_SPLITS = (
    MLA_Q_RANK, MLA_KV_RANK, MLA_ROPE, MLA_WIDTH,
    NSA_HEADS * NSA_DK, NSA_DK, NSA_DV, NSA_DK, NSA_DV,
    NSA_DK, NSA_DV, 3 * NSA_HEADS, NSA_WIDTH,
    MEM_WIDTH, MEM_WIDTH,
)

LANES = 128
HEAD_PAD = 256
VMEM_LIMIT = 56 * 1024 * 1024

F_CQ, F_CKV, F_Z, F_KC, F_VC, F_KROPE = 0, 512, 1024, 3072, 3328, 3456
F_ZNSA, F_ZMEM = F_Z + MLA_WIDTH, F_Z + MLA_WIDTH + NSA_WIDTH
F_GATE = F_KC + NSA_DK
F_WIDTH = 3584
B_QNSA, B_QMEM, B_KS, B_KW, B_VS, B_VW = 0, 1024, 1536, 1792, 2048, 2176
B_WIDTH = 2304

_NT = (((1,), (1,)), ((), ()))


def _params(*sem):
    return pltpu.CompilerParams(dimension_semantics=sem, vmem_limit_bytes=VMEM_LIMIT)


def _sigmoid(x):
    return 1.0 / (1.0 + jnp.exp(-x))


def _silu(x):
    return x * _sigmoid(x)


def _rmsnorm(x, g):
    ms = jnp.mean(x * x, axis=-1, keepdims=True)
    return (x * lax.rsqrt(ms + EPS)) * g


def _norm_proj_kernel(x_ref, g_ref, w_ref, cs_ref, o_ref, xn_ref):
    @pl.when(pl.program_id(1) == 0)
    def _():
        xn_ref[...] = _rmsnorm(x_ref[...], g_ref[...]).astype(BF16)

    acc = jnp.dot(xn_ref[...], w_ref[...], preferred_element_type=F32)
    o_ref[...] = (acc * cs_ref[...]).astype(o_ref.dtype)


def _norm_proj(x, g, w, colscale, out_dtype, tm, tn, name):
    m, k = x.shape
    n = w.shape[1]
    return pl.pallas_call(
        _norm_proj_kernel,
        grid=(m // tm, n // tn),
        in_specs=[
            pl.BlockSpec((tm, k), lambda i, j: (i, 0)),
            pl.BlockSpec((1, k), lambda i, j: (0, 0)),
            pl.BlockSpec((k, tn), lambda i, j: (0, j)),
            pl.BlockSpec((1, tn), lambda i, j: (0, j)),
        ],
        out_specs=pl.BlockSpec((tm, tn), lambda i, j: (i, j)),
        out_shape=jax.ShapeDtypeStruct((m, n), out_dtype),
        scratch_shapes=[pltpu.VMEM((tm, k), BF16)],
        compiler_params=_params("arbitrary", "arbitrary"),
        name=name,
    )(x, g.reshape(1, k), w, colscale.reshape(1, n))


def _rope_half(hi, cos2, sin2):
    up = pltpu.roll(hi, 32, axis=1)
    down = pltpu.roll(hi, 96, axis=1)
    return hi * cos2 + (up - down) * sin2


def _mla_up_kernel(cq_ref, ckv_ref, kr_ref, gq_ref, gkv_ref, wq_ref, wk_ref, wv_ref,
                   cos_ref, sin_ref, q_ref, k_ref, v_ref):
    cos2 = cos_ref[...]
    sin2 = sin_ref[...]
    scale = (MLA_NOPE + MLA_ROPE) ** -0.5

    cqn = _rmsnorm(cq_ref[...], gq_ref[...]).astype(BF16)
    y = jnp.dot(cqn, wq_ref[...], preferred_element_type=F32)
    for h in range(MLA_HEADS):
        lo = y[:, h * HEAD_PAD:h * HEAD_PAD + LANES]
        hi = y[:, h * HEAD_PAD + LANES:(h + 1) * HEAD_PAD]
        q_ref[:, h * HEAD_PAD:h * HEAD_PAD + LANES] = (lo * scale).astype(BF16)
        q_ref[:, h * HEAD_PAD + LANES:(h + 1) * HEAD_PAD] = (
            _rope_half(hi, cos2, sin2) * scale).astype(BF16)

    ckn = _rmsnorm(ckv_ref[...], gkv_ref[...]).astype(BF16)
    kn = jnp.dot(ckn, wk_ref[...], preferred_element_type=F32)
    k_pe = _rope_half(kr_ref[...], cos2, sin2).astype(BF16)
    for h in range(MLA_HEADS):
        k_ref[:, h * HEAD_PAD:h * HEAD_PAD + LANES] = kn[:, h * LANES:(h + 1) * LANES].astype(BF16)
        k_ref[:, h * HEAD_PAD + LANES:(h + 1) * HEAD_PAD] = k_pe
    v_ref[...] = jnp.dot(ckn, wv_ref[...], preferred_element_type=F32).astype(BF16)


def _mla_up(hf, gq, gkv, wq, wk, wv, cos2, sin2, seq, tm):
    m = hf.shape[0]
    steps_per_seq = seq // tm
    hq = MLA_HEADS * HEAD_PAD
    const = lambda i: (0, 0)
    return pl.pallas_call(
        _mla_up_kernel,
        grid=(m // tm,),
        in_specs=[
            pl.BlockSpec((tm, MLA_Q_RANK), lambda i: (i, F_CQ // MLA_Q_RANK)),
            pl.BlockSpec((tm, MLA_KV_RANK), lambda i: (i, F_CKV // MLA_KV_RANK)),
            pl.BlockSpec((tm, LANES), lambda i: (i, F_KROPE // LANES)),
            pl.BlockSpec((1, MLA_Q_RANK), const),
            pl.BlockSpec((1, MLA_KV_RANK), const),
            pl.BlockSpec((MLA_Q_RANK, hq), const),
            pl.BlockSpec((MLA_KV_RANK, MLA_HEADS * MLA_NOPE), const),
            pl.BlockSpec((MLA_KV_RANK, MLA_WIDTH), const),
            pl.BlockSpec((tm, LANES), lambda i: (i % steps_per_seq, 0)),
            pl.BlockSpec((tm, LANES), lambda i: (i % steps_per_seq, 0)),
        ],
        out_specs=[
            pl.BlockSpec((tm, hq), lambda i: (i, 0)),
            pl.BlockSpec((tm, hq), lambda i: (i, 0)),
            pl.BlockSpec((tm, MLA_WIDTH), lambda i: (i, 0)),
        ],
        out_shape=[
            jax.ShapeDtypeStruct((m, hq), BF16),
            jax.ShapeDtypeStruct((m, hq), BF16),
            jax.ShapeDtypeStruct((m, MLA_WIDTH), BF16),
        ],
        compiler_params=_params("arbitrary"),
        name="mla_up",
    )(hf, hf, hf, gq.reshape(1, -1), gkv.reshape(1, -1), wq, wk, wv, cos2, sin2)


def _mla_attn_kernel(q_ref, k_ref, v_ref, z_ref, o_ref, *, tq, tk):
    qi = pl.program_id(2)
    q = q_ref[0]

    def tile(j, carry, diagonal):
        m, l, acc = carry
        kb = pl.multiple_of(j * tk, tk)
        k = k_ref[0, pl.ds(kb, tk), :]
        v = v_ref[0, pl.ds(kb, tk), :]
        s = lax.dot_general(q, k, _NT, preferred_element_type=F32)
        if diagonal:
            qpos = qi * tq + lax.broadcasted_iota(jnp.int32, (tq, tk), 0)
            kpos = kb + lax.broadcasted_iota(jnp.int32, (tq, tk), 1)
            visible = kpos <= qpos
            s = jnp.where(visible, s, MASK_VALUE)
        m_new = jnp.maximum(m, jnp.max(s, axis=-1, keepdims=True))
        alpha = jnp.exp(m - m_new)
        p = jnp.exp(s - m_new)
        if diagonal:
            p = jnp.where(visible, p, 0.0)
        l = alpha * l + jnp.sum(p, axis=-1, keepdims=True)
        acc = alpha * acc + jnp.dot(p.astype(BF16), v, preferred_element_type=F32)
        return m_new, l, acc

    init = (jnp.full((tq, 1), MASK_VALUE, F32), jnp.zeros((tq, 1), F32), jnp.zeros((tq, MLA_V), F32))
    tiles_per_q = tq // tk
    n_full = qi * tiles_per_q
    carry = lax.fori_loop(0, n_full, lambda j, c: tile(j, c, False), init)
    for d in range(tiles_per_q):
        carry = tile(n_full + d, carry, True)
    _, l, acc = carry
    o = acc / (l + SOFTMAX_EPS)
    o_ref[0] = (o * _silu(z_ref[0])).astype(BF16)


def _mla_attn(q, k, v, hf, tq, tk):
    b, s, _ = q.shape
    kern = functools.partial(_mla_attn_kernel, tq=tq, tk=tk)
    return pl.pallas_call(
        kern,
        grid=(b, MLA_HEADS, s // tq),
        in_specs=[
            pl.BlockSpec((1, tq, HEAD_PAD), lambda bi, h, i: (bi, i, h)),
            pl.BlockSpec((1, s, HEAD_PAD), lambda bi, h, i: (bi, 0, h)),
            pl.BlockSpec((1, s, MLA_V), lambda bi, h, i: (bi, 0, h)),
            pl.BlockSpec((1, tq, MLA_V), lambda bi, h, i: (bi, i, F_Z // MLA_V + h)),
        ],
        out_specs=pl.BlockSpec((1, tq, MLA_V), lambda bi, h, i: (bi, i, h)),
        out_shape=jax.ShapeDtypeStruct((b, s, MLA_WIDTH), BF16),
        compiler_params=_params("arbitrary", "arbitrary", "arbitrary"),
        name="mla_attn",
    )(q, k, v, hf)


def _compress_one(x_ref, pe_ref, w1_ref, w2_ref, o_ref):
    x = x_ref[0]
    half = x.shape[1]
    a = jnp.dot((x + pe_ref[0:1, :]).astype(BF16), w1_ref[0:half, :], preferred_element_type=F32)
    b = jnp.dot((x + pe_ref[1:2, :]).astype(BF16), w1_ref[half:2 * half, :], preferred_element_type=F32)
    h1 = a + pltpu.roll(b, b.shape[0] - 1, axis=0)
    o_ref[0] = jnp.dot(_silu(h1).astype(BF16), w2_ref[...], preferred_element_type=F32).astype(BF16)


def _compress_kernel(xk_ref, xv_ref, pek_ref, pev_ref, w1k_ref, w2k_ref, w1v_ref, w2v_ref,
                     ok_ref, ov_ref):
    _compress_one(xk_ref, pek_ref, w1k_ref, w2k_ref, ok_ref)
    _compress_one(xv_ref, pev_ref, w1v_ref, w2v_ref, ov_ref)


def _compress(xk, xv, pek, pev, w1k, w2k, w1v, w2v):
    b, chunks, _ = xk.shape
    full = lambda a: pl.BlockSpec(a.shape, lambda bi: (0,) * a.ndim)
    per_b = lambda a: pl.BlockSpec((1,) + a.shape[1:], lambda bi: (bi, 0, 0))
    return pl.pallas_call(
        _compress_kernel,
        grid=(b,),
        in_specs=[per_b(xk), per_b(xv), full(pek), full(pev), full(w1k), full(w2k), full(w1v), full(w2v)],
        out_specs=[
            pl.BlockSpec((1, chunks, HEAD_PAD), lambda bi: (bi, 0, 0)),
            pl.BlockSpec((1, chunks, NSA_DV), lambda bi: (bi, 0, 0)),
        ],
        out_shape=[
            jax.ShapeDtypeStruct((b, chunks, HEAD_PAD), BF16),
            jax.ShapeDtypeStruct((b, chunks, NSA_DV), BF16),
        ],
        compiler_params=_params("arbitrary"),
        name="nsa_compress",
    )(xk, xv, pek, pev, w1k, w2k, w1v, w2v)


def _alibi_slope(h):
    return 2.0 ** (-8.0 * (h + 1) / NSA_HEADS)


def _softmax_rows(s, mask):
    s = jnp.where(mask, s, MASK_VALUE)
    m = jnp.max(s, axis=-1, keepdims=True)
    p = jnp.where(mask, jnp.exp(s - m), 0.0)
    return p / (jnp.sum(p, axis=-1, keepdims=True) + SOFTMAX_EPS)


def _nsa_attn_kernel(q_ref, g_ref, z_ref, kc_ref, vc_ref, ks_ref, vs_ref, kw_ref, vw_ref,
                     ovt_ref, e_ref, o_ref, *, tq, tk, n_c, n_s, top_n):
    qi = pl.program_id(1)
    t0 = qi * tq
    heads = NSA_HEADS
    ncp = kc_ref.shape[1]

    q_all = q_ref[0]
    qst = jnp.concatenate([q_all[:, h * HEAD_PAD:(h + 1) * HEAD_PAD] for h in range(heads)], axis=0)

    def rows(a, h):
        return a[h * tq:(h + 1) * tq]

    s_c = lax.dot_general(qst, kc_ref[0], _NT, preferred_element_type=F32)
    t_c = t0 + lax.broadcasted_iota(jnp.int32, (tq, ncp), 0)
    n_i = lax.broadcasted_iota(jnp.int32, (tq, ncp), 1)
    mask_c = (n_i * CMP_STRIDE + (CMP_LEN - 1) <= t_c) & (n_i < n_c)
    dist_c = t_c.astype(F32) - (n_i.astype(F32) * CMP_STRIDE + (CMP_LEN - 1) / 2.0)
    p_c = [_softmax_rows(rows(s_c, h) - _alibi_slope(h) * dist_c, mask_c) for h in range(heads)]
    o_cmp = jnp.dot(jnp.concatenate(p_c, axis=0).astype(BF16), vc_ref[0], preferred_element_type=F32)
    p_sum = p_c[0]
    for h in range(1, heads):
        p_sum = p_sum + p_c[h]
    p_hi = p_sum.astype(BF16)
    p_lo = (p_sum - p_hi.astype(F32)).astype(BF16)
    ovt = ovt_ref[...]
    imp = (lax.dot_general(ovt, p_hi, _NT, preferred_element_type=F32)
           + lax.dot_general(ovt, p_lo, _NT, preferred_element_type=F32))

    blk = lax.broadcasted_iota(jnp.int32, (n_s, tq), 0)
    cur = (t0 + lax.broadcasted_iota(jnp.int32, (n_s, tq), 1)) // SLC_LEN
    forced = (blk == 0) | (blk == cur) | (blk == cur - 1)
    imp = jnp.where(forced, NEG_BIG, imp)
    imp = jnp.where(blk > cur, -NEG_BIG, imp)
    rank = jnp.zeros((n_s, tq), F32)
    for jp in range(n_s):
        row = imp[jp:jp + 1, :]
        gt = jnp.where(row > imp, 1.0, 0.0)
        ge = jnp.where(row >= imp, 1.0, 0.0)
        rank = rank + jnp.where(blk > jp, ge, gt)
    sel_t = jnp.where(rank < top_n, 1.0, 0.0)
    sel_t = jnp.concatenate([sel_t, jnp.zeros((LANES - n_s, tq), F32)], axis=0)
    sel_q = sel_t.T.astype(BF16)

    t_s = t0 + lax.broadcasted_iota(jnp.int32, (tq, tk), 0)

    def slc_tile(j, carry, diagonal):
        ms, ls, acc = carry
        kb = pl.multiple_of(j * tk, tk)
        k = ks_ref[0, pl.ds(kb, tk), :]
        v = vs_ref[0, pl.ds(kb, tk), :]
        s_st = lax.dot_general(qst, k, _NT, preferred_element_type=F32)
        keep = jnp.dot(sel_q, e_ref[j], preferred_element_type=F32) > 0.5
        kpos = kb + lax.broadcasted_iota(jnp.int32, (tq, tk), 1)
        if diagonal:
            keep = keep & (kpos <= t_s)
        dist = (t_s - kpos).astype(F32)
        new_m, new_l, ps, alphas = [], [], [], []
        for h in range(heads):
            s = jnp.where(keep, rows(s_st, h) - _alibi_slope(h) * dist, MASK_VALUE)
            m_new = jnp.maximum(ms[h], jnp.max(s, axis=-1, keepdims=True))
            alpha = jnp.exp(ms[h] - m_new)
            p = jnp.where(keep, jnp.exp(s - m_new), 0.0)
            new_m.append(m_new)
            new_l.append(alpha * ls[h] + jnp.sum(p, axis=-1, keepdims=True))
            ps.append(p.astype(BF16))
            alphas.append(jnp.broadcast_to(alpha, (tq, NSA_DV)))
        pv = jnp.dot(jnp.concatenate(ps, axis=0), v, preferred_element_type=F32)
        acc = jnp.concatenate(alphas, axis=0) * acc + pv
        return tuple(new_m), tuple(new_l), acc

    init = (tuple(jnp.full((tq, 1), MASK_VALUE, F32) for _ in range(heads)),
            tuple(jnp.zeros((tq, 1), F32) for _ in range(heads)),
            jnp.zeros((heads * tq, NSA_DV), F32))
    jd = t0 // tk
    carry = lax.fori_loop(0, jd, lambda j, c: slc_tile(j, c, False), init)
    _, ls, acc = slc_tile(jd, carry, True)
    o_slc = [rows(acc, h) / (ls[h] + SOFTMAX_EPS) for h in range(heads)]

    wlen = WIN + tq
    start = pl.multiple_of(jnp.maximum(t0 - WIN, 0), LANES)
    kw = kw_ref[0, pl.ds(start, wlen), :]
    vw = vw_ref[0, pl.ds(start, wlen), :]
    s_w = lax.dot_general(qst, kw, _NT, preferred_element_type=F32)
    rel = (t0 + lax.broadcasted_iota(jnp.int32, (tq, wlen), 0)
           - (start + lax.broadcasted_iota(jnp.int32, (tq, wlen), 1)))
    mask_w = (rel >= 0) & (rel < WIN)
    rel_f = rel.astype(F32)
    p_w = [_softmax_rows(rows(s_w, h) - _alibi_slope(h) * rel_f, mask_w) for h in range(heads)]
    o_win = jnp.dot(jnp.concatenate(p_w, axis=0).astype(BF16), vw, preferred_element_type=F32)

    gates = _sigmoid(g_ref[0])
    z = z_ref[0]
    for h in range(heads):
        c = NSA_DK + 3 * h
        o = (gates[:, c:c + 1] * rows(o_cmp, h) + gates[:, c + 1:c + 2] * o_slc[h]
             + gates[:, c + 2:c + 3] * rows(o_win, h))
        zh = z[:, h * NSA_DV:(h + 1) * NSA_DV]
        o_ref[0, :, h * NSA_DV:(h + 1) * NSA_DV] = (o * _silu(zh)).astype(BF16)


def _nsa_attn(hb, hf, k_cmp, v_cmp, ovt, e, tq, tk):
    b, s, _ = hb.shape
    ncp = k_cmp.shape[1]
    n_s = s // SLC_LEN
    kern = functools.partial(_nsa_attn_kernel, tq=tq, tk=tk, n_c=ncp - 1, n_s=n_s,
                             top_n=min(SLC_TOPN, n_s))
    qw = NSA_HEADS * HEAD_PAD
    return pl.pallas_call(
        kern,
        grid=(b, s // tq),
        in_specs=[
            pl.BlockSpec((1, tq, qw), lambda bi, i: (bi, i, B_QNSA // qw)),
            pl.BlockSpec((1, tq, HEAD_PAD), lambda bi, i: (bi, i, F_KC // HEAD_PAD)),
            pl.BlockSpec((1, tq, NSA_WIDTH), lambda bi, i: (bi, i, F_ZNSA // NSA_WIDTH)),
            pl.BlockSpec((1, ncp, HEAD_PAD), lambda bi, i: (bi, 0, 0)),
            pl.BlockSpec((1, ncp, NSA_DV), lambda bi, i: (bi, 0, 0)),
            pl.BlockSpec((1, s, HEAD_PAD), lambda bi, i: (bi, 0, B_KS // HEAD_PAD)),
            pl.BlockSpec((1, s, NSA_DV), lambda bi, i: (bi, 0, B_VS // NSA_DV)),
            pl.BlockSpec((1, s, HEAD_PAD), lambda bi, i: (bi, 0, B_KW // HEAD_PAD)),
            pl.BlockSpec((1, s, NSA_DV), lambda bi, i: (bi, 0, B_VW // NSA_DV)),
            pl.BlockSpec(ovt.shape, lambda bi, i: (0, 0)),
            pl.BlockSpec(e.shape, lambda bi, i: (0, 0, 0)),
        ],
        out_specs=pl.BlockSpec((1, tq, NSA_WIDTH), lambda bi, i: (bi, i, 0)),
        out_shape=jax.ShapeDtypeStruct((b, s, NSA_WIDTH), BF16),
        compiler_params=_params("arbitrary", "arbitrary"),
        name="nsa_attn",
    )(hb, hf, hf, k_cmp, v_cmp, hb, hb, hb, hb, ovt, e)


def _mem_attn_kernel(q_ref, k_ref, v_ref, z_ref, o_ref):
    q = q_ref[0]
    k = k_ref[0]
    v = v_ref[0]
    z = z_ref[0]
    for h in range(MEM_HEADS):
        sl = slice(h * MEM_DH, (h + 1) * MEM_DH)
        s = lax.dot_general(q[:, sl], k[:, sl], _NT, preferred_element_type=F32)
        p = jnp.exp(s - jnp.max(s, axis=-1, keepdims=True))
        o = jnp.dot(p.astype(BF16), v[:, sl], preferred_element_type=F32)
        o = o / jnp.sum(p, axis=-1, keepdims=True)
        o_ref[0, :, sl] = (o * _silu(z[:, sl])).astype(BF16)


def _mem_attn(hb, mem_kv, hf, tq):
    b, s, _ = hb.shape
    mlen = mem_kv.shape[1]
    return pl.pallas_call(
        _mem_attn_kernel,
        grid=(b, s // tq),
        in_specs=[
            pl.BlockSpec((1, tq, MEM_WIDTH), lambda bi, i: (bi, i, B_QMEM // MEM_WIDTH)),
            pl.BlockSpec((1, mlen, MEM_WIDTH), lambda bi, i: (bi, 0, 0)),
            pl.BlockSpec((1, mlen, MEM_WIDTH), lambda bi, i: (bi, 0, 1)),
            pl.BlockSpec((1, tq, MEM_WIDTH), lambda bi, i: (bi, i, F_ZMEM // MEM_WIDTH)),
        ],
        out_specs=pl.BlockSpec((1, tq, MEM_WIDTH), lambda bi, i: (bi, i, 0)),
        out_shape=jax.ShapeDtypeStruct((b, s, MEM_WIDTH), BF16),
        compiler_params=_params("arbitrary", "arbitrary"),
        name="mem_attn",
    )(hb, mem_kv, mem_kv, hf)


def _out_proj_kernel(x_ref, a_ref, n_ref, m_ref, w_ref, g_ref, o_ref, *, final_norm):
    y = x_ref[...]
    y = y + jnp.dot(a_ref[...], w_ref[0:MLA_WIDTH, :], preferred_element_type=F32)
    y = y + jnp.dot(n_ref[...], w_ref[MLA_WIDTH:MLA_WIDTH + NSA_WIDTH, :], preferred_element_type=F32)
    y = y + jnp.dot(m_ref[...], w_ref[MLA_WIDTH + NSA_WIDTH:, :], preferred_element_type=F32)
    if final_norm:
        y = _rmsnorm(y, g_ref[...])
    o_ref[...] = y


def _out_proj(x, o_mla, o_nsa, o_mem, w_out, g, final_norm, tm):
    m, d = x.shape
    kern = functools.partial(_out_proj_kernel, final_norm=final_norm)
    row = lambda width: pl.BlockSpec((tm, width), lambda i: (i, 0))
    return pl.pallas_call(
        kern,
        grid=(m // tm,),
        in_specs=[row(d), row(MLA_WIDTH), row(NSA_WIDTH), row(MEM_WIDTH),
                  pl.BlockSpec(w_out.shape, lambda i: (0, 0)),
                  pl.BlockSpec((1, d), lambda i: (0, 0))],
        out_specs=row(d),
        out_shape=jax.ShapeDtypeStruct((m, d), F32),
        compiler_params=_params("arbitrary"),
        name="out_proj",
    )(x, o_mla, o_nsa, o_mem, w_out, g.reshape(1, d))


def _pad_cols(w, width):
    return jnp.pad(w, ((0, 0), (0, width - w.shape[1])))


def _split_w_in(w_in):
    offs = [0]
    for n in IN_SPLITS:
        offs.append(offs[-1] + n)
    return [w_in[:, offs[i]:offs[i + 1]] for i in range(len(IN_SPLITS))]


def _layout_w_in(w_in):
    (c_q, c_kv, k_rope, z_mla, q_nsa, k_c, v_c, k_s, v_s, k_w, v_w, g_nsa, z_nsa,
     q_mem, z_mem) = _split_w_in(w_in)
    d = w_in.shape[0]
    wf = jnp.concatenate([
        c_q, c_kv, z_mla, z_nsa, z_mem,
        _pad_cols(jnp.concatenate([k_c, g_nsa], axis=1), HEAD_PAD), v_c,
        _pad_cols(k_rope, LANES)], axis=1).astype(BF16)
    q_heads = jnp.pad(q_nsa.reshape(d, NSA_HEADS, NSA_DK), ((0, 0), (0, 0), (0, HEAD_PAD - NSA_DK)))
    wb = jnp.concatenate([
        q_heads.reshape(d, NSA_HEADS * HEAD_PAD), q_mem,
        _pad_cols(k_s, HEAD_PAD), _pad_cols(k_w, HEAD_PAD), v_s, v_w], axis=1).astype(BF16)
    scale_b = jnp.concatenate([
        jnp.full((NSA_HEADS * HEAD_PAD,), NSA_DK ** -0.5, F32),
        jnp.full((MEM_WIDTH,), MEM_DH ** -0.5, F32),
        jnp.ones((B_WIDTH - B_QMEM - MEM_WIDTH,), F32)])
    return wf, wb, scale_b


def _rope_tables(seq):
    pos = jnp.arange(seq, dtype=F32)
    inv_freq = ROPE_THETA ** (-jnp.arange(0, MLA_ROPE, 2, dtype=F32) / MLA_ROPE)
    ang = pos[:, None] * inv_freq[None, :]
    zeros = jnp.zeros((seq, LANES - MLA_ROPE), F32)
    cos2 = jnp.concatenate([jnp.cos(ang), jnp.cos(ang), zeros], axis=1)
    sin2 = jnp.concatenate([jnp.sin(ang), jnp.sin(ang), zeros], axis=1)
    return cos2, sin2


def _selection_tables(seq, tk):
    chunks = seq // CMP_STRIDE
    n_s = seq // SLC_LEN
    c_start = jnp.arange(chunks) * CMP_STRIDE
    s_start = jnp.arange(n_s) * SLC_LEN
    overlap_t = ((c_start[None, :] < s_start[:, None] + SLC_LEN)
                 & (c_start[None, :] + CMP_LEN > s_start[:, None])
                 & (jnp.arange(chunks)[None, :] < chunks - 1))
    key_block = jnp.arange(seq) // SLC_LEN
    expand = (jnp.arange(LANES)[:, None] == key_block[None, :])
    expand = expand.reshape(LANES, seq // tk, tk).transpose(1, 0, 2)
    return overlap_t.astype(BF16), expand.astype(BF16)


def _layer(x2, mem2, batch, seq, tables, norm_g, w_in, q_norm_g, w_uq, kv_norm_g, w_ukv,
           cmp_pe_k, cmp_pe_v, cmp_w1k, cmp_w2k, cmp_w1v, cmp_w2v, mem_norm_g, w_mem_kv, w_out,
           final_g, final_norm):
    cos2, sin2, ovt, expand, tk_nsa = tables
    d = x2.shape[1]
    wf, wb, scale_b = _layout_w_in(w_in)
    hf = _norm_proj(x2, norm_g, wf, jnp.ones((F_WIDTH,), F32), F32, 1024, 512, "in_proj_f32")
    hb = _norm_proj(x2, norm_g, wb, scale_b, BF16, 1024, 768, "in_proj_bf16")

    wq = jnp.pad(w_uq.reshape(MLA_Q_RANK, MLA_HEADS, MLA_NOPE + MLA_ROPE),
                 ((0, 0), (0, 0), (0, HEAD_PAD - MLA_NOPE - MLA_ROPE)))
    wq = wq.reshape(MLA_Q_RANK, MLA_HEADS * HEAD_PAD).astype(BF16)
    wkv = w_ukv.reshape(MLA_KV_RANK, MLA_HEADS, MLA_NOPE + MLA_V)
    wk = wkv[:, :, :MLA_NOPE].reshape(MLA_KV_RANK, MLA_HEADS * MLA_NOPE).astype(BF16)
    wv = wkv[:, :, MLA_NOPE:].reshape(MLA_KV_RANK, MLA_WIDTH).astype(BF16)
    q, k, v = _mla_up(hf, q_norm_g, kv_norm_g, wq, wk, wv, cos2, sin2, seq, 512)
    hf3 = hf.reshape(batch, seq, F_WIDTH)
    hb3 = hb.reshape(batch, seq, B_WIDTH)
    o_mla = _mla_attn(q.reshape(batch, seq, -1), k.reshape(batch, seq, -1),
                      v.reshape(batch, seq, -1), hf3, 512, 512)

    chunks = seq // CMP_STRIDE
    xk = hf3[:, :, F_KC:F_KC + NSA_DK].reshape(batch, chunks, CMP_STRIDE * NSA_DK)
    xv = hf3[:, :, F_VC:F_VC + NSA_DV].reshape(batch, chunks, CMP_STRIDE * NSA_DV)
    k_cmp, v_cmp = _compress(
        xk, xv,
        cmp_pe_k.reshape(2, CMP_STRIDE * NSA_DK), cmp_pe_v.reshape(2, CMP_STRIDE * NSA_DV),
        _pad_cols(cmp_w1k, HEAD_PAD).astype(BF16),
        jnp.pad(cmp_w2k, ((0, HEAD_PAD - NSA_DK), (0, HEAD_PAD - NSA_DK))).astype(BF16),
        cmp_w1v.astype(BF16), cmp_w2v.astype(BF16))
    o_nsa = _nsa_attn(hb3, hf3, k_cmp, v_cmp, ovt, expand, 128, tk_nsa)

    mem_kv = _norm_proj(mem2, mem_norm_g, w_mem_kv.astype(BF16), jnp.ones((2 * MEM_WIDTH,), F32),
                        BF16, mem2.shape[0] // batch, MEM_WIDTH, "mem_kv_proj")
    o_mem = _mem_attn(hb3, mem_kv.reshape(batch, -1, 2 * MEM_WIDTH), hf3, 512)

    return _out_proj(x2, o_mla.reshape(-1, MLA_WIDTH), o_nsa.reshape(-1, NSA_WIDTH),
                     o_mem.reshape(-1, MEM_WIDTH), w_out.astype(BF16), final_g, final_norm, 512)


def kernel(x, mem, norm_g, w_in, q_norm_g, w_uq, kv_norm_g, w_ukv, cmp_pe_k, cmp_pe_v,
           cmp_w1k, cmp_w2k, cmp_w1v, cmp_w2v, mem_norm_g, w_mem_kv, w_out, final_norm_g):
    batch, seq, d = x.shape
    depth = norm_g.shape[0]
    tk_nsa = 512
    cos2, sin2 = _rope_tables(seq)
    ovt, expand = _selection_tables(seq, tk_nsa)
    tables = (cos2, sin2, ovt, expand, tk_nsa)
    x2 = x.reshape(batch * seq, d)
    mem2 = mem.reshape(batch * mem.shape[1], d)
    for l in range(depth):
        x2 = _layer(x2, mem2, batch, seq, tables, norm_g[l], w_in[l], q_norm_g[l], w_uq[l],
                    kv_norm_g[l], w_ukv[l], cmp_pe_k[l], cmp_pe_v[l], cmp_w1k[l], cmp_w2k[l],
                    cmp_w1v[l], cmp_w2v[l], mem_norm_g[l], w_mem_kv[l], w_out[l],
                    final_norm_g, l == depth - 1)
    return x2.reshape(batch, seq, d)
```

```python
import functools

import jax
import jax.numpy as jnp
from jax import lax
from jax.experimental import pallas as pl
from jax.experimental.pallas import tpu as pltpu

F32 = jnp.float32
BF16 = jnp.bfloat16

EPS = 1e-6
NEG_BIG = 1e9
MASK_VALUE = -1e30
SOFTMAX_EPS = 1e-20
LOG2E = 1.4426950408889634

MLA_HEADS = 8
MLA_NOPE = 128
MLA_ROPE = 64
MLA_V = 128
MLA_Q_RANK = 512
MLA_KV_RANK = 512
ROPE_THETA = 10000.0

NSA_HEADS = 4
NSA_DK = 192
NSA_DV = 128
CMP_LEN = 32
CMP_STRIDE = 16
SLC_LEN = 64
SLC_TOPN = 16
WIN = 512

MEM_HEADS = 4
MEM_DH = 128

MLA_WIDTH = MLA_HEADS * MLA_V
NSA_WIDTH = NSA_HEADS * NSA_DV
MEM_WIDTH = MEM_HEADS * MEM_DH

IN_SPLITS = (
    MLA_Q_RANK, MLA_KV_RANK, MLA_ROPE, MLA_WIDTH,
    NSA_HEADS * NSA_DK, NSA_DK, NSA_DV, NSA_DK, NSA_DV,
    NSA_DK, NSA_DV, 3 * NSA_HEADS, NSA_WIDTH,
    MEM_WIDTH, MEM_WIDTH,
)

LANES = 128
MXU_COLS = 256
HEAD_PAD = 256
VMEM_LIMIT = 56 * 1024 * 1024
MLA_TILE = 512

F_CQ, F_CKV, F_Z, F_KC, F_VC, F_KROPE = 0, 512, 1024, 3072, 3328, 3456
F_ZNSA, F_ZMEM = F_Z + MLA_WIDTH, F_Z + MLA_WIDTH + NSA_WIDTH
F_GATE = F_KC + NSA_DK
F_WIDTH = 3584
B_QNSA, B_QMEM, B_KS, B_KW, B_VS, B_VW = 0, 1024, 1536, 1792, 2048, 2176
B_WIDTH = 2304

_NT = (((1,), (1,)), ((), ()))


def _params(*sem):
    return pltpu.CompilerParams(dimension_semantics=sem, vmem_limit_bytes=VMEM_LIMIT)


def _sigmoid(x):
    return 1.0 / (1.0 + jnp.exp(-x))


def _silu(x):
    return x * _sigmoid(x)


def _rmsnorm(x, g):
    ms = jnp.mean(x * x, axis=-1, keepdims=True)
    return (x * lax.rsqrt(ms + EPS)) * g


def _norm_proj_kernel(x_ref, g_ref, w_ref, cs_ref, o_ref, xn_ref):
    @pl.when(pl.program_id(1) == 0)
    def _():
        xn_ref[...] = _rmsnorm(x_ref[...], g_ref[...]).astype(BF16)

    acc = jnp.dot(xn_ref[...], w_ref[...], preferred_element_type=F32)
    o_ref[...] = (acc * cs_ref[...]).astype(o_ref.dtype)


def _norm_proj(x, g, w, colscale, out_dtype, tm, tn, name):
    m, k = x.shape
    n = w.shape[1]
    return pl.pallas_call(
        _norm_proj_kernel,
        grid=(m // tm, n // tn),
        in_specs=[
            pl.BlockSpec((tm, k), lambda i, j: (i, 0)),
            pl.BlockSpec((1, k), lambda i, j: (0, 0)),
            pl.BlockSpec((k, tn), lambda i, j: (0, j)),
            pl.BlockSpec((1, tn), lambda i, j: (0, j)),
        ],
        out_specs=pl.BlockSpec((tm, tn), lambda i, j: (i, j)),
        out_shape=jax.ShapeDtypeStruct((m, n), out_dtype),
        scratch_shapes=[pltpu.VMEM((tm, k), BF16)],
        compiler_params=_params("arbitrary", "arbitrary"),
        name=name,
    )(x, g.reshape(1, k), w, colscale.reshape(1, n))


def _rope_half(hi, cos2, sin2):
    up = pltpu.roll(hi, 32, axis=1)
    down = pltpu.roll(hi, 96, axis=1)
    return hi * cos2 + (up - down) * sin2


def _mla_up_kernel(cq_ref, ckv_ref, kr_ref, gq_ref, gkv_ref, wqt_ref, wk_ref, wvt_ref,
                   cos_ref, sin_ref, cost_ref, sint_ref, qt_ref, k_ref, vt_ref):
    scale = LOG2E * (MLA_NOPE + MLA_ROPE) ** -0.5
    half = MLA_ROPE // 2
    cos_t = cost_ref[...]
    sin_t = sint_ref[...]

    cqn = _rmsnorm(cq_ref[...], gq_ref[...]).astype(BF16)
    yt = lax.dot_general(wqt_ref[...], cqn, _NT, preferred_element_type=F32)
    for h in range(MLA_HEADS):
        r = h * HEAD_PAD
        x1 = yt[r + MLA_NOPE:r + MLA_NOPE + half]
        x2 = yt[r + MLA_NOPE + half:r + MLA_NOPE + MLA_ROPE]
        qt_ref[0, r:r + MLA_NOPE, :] = (yt[r:r + MLA_NOPE] * scale).astype(BF16)
        qt_ref[0, r + MLA_NOPE:r + MLA_NOPE + half, :] = (
            (x1 * cos_t - x2 * sin_t) * scale).astype(BF16)
        qt_ref[0, r + MLA_NOPE + half:r + MLA_NOPE + MLA_ROPE, :] = (
            (x1 * sin_t + x2 * cos_t) * scale).astype(BF16)
        qt_ref[0, r + MLA_NOPE + MLA_ROPE:r + HEAD_PAD, :] = jnp.zeros(
            (HEAD_PAD - MLA_NOPE - MLA_ROPE, yt.shape[1]), BF16)

    ckn = _rmsnorm(ckv_ref[...], gkv_ref[...]).astype(BF16)
    kn = jnp.dot(ckn, wk_ref[...], preferred_element_type=F32)
    k_pe = _rope_half(kr_ref[...], cos_ref[...], sin_ref[...]).astype(BF16)
    for h in range(MLA_HEADS):
        k_ref[:, h * HEAD_PAD:h * HEAD_PAD + LANES] = kn[:, h * LANES:(h + 1) * LANES].astype(BF16)
        k_ref[:, h * HEAD_PAD + LANES:(h + 1) * HEAD_PAD] = k_pe
    vt_ref[0] = lax.dot_general(wvt_ref[...], ckn, _NT, preferred_element_type=F32).astype(BF16)


def _mla_up(hf, gq, gkv, wqt, wk, wvt, rope, seq, tm):
    cos2, sin2, cos_t, sin_t = rope
    m = hf.shape[0]
    steps_per_seq = seq // tm
    hq = MLA_HEADS * HEAD_PAD
    half = MLA_ROPE // 2
    const = lambda i: (0, 0)
    return pl.pallas_call(
        _mla_up_kernel,
        grid=(m // tm,),
        in_specs=[
            pl.BlockSpec((tm, MLA_Q_RANK), lambda i: (i, F_CQ // MLA_Q_RANK)),
            pl.BlockSpec((tm, MLA_KV_RANK), lambda i: (i, F_CKV // MLA_KV_RANK)),
            pl.BlockSpec((tm, LANES), lambda i: (i, F_KROPE // LANES)),
            pl.BlockSpec((1, MLA_Q_RANK), const),
            pl.BlockSpec((1, MLA_KV_RANK), const),
            pl.BlockSpec((hq, MLA_Q_RANK), const),
            pl.BlockSpec((MLA_KV_RANK, MLA_HEADS * MLA_NOPE), const),
            pl.BlockSpec((MLA_WIDTH, MLA_KV_RANK), const),
            pl.BlockSpec((tm, LANES), lambda i: (i % steps_per_seq, 0)),
            pl.BlockSpec((tm, LANES), lambda i: (i % steps_per_seq, 0)),
            pl.BlockSpec((half, tm), lambda i: (0, i % steps_per_seq)),
            pl.BlockSpec((half, tm), lambda i: (0, i % steps_per_seq)),
        ],
        out_specs=[
            pl.BlockSpec((1, hq, tm), lambda i: (i, 0, 0)),
            pl.BlockSpec((tm, hq), lambda i: (i, 0)),
            pl.BlockSpec((1, MLA_WIDTH, tm), lambda i: (i, 0, 0)),
        ],
        out_shape=[
            jax.ShapeDtypeStruct((m // tm, hq, tm), BF16),
            jax.ShapeDtypeStruct((m, hq), BF16),
            jax.ShapeDtypeStruct((m // tm, MLA_WIDTH, tm), BF16),
        ],
        compiler_params=_params("arbitrary"),
        name="mla_up",
    )(hf, hf, hf, gq.reshape(1, -1), gkv.reshape(1, -1), wqt, wk, wvt, cos2, sin2, cos_t, sin_t)


def _mla_attn_kernel(qt_ref, k_ref, vt_ref, z_ref, o_ref, m_ref, l_ref, acc_ref, s0_ref, *,
                     tile_len, hps):
    t = tile_len
    qi = pl.program_id(2)
    qts = [qt_ref[0, h * HEAD_PAD:(h + 1) * HEAD_PAD, :] for h in range(hps)]

    def scores(j, h):
        kb = pl.multiple_of(j * t, t)
        return jnp.dot(k_ref[0, pl.ds(kb, t), h * HEAD_PAD:(h + 1) * HEAD_PAD], qts[h],
                       preferred_element_type=F32)

    def softmax(h, s, diagonal):
        if diagonal:
            kpos = lax.broadcasted_iota(jnp.int32, (t, t), 0)
            qpos = lax.broadcasted_iota(jnp.int32, (t, t), 1)
            visible = kpos <= qpos
            s = jnp.where(visible, s, MASK_VALUE)
        m = m_ref[h]
        m_new = jnp.maximum(m, jnp.max(s, axis=0, keepdims=True))
        alpha = jnp.exp2(m - m_new)
        p = jnp.exp2(s - m_new)
        if diagonal:
            p = jnp.where(visible, p, 0.0)
        m_ref[h] = m_new
        l_ref[h] = alpha * l_ref[h] + jnp.sum(p, axis=0, keepdims=True)
        return alpha, p.astype(BF16)

    def accumulate(j, h, alpha, p):
        vt = vt_ref[j, h * MLA_V:(h + 1) * MLA_V, :]
        acc_ref[h] = alpha * acc_ref[h] + jnp.dot(vt, p, preferred_element_type=F32)

    def tile(j, diagonal):
        alpha, p = softmax(0, s0_ref[...], diagonal)
        for h in range(1, hps):
            s = scores(j, h)
            accumulate(j, h - 1, alpha, p)
            alpha, p = softmax(h, s, diagonal)
        if not diagonal:
            s0_ref[...] = scores(j + 1, 0)
        accumulate(j, hps - 1, alpha, p)

    for h in range(hps):
        m_ref[h] = jnp.full((1, t), MASK_VALUE, F32)
        l_ref[h] = jnp.zeros((1, t), F32)
        acc_ref[h] = jnp.zeros((MLA_V, t), F32)
    s0_ref[...] = scores(0, 0)

    def body(j, carry):
        tile(j, False)
        return carry

    lax.fori_loop(0, qi, body, 0)

    tile(qi, True)
    for h in range(hps):
        o = (acc_ref[h] / (l_ref[h] + SOFTMAX_EPS)).T
        cols = slice(h * MLA_V, (h + 1) * MLA_V)
        o_ref[0, :, cols] = (o * _silu(z_ref[0, :, cols])).astype(BF16)


def _mla_attn(qt, k, vt, hf, tile_len, hps):
    b, s, _ = k.shape
    t = tile_len
    tiles = s // t
    kern = functools.partial(_mla_attn_kernel, tile_len=t, hps=hps)
    qw, vw = hps * HEAD_PAD, hps * MLA_V
    return pl.pallas_call(
        kern,
        grid=(b, MLA_HEADS // hps, tiles),
        in_specs=[
            pl.BlockSpec((1, qw, t), lambda bi, h, i: (bi * tiles + i, h, 0)),
            pl.BlockSpec((1, s, qw), lambda bi, h, i: (bi, 0, h)),
            pl.BlockSpec((tiles, vw, t), lambda bi, h, i: (bi, h, 0)),
            pl.BlockSpec((1, t, vw), lambda bi, h, i: (bi, i, F_Z // vw + h)),
        ],
        out_specs=pl.BlockSpec((1, t, vw), lambda bi, h, i: (bi, i, h)),
        out_shape=jax.ShapeDtypeStruct((b, s, MLA_WIDTH), BF16),
        scratch_shapes=[
            pltpu.VMEM((hps, 1, t), F32),
            pltpu.VMEM((hps, 1, t), F32),
            pltpu.VMEM((hps, MLA_V, t), F32),
            pltpu.VMEM((t, t), F32),
        ],
        compiler_params=_params("arbitrary", "arbitrary", "arbitrary"),
        name="mla_attn",
    )(qt, k, vt, hf)


def _compress_one(x_ref, pe_ref, w1_ref, w2_ref, o_ref):
    x = x_ref[0]
    half = x.shape[1]
    a = jnp.dot((x + pe_ref[0:1, :]).astype(BF16), w1_ref[0:half, :], preferred_element_type=F32)
    b = jnp.dot((x + pe_ref[1:2, :]).astype(BF16), w1_ref[half:2 * half, :], preferred_element_type=F32)
    h1 = a + pltpu.roll(b, b.shape[0] - 1, axis=0)
    o_ref[0] = jnp.dot(_silu(h1).astype(BF16), w2_ref[...], preferred_element_type=F32).astype(BF16)


def _compress_kernel(xk_ref, xv_ref, pek_ref, pev_ref, w1k_ref, w2k_ref, w1v_ref, w2v_ref,
                     ok_ref, ov_ref):
    _compress_one(xk_ref, pek_ref, w1k_ref, w2k_ref, ok_ref)
    _compress_one(xv_ref, pev_ref, w1v_ref, w2v_ref, ov_ref)


def _compress(xk, xv, pek, pev, w1k, w2k, w1v, w2v):
    b, chunks, _ = xk.shape
    full = lambda a: pl.BlockSpec(a.shape, lambda bi: (0,) * a.ndim)
    per_b = lambda a: pl.BlockSpec((1,) + a.shape[1:], lambda bi: (bi, 0, 0))
    return pl.pallas_call(
        _compress_kernel,
        grid=(b,),
        in_specs=[per_b(xk), per_b(xv), full(pek), full(pev), full(w1k), full(w2k), full(w1v), full(w2v)],
        out_specs=[
            pl.BlockSpec((1, chunks, HEAD_PAD), lambda bi: (bi, 0, 0)),
            pl.BlockSpec((1, chunks, NSA_DV), lambda bi: (bi, 0, 0)),
        ],
        out_shape=[
            jax.ShapeDtypeStruct((b, chunks, HEAD_PAD), BF16),
            jax.ShapeDtypeStruct((b, chunks, NSA_DV), BF16),
        ],
        compiler_params=_params("arbitrary"),
        name="nsa_compress",
    )(xk, xv, pek, pev, w1k, w2k, w1v, w2v)


def _alibi_slope(h):
    return 2.0 ** (-8.0 * (h + 1) / NSA_HEADS)


def _softmax_rows(s, mask):
    s = jnp.where(mask, s, MASK_VALUE)
    m = jnp.max(s, axis=-1, keepdims=True)
    p = jnp.where(mask, jnp.exp(s - m), 0.0)
    return p / (jnp.sum(p, axis=-1, keepdims=True) + SOFTMAX_EPS)


def _nsa_attn_kernel(q_ref, g_ref, z_ref, kc_ref, vc_ref, ks_ref, vs_ref, kw_ref, vw_ref,
                     ovt_ref, e_ref, o_ref, *, tq, tk, n_c, n_s, top_n):
    qi = pl.program_id(1)
    t0 = qi * tq
    heads = NSA_HEADS
    ncp = kc_ref.shape[1]

    q_all = q_ref[0]
    qst = jnp.concatenate([q_all[:, h * HEAD_PAD:(h + 1) * HEAD_PAD] for h in range(heads)], axis=0)

    def rows(a, h):
        return a[h * tq:(h + 1) * tq]

    s_c = lax.dot_general(qst, kc_ref[0], _NT, preferred_element_type=F32)
    t_c = t0 + lax.broadcasted_iota(jnp.int32, (tq, ncp), 0)
    n_i = lax.broadcasted_iota(jnp.int32, (tq, ncp), 1)
    mask_c = (n_i * CMP_STRIDE + (CMP_LEN - 1) <= t_c) & (n_i < n_c)
    dist_c = t_c.astype(F32) - (n_i.astype(F32) * CMP_STRIDE + (CMP_LEN - 1) / 2.0)
    p_c = [_softmax_rows(rows(s_c, h) - _alibi_slope(h) * dist_c, mask_c) for h in range(heads)]
    o_cmp = jnp.dot(jnp.concatenate(p_c, axis=0).astype(BF16), vc_ref[0], preferred_element_type=F32)
    p_sum = p_c[0]
    for h in range(1, heads):
        p_sum = p_sum + p_c[h]
    p_hi = p_sum.astype(BF16)
    p_lo = (p_sum - p_hi.astype(F32)).astype(BF16)
    ovt = ovt_ref[...]
    imp = (lax.dot_general(ovt, p_hi, _NT, preferred_element_type=F32)
           + lax.dot_general(ovt, p_lo, _NT, preferred_element_type=F32))

    blk = lax.broadcasted_iota(jnp.int32, (n_s, tq), 0)
    cur = (t0 + lax.broadcasted_iota(jnp.int32, (n_s, tq), 1)) // SLC_LEN
    forced = (blk == 0) | (blk == cur) | (blk == cur - 1)
    imp = jnp.where(forced, NEG_BIG, imp)
    imp = jnp.where(blk > cur, -NEG_BIG, imp)
    rank = jnp.zeros((n_s, tq), F32)
    for jp in range(n_s):
        row = imp[jp:jp + 1, :]
        gt = jnp.where(row > imp, 1.0, 0.0)
        ge = jnp.where(row >= imp, 1.0, 0.0)
        rank = rank + jnp.where(blk > jp, ge, gt)
    sel_t = jnp.where(rank < top_n, 1.0, 0.0)
    sel_t = jnp.concatenate([sel_t, jnp.zeros((LANES - n_s, tq), F32)], axis=0)
    sel_q = sel_t.T.astype(BF16)

    t_s = t0 + lax.broadcasted_iota(jnp.int32, (tq, tk), 0)

    def slc_tile(j, carry, diagonal):
        ms, ls, acc = carry
        kb = pl.multiple_of(j * tk, tk)
        k = ks_ref[0, pl.ds(kb, tk), :]
        v = vs_ref[0, pl.ds(kb, tk), :]
        s_st = lax.dot_general(qst, k, _NT, preferred_element_type=F32)
        keep = jnp.dot(sel_q, e_ref[j], preferred_element_type=F32) > 0.5
        kpos = kb + lax.broadcasted_iota(jnp.int32, (tq, tk), 1)
        if diagonal:
            keep = keep & (kpos <= t_s)
        dist = (t_s - kpos).astype(F32)
        new_m, new_l, ps, alphas = [], [], [], []
        for h in range(heads):
            s = jnp.where(keep, rows(s_st, h) - _alibi_slope(h) * dist, MASK_VALUE)
            m_new = jnp.maximum(ms[h], jnp.max(s, axis=-1, keepdims=True))
            alpha = jnp.exp(ms[h] - m_new)
            p = jnp.where(keep, jnp.exp(s - m_new), 0.0)
            new_m.append(m_new)
            new_l.append(alpha * ls[h] + jnp.sum(p, axis=-1, keepdims=True))
            ps.append(p.astype(BF16))
            alphas.append(jnp.broadcast_to(alpha, (tq, NSA_DV)))
        pv = jnp.dot(jnp.concatenate(ps, axis=0), v, preferred_element_type=F32)
        acc = jnp.concatenate(alphas, axis=0) * acc + pv
        return tuple(new_m), tuple(new_l), acc

    init = (tuple(jnp.full((tq, 1), MASK_VALUE, F32) for _ in range(heads)),
            tuple(jnp.zeros((tq, 1), F32) for _ in range(heads)),
            jnp.zeros((heads * tq, NSA_DV), F32))
    jd = t0 // tk
    carry = lax.fori_loop(0, jd, lambda j, c: slc_tile(j, c, False), init)
    _, ls, acc = slc_tile(jd, carry, True)
    o_slc = [rows(acc, h) / (ls[h] + SOFTMAX_EPS) for h in range(heads)]

    wlen = WIN + tq
    start = pl.multiple_of(jnp.maximum(t0 - WIN, 0), LANES)
    kw = kw_ref[0, pl.ds(start, wlen), :]
    vw = vw_ref[0, pl.ds(start, wlen), :]
    s_w = lax.dot_general(qst, kw, _NT, preferred_element_type=F32)
    rel = (t0 + lax.broadcasted_iota(jnp.int32, (tq, wlen), 0)
           - (start + lax.broadcasted_iota(jnp.int32, (tq, wlen), 1)))
    mask_w = (rel >= 0) & (rel < WIN)
    rel_f = rel.astype(F32)
    p_w = [_softmax_rows(rows(s_w, h) - _alibi_slope(h) * rel_f, mask_w) for h in range(heads)]
    o_win = jnp.dot(jnp.concatenate(p_w, axis=0).astype(BF16), vw, preferred_element_type=F32)

    gates = _sigmoid(g_ref[0])
    z = z_ref[0]
    for h in range(heads):
        c = NSA_DK + 3 * h
        o = (gates[:, c:c + 1] * rows(o_cmp, h) + gates[:, c + 1:c + 2] * o_slc[h]
             + gates[:, c + 2:c + 3] * rows(o_win, h))
        zh = z[:, h * NSA_DV:(h + 1) * NSA_DV]
        o_ref[0, :, h * NSA_DV:(h + 1) * NSA_DV] = (o * _silu(zh)).astype(BF16)


def _nsa_attn(hb, hf, k_cmp, v_cmp, ovt, e, tq, tk):
    b, s, _ = hb.shape
    ncp = k_cmp.shape[1]
    n_s = s // SLC_LEN
    kern = functools.partial(_nsa_attn_kernel, tq=tq, tk=tk, n_c=ncp - 1, n_s=n_s,
                             top_n=min(SLC_TOPN, n_s))
    qw = NSA_HEADS * HEAD_PAD
    return pl.pallas_call(
        kern,
        grid=(b, s // tq),
        in_specs=[
            pl.BlockSpec((1, tq, qw), lambda bi, i: (bi, i, B_QNSA // qw)),
            pl.BlockSpec((1, tq, HEAD_PAD), lambda bi, i: (bi, i, F_KC // HEAD_PAD)),
            pl.BlockSpec((1, tq, NSA_WIDTH), lambda bi, i: (bi, i, F_ZNSA // NSA_WIDTH)),
            pl.BlockSpec((1, ncp, HEAD_PAD), lambda bi, i: (bi, 0, 0)),
            pl.BlockSpec((1, ncp, NSA_DV), lambda bi, i: (bi, 0, 0)),
            pl.BlockSpec((1, s, HEAD_PAD), lambda bi, i: (bi, 0, B_KS // HEAD_PAD)),
            pl.BlockSpec((1, s, NSA_DV), lambda bi, i: (bi, 0, B_VS // NSA_DV)),
            pl.BlockSpec((1, s, HEAD_PAD), lambda bi, i: (bi, 0, B_KW // HEAD_PAD)),
            pl.BlockSpec((1, s, NSA_DV), lambda bi, i: (bi, 0, B_VW // NSA_DV)),
            pl.BlockSpec(ovt.shape, lambda bi, i: (0, 0)),
            pl.BlockSpec(e.shape, lambda bi, i: (0, 0, 0)),
        ],
        out_specs=pl.BlockSpec((1, tq, NSA_WIDTH), lambda bi, i: (bi, i, 0)),
        out_shape=jax.ShapeDtypeStruct((b, s, NSA_WIDTH), BF16),
        compiler_params=_params("arbitrary", "arbitrary"),
        name="nsa_attn",
    )(hb, hf, hf, k_cmp, v_cmp, hb, hb, hb, hb, ovt, e)


def _mem_attn_kernel(q_ref, k_ref, v_ref, z_ref, o_ref):
    q = q_ref[0]
    k = k_ref[0]
    v = v_ref[0]
    z = z_ref[0]
    for h in range(MEM_HEADS):
        sl = slice(h * MEM_DH, (h + 1) * MEM_DH)
        s = lax.dot_general(q[:, sl], k[:, sl], _NT, preferred_element_type=F32)
        p = jnp.exp(s - jnp.max(s, axis=-1, keepdims=True))
        o = jnp.dot(p.astype(BF16), v[:, sl], preferred_element_type=F32)
        o = o / jnp.sum(p, axis=-1, keepdims=True)
        o_ref[0, :, sl] = (o * _silu(z[:, sl])).astype(BF16)


def _mem_attn(hb, mem_kv, hf, tq):
    b, s, _ = hb.shape
    mlen = mem_kv.shape[1]
    return pl.pallas_call(
        _mem_attn_kernel,
        grid=(b, s // tq),
        in_specs=[
            pl.BlockSpec((1, tq, MEM_WIDTH), lambda bi, i: (bi, i, B_QMEM // MEM_WIDTH)),
            pl.BlockSpec((1, mlen, MEM_WIDTH), lambda bi, i: (bi, 0, 0)),
            pl.BlockSpec((1, mlen, MEM_WIDTH), lambda bi, i: (bi, 0, 1)),
            pl.BlockSpec((1, tq, MEM_WIDTH), lambda bi, i: (bi, i, F_ZMEM // MEM_WIDTH)),
        ],
        out_specs=pl.BlockSpec((1, tq, MEM_WIDTH), lambda bi, i: (bi, i, 0)),
        out_shape=jax.ShapeDtypeStruct((b, s, MEM_WIDTH), BF16),
        compiler_params=_params("arbitrary", "arbitrary"),
        name="mem_attn",
    )(hb, mem_kv, mem_kv, hf)


def _out_proj_kernel(x_ref, a_ref, n_ref, m_ref, w_ref, g_ref, o_ref, *, final_norm):
    y = x_ref[...]
    y = y + jnp.dot(a_ref[...], w_ref[0:MLA_WIDTH, :], preferred_element_type=F32)
    y = y + jnp.dot(n_ref[...], w_ref[MLA_WIDTH:MLA_WIDTH + NSA_WIDTH, :], preferred_element_type=F32)
    y = y + jnp.dot(m_ref[...], w_ref[MLA_WIDTH + NSA_WIDTH:, :], preferred_element_type=F32)
    if final_norm:
        y = _rmsnorm(y, g_ref[...])
    o_ref[...] = y


def _out_proj(x, o_mla, o_nsa, o_mem, w_out, g, final_norm, tm):
    m, d = x.shape
    kern = functools.partial(_out_proj_kernel, final_norm=final_norm)
    row = lambda width: pl.BlockSpec((tm, width), lambda i: (i, 0))
    return pl.pallas_call(
        kern,
        grid=(m // tm,),
        in_specs=[row(d), row(MLA_WIDTH), row(NSA_WIDTH), row(MEM_WIDTH),
                  pl.BlockSpec(w_out.shape, lambda i: (0, 0)),
                  pl.BlockSpec((1, d), lambda i: (0, 0))],
        out_specs=row(d),
        out_shape=jax.ShapeDtypeStruct((m, d), F32),
        compiler_params=_params("arbitrary"),
        name="out_proj",
    )(x, o_mla, o_nsa, o_mem, w_out, g.reshape(1, d))


def _pad_cols(w, width):
    return jnp.pad(w, ((0, 0), (0, width - w.shape[1])))


def _split_w_in(w_in):
    offs = [0]
    for n in IN_SPLITS:
        offs.append(offs[-1] + n)
    return [w_in[:, offs[i]:offs[i + 1]] for i in range(len(IN_SPLITS))]


def _layout_w_in(w_in):
    (c_q, c_kv, k_rope, z_mla, q_nsa, k_c, v_c, k_s, v_s, k_w, v_w, g_nsa, z_nsa,
     q_mem, z_mem) = _split_w_in(w_in)
    d = w_in.shape[0]
    wf = jnp.concatenate([
        c_q, c_kv, z_mla, z_nsa, z_mem,
        _pad_cols(jnp.concatenate([k_c, g_nsa], axis=1), HEAD_PAD), v_c,
        _pad_cols(k_rope, LANES)], axis=1).astype(BF16)
    q_heads = jnp.pad(q_nsa.reshape(d, NSA_HEADS, NSA_DK), ((0, 0), (0, 0), (0, HEAD_PAD - NSA_DK)))
    wb = jnp.concatenate([
        q_heads.reshape(d, NSA_HEADS * HEAD_PAD), q_mem,
        _pad_cols(k_s, HEAD_PAD), _pad_cols(k_w, HEAD_PAD), v_s, v_w], axis=1).astype(BF16)
    scale_b = jnp.concatenate([
        jnp.full((NSA_HEADS * HEAD_PAD,), NSA_DK ** -0.5, F32),
        jnp.full((MEM_WIDTH,), MEM_DH ** -0.5, F32),
        jnp.ones((B_WIDTH - B_QMEM - MEM_WIDTH,), F32)])
    return wf, wb, scale_b


def _rope_tables(seq):
    pos = jnp.arange(seq, dtype=F32)
    inv_freq = ROPE_THETA ** (-jnp.arange(0, MLA_ROPE, 2, dtype=F32) / MLA_ROPE)
    ang = pos[:, None] * inv_freq[None, :]
    zeros = jnp.zeros((seq, LANES - MLA_ROPE), F32)
    cos2 = jnp.concatenate([jnp.cos(ang), jnp.cos(ang), zeros], axis=1)
    sin2 = jnp.concatenate([jnp.sin(ang), jnp.sin(ang), zeros], axis=1)
    return cos2, sin2, jnp.cos(ang).T, jnp.sin(ang).T


def _selection_tables(seq, tk):
    chunks = seq // CMP_STRIDE
    n_s = seq // SLC_LEN
    c_start = jnp.arange(chunks) * CMP_STRIDE
    s_start = jnp.arange(n_s) * SLC_LEN
    overlap_t = ((c_start[None, :] < s_start[:, None] + SLC_LEN)
                 & (c_start[None, :] + CMP_LEN > s_start[:, None])
                 & (jnp.arange(chunks)[None, :] < chunks - 1))
    key_block = jnp.arange(seq) // SLC_LEN
    expand = (jnp.arange(LANES)[:, None] == key_block[None, :])
    expand = expand.reshape(LANES, seq // tk, tk).transpose(1, 0, 2)
    return overlap_t.astype(BF16), expand.astype(BF16)


def _layer(x2, mem2, batch, seq, tables, norm_g, w_in, q_norm_g, w_uq, kv_norm_g, w_ukv,
           cmp_pe_k, cmp_pe_v, cmp_w1k, cmp_w2k, cmp_w1v, cmp_w2v, mem_norm_g, w_mem_kv, w_out,
           final_g, final_norm):
    rope, ovt, expand, tk_nsa = tables
    d = x2.shape[1]
    wf, wb, scale_b = _layout_w_in(w_in)
    hf = _norm_proj(x2, norm_g, wf, jnp.ones((F_WIDTH,), F32), F32, 1024, 512, "in_proj_f32")
    hb = _norm_proj(x2, norm_g, wb, scale_b, BF16, 1024, 768, "in_proj_bf16")

    wq = jnp.pad(w_uq.reshape(MLA_Q_RANK, MLA_HEADS, MLA_NOPE + MLA_ROPE),
                 ((0, 0), (0, 0), (0, HEAD_PAD - MLA_NOPE - MLA_ROPE)))
    wqt = wq.reshape(MLA_Q_RANK, MLA_HEADS * HEAD_PAD).T.astype(BF16)
    wkv = w_ukv.reshape(MLA_KV_RANK, MLA_HEADS, MLA_NOPE + MLA_V)
    wk = wkv[:, :, :MLA_NOPE].reshape(MLA_KV_RANK, MLA_HEADS * MLA_NOPE).astype(BF16)
    wvt = wkv[:, :, MLA_NOPE:].reshape(MLA_KV_RANK, MLA_WIDTH).T.astype(BF16)
    qt, k, vt = _mla_up(hf, q_norm_g, kv_norm_g, wqt, wk, wvt, rope, seq, MLA_TILE)
    hf3 = hf.reshape(batch, seq, F_WIDTH)
    hb3 = hb.reshape(batch, seq, B_WIDTH)
    o_mla = _mla_attn(qt, k.reshape(batch, seq, -1), vt, hf3, MLA_TILE, 4)

    chunks = seq // CMP_STRIDE
    xk = hf3[:, :, F_KC:F_KC + NSA_DK].reshape(batch, chunks, CMP_STRIDE * NSA_DK)
    xv = hf3[:, :, F_VC:F_VC + NSA_DV].reshape(batch, chunks, CMP_STRIDE * NSA_DV)
    k_cmp, v_cmp = _compress(
        xk, xv,
        cmp_pe_k.reshape(2, CMP_STRIDE * NSA_DK), cmp_pe_v.reshape(2, CMP_STRIDE * NSA_DV),
        _pad_cols(cmp_w1k, HEAD_PAD).astype(BF16),
        jnp.pad(cmp_w2k, ((0, HEAD_PAD - NSA_DK), (0, HEAD_PAD - NSA_DK))).astype(BF16),
        cmp_w1v.astype(BF16), cmp_w2v.astype(BF16))
    o_nsa = _nsa_attn(hb3, hf3, k_cmp, v_cmp, ovt, expand, 128, tk_nsa)

    mem_kv = _norm_proj(mem2, mem_norm_g, w_mem_kv.astype(BF16), jnp.ones((2 * MEM_WIDTH,), F32),
                        BF16, mem2.shape[0] // batch, MEM_WIDTH, "mem_kv_proj")
    o_mem = _mem_attn(hb3, mem_kv.reshape(batch, -1, 2 * MEM_WIDTH), hf3, 512)

    return _out_proj(x2, o_mla.reshape(-1, MLA_WIDTH), o_nsa.reshape(-1, NSA_WIDTH),
                     o_mem.reshape(-1, MEM_WIDTH), w_out.astype(BF16), final_g, final_norm, 512)


def kernel(x, mem, norm_g, w_in, q_norm_g, w_uq, kv_norm_g, w_ukv, cmp_pe_k, cmp_pe_v,
           cmp_w1k, cmp_w2k, cmp_w1v, cmp_w2v, mem_norm_g, w_mem_kv, w_out, final_norm_g):
    batch, seq, d = x.shape
    depth = norm_g.shape[0]
    tk_nsa = 512
    ovt, expand = _selection_tables(seq, tk_nsa)
    tables = (_rope_tables(seq), ovt, expand, tk_nsa)
    x2 = x.reshape(batch * seq, d)
    mem2 = mem.reshape(batch * mem.shape[1], d)
    for l in range(depth):
        x2 = _layer(x2, mem2, batch, seq, tables, norm_g[l], w_in[l], q_norm_g[l], w_uq[l],
                    kv_norm_g[l], w_ukv[l], cmp_pe_k[l], cmp_pe_v[l], cmp_w1k[l], cmp_w2k[l],
                    cmp_w1v[l], cmp_w2v[l], mem_norm_g[l], w_mem_kv[l], w_out[l],
                    final_norm_g, l == depth - 1)
    return x2.reshape(batch, seq, d)
```

```python
import functools

import jax
import jax.numpy as jnp
from jax import lax
from jax.experimental import pallas as pl
from jax.experimental.pallas import tpu as pltpu

F32 = jnp.float32
BF16 = jnp.bfloat16

EPS = 1e-6
NEG_BIG = 1e9
MASK_VALUE = -1e30
SOFTMAX_EPS = 1e-20
LOG2E = 1.4426950408889634

MLA_HEADS = 8
MLA_NOPE = 128
MLA_ROPE = 64
MLA_V = 128
MLA_Q_RANK = 512
MLA_KV_RANK = 512
ROPE_THETA = 10000.0

NSA_HEADS = 4
NSA_DK = 192
NSA_DV = 128
CMP_LEN = 32
CMP_STRIDE = 16
SLC_LEN = 64
SLC_TOPN = 16
WIN = 512

MEM_HEADS = 4
MEM_DH = 128

MLA_WIDTH = MLA_HEADS * MLA_V
NSA_WIDTH = NSA_HEADS * NSA_DV
MEM_WIDTH = MEM_HEADS * MEM_DH

IN_SPLITS = (
    MLA_Q_RANK, MLA_KV_RANK, MLA_ROPE, MLA_WIDTH,
    NSA_HEADS * NSA_DK, NSA_DK, NSA_DV, NSA_DK, NSA_DV,
    NSA_DK, NSA_DV, 3 * NSA_HEADS, NSA_WIDTH,
    MEM_WIDTH, MEM_WIDTH,
)

LANES = 128
MXU_COLS = 256
HEAD_PAD = 256
VMEM_LIMIT = 56 * 1024 * 1024
MLA_TILE = 512
NSA_TQ = 256
NSA_TK = 512

F_CQ, F_CKV, F_Z, F_KC, F_VC, F_KROPE = 0, 512, 1024, 3072, 3328, 3456
F_ZNSA, F_ZMEM = F_Z + MLA_WIDTH, F_Z + MLA_WIDTH + NSA_WIDTH
F_GATE = F_KC + NSA_DK
F_WIDTH = 3584
B_QNSA, B_QMEM, B_KS, B_KW, B_VS, B_VW = 0, 1024, 1536, 1792, 2048, 2176
B_WIDTH = 2304

_NT = (((1,), (1,)), ((), ()))


def _params(*sem):
    return pltpu.CompilerParams(dimension_semantics=sem, vmem_limit_bytes=VMEM_LIMIT)


def _sigmoid(x):
    return 1.0 / (1.0 + jnp.exp(-x))


def _silu(x):
    return x * _sigmoid(x)


def _rmsnorm(x, g):
    ms = jnp.mean(x * x, axis=-1, keepdims=True)
    return (x * lax.rsqrt(ms + EPS)) * g


def _norm_proj_kernel(x_ref, g_ref, w_ref, cs_ref, o_ref, xn_ref):
    @pl.when(pl.program_id(1) == 0)
    def _():
        xn_ref[...] = _rmsnorm(x_ref[...], g_ref[...]).astype(BF16)

    acc = jnp.dot(xn_ref[...], w_ref[...], preferred_element_type=F32)
    o_ref[...] = (acc * cs_ref[...]).astype(o_ref.dtype)


def _norm_proj(x, g, w, colscale, out_dtype, tm, tn, name):
    m, k = x.shape
    n = w.shape[1]
    return pl.pallas_call(
        _norm_proj_kernel,
        grid=(m // tm, n // tn),
        in_specs=[
            pl.BlockSpec((tm, k), lambda i, j: (i, 0)),
            pl.BlockSpec((1, k), lambda i, j: (0, 0)),
            pl.BlockSpec((k, tn), lambda i, j: (0, j)),
            pl.BlockSpec((1, tn), lambda i, j: (0, j)),
        ],
        out_specs=pl.BlockSpec((tm, tn), lambda i, j: (i, j)),
        out_shape=jax.ShapeDtypeStruct((m, n), out_dtype),
        scratch_shapes=[pltpu.VMEM((tm, k), BF16)],
        compiler_params=_params("arbitrary", "arbitrary"),
        name=name,
    )(x, g.reshape(1, k), w, colscale.reshape(1, n))


def _rope_half(hi, cos2, sin2):
    up = pltpu.roll(hi, 32, axis=1)
    down = pltpu.roll(hi, 96, axis=1)
    return hi * cos2 + (up - down) * sin2


def _mla_up_kernel(cq_ref, ckv_ref, kr_ref, gq_ref, gkv_ref, wqt_ref, wk_ref, wvt_ref,
                   cos_ref, sin_ref, cost_ref, sint_ref, qt_ref, k_ref, vt_ref):
    scale = LOG2E * (MLA_NOPE + MLA_ROPE) ** -0.5
    half = MLA_ROPE // 2
    cos_t = cost_ref[...]
    sin_t = sint_ref[...]

    cqn = _rmsnorm(cq_ref[...], gq_ref[...]).astype(BF16)
    yt = lax.dot_general(wqt_ref[...], cqn, _NT, preferred_element_type=F32)
    for h in range(MLA_HEADS):
        r = h * HEAD_PAD
        x1 = yt[r + MLA_NOPE:r + MLA_NOPE + half]
        x2 = yt[r + MLA_NOPE + half:r + MLA_NOPE + MLA_ROPE]
        qt_ref[0, r:r + MLA_NOPE, :] = (yt[r:r + MLA_NOPE] * scale).astype(BF16)
        qt_ref[0, r + MLA_NOPE:r + MLA_NOPE + half, :] = (
            (x1 * cos_t - x2 * sin_t) * scale).astype(BF16)
        qt_ref[0, r + MLA_NOPE + half:r + MLA_NOPE + MLA_ROPE, :] = (
            (x1 * sin_t + x2 * cos_t) * scale).astype(BF16)
        qt_ref[0, r + MLA_NOPE + MLA_ROPE:r + HEAD_PAD, :] = jnp.zeros(
            (HEAD_PAD - MLA_NOPE - MLA_ROPE, yt.shape[1]), BF16)

    ckn = _rmsnorm(ckv_ref[...], gkv_ref[...]).astype(BF16)
    kn = jnp.dot(ckn, wk_ref[...], preferred_element_type=F32)
    k_pe = _rope_half(kr_ref[...], cos_ref[...], sin_ref[...]).astype(BF16)
    for h in range(MLA_HEADS):
        k_ref[:, h * HEAD_PAD:h * HEAD_PAD + LANES] = kn[:, h * LANES:(h + 1) * LANES].astype(BF16)
        k_ref[:, h * HEAD_PAD + LANES:(h + 1) * HEAD_PAD] = k_pe
    vt_ref[0] = lax.dot_general(wvt_ref[...], ckn, _NT, preferred_element_type=F32).astype(BF16)


def _mla_up(hf, gq, gkv, wqt, wk, wvt, rope, seq, tm):
    cos2, sin2, cos_t, sin_t = rope
    m = hf.shape[0]
    steps_per_seq = seq // tm
    hq = MLA_HEADS * HEAD_PAD
    half = MLA_ROPE // 2
    const = lambda i: (0, 0)
    return pl.pallas_call(
        _mla_up_kernel,
        grid=(m // tm,),
        in_specs=[
            pl.BlockSpec((tm, MLA_Q_RANK), lambda i: (i, F_CQ // MLA_Q_RANK)),
            pl.BlockSpec((tm, MLA_KV_RANK), lambda i: (i, F_CKV // MLA_KV_RANK)),
            pl.BlockSpec((tm, LANES), lambda i: (i, F_KROPE // LANES)),
            pl.BlockSpec((1, MLA_Q_RANK), const),
            pl.BlockSpec((1, MLA_KV_RANK), const),
            pl.BlockSpec((hq, MLA_Q_RANK), const),
            pl.BlockSpec((MLA_KV_RANK, MLA_HEADS * MLA_NOPE), const),
            pl.BlockSpec((MLA_WIDTH, MLA_KV_RANK), const),
            pl.BlockSpec((tm, LANES), lambda i: (i % steps_per_seq, 0)),
            pl.BlockSpec((tm, LANES), lambda i: (i % steps_per_seq, 0)),
            pl.BlockSpec((half, tm), lambda i: (0, i % steps_per_seq)),
            pl.BlockSpec((half, tm), lambda i: (0, i % steps_per_seq)),
        ],
        out_specs=[
            pl.BlockSpec((1, hq, tm), lambda i: (i, 0, 0)),
            pl.BlockSpec((tm, hq), lambda i: (i, 0)),
            pl.BlockSpec((1, MLA_WIDTH, tm), lambda i: (i, 0, 0)),
        ],
        out_shape=[
            jax.ShapeDtypeStruct((m // tm, hq, tm), BF16),
            jax.ShapeDtypeStruct((m, hq), BF16),
            jax.ShapeDtypeStruct((m // tm, MLA_WIDTH, tm), BF16),
        ],
        compiler_params=_params("arbitrary"),
        name="mla_up",
    )(hf, hf, hf, gq.reshape(1, -1), gkv.reshape(1, -1), wqt, wk, wvt, cos2, sin2, cos_t, sin_t)


def _mla_attn_kernel(qt_ref, k_ref, vt_ref, z_ref, o_ref, m_ref, l_ref, acc_ref, s0_ref, *,
                     tile_len, hps):
    t = tile_len
    qi = pl.program_id(2)
    qts = [qt_ref[0, h * HEAD_PAD:(h + 1) * HEAD_PAD, :] for h in range(hps)]

    def scores(j, h):
        kb = pl.multiple_of(j * t, t)
        return jnp.dot(k_ref[0, pl.ds(kb, t), h * HEAD_PAD:(h + 1) * HEAD_PAD], qts[h],
                       preferred_element_type=F32)

    def softmax(h, s, diagonal):
        if diagonal:
            kpos = lax.broadcasted_iota(jnp.int32, (t, t), 0)
            qpos = lax.broadcasted_iota(jnp.int32, (t, t), 1)
            visible = kpos <= qpos
            s = jnp.where(visible, s, MASK_VALUE)
        m = m_ref[h]
        m_new = jnp.maximum(m, jnp.max(s, axis=0, keepdims=True))
        alpha = jnp.exp2(m - m_new)
        p = jnp.exp2(s - m_new)
        if diagonal:
            p = jnp.where(visible, p, 0.0)
        m_ref[h] = m_new
        l_ref[h] = alpha * l_ref[h] + jnp.sum(p, axis=0, keepdims=True)
        return alpha, p.astype(BF16)

    def accumulate(j, h, alpha, p):
        vt = vt_ref[j, h * MLA_V:(h + 1) * MLA_V, :]
        acc_ref[h] = alpha * acc_ref[h] + jnp.dot(vt, p, preferred_element_type=F32)

    def tile(j, diagonal):
        alpha, p = softmax(0, s0_ref[...], diagonal)
        for h in range(1, hps):
            s = scores(j, h)
            accumulate(j, h - 1, alpha, p)
            alpha, p = softmax(h, s, diagonal)
        if not diagonal:
            s0_ref[...] = scores(j + 1, 0)
        accumulate(j, hps - 1, alpha, p)

    for h in range(hps):
        m_ref[h] = jnp.full((1, t), MASK_VALUE, F32)
        l_ref[h] = jnp.zeros((1, t), F32)
        acc_ref[h] = jnp.zeros((MLA_V, t), F32)
    s0_ref[...] = scores(0, 0)

    def body(j, carry):
        tile(j, False)
        return carry

    lax.fori_loop(0, qi, body, 0)

    tile(qi, True)
    for h in range(hps):
        o = (acc_ref[h] / (l_ref[h] + SOFTMAX_EPS)).T
        cols = slice(h * MLA_V, (h + 1) * MLA_V)
        o_ref[0, :, cols] = (o * _silu(z_ref[0, :, cols])).astype(BF16)


def _mla_attn(qt, k, vt, hf, tile_len, hps):
    b, s, _ = k.shape
    t = tile_len
    tiles = s // t
    kern = functools.partial(_mla_attn_kernel, tile_len=t, hps=hps)
    qw, vw = hps * HEAD_PAD, hps * MLA_V
    return pl.pallas_call(
        kern,
        grid=(b, MLA_HEADS // hps, tiles),
        in_specs=[
            pl.BlockSpec((1, qw, t), lambda bi, h, i: (bi * tiles + i, h, 0)),
            pl.BlockSpec((1, s, qw), lambda bi, h, i: (bi, 0, h)),
            pl.BlockSpec((tiles, vw, t), lambda bi, h, i: (bi, h, 0)),
            pl.BlockSpec((1, t, vw), lambda bi, h, i: (bi, i, F_Z // vw + h)),
        ],
        out_specs=pl.BlockSpec((1, t, vw), lambda bi, h, i: (bi, i, h)),
        out_shape=jax.ShapeDtypeStruct((b, s, MLA_WIDTH), BF16),
        scratch_shapes=[
            pltpu.VMEM((hps, 1, t), F32),
            pltpu.VMEM((hps, 1, t), F32),
            pltpu.VMEM((hps, MLA_V, t), F32),
            pltpu.VMEM((t, t), F32),
        ],
        compiler_params=_params("arbitrary", "arbitrary", "arbitrary"),
        name="mla_attn",
    )(qt, k, vt, hf)


def _compress_one(x_ref, pe_ref, w1_ref, w2_ref, o_ref):
    x = x_ref[0]
    half = x.shape[1]
    a = jnp.dot((x + pe_ref[0:1, :]).astype(BF16), w1_ref[0:half, :], preferred_element_type=F32)
    b = jnp.dot((x + pe_ref[1:2, :]).astype(BF16), w1_ref[half:2 * half, :], preferred_element_type=F32)
    h1 = a + pltpu.roll(b, b.shape[0] - 1, axis=0)
    o_ref[0] = jnp.dot(_silu(h1).astype(BF16), w2_ref[...], preferred_element_type=F32).astype(BF16)


def _compress_kernel(xk_ref, xv_ref, pek_ref, pev_ref, w1k_ref, w2k_ref, w1v_ref, w2v_ref,
                     ok_ref, ov_ref):
    _compress_one(xk_ref, pek_ref, w1k_ref, w2k_ref, ok_ref)
    _compress_one(xv_ref, pev_ref, w1v_ref, w2v_ref, ov_ref)


def _compress(xk, xv, pek, pev, w1k, w2k, w1v, w2v):
    b, chunks, _ = xk.shape
    full = lambda a: pl.BlockSpec(a.shape, lambda bi: (0,) * a.ndim)
    per_b = lambda a: pl.BlockSpec((1,) + a.shape[1:], lambda bi: (bi, 0, 0))
    return pl.pallas_call(
        _compress_kernel,
        grid=(b,),
        in_specs=[per_b(xk), per_b(xv), full(pek), full(pev), full(w1k), full(w2k), full(w1v), full(w2v)],
        out_specs=[
            pl.BlockSpec((1, chunks, HEAD_PAD), lambda bi: (bi, 0, 0)),
            pl.BlockSpec((1, chunks, NSA_DV), lambda bi: (bi, 0, 0)),
        ],
        out_shape=[
            jax.ShapeDtypeStruct((b, chunks, HEAD_PAD), BF16),
            jax.ShapeDtypeStruct((b, chunks, NSA_DV), BF16),
        ],
        compiler_params=_params("arbitrary"),
        name="nsa_compress",
    )(xk, xv, pek, pev, w1k, w2k, w1v, w2v)


def _alibi_slope(h):
    return 2.0 ** (-8.0 * (h + 1) / NSA_HEADS)


def _nsa_attn_kernel(q_ref, g_ref, z_ref, kc_ref, vc_ref, ks_ref, vs_ref, kw_ref, vw_ref,
                     ovt_ref, oh_ref, tabs_ref, tabw_ref, slope_ref, o_ref,
                     kaug_ref, kwp_ref, vst_ref, vwt_ref, vct_ref, m_ref, l_ref, acc_ref, s0_ref,
                     *, tq, tk, n_c, n_s, top_n):
    qi = pl.program_id(1)
    t0 = qi * tq
    heads = NSA_HEADS
    seq = ks_ref.shape[1]
    ncp = kc_ref.shape[1]
    hw = 2 * tq
    aug = HEAD_PAD - NSA_DK
    wlen = WIN + tq

    @pl.when(qi == 0)
    def _():
        kaug_ref[...] = ks_ref[0] + oh_ref[...]
        col = lax.broadcasted_iota(jnp.int32, (WIN, HEAD_PAD), 1)
        kwp_ref[0:WIN, :] = jnp.where(col == NSA_DK, 1.0, 0.0).astype(BF16)
        kwp_ref[WIN:, :] = kw_ref[0]
        for i in range(seq // tk):
            vst_ref[i] = vs_ref[0, i * tk:(i + 1) * tk, :].astype(F32).T.astype(BF16)
        for i in range(WIN // LANES):
            vwt_ref[i] = jnp.zeros((NSA_DV, LANES), BF16)
        for i in range(seq // LANES):
            vwt_ref[WIN // LANES + i] = (
                vw_ref[0, i * LANES:(i + 1) * LANES, :].astype(F32).T.astype(BF16))
        vct_ref[...] = vc_ref[0].astype(F32).T.astype(BF16)

    q_all = q_ref[0]
    qt = [q_all[:, h * HEAD_PAD:(h + 1) * HEAD_PAD].astype(F32).T[0:NSA_DK] for h in range(heads)]

    def stacked_qt(extra):
        cols = [jnp.concatenate([qt[h], extra], axis=0) for h in range(heads)]
        return [jnp.concatenate(cols[2 * x:2 * x + 2], axis=1).astype(BF16) for x in range(2)]

    flag_rows = jnp.where(lax.broadcasted_iota(jnp.int32, (aug, tq), 0) == 0, MASK_VALUE, 0.0)
    qt_win = stacked_qt(flag_rows)

    n_i = lax.broadcasted_iota(jnp.int32, (ncp, tq), 0)
    t_c = t0 + lax.broadcasted_iota(jnp.int32, (ncp, tq), 1)
    mask_c = (n_i * CMP_STRIDE + (CMP_LEN - 1) <= t_c) & (n_i < n_c)
    dist_c = (t_c.astype(F32) - (n_i.astype(F32) * CMP_STRIDE + (CMP_LEN - 1) / 2.0)) * LOG2E
    kc = kc_ref[0]
    o_cmp, p_sum = [], None
    for x in range(2):
        s_c = jnp.dot(kc, qt_win[x], preferred_element_type=F32)
        p_x = []
        for hl in range(2):
            s = s_c[:, hl * tq:(hl + 1) * tq] - _alibi_slope(2 * x + hl) * dist_c
            s = jnp.where(mask_c, s, MASK_VALUE)
            p = jnp.where(mask_c, jnp.exp2(s - jnp.max(s, axis=0, keepdims=True)), 0.0)
            p = p / (jnp.sum(p, axis=0, keepdims=True) + SOFTMAX_EPS)
            p_sum = p if p_sum is None else p_sum + p
            p_x.append(p.astype(BF16))
        o_cmp.append(jnp.dot(vct_ref[...], jnp.concatenate(p_x, axis=1),
                             preferred_element_type=F32))
    p_hi = p_sum.astype(BF16)
    p_lo = (p_sum - p_hi.astype(F32)).astype(BF16)
    ovt = ovt_ref[...]
    imp = (jnp.dot(ovt, p_hi, preferred_element_type=F32)
           + jnp.dot(ovt, p_lo, preferred_element_type=F32))

    kwb = kwp_ref[pl.ds(pl.multiple_of(t0, LANES), wlen), :]
    tile0 = qi * (tq // LANES)
    vwb = jnp.concatenate([vwt_ref[tile0 + r] for r in range(wlen // LANES)], axis=1)
    o_win = []
    for x in range(2):
        s = jnp.dot(kwb, qt_win[x], preferred_element_type=F32) + tabw_ref[:, x * hw:(x + 1) * hw]
        p = jnp.exp2(s - jnp.max(s, axis=0, keepdims=True))
        l = jnp.sum(p, axis=0, keepdims=True)
        o_win.append(jnp.dot(vwb, p.astype(BF16), preferred_element_type=F32) / (l + SOFTMAX_EPS))

    blk = lax.broadcasted_iota(jnp.int32, (n_s, tq), 0)
    cur = (t0 + lax.broadcasted_iota(jnp.int32, (n_s, tq), 1)) // SLC_LEN
    forced = (blk == 0) | (blk == cur) | (blk == cur - 1)
    imp = jnp.where(forced, NEG_BIG, imp)
    imp = jnp.where(blk > cur, -NEG_BIG, imp)
    sub = lax.broadcasted_iota(jnp.int32, (8, tq), 0)
    groups = [imp[8 * g:8 * g + 8] for g in range(n_s // 8)]
    ranks = [jnp.zeros((8, tq), F32) for _ in groups]
    for jp in range(n_s):
        row = imp[jp:jp + 1, :]
        for g, grp in enumerate(groups):
            ge = jnp.where(row >= grp, 1.0, 0.0)
            gt = jnp.where(row > grp, 1.0, 0.0)
            if 8 * g > jp:
                beats = ge
            elif 8 * g + 8 <= jp:
                beats = gt
            else:
                beats = jnp.where(sub + 8 * g > jp, ge, gt)
            ranks[g] = ranks[g] + beats
    sel_rows = jnp.where(jnp.concatenate(ranks, axis=0) < top_n, 0.0, MASK_VALUE)
    if n_s < aug:
        sel_rows = jnp.concatenate([sel_rows, jnp.zeros((aug - n_s, tq), F32)], axis=0)
    qt_sel = stacked_qt(sel_rows)

    jd = t0 // tk

    def scores(j, x):
        kb = j * tk if isinstance(j, int) else pl.multiple_of(j * tk, tk)
        return jnp.dot(kaug_ref[pl.ds(kb, tk), :], qt_sel[x], preferred_element_type=F32)

    def softmax(j, x, s, diagonal):
        u = s + tabs_ref[:, x * hw:(x + 1) * hw]
        if diagonal:
            r_minus_q = (lax.broadcasted_iota(jnp.int32, (tk, hw), 0)
                         - (lax.broadcasted_iota(jnp.int32, (tk, hw), 1) & (tq - 1)))
            u = jnp.where(r_minus_q <= t0 - j * tk, u, MASK_VALUE)
        off = slope_ref[0:1, x * hw:(x + 1) * hw] * (j * tk - t0).astype(F32)
        m_old = m_ref[x]
        m_new = jnp.maximum(m_old, jnp.max(u, axis=0, keepdims=True) + off)
        alpha = jnp.exp2(m_old - m_new)
        p = jnp.exp2(u - (m_new - off))
        m_ref[x] = m_new
        l_ref[x] = alpha * l_ref[x] + jnp.sum(p, axis=0, keepdims=True)
        return alpha, p.astype(BF16)

    def accumulate(j, x, alpha, p):
        acc_ref[x] = alpha * acc_ref[x] + jnp.dot(vst_ref[j], p, preferred_element_type=F32)

    def tile(j, diagonal):
        alpha, p = softmax(j, 0, s0_ref[...], diagonal)
        s1 = scores(j, 1)
        accumulate(j, 0, alpha, p)
        alpha, p = softmax(j, 1, s1, diagonal)
        if not diagonal:
            s0_ref[...] = scores(j + 1, 0)
        accumulate(j, 1, alpha, p)

    for x in range(2):
        m_ref[x] = jnp.full((1, hw), MASK_VALUE, F32)
        l_ref[x] = jnp.zeros((1, hw), F32)
        acc_ref[x] = jnp.zeros((NSA_DV, hw), F32)
    s0_ref[...] = scores(0, 0)

    def body(j, carry):
        tile(j, False)
        return carry

    lax.fori_loop(0, jd, body, 0)
    tile(jd, True)
    o_slc = [acc_ref[x] / (l_ref[x] + SOFTMAX_EPS) for x in range(2)]

    gates = _sigmoid(g_ref[0]).T
    z = z_ref[0]
    for h in range(heads):
        x, lanes = h // 2, slice((h % 2) * tq, (h % 2 + 1) * tq)
        c = NSA_DK + 3 * h
        o = (gates[c:c + 1] * o_cmp[x][:, lanes] + gates[c + 1:c + 2] * o_slc[x][:, lanes]
             + gates[c + 2:c + 3] * o_win[x][:, lanes])
        zh = z[:, h * NSA_DV:(h + 1) * NSA_DV]
        o_ref[0, :, h * NSA_DV:(h + 1) * NSA_DV] = (o.T * _silu(zh)).astype(BF16)


def _nsa_attn(hb, hf, k_cmp, v_cmp, tables, tq, tk):
    ovt, onehot, tab_sel, tab_win, slope_rows = tables
    b, s, _ = hb.shape
    ncp = k_cmp.shape[1]
    n_s = s // SLC_LEN
    assert n_s <= HEAD_PAD - NSA_DK and n_s % 8 == 0, "selection blocks must fit the spare rows"
    assert tq & (tq - 1) == 0 and tq % LANES == 0 and tk % tq == 0 and s % tk == 0
    kern = functools.partial(_nsa_attn_kernel, tq=tq, tk=tk, n_c=ncp - 1, n_s=n_s,
                             top_n=min(SLC_TOPN, n_s))
    qw = NSA_HEADS * HEAD_PAD
    const2 = lambda a: pl.BlockSpec(a.shape, lambda bi, i: (0, 0))
    return pl.pallas_call(
        kern,
        grid=(b, s // tq),
        in_specs=[
            pl.BlockSpec((1, tq, qw), lambda bi, i: (bi, i, B_QNSA // qw)),
            pl.BlockSpec((1, tq, HEAD_PAD), lambda bi, i: (bi, i, F_KC // HEAD_PAD)),
            pl.BlockSpec((1, tq, NSA_WIDTH), lambda bi, i: (bi, i, F_ZNSA // NSA_WIDTH)),
            pl.BlockSpec((1, ncp, HEAD_PAD), lambda bi, i: (bi, 0, 0)),
            pl.BlockSpec((1, ncp, NSA_DV), lambda bi, i: (bi, 0, 0)),
            pl.BlockSpec((1, s, HEAD_PAD), lambda bi, i: (bi, 0, B_KS // HEAD_PAD)),
            pl.BlockSpec((1, s, NSA_DV), lambda bi, i: (bi, 0, B_VS // NSA_DV)),
            pl.BlockSpec((1, s, HEAD_PAD), lambda bi, i: (bi, 0, B_KW // HEAD_PAD)),
            pl.BlockSpec((1, s, NSA_DV), lambda bi, i: (bi, 0, B_VW // NSA_DV)),
            const2(ovt), const2(onehot), const2(tab_sel), const2(tab_win), const2(slope_rows),
        ],
        out_specs=pl.BlockSpec((1, tq, NSA_WIDTH), lambda bi, i: (bi, i, 0)),
        out_shape=jax.ShapeDtypeStruct((b, s, NSA_WIDTH), BF16),
        scratch_shapes=[
            pltpu.VMEM((s, HEAD_PAD), BF16),
            pltpu.VMEM((s + WIN, HEAD_PAD), BF16),
            pltpu.VMEM((s // tk, NSA_DV, tk), BF16),
            pltpu.VMEM(((s + WIN) // LANES, NSA_DV, LANES), BF16),
            pltpu.VMEM((NSA_DV, ncp), BF16),
            pltpu.VMEM((2, 1, 2 * tq), F32),
            pltpu.VMEM((2, 1, 2 * tq), F32),
            pltpu.VMEM((2, NSA_DV, 2 * tq), F32),
            pltpu.VMEM((tk, 2 * tq), F32),
        ],
        compiler_params=_params("arbitrary", "arbitrary"),
        name="nsa_attn",
    )(hb, hf, hf, k_cmp, v_cmp, hb, hb, hb, hb, ovt, onehot, tab_sel, tab_win, slope_rows)


def _mem_attn_kernel(q_ref, k_ref, v_ref, z_ref, o_ref):
    q = q_ref[0]
    k = k_ref[0]
    v = v_ref[0]
    z = z_ref[0]
    for h in range(MEM_HEADS):
        sl = slice(h * MEM_DH, (h + 1) * MEM_DH)
        s = lax.dot_general(q[:, sl], k[:, sl], _NT, preferred_element_type=F32)
        p = jnp.exp(s - jnp.max(s, axis=-1, keepdims=True))
        o = jnp.dot(p.astype(BF16), v[:, sl], preferred_element_type=F32)
        o = o / jnp.sum(p, axis=-1, keepdims=True)
        o_ref[0, :, sl] = (o * _silu(z[:, sl])).astype(BF16)


def _mem_attn(hb, mem_kv, hf, tq):
    b, s, _ = hb.shape
    mlen = mem_kv.shape[1]
    return pl.pallas_call(
        _mem_attn_kernel,
        grid=(b, s // tq),
        in_specs=[
            pl.BlockSpec((1, tq, MEM_WIDTH), lambda bi, i: (bi, i, B_QMEM // MEM_WIDTH)),
            pl.BlockSpec((1, mlen, MEM_WIDTH), lambda bi, i: (bi, 0, 0)),
            pl.BlockSpec((1, mlen, MEM_WIDTH), lambda bi, i: (bi, 0, 1)),
            pl.BlockSpec((1, tq, MEM_WIDTH), lambda bi, i: (bi, i, F_ZMEM // MEM_WIDTH)),
        ],
        out_specs=pl.BlockSpec((1, tq, MEM_WIDTH), lambda bi, i: (bi, i, 0)),
        out_shape=jax.ShapeDtypeStruct((b, s, MEM_WIDTH), BF16),
        compiler_params=_params("arbitrary", "arbitrary"),
        name="mem_attn",
    )(hb, mem_kv, mem_kv, hf)


def _out_proj_kernel(x_ref, a_ref, n_ref, m_ref, w_ref, g_ref, o_ref, *, final_norm):
    y = x_ref[...]
    y = y + jnp.dot(a_ref[...], w_ref[0:MLA_WIDTH, :], preferred_element_type=F32)
    y = y + jnp.dot(n_ref[...], w_ref[MLA_WIDTH:MLA_WIDTH + NSA_WIDTH, :], preferred_element_type=F32)
    y = y + jnp.dot(m_ref[...], w_ref[MLA_WIDTH + NSA_WIDTH:, :], preferred_element_type=F32)
    if final_norm:
        y = _rmsnorm(y, g_ref[...])
    o_ref[...] = y


def _out_proj(x, o_mla, o_nsa, o_mem, w_out, g, final_norm, tm):
    m, d = x.shape
    kern = functools.partial(_out_proj_kernel, final_norm=final_norm)
    row = lambda width: pl.BlockSpec((tm, width), lambda i: (i, 0))
    return pl.pallas_call(
        kern,
        grid=(m // tm,),
        in_specs=[row(d), row(MLA_WIDTH), row(NSA_WIDTH), row(MEM_WIDTH),
                  pl.BlockSpec(w_out.shape, lambda i: (0, 0)),
                  pl.BlockSpec((1, d), lambda i: (0, 0))],
        out_specs=row(d),
        out_shape=jax.ShapeDtypeStruct((m, d), F32),
        compiler_params=_params("arbitrary"),
        name="out_proj",
    )(x, o_mla, o_nsa, o_mem, w_out, g.reshape(1, d))


def _pad_cols(w, width):
    return jnp.pad(w, ((0, 0), (0, width - w.shape[1])))


def _split_w_in(w_in):
    offs = [0]
    for n in IN_SPLITS:
        offs.append(offs[-1] + n)
    return [w_in[:, offs[i]:offs[i + 1]] for i in range(len(IN_SPLITS))]


def _layout_w_in(w_in):
    (c_q, c_kv, k_rope, z_mla, q_nsa, k_c, v_c, k_s, v_s, k_w, v_w, g_nsa, z_nsa,
     q_mem, z_mem) = _split_w_in(w_in)
    d = w_in.shape[0]
    wf = jnp.concatenate([
        c_q, c_kv, z_mla, z_nsa, z_mem,
        _pad_cols(jnp.concatenate([k_c, g_nsa], axis=1), HEAD_PAD), v_c,
        _pad_cols(k_rope, LANES)], axis=1).astype(BF16)
    q_heads = jnp.pad(q_nsa.reshape(d, NSA_HEADS, NSA_DK), ((0, 0), (0, 0), (0, HEAD_PAD - NSA_DK)))
    wb = jnp.concatenate([
        q_heads.reshape(d, NSA_HEADS * HEAD_PAD), q_mem,
        _pad_cols(k_s, HEAD_PAD), _pad_cols(k_w, HEAD_PAD), v_s, v_w], axis=1).astype(BF16)
    scale_b = jnp.concatenate([
        jnp.full((NSA_HEADS * HEAD_PAD,), LOG2E * NSA_DK ** -0.5, F32),
        jnp.full((MEM_WIDTH,), MEM_DH ** -0.5, F32),
        jnp.ones((B_WIDTH - B_QMEM - MEM_WIDTH,), F32)])
    return wf, wb, scale_b


def _rope_tables(seq):
    pos = jnp.arange(seq, dtype=F32)
    inv_freq = ROPE_THETA ** (-jnp.arange(0, MLA_ROPE, 2, dtype=F32) / MLA_ROPE)
    ang = pos[:, None] * inv_freq[None, :]
    zeros = jnp.zeros((seq, LANES - MLA_ROPE), F32)
    cos2 = jnp.concatenate([jnp.cos(ang), jnp.cos(ang), zeros], axis=1)
    sin2 = jnp.concatenate([jnp.sin(ang), jnp.sin(ang), zeros], axis=1)
    return cos2, sin2, jnp.cos(ang).T, jnp.sin(ang).T


def _nsa_tables(seq, tq, tk):
    chunks = seq // CMP_STRIDE
    n_s = seq // SLC_LEN
    c_start = jnp.arange(chunks) * CMP_STRIDE
    s_start = jnp.arange(n_s) * SLC_LEN
    overlap_t = ((c_start[None, :] < s_start[:, None] + SLC_LEN)
                 & (c_start[None, :] + CMP_LEN > s_start[:, None])
                 & (jnp.arange(chunks)[None, :] < chunks - 1))
    key_block = jnp.arange(seq) // SLC_LEN
    onehot = jnp.arange(HEAD_PAD)[None, :] == (NSA_DK + key_block)[:, None]
    slope = jnp.repeat(jnp.array([_alibi_slope(h) for h in range(NSA_HEADS)], F32) * LOG2E, tq)
    q_lane = jnp.tile(jnp.arange(tq), NSA_HEADS)
    tab_sel = jnp.arange(tk, dtype=F32)[:, None] * slope[None, :]
    rel = q_lane[None, :] - jnp.arange(WIN + tq)[:, None] + WIN
    tab_win = jnp.where((rel >= 0) & (rel < WIN), -slope[None, :] * rel.astype(F32), MASK_VALUE)
    slope_rows = jnp.broadcast_to(slope[None, :], (8, NSA_HEADS * tq))
    return overlap_t.astype(BF16), onehot.astype(BF16), tab_sel, tab_win, slope_rows


def _layer(x2, mem2, batch, seq, tables, norm_g, w_in, q_norm_g, w_uq, kv_norm_g, w_ukv,
           cmp_pe_k, cmp_pe_v, cmp_w1k, cmp_w2k, cmp_w1v, cmp_w2v, mem_norm_g, w_mem_kv, w_out,
           final_g, final_norm):
    rope, nsa_tables = tables
    d = x2.shape[1]
    wf, wb, scale_b = _layout_w_in(w_in)
    hf = _norm_proj(x2, norm_g, wf, jnp.ones((F_WIDTH,), F32), F32, 1024, 512, "in_proj_f32")
    hb = _norm_proj(x2, norm_g, wb, scale_b, BF16, 1024, 768, "in_proj_bf16")

    wq = jnp.pad(w_uq.reshape(MLA_Q_RANK, MLA_HEADS, MLA_NOPE + MLA_ROPE),
                 ((0, 0), (0, 0), (0, HEAD_PAD - MLA_NOPE - MLA_ROPE)))
    wqt = wq.reshape(MLA_Q_RANK, MLA_HEADS * HEAD_PAD).T.astype(BF16)
    wkv = w_ukv.reshape(MLA_KV_RANK, MLA_HEADS, MLA_NOPE + MLA_V)
    wk = wkv[:, :, :MLA_NOPE].reshape(MLA_KV_RANK, MLA_HEADS * MLA_NOPE).astype(BF16)
    wvt = wkv[:, :, MLA_NOPE:].reshape(MLA_KV_RANK, MLA_WIDTH).T.astype(BF16)
    qt, k, vt = _mla_up(hf, q_norm_g, kv_norm_g, wqt, wk, wvt, rope, seq, MLA_TILE)
    hf3 = hf.reshape(batch, seq, F_WIDTH)
    hb3 = hb.reshape(batch, seq, B_WIDTH)
    o_mla = _mla_attn(qt, k.reshape(batch, seq, -1), vt, hf3, MLA_TILE, 4)

    chunks = seq // CMP_STRIDE
    xk = hf3[:, :, F_KC:F_KC + NSA_DK].reshape(batch, chunks, CMP_STRIDE * NSA_DK)
    xv = hf3[:, :, F_VC:F_VC + NSA_DV].reshape(batch, chunks, CMP_STRIDE * NSA_DV)
    k_cmp, v_cmp = _compress(
        xk, xv,
        cmp_pe_k.reshape(2, CMP_STRIDE * NSA_DK), cmp_pe_v.reshape(2, CMP_STRIDE * NSA_DV),
        _pad_cols(cmp_w1k, HEAD_PAD).astype(BF16),
        jnp.pad(cmp_w2k, ((0, HEAD_PAD - NSA_DK), (0, HEAD_PAD - NSA_DK))).astype(BF16),
        cmp_w1v.astype(BF16), cmp_w2v.astype(BF16))
    o_nsa = _nsa_attn(hb3, hf3, k_cmp, v_cmp, nsa_tables, NSA_TQ, NSA_TK)

    mem_kv = _norm_proj(mem2, mem_norm_g, w_mem_kv.astype(BF16), jnp.ones((2 * MEM_WIDTH,), F32),
                        BF16, mem2.shape[0] // batch, MEM_WIDTH, "mem_kv_proj")
    o_mem = _mem_attn(hb3, mem_kv.reshape(batch, -1, 2 * MEM_WIDTH), hf3, 512)

    return _out_proj(x2, o_mla.reshape(-1, MLA_WIDTH), o_nsa.reshape(-1, NSA_WIDTH),
                     o_mem.reshape(-1, MEM_WIDTH), w_out.astype(BF16), final_g, final_norm, 512)


def kernel(x, mem, norm_g, w_in, q_norm_g, w_uq, kv_norm_g, w_ukv, cmp_pe_k, cmp_pe_v,
           cmp_w1k, cmp_w2k, cmp_w1v, cmp_w2v, mem_norm_g, w_mem_kv, w_out, final_norm_g):
    batch, seq, d = x.shape
    depth = norm_g.shape[0]
    tables = (_rope_tables(seq), _nsa_tables(seq, NSA_TQ, NSA_TK))
    x2 = x.reshape(batch * seq, d)
    mem2 = mem.reshape(batch * mem.shape[1], d)
    for l in range(depth):
        x2 = _layer(x2, mem2, batch, seq, tables, norm_g[l], w_in[l], q_norm_g[l], w_uq[l],
                    kv_norm_g[l], w_ukv[l], cmp_pe_k[l], cmp_pe_v[l], cmp_w1k[l], cmp_w2k[l],
                    cmp_w1v[l], cmp_w2v[l], mem_norm_g[l], w_mem_kv[l], w_out[l],
                    final_norm_g, l == depth - 1)
    return x2.reshape(batch, seq, d)
```

```python
import functools

import jax
import jax.numpy as jnp
from jax import lax
from jax.experimental import pallas as pl
from jax.experimental.pallas import tpu as pltpu

F32 = jnp.float32
BF16 = jnp.bfloat16

EPS = 1e-6
NEG_BIG = 1e9
MASK_VALUE = -1e30
SOFTMAX_EPS = 1e-20
LOG2E = 1.4426950408889634

MLA_HEADS = 8
MLA_NOPE = 128
MLA_ROPE = 64
MLA_V = 128
MLA_Q_RANK = 512
MLA_KV_RANK = 512
ROPE_THETA = 10000.0

NSA_HEADS = 4
NSA_DK = 192
NSA_DV = 128
CMP_LEN = 32
CMP_STRIDE = 16
SLC_LEN = 64
SLC_TOPN = 16
WIN = 512

MEM_HEADS = 4
MEM_DH = 128

MLA_WIDTH = MLA_HEADS * MLA_V
NSA_WIDTH = NSA_HEADS * NSA_DV
MEM_WIDTH = MEM_HEADS * MEM_DH

IN_SPLITS = (
    MLA_Q_RANK, MLA_KV_RANK, MLA_ROPE, MLA_WIDTH,
    NSA_HEADS * NSA_DK, NSA_DK, NSA_DV, NSA_DK, NSA_DV,
    NSA_DK, NSA_DV, 3 * NSA_HEADS, NSA_WIDTH,
    MEM_WIDTH, MEM_WIDTH,
)

LANES = 128
MXU_COLS = 256
HEAD_PAD = 256
VMEM_LIMIT = 56 * 1024 * 1024
MLA_TILE = 512
SUM_ROWS = 16
NSA_TQ = 256
NSA_TK = 512

F_CQ, F_CKV, F_Z, F_KC, F_VC, F_KROPE = 0, 512, 1024, 3072, 3328, 3456
F_ZNSA, F_ZMEM = F_Z + MLA_WIDTH, F_Z + MLA_WIDTH + NSA_WIDTH
F_GATE = F_KC + NSA_DK
F_WIDTH = 3584
B_QNSA, B_QMEM, B_KS, B_KW, B_VS, B_VW = 0, 1024, 1536, 1792, 2048, 2176
B_WIDTH = 2304

_NT = (((1,), (1,)), ((), ()))


def _params(*sem):
    return pltpu.CompilerParams(dimension_semantics=sem, vmem_limit_bytes=VMEM_LIMIT)


def _sigmoid(x):
    return 1.0 / (1.0 + jnp.exp(-x))


def _silu(x):
    return x * _sigmoid(x)


def _rmsnorm(x, g):
    ms = jnp.mean(x * x, axis=-1, keepdims=True)
    return (x * lax.rsqrt(ms + EPS)) * g


def _norm_proj_kernel(x_ref, g_ref, w_ref, cs_ref, o_ref, xn_ref):
    @pl.when(pl.program_id(1) == 0)
    def _():
        xn_ref[...] = _rmsnorm(x_ref[...], g_ref[...]).astype(BF16)

    acc = jnp.dot(xn_ref[...], w_ref[...], preferred_element_type=F32)
    o_ref[...] = (acc * cs_ref[...]).astype(o_ref.dtype)


def _norm_proj(x, g, w, colscale, out_dtype, tm, tn, name):
    m, k = x.shape
    n = w.shape[1]
    return pl.pallas_call(
        _norm_proj_kernel,
        grid=(m // tm, n // tn),
        in_specs=[
            pl.BlockSpec((tm, k), lambda i, j: (i, 0)),
            pl.BlockSpec((1, k), lambda i, j: (0, 0)),
            pl.BlockSpec((k, tn), lambda i, j: (0, j)),
            pl.BlockSpec((1, tn), lambda i, j: (0, j)),
        ],
        out_specs=pl.BlockSpec((tm, tn), lambda i, j: (i, j)),
        out_shape=jax.ShapeDtypeStruct((m, n), out_dtype),
        scratch_shapes=[pltpu.VMEM((tm, k), BF16)],
        compiler_params=_params("arbitrary", "arbitrary"),
        name=name,
    )(x, g.reshape(1, k), w, colscale.reshape(1, n))


def _rope_half(hi, cos2, sin2):
    up = pltpu.roll(hi, 32, axis=1)
    down = pltpu.roll(hi, 96, axis=1)
    return hi * cos2 + (up - down) * sin2


def _mla_up_kernel(cq_ref, ckv_ref, kr_ref, gq_ref, gkv_ref, wqt_ref, wk_ref, wvt_ref,
                   cos_ref, sin_ref, cost_ref, sint_ref, qt_ref, k_ref, vt_ref):
    scale = LOG2E * (MLA_NOPE + MLA_ROPE) ** -0.5
    half = MLA_ROPE // 2
    cos_t = cost_ref[...]
    sin_t = sint_ref[...]

    cqn = _rmsnorm(cq_ref[...], gq_ref[...]).astype(BF16)
    yt = lax.dot_general(wqt_ref[...], cqn, _NT, preferred_element_type=F32)
    for h in range(MLA_HEADS):
        r = h * HEAD_PAD
        x1 = yt[r + MLA_NOPE:r + MLA_NOPE + half]
        x2 = yt[r + MLA_NOPE + half:r + MLA_NOPE + MLA_ROPE]
        qt_ref[0, r:r + MLA_NOPE, :] = (yt[r:r + MLA_NOPE] * scale).astype(BF16)
        qt_ref[0, r + MLA_NOPE:r + MLA_NOPE + half, :] = (
            (x1 * cos_t - x2 * sin_t) * scale).astype(BF16)
        qt_ref[0, r + MLA_NOPE + half:r + MLA_NOPE + MLA_ROPE, :] = (
            (x1 * sin_t + x2 * cos_t) * scale).astype(BF16)
        qt_ref[0, r + MLA_NOPE + MLA_ROPE:r + HEAD_PAD, :] = jnp.zeros(
            (HEAD_PAD - MLA_NOPE - MLA_ROPE, yt.shape[1]), BF16)

    ckn = _rmsnorm(ckv_ref[...], gkv_ref[...]).astype(BF16)
    kn = jnp.dot(ckn, wk_ref[...], preferred_element_type=F32)
    k_pe = _rope_half(kr_ref[...], cos_ref[...], sin_ref[...]).astype(BF16)
    for h in range(MLA_HEADS):
        k_ref[:, h * HEAD_PAD:h * HEAD_PAD + LANES] = kn[:, h * LANES:(h + 1) * LANES].astype(BF16)
        k_ref[:, h * HEAD_PAD + LANES:(h + 1) * HEAD_PAD] = k_pe
    vt = lax.dot_general(wvt_ref[...], ckn, _NT, preferred_element_type=F32)
    ones = jnp.ones((SUM_ROWS, vt.shape[1]), BF16)
    for h in range(MLA_HEADS):
        r = h * (MLA_V + SUM_ROWS)
        vt_ref[0, r:r + MLA_V, :] = vt[h * MLA_V:(h + 1) * MLA_V].astype(BF16)
        vt_ref[0, r + MLA_V:r + MLA_V + SUM_ROWS, :] = ones


def _mla_up(hf, gq, gkv, wqt, wk, wvt, rope, seq, tm):
    cos2, sin2, cos_t, sin_t = rope
    m = hf.shape[0]
    steps_per_seq = seq // tm
    hq = MLA_HEADS * HEAD_PAD
    half = MLA_ROPE // 2
    const = lambda i: (0, 0)
    return pl.pallas_call(
        _mla_up_kernel,
        grid=(m // tm,),
        in_specs=[
            pl.BlockSpec((tm, MLA_Q_RANK), lambda i: (i, F_CQ // MLA_Q_RANK)),
            pl.BlockSpec((tm, MLA_KV_RANK), lambda i: (i, F_CKV // MLA_KV_RANK)),
            pl.BlockSpec((tm, LANES), lambda i: (i, F_KROPE // LANES)),
            pl.BlockSpec((1, MLA_Q_RANK), const),
            pl.BlockSpec((1, MLA_KV_RANK), const),
            pl.BlockSpec((hq, MLA_Q_RANK), const),
            pl.BlockSpec((MLA_KV_RANK, MLA_HEADS * MLA_NOPE), const),
            pl.BlockSpec((MLA_WIDTH, MLA_KV_RANK), const),
            pl.BlockSpec((tm, LANES), lambda i: (i % steps_per_seq, 0)),
            pl.BlockSpec((tm, LANES), lambda i: (i % steps_per_seq, 0)),
            pl.BlockSpec((half, tm), lambda i: (0, i % steps_per_seq)),
            pl.BlockSpec((half, tm), lambda i: (0, i % steps_per_seq)),
        ],
        out_specs=[
            pl.BlockSpec((1, hq, tm), lambda i: (i, 0, 0)),
            pl.BlockSpec((tm, hq), lambda i: (i, 0)),
            pl.BlockSpec((1, MLA_HEADS * (MLA_V + SUM_ROWS), tm), lambda i: (i, 0, 0)),
        ],
        out_shape=[
            jax.ShapeDtypeStruct((m // tm, hq, tm), BF16),
            jax.ShapeDtypeStruct((m, hq), BF16),
            jax.ShapeDtypeStruct((m // tm, MLA_HEADS * (MLA_V + SUM_ROWS), tm), BF16),
        ],
        compiler_params=_params("arbitrary"),
        name="mla_up",
    )(hf, hf, hf, gq.reshape(1, -1), gkv.reshape(1, -1), wqt, wk, wvt, cos2, sin2, cos_t, sin_t)


def _mla_attn_kernel(qt_ref, k_ref, vt_ref, z_ref, o_ref, m_ref, acc_ref, s0_ref, *,
                     tile_len, hps):
    t = tile_len
    va = MLA_V + SUM_ROWS
    qi = pl.program_id(2)
    qts = [qt_ref[0, h * HEAD_PAD:(h + 1) * HEAD_PAD, :] for h in range(hps)]

    def scores(j, h):
        kb = pl.multiple_of(j * t, t)
        return jnp.dot(k_ref[0, pl.ds(kb, t), h * HEAD_PAD:(h + 1) * HEAD_PAD], qts[h],
                       preferred_element_type=F32)

    def softmax(h, s, diagonal):
        if diagonal:
            kpos = lax.broadcasted_iota(jnp.int32, (t, t), 0)
            qpos = lax.broadcasted_iota(jnp.int32, (t, t), 1)
            visible = kpos <= qpos
            s = jnp.where(visible, s, MASK_VALUE)
        m = m_ref[h]
        m_new = jnp.maximum(m, jnp.max(s, axis=0, keepdims=True))
        alpha = jnp.exp2(m - m_new)
        p = jnp.exp2(s - m_new)
        if diagonal:
            p = jnp.where(visible, p, 0.0)
        m_ref[h] = m_new
        return alpha, p.astype(BF16)

    def accumulate(j, h, alpha, p):
        vt = vt_ref[j, h * va:(h + 1) * va, :]
        acc_ref[h] = alpha * acc_ref[h] + jnp.dot(vt, p, preferred_element_type=F32)

    def tile(j, diagonal):
        alpha, p = softmax(0, s0_ref[...], diagonal)
        for h in range(1, hps):
            s = scores(j, h)
            accumulate(j, h - 1, alpha, p)
            alpha, p = softmax(h, s, diagonal)
        if not diagonal:
            s0_ref[...] = scores(j + 1, 0)
        accumulate(j, hps - 1, alpha, p)

    for h in range(hps):
        m_ref[h] = jnp.full((1, t), MASK_VALUE, F32)
        acc_ref[h] = jnp.zeros((va, t), F32)
    s0_ref[...] = scores(0, 0)

    def body(j, carry):
        tile(j, False)
        return carry

    lax.fori_loop(0, qi, body, 0)

    tile(qi, True)
    for h in range(hps):
        l = acc_ref[h, MLA_V:MLA_V + 1, :]
        o = (acc_ref[h, 0:MLA_V, :] / (l + SOFTMAX_EPS)).T
        cols = slice(h * MLA_V, (h + 1) * MLA_V)
        o_ref[0, :, cols] = (o * _silu(z_ref[0, :, cols])).astype(BF16)


def _mla_attn(qt, k, vt, hf, tile_len, hps):
    b, s, _ = k.shape
    t = tile_len
    tiles = s // t
    kern = functools.partial(_mla_attn_kernel, tile_len=t, hps=hps)
    qw, vw, va = hps * HEAD_PAD, hps * MLA_V, MLA_V + SUM_ROWS
    return pl.pallas_call(
        kern,
        grid=(b, MLA_HEADS // hps, tiles),
        in_specs=[
            pl.BlockSpec((1, qw, t), lambda bi, h, i: (bi * tiles + i, h, 0)),
            pl.BlockSpec((1, s, qw), lambda bi, h, i: (bi, 0, h)),
            pl.BlockSpec((tiles, hps * va, t), lambda bi, h, i: (bi, h, 0)),
            pl.BlockSpec((1, t, vw), lambda bi, h, i: (bi, i, F_Z // vw + h)),
        ],
        out_specs=pl.BlockSpec((1, t, vw), lambda bi, h, i: (bi, i, h)),
        out_shape=jax.ShapeDtypeStruct((b, s, MLA_WIDTH), BF16),
        scratch_shapes=[
            pltpu.VMEM((hps, 1, t), F32),
            pltpu.VMEM((hps, va, t), F32),
            pltpu.VMEM((t, t), F32),
        ],
        compiler_params=_params("arbitrary", "arbitrary", "arbitrary"),
        name="mla_attn",
    )(qt, k, vt, hf)


def _compress_one(x_refs, pe_ref, w1_ref, w2_ref, o_ref):
    chunks = x_refs[0].shape[1] // CMP_STRIDE
    a = b = None
    for i in range(CMP_STRIDE):
        for c, x_ref in enumerate(x_refs):
            lanes = slice(c * LANES, (c + 1) * LANES)
            x = x_ref[0, pl.ds(i, chunks, stride=CMP_STRIDE), :]
            ai = jnp.dot((x + pe_ref[i:i + 1, lanes]).astype(BF16), w1_ref[i, lanes, :],
                         preferred_element_type=F32)
            bi = jnp.dot((x + pe_ref[CMP_STRIDE + i:CMP_STRIDE + i + 1, lanes]).astype(BF16),
                         w1_ref[CMP_STRIDE + i, lanes, :], preferred_element_type=F32)
            a = ai if a is None else a + ai
            b = bi if b is None else b + bi
    h1 = a + pltpu.roll(b, chunks - 1, axis=0)
    o_ref[0] = jnp.dot(_silu(h1).astype(BF16), w2_ref[...], preferred_element_type=F32).astype(BF16)


def _compress_kernel(xk0_ref, xk1_ref, xv_ref, pek_ref, pev_ref, w1k_ref, w2k_ref, w1v_ref, w2v_ref,
                     ok_ref, ov_ref):
    _compress_one((xk0_ref, xk1_ref), pek_ref, w1k_ref, w2k_ref, ok_ref)
    _compress_one((xv_ref,), pev_ref, w1v_ref, w2v_ref, ov_ref)


def _compress(hf, pek, pev, w1k, w2k, w1v, w2v):
    b, s, _ = hf.shape
    chunks = s // CMP_STRIDE
    full = lambda a: pl.BlockSpec(a.shape, lambda bi: (0,) * a.ndim)
    return pl.pallas_call(
        _compress_kernel,
        grid=(b,),
        in_specs=[pl.BlockSpec((1, s, LANES), lambda bi: (bi, 0, F_KC // LANES)),
                  pl.BlockSpec((1, s, LANES), lambda bi: (bi, 0, F_KC // LANES + 1)),
                  pl.BlockSpec((1, s, NSA_DV), lambda bi: (bi, 0, F_VC // NSA_DV)),
                  full(pek), full(pev), full(w1k), full(w2k), full(w1v), full(w2v)],
        out_specs=[
            pl.BlockSpec((1, chunks, HEAD_PAD), lambda bi: (bi, 0, 0)),
            pl.BlockSpec((1, chunks, NSA_DV), lambda bi: (bi, 0, 0)),
        ],
        out_shape=[
            jax.ShapeDtypeStruct((b, chunks, HEAD_PAD), BF16),
            jax.ShapeDtypeStruct((b, chunks, NSA_DV), BF16),
        ],
        compiler_params=_params("arbitrary"),
        name="nsa_compress",
    )(hf, hf, hf, pek, pev, w1k, w2k, w1v, w2v)


def _alibi_slope(h):
    return 2.0 ** (-8.0 * (h + 1) / NSA_HEADS)


def _nsa_attn_kernel(q_ref, g_ref, z_ref, kc_ref, vc_ref, ks_ref, vs_ref, kw_ref, vw_ref,
                     ovt_ref, oh_ref, tabs_ref, tabw_ref, slope_ref, o_ref,
                     kaug_ref, kwp_ref, vst_ref, vwt_ref, vct_ref, m_ref, acc_ref, s0_ref,
                     *, tq, tk, n_c, n_s, top_n):
    qi = pl.program_id(1)
    t0 = qi * tq
    heads = NSA_HEADS
    seq = ks_ref.shape[1]
    ncp = kc_ref.shape[1]
    hw = 2 * tq
    aug = HEAD_PAD - NSA_DK
    wlen = WIN + tq
    va = NSA_DV + SUM_ROWS

    @pl.when(qi == 0)
    def _():
        kaug_ref[...] = ks_ref[0] + oh_ref[...]
        col = lax.broadcasted_iota(jnp.int32, (WIN, HEAD_PAD), 1)
        kwp_ref[0:WIN, :] = jnp.where(col == NSA_DK, 1.0, 0.0).astype(BF16)
        kwp_ref[WIN:, :] = kw_ref[0]
        for i in range(seq // tk):
            vst_ref[i, 0:NSA_DV, :] = vs_ref[0, i * tk:(i + 1) * tk, :].astype(F32).T.astype(BF16)
            vst_ref[i, NSA_DV:va, :] = jnp.ones((SUM_ROWS, tk), BF16)
        for i in range(WIN // LANES):
            vwt_ref[i] = jnp.zeros((va, LANES), BF16)
        for i in range(seq // LANES):
            vwt_ref[WIN // LANES + i, 0:NSA_DV, :] = (
                vw_ref[0, i * LANES:(i + 1) * LANES, :].astype(F32).T.astype(BF16))
            vwt_ref[WIN // LANES + i, NSA_DV:va, :] = jnp.ones((SUM_ROWS, LANES), BF16)
        vct_ref[...] = vc_ref[0].astype(F32).T.astype(BF16)

    q_all = q_ref[0]
    qt = [q_all[:, h * HEAD_PAD:(h + 1) * HEAD_PAD].astype(F32).T[0:NSA_DK] for h in range(heads)]

    def stacked_qt(extra):
        cols = [jnp.concatenate([qt[h], extra], axis=0) for h in range(heads)]
        return [jnp.concatenate(cols[2 * x:2 * x + 2], axis=1).astype(BF16) for x in range(2)]

    flag_rows = jnp.where(lax.broadcasted_iota(jnp.int32, (aug, tq), 0) == 0, MASK_VALUE, 0.0)
    qt_win = stacked_qt(flag_rows)

    n_i = lax.broadcasted_iota(jnp.int32, (ncp, tq), 0)
    t_c = t0 + lax.broadcasted_iota(jnp.int32, (ncp, tq), 1)
    mask_c = (n_i * CMP_STRIDE + (CMP_LEN - 1) <= t_c) & (n_i < n_c)
    dist_c = (t_c.astype(F32) - (n_i.astype(F32) * CMP_STRIDE + (CMP_LEN - 1) / 2.0)) * LOG2E
    kc = kc_ref[0]
    o_cmp, p_sum = [], None
    for x in range(2):
        s_c = jnp.dot(kc, qt_win[x], preferred_element_type=F32)
        p_x = []
        for hl in range(2):
            s = s_c[:, hl * tq:(hl + 1) * tq] - _alibi_slope(2 * x + hl) * dist_c
            s = jnp.where(mask_c, s, MASK_VALUE)
            p = jnp.where(mask_c, jnp.exp2(s - jnp.max(s, axis=0, keepdims=True)), 0.0)
            p = p / (jnp.sum(p, axis=0, keepdims=True) + SOFTMAX_EPS)
            p_sum = p if p_sum is None else p_sum + p
            p_x.append(p.astype(BF16))
        o_cmp.append(jnp.dot(vct_ref[...], jnp.concatenate(p_x, axis=1),
                             preferred_element_type=F32))
    p_hi = p_sum.astype(BF16)
    p_lo = (p_sum - p_hi.astype(F32)).astype(BF16)
    ovt = ovt_ref[...]
    imp = (jnp.dot(ovt, p_hi, preferred_element_type=F32)
           + jnp.dot(ovt, p_lo, preferred_element_type=F32))

    kwb = kwp_ref[pl.ds(pl.multiple_of(t0, LANES), wlen), :]
    tile0 = qi * (tq // LANES)
    vwb = jnp.concatenate([vwt_ref[tile0 + r] for r in range(wlen // LANES)], axis=1)
    o_win = []
    for x in range(2):
        s = jnp.dot(kwb, qt_win[x], preferred_element_type=F32) + tabw_ref[:, x * hw:(x + 1) * hw]
        p = jnp.exp2(s - jnp.max(s, axis=0, keepdims=True))
        pv = jnp.dot(vwb, p.astype(BF16), preferred_element_type=F32)
        o_win.append(pv[0:NSA_DV] / (pv[NSA_DV:NSA_DV + 1] + SOFTMAX_EPS))

    blk = lax.broadcasted_iota(jnp.int32, (n_s, tq), 0)
    cur = (t0 + lax.broadcasted_iota(jnp.int32, (n_s, tq), 1)) // SLC_LEN
    forced = (blk == 0) | (blk == cur) | (blk == cur - 1)
    imp = jnp.where(forced, NEG_BIG, imp)
    imp = jnp.where(blk > cur, -NEG_BIG, imp)
    sub = lax.broadcasted_iota(jnp.int32, (8, tq), 0)
    groups = [imp[8 * g:8 * g + 8] for g in range(n_s // 8)]
    ranks = [jnp.zeros((8, tq), F32) for _ in groups]
    for jp in range(n_s):
        row = imp[jp:jp + 1, :]
        for g, grp in enumerate(groups):
            ge = jnp.where(row >= grp, 1.0, 0.0)
            gt = jnp.where(row > grp, 1.0, 0.0)
            if 8 * g > jp:
                beats = ge
            elif 8 * g + 8 <= jp:
                beats = gt
            else:
                beats = jnp.where(sub + 8 * g > jp, ge, gt)
            ranks[g] = ranks[g] + beats
    sel_rows = jnp.where(jnp.concatenate(ranks, axis=0) < top_n, 0.0, MASK_VALUE)
    if n_s < aug:
        sel_rows = jnp.concatenate([sel_rows, jnp.zeros((aug - n_s, tq), F32)], axis=0)
    qt_sel = stacked_qt(sel_rows)

    jd = t0 // tk

    def scores(j, x):
        kb = j * tk if isinstance(j, int) else pl.multiple_of(j * tk, tk)
        return jnp.dot(kaug_ref[pl.ds(kb, tk), :], qt_sel[x], preferred_element_type=F32)

    def softmax(j, x, s, diagonal):
        u = s + tabs_ref[:, x * hw:(x + 1) * hw]
        if diagonal:
            r_minus_q = (lax.broadcasted_iota(jnp.int32, (tk, hw), 0)
                         - (lax.broadcasted_iota(jnp.int32, (tk, hw), 1) & (tq - 1)))
            u = jnp.where(r_minus_q <= t0 - j * tk, u, MASK_VALUE)
        off = slope_ref[0:1, x * hw:(x + 1) * hw] * (j * tk - t0).astype(F32)
        m_old = m_ref[x]
        m_new = jnp.maximum(m_old, jnp.max(u, axis=0, keepdims=True) + off)
        alpha = jnp.exp2(m_old - m_new)
        p = jnp.exp2(u - (m_new - off))
        m_ref[x] = m_new
        return alpha, p.astype(BF16)

    def accumulate(j, x, alpha, p):
        acc_ref[x] = alpha * acc_ref[x] + jnp.dot(vst_ref[j], p, preferred_element_type=F32)

    def tile(j, diagonal):
        alpha, p = softmax(j, 0, s0_ref[...], diagonal)
        s1 = scores(j, 1)
        accumulate(j, 0, alpha, p)
        alpha, p = softmax(j, 1, s1, diagonal)
        if not diagonal:
            s0_ref[...] = scores(j + 1, 0)
        accumulate(j, 1, alpha, p)

    for x in range(2):
        m_ref[x] = jnp.full((1, hw), MASK_VALUE, F32)
        acc_ref[x] = jnp.zeros((va, hw), F32)
    s0_ref[...] = scores(0, 0)

    def body(j, carry):
        tile(j, False)
        return carry

    lax.fori_loop(0, jd, body, 0)
    tile(jd, True)
    o_slc = [acc_ref[x, 0:NSA_DV, :] / (acc_ref[x, NSA_DV:NSA_DV + 1, :] + SOFTMAX_EPS)
             for x in range(2)]

    gates = _sigmoid(g_ref[0]).T
    z = z_ref[0]
    for h in range(heads):
        x, lanes = h // 2, slice((h % 2) * tq, (h % 2 + 1) * tq)
        c = NSA_DK + 3 * h
        o = (gates[c:c + 1] * o_cmp[x][:, lanes] + gates[c + 1:c + 2] * o_slc[x][:, lanes]
             + gates[c + 2:c + 3] * o_win[x][:, lanes])
        zh = z[:, h * NSA_DV:(h + 1) * NSA_DV]
        o_ref[0, :, h * NSA_DV:(h + 1) * NSA_DV] = (o.T * _silu(zh)).astype(BF16)


def _nsa_attn(hb, hf, k_cmp, v_cmp, tables, tq, tk):
    ovt, onehot, tab_sel, tab_win, slope_rows = tables
    b, s, _ = hb.shape
    ncp = k_cmp.shape[1]
    n_s = s // SLC_LEN
    assert n_s <= HEAD_PAD - NSA_DK and n_s % 8 == 0, "selection blocks must fit the spare rows"
    assert tq & (tq - 1) == 0 and tq % LANES == 0 and tk % tq == 0 and s % tk == 0
    kern = functools.partial(_nsa_attn_kernel, tq=tq, tk=tk, n_c=ncp - 1, n_s=n_s,
                             top_n=min(SLC_TOPN, n_s))
    qw = NSA_HEADS * HEAD_PAD
    const2 = lambda a: pl.BlockSpec(a.shape, lambda bi, i: (0, 0))
    return pl.pallas_call(
        kern,
        grid=(b, s // tq),
        in_specs=[
            pl.BlockSpec((1, tq, qw), lambda bi, i: (bi, i, B_QNSA // qw)),
            pl.BlockSpec((1, tq, HEAD_PAD), lambda bi, i: (bi, i, F_KC // HEAD_PAD)),
            pl.BlockSpec((1, tq, NSA_WIDTH), lambda bi, i: (bi, i, F_ZNSA // NSA_WIDTH)),
            pl.BlockSpec((1, ncp, HEAD_PAD), lambda bi, i: (bi, 0, 0)),
            pl.BlockSpec((1, ncp, NSA_DV), lambda bi, i: (bi, 0, 0)),
            pl.BlockSpec((1, s, HEAD_PAD), lambda bi, i: (bi, 0, B_KS // HEAD_PAD)),
            pl.BlockSpec((1, s, NSA_DV), lambda bi, i: (bi, 0, B_VS // NSA_DV)),
            pl.BlockSpec((1, s, HEAD_PAD), lambda bi, i: (bi, 0, B_KW // HEAD_PAD)),
            pl.BlockSpec((1, s, NSA_DV), lambda bi, i: (bi, 0, B_VW // NSA_DV)),
            const2(ovt), const2(onehot), const2(tab_sel), const2(tab_win), const2(slope_rows),
        ],
        out_specs=pl.BlockSpec((1, tq, NSA_WIDTH), lambda bi, i: (bi, i, 0)),
        out_shape=jax.ShapeDtypeStruct((b, s, NSA_WIDTH), BF16),
        scratch_shapes=[
            pltpu.VMEM((s, HEAD_PAD), BF16),
            pltpu.VMEM((s + WIN, HEAD_PAD), BF16),
            pltpu.VMEM((s // tk, NSA_DV + SUM_ROWS, tk), BF16),
            pltpu.VMEM(((s + WIN) // LANES, NSA_DV + SUM_ROWS, LANES), BF16),
            pltpu.VMEM((NSA_DV, ncp), BF16),
            pltpu.VMEM((2, 1, 2 * tq), F32),
            pltpu.VMEM((2, NSA_DV + SUM_ROWS, 2 * tq), F32),
            pltpu.VMEM((tk, 2 * tq), F32),
        ],
        compiler_params=_params("arbitrary", "arbitrary"),
        name="nsa_attn",
    )(hb, hf, hf, k_cmp, v_cmp, hb, hb, hb, hb, ovt, onehot, tab_sel, tab_win, slope_rows)


def _mem_attn_kernel(q_ref, k_ref, v_ref, z_ref, o_ref):
    q = q_ref[0]
    k = k_ref[0]
    v = v_ref[0]
    z = z_ref[0]
    for h in range(MEM_HEADS):
        sl = slice(h * MEM_DH, (h + 1) * MEM_DH)
        s = lax.dot_general(q[:, sl], k[:, sl], _NT, preferred_element_type=F32)
        p = jnp.exp(s - jnp.max(s, axis=-1, keepdims=True))
        o = jnp.dot(p.astype(BF16), v[:, sl], preferred_element_type=F32)
        o = o / jnp.sum(p, axis=-1, keepdims=True)
        o_ref[0, :, sl] = (o * _silu(z[:, sl])).astype(BF16)


def _mem_attn(hb, mem_kv, hf, tq):
    b, s, _ = hb.shape
    mlen = mem_kv.shape[1]
    return pl.pallas_call(
        _mem_attn_kernel,
        grid=(b, s // tq),
        in_specs=[
            pl.BlockSpec((1, tq, MEM_WIDTH), lambda bi, i: (bi, i, B_QMEM // MEM_WIDTH)),
            pl.BlockSpec((1, mlen, MEM_WIDTH), lambda bi, i: (bi, 0, 0)),
            pl.BlockSpec((1, mlen, MEM_WIDTH), lambda bi, i: (bi, 0, 1)),
            pl.BlockSpec((1, tq, MEM_WIDTH), lambda bi, i: (bi, i, F_ZMEM // MEM_WIDTH)),
        ],
        out_specs=pl.BlockSpec((1, tq, MEM_WIDTH), lambda bi, i: (bi, i, 0)),
        out_shape=jax.ShapeDtypeStruct((b, s, MEM_WIDTH), BF16),
        compiler_params=_params("arbitrary", "arbitrary"),
        name="mem_attn",
    )(hb, mem_kv, mem_kv, hf)


def _out_proj_kernel(x_ref, a_ref, n_ref, m_ref, w_ref, g_ref, o_ref, *, final_norm):
    y = x_ref[...]
    y = y + jnp.dot(a_ref[...], w_ref[0:MLA_WIDTH, :], preferred_element_type=F32)
    y = y + jnp.dot(n_ref[...], w_ref[MLA_WIDTH:MLA_WIDTH + NSA_WIDTH, :], preferred_element_type=F32)
    y = y + jnp.dot(m_ref[...], w_ref[MLA_WIDTH + NSA_WIDTH:, :], preferred_element_type=F32)
    if final_norm:
        y = _rmsnorm(y, g_ref[...])
    o_ref[...] = y


def _out_proj(x, o_mla, o_nsa, o_mem, w_out, g, final_norm, tm):
    m, d = x.shape
    kern = functools.partial(_out_proj_kernel, final_norm=final_norm)
    row = lambda width: pl.BlockSpec((tm, width), lambda i: (i, 0))
    return pl.pallas_call(
        kern,
        grid=(m // tm,),
        in_specs=[row(d), row(MLA_WIDTH), row(NSA_WIDTH), row(MEM_WIDTH),
                  pl.BlockSpec(w_out.shape, lambda i: (0, 0)),
                  pl.BlockSpec((1, d), lambda i: (0, 0))],
        out_specs=row(d),
        out_shape=jax.ShapeDtypeStruct((m, d), F32),
        compiler_params=_params("arbitrary"),
        name="out_proj",
    )(x, o_mla, o_nsa, o_mem, w_out, g.reshape(1, d))


def _pad_cols(w, width):
    return jnp.pad(w, ((0, 0), (0, width - w.shape[1])))


def _w_in_moves():
    names = ("c_q", "c_kv", "k_rope", "z_mla", "q_nsa", "k_c", "v_c", "k_s", "v_s", "k_w", "v_w",
             "g_nsa", "z_nsa", "q_mem", "z_mem")
    src, off = {}, 0
    for name, n in zip(names, IN_SPLITS):
        src[name] = (off, n)
        off += n
    dst = {"c_q": (0, F_CQ), "c_kv": (0, F_CKV), "z_mla": (0, F_Z), "z_nsa": (0, F_ZNSA),
           "z_mem": (0, F_ZMEM), "k_c": (0, F_KC), "g_nsa": (0, F_GATE), "v_c": (0, F_VC),
           "k_rope": (0, F_KROPE), "q_mem": (1, B_QMEM), "k_s": (1, B_KS), "k_w": (1, B_KW),
           "v_s": (1, B_VS), "v_w": (1, B_VW)}
    moves = [(src[n][0], src[n][1], dst[n][0], dst[n][1]) for n in dst]
    for h in range(NSA_HEADS):
        moves.append((src["q_nsa"][0] + h * NSA_DK, NSA_DK, 1, B_QNSA + h * HEAD_PAD))
    return moves


def _w_prep_kernel(w_ref, wf_ref, wb_ref):
    outs = (wf_ref, wb_ref)
    wf_ref[...] = jnp.zeros(wf_ref.shape, BF16)
    wb_ref[...] = jnp.zeros(wb_ref.shape, BF16)
    for s0, width, which, d0 in _w_in_moves():
        outs[which][:, d0:d0 + width] = w_ref[:, s0:s0 + width].astype(BF16)


def _layout_w_in(w_in, tr=256):
    d, n = w_in.shape
    wf, wb = pl.pallas_call(
        _w_prep_kernel,
        grid=(d // tr,),
        in_specs=[pl.BlockSpec((tr, n), lambda i: (i, 0))],
        out_specs=[pl.BlockSpec((tr, F_WIDTH), lambda i: (i, 0)),
                   pl.BlockSpec((tr, B_WIDTH), lambda i: (i, 0))],
        out_shape=[jax.ShapeDtypeStruct((d, F_WIDTH), BF16), jax.ShapeDtypeStruct((d, B_WIDTH), BF16)],
        compiler_params=_params("arbitrary"),
        name="w_in_layout",
    )(w_in)
    scale_b = jnp.concatenate([
        jnp.full((NSA_HEADS * HEAD_PAD,), LOG2E * NSA_DK ** -0.5, F32),
        jnp.full((MEM_WIDTH,), MEM_DH ** -0.5, F32),
        jnp.ones((B_WIDTH - B_QMEM - MEM_WIDTH,), F32)])
    return wf, wb, scale_b


def _rope_tables(seq):
    pos = jnp.arange(seq, dtype=F32)
    inv_freq = ROPE_THETA ** (-jnp.arange(0, MLA_ROPE, 2, dtype=F32) / MLA_ROPE)
    ang = pos[:, None] * inv_freq[None, :]
    zeros = jnp.zeros((seq, LANES - MLA_ROPE), F32)
    cos2 = jnp.concatenate([jnp.cos(ang), jnp.cos(ang), zeros], axis=1)
    sin2 = jnp.concatenate([jnp.sin(ang), jnp.sin(ang), zeros], axis=1)
    return cos2, sin2, jnp.cos(ang).T, jnp.sin(ang).T


def _nsa_tables(seq, tq, tk):
    chunks = seq // CMP_STRIDE
    n_s = seq // SLC_LEN
    c_start = jnp.arange(chunks) * CMP_STRIDE
    s_start = jnp.arange(n_s) * SLC_LEN
    overlap_t = ((c_start[None, :] < s_start[:, None] + SLC_LEN)
                 & (c_start[None, :] + CMP_LEN > s_start[:, None])
                 & (jnp.arange(chunks)[None, :] < chunks - 1))
    key_block = jnp.arange(seq) // SLC_LEN
    onehot = jnp.arange(HEAD_PAD)[None, :] == (NSA_DK + key_block)[:, None]
    slope = jnp.repeat(jnp.array([_alibi_slope(h) for h in range(NSA_HEADS)], F32) * LOG2E, tq)
    q_lane = jnp.tile(jnp.arange(tq), NSA_HEADS)
    tab_sel = jnp.arange(tk, dtype=F32)[:, None] * slope[None, :]
    rel = q_lane[None, :] - jnp.arange(WIN + tq)[:, None] + WIN
    tab_win = jnp.where((rel >= 0) & (rel < WIN), -slope[None, :] * rel.astype(F32), MASK_VALUE)
    slope_rows = jnp.broadcast_to(slope[None, :], (8, NSA_HEADS * tq))
    return overlap_t.astype(BF16), onehot.astype(BF16), tab_sel, tab_win, slope_rows


def _layer(x2, mem2, batch, seq, tables, norm_g, w_in, q_norm_g, w_uq, kv_norm_g, w_ukv,
           cmp_pe_k, cmp_pe_v, cmp_w1k, cmp_w2k, cmp_w1v, cmp_w2v, mem_norm_g, w_mem_kv, w_out,
           final_g, final_norm):
    rope, nsa_tables = tables
    d = x2.shape[1]
    wf, wb, scale_b = _layout_w_in(w_in)
    hf = _norm_proj(x2, norm_g, wf, jnp.ones((F_WIDTH,), F32), F32, 1024, 512, "in_proj_f32")
    hb = _norm_proj(x2, norm_g, wb, scale_b, BF16, 1024, 768, "in_proj_bf16")

    wq = jnp.pad(w_uq.reshape(MLA_Q_RANK, MLA_HEADS, MLA_NOPE + MLA_ROPE),
                 ((0, 0), (0, 0), (0, HEAD_PAD - MLA_NOPE - MLA_ROPE)))
    wqt = wq.reshape(MLA_Q_RANK, MLA_HEADS * HEAD_PAD).T.astype(BF16)
    wkv = w_ukv.reshape(MLA_KV_RANK, MLA_HEADS, MLA_NOPE + MLA_V)
    wk = wkv[:, :, :MLA_NOPE].reshape(MLA_KV_RANK, MLA_HEADS * MLA_NOPE).astype(BF16)
    wvt = wkv[:, :, MLA_NOPE:].reshape(MLA_KV_RANK, MLA_WIDTH).T.astype(BF16)
    qt, k, vt = _mla_up(hf, q_norm_g, kv_norm_g, wqt, wk, wvt, rope, seq, MLA_TILE)
    hf3 = hf.reshape(batch, seq, F_WIDTH)
    hb3 = hb.reshape(batch, seq, B_WIDTH)
    o_mla = _mla_attn(qt, k.reshape(batch, seq, -1), vt, hf3, MLA_TILE, 4)

    pad_k = HEAD_PAD - NSA_DK
    k_cmp, v_cmp = _compress(
        hf3, _pad_cols(cmp_pe_k, HEAD_PAD), cmp_pe_v,
        jnp.pad(cmp_w1k.reshape(CMP_LEN, NSA_DK, NSA_DK), ((0, 0), (0, pad_k), (0, pad_k))).astype(BF16),
        jnp.pad(cmp_w2k, ((0, pad_k), (0, pad_k))).astype(BF16),
        cmp_w1v.reshape(CMP_LEN, NSA_DV, NSA_DV).astype(BF16), cmp_w2v.astype(BF16))
    o_nsa = _nsa_attn(hb3, hf3, k_cmp, v_cmp, nsa_tables, NSA_TQ, NSA_TK)

    mem_kv = _norm_proj(mem2, mem_norm_g, w_mem_kv.astype(BF16), jnp.ones((2 * MEM_WIDTH,), F32),
                        BF16, mem2.shape[0] // batch, MEM_WIDTH, "mem_kv_proj")
    o_mem = _mem_attn(hb3, mem_kv.reshape(batch, -1, 2 * MEM_WIDTH), hf3, 512)

    return _out_proj(x2, o_mla.reshape(-1, MLA_WIDTH), o_nsa.reshape(-1, NSA_WIDTH),
                     o_mem.reshape(-1, MEM_WIDTH), w_out.astype(BF16), final_g, final_norm, 512)


def kernel(x, mem, norm_g, w_in, q_norm_g, w_uq, kv_norm_g, w_ukv, cmp_pe_k, cmp_pe_v,
           cmp_w1k, cmp_w2k, cmp_w1v, cmp_w2v, mem_norm_g, w_mem_kv, w_out, final_norm_g):
    batch, seq, d = x.shape
    depth = norm_g.shape[0]
    tables = (_rope_tables(seq), _nsa_tables(seq, NSA_TQ, NSA_TK))
    x2 = x.reshape(batch * seq, d)
    mem2 = mem.reshape(batch * mem.shape[1], d)
    for l in range(depth):
        x2 = _layer(x2, mem2, batch, seq, tables, norm_g[l], w_in[l], q_norm_g[l], w_uq[l],
                    kv_norm_g[l], w_ukv[l], cmp_pe_k[l], cmp_pe_v[l], cmp_w1k[l], cmp_w2k[l],
                    cmp_w1v[l], cmp_w2v[l], mem_norm_g[l], w_mem_kv[l], w_out[l],
                    final_norm_g, l == depth - 1)
    return x2.reshape(batch, seq, d)
```

```python
import functools

import jax
import jax.numpy as jnp
from jax import lax
from jax.experimental import pallas as pl
from jax.experimental.pallas import tpu as pltpu

F32 = jnp.float32
BF16 = jnp.bfloat16

EPS = 1e-6
NEG_BIG = 1e9
MASK_VALUE = -1e30
SOFTMAX_EPS = 1e-20
LOG2E = 1.4426950408889634

MLA_HEADS = 8
MLA_NOPE = 128
MLA_ROPE = 64
MLA_V = 128
MLA_Q_RANK = 512
MLA_KV_RANK = 512
ROPE_THETA = 10000.0

NSA_HEADS = 4
NSA_DK = 192
NSA_DV = 128
CMP_LEN = 32
CMP_STRIDE = 16
SLC_LEN = 64
SLC_TOPN = 16
WIN = 512

MEM_HEADS = 4
MEM_DH = 128

MLA_WIDTH = MLA_HEADS * MLA_V
NSA_WIDTH = NSA_HEADS * NSA_DV
MEM_WIDTH = MEM_HEADS * MEM_DH

IN_SPLITS = (
    MLA_Q_RANK, MLA_KV_RANK, MLA_ROPE, MLA_WIDTH,
    NSA_HEADS * NSA_DK, NSA_DK, NSA_DV, NSA_DK, NSA_DV,
    NSA_DK, NSA_DV, 3 * NSA_HEADS, NSA_WIDTH,
    MEM_WIDTH, MEM_WIDTH,
)

LANES = 128
MXU_COLS = 256
HEAD_PAD = 256
VMEM_LIMIT = 56 * 1024 * 1024
MLA_TILE = 512
SUM_ROWS = 16
NSA_TQ = 256
NSA_TK = 512

F_CQ, F_CKV, F_Z, F_KC, F_VC, F_KROPE = 0, 512, 1024, 3072, 3328, 3456
F_ZNSA, F_ZMEM = F_Z + MLA_WIDTH, F_Z + MLA_WIDTH + NSA_WIDTH
F_GATE = F_KC + NSA_DK
F_WIDTH = 3584
B_QNSA, B_QMEM, B_KS, B_KW, B_VS, B_VW = 0, 1024, 1536, 1792, 2048, 2176
B_WIDTH = 2304

_NT = (((1,), (1,)), ((), ()))


def _params(*sem):
    return pltpu.CompilerParams(dimension_semantics=sem, vmem_limit_bytes=VMEM_LIMIT)


def _sigmoid(x):
    return 1.0 / (1.0 + jnp.exp(-x))


def _silu(x):
    return x * _sigmoid(x)


def _rmsnorm(x, g):
    ms = jnp.mean(x * x, axis=-1, keepdims=True)
    return (x * lax.rsqrt(ms + EPS)) * g


def _norm_proj_kernel(x_ref, g_ref, w_ref, cs_ref, o_ref, xn_ref):
    @pl.when(pl.program_id(1) == 0)
    def _():
        xn_ref[...] = _rmsnorm(x_ref[...], g_ref[...]).astype(BF16)

    acc = jnp.dot(xn_ref[...], w_ref[...], preferred_element_type=F32)
    o_ref[...] = (acc * cs_ref[...]).astype(o_ref.dtype)


def _norm_proj(x, g, w, colscale, out_dtype, tm, tn, name):
    m, k = x.shape
    n = w.shape[1]
    return pl.pallas_call(
        _norm_proj_kernel,
        grid=(m // tm, n // tn),
        in_specs=[
            pl.BlockSpec((tm, k), lambda i, j: (i, 0)),
            pl.BlockSpec((1, k), lambda i, j: (0, 0)),
            pl.BlockSpec((k, tn), lambda i, j: (0, j)),
            pl.BlockSpec((1, tn), lambda i, j: (0, j)),
        ],
        out_specs=pl.BlockSpec((tm, tn), lambda i, j: (i, j)),
        out_shape=jax.ShapeDtypeStruct((m, n), out_dtype),
        scratch_shapes=[pltpu.VMEM((tm, k), BF16)],
        compiler_params=_params("arbitrary", "arbitrary"),
        name=name,
    )(x, g.reshape(1, k), w, colscale.reshape(1, n))


def _in_proj_kernel(x_ref, g_ref, wf_ref, wb_ref, cs_ref, of_ref, ob_ref, xn_ref, *, nf):
    j = pl.program_id(1)

    @pl.when(j == 0)
    def _():
        xn_ref[...] = _rmsnorm(x_ref[...], g_ref[...]).astype(BF16)

    @pl.when(j < nf)
    def _():
        of_ref[...] = jnp.dot(xn_ref[...], wf_ref[jnp.minimum(j, nf - 1)],
                              preferred_element_type=F32)

    @pl.when(j >= nf)
    def _():
        jb = jnp.maximum(j - nf, 0)
        acc = jnp.dot(xn_ref[...], wb_ref[jb], preferred_element_type=F32)
        ob_ref[...] = (acc * cs_ref[jb]).astype(BF16)


def _in_proj(x, g, wf, wb, scale_b, tm):
    m, k = x.shape
    (nf, _, tnf), (nb, _, tnb) = wf.shape, wb.shape
    whole = lambda a: pl.BlockSpec(a.shape, lambda i, j: (0, 0, 0), pipeline_mode=pl.Buffered(1))
    cs = scale_b.reshape(nb, 1, tnb)
    return pl.pallas_call(
        functools.partial(_in_proj_kernel, nf=nf),
        grid=(m // tm, nf + nb),
        in_specs=[
            pl.BlockSpec((tm, k), lambda i, j: (i, 0)),
            pl.BlockSpec((1, k), lambda i, j: (0, 0)),
            whole(wf), whole(wb), whole(cs),
        ],
        out_specs=[pl.BlockSpec((tm, tnf), lambda i, j: (i, jnp.minimum(j, nf - 1))),
                   pl.BlockSpec((tm, tnb), lambda i, j: (i, jnp.maximum(j - nf, 0)))],
        out_shape=[jax.ShapeDtypeStruct((m, nf * tnf), F32),
                   jax.ShapeDtypeStruct((m, nb * tnb), BF16)],
        scratch_shapes=[pltpu.VMEM((tm, k), BF16)],
        compiler_params=_params("arbitrary", "arbitrary"),
        name="in_proj",
    )(x, g.reshape(1, k), wf, wb, cs)


def _rope_half(hi, cos2, sin2):
    up = pltpu.roll(hi, 32, axis=1)
    down = pltpu.roll(hi, 96, axis=1)
    return hi * cos2 + (up - down) * sin2


def _mla_up_kernel(cq_ref, ckv_ref, kr_ref, gq_ref, gkv_ref, wqt_ref, wk_ref, wvt_ref,
                   cos_ref, sin_ref, cost_ref, sint_ref, qt_ref, k_ref, vt_ref):
    scale = LOG2E * (MLA_NOPE + MLA_ROPE) ** -0.5
    half = MLA_ROPE // 2
    cos_t = cost_ref[...]
    sin_t = sint_ref[...]

    cqn = _rmsnorm(cq_ref[...], gq_ref[...]).astype(BF16)
    yt = lax.dot_general(wqt_ref[...], cqn, _NT, preferred_element_type=F32)
    for h in range(MLA_HEADS):
        r = h * HEAD_PAD
        x1 = yt[r + MLA_NOPE:r + MLA_NOPE + half]
        x2 = yt[r + MLA_NOPE + half:r + MLA_NOPE + MLA_ROPE]
        qt_ref[0, r:r + MLA_NOPE, :] = (yt[r:r + MLA_NOPE] * scale).astype(BF16)
        qt_ref[0, r + MLA_NOPE:r + MLA_NOPE + half, :] = (
            (x1 * cos_t - x2 * sin_t) * scale).astype(BF16)
        qt_ref[0, r + MLA_NOPE + half:r + MLA_NOPE + MLA_ROPE, :] = (
            (x1 * sin_t + x2 * cos_t) * scale).astype(BF16)
        qt_ref[0, r + MLA_NOPE + MLA_ROPE:r + HEAD_PAD, :] = jnp.zeros(
            (HEAD_PAD - MLA_NOPE - MLA_ROPE, yt.shape[1]), BF16)

    ckn = _rmsnorm(ckv_ref[...], gkv_ref[...]).astype(BF16)
    kn = jnp.dot(ckn, wk_ref[...], preferred_element_type=F32)
    k_pe = _rope_half(kr_ref[...], cos_ref[...], sin_ref[...]).astype(BF16)
    for h in range(MLA_HEADS):
        k_ref[:, h * HEAD_PAD:h * HEAD_PAD + LANES] = kn[:, h * LANES:(h + 1) * LANES].astype(BF16)
        k_ref[:, h * HEAD_PAD + LANES:(h + 1) * HEAD_PAD] = k_pe
    vt = lax.dot_general(wvt_ref[...], ckn, _NT, preferred_element_type=F32)
    ones = jnp.ones((SUM_ROWS, vt.shape[1]), BF16)
    for h in range(MLA_HEADS):
        r = h * (MLA_V + SUM_ROWS)
        vt_ref[0, r:r + MLA_V, :] = vt[h * MLA_V:(h + 1) * MLA_V].astype(BF16)
        vt_ref[0, r + MLA_V:r + MLA_V + SUM_ROWS, :] = ones


def _mla_up(hf, gq, gkv, wqt, wk, wvt, rope, seq, tm):
    cos2, sin2, cos_t, sin_t = rope
    m = hf.shape[0]
    steps_per_seq = seq // tm
    hq = MLA_HEADS * HEAD_PAD
    half = MLA_ROPE // 2
    const = lambda i: (0, 0)
    return pl.pallas_call(
        _mla_up_kernel,
        grid=(m // tm,),
        in_specs=[
            pl.BlockSpec((tm, MLA_Q_RANK), lambda i: (i, F_CQ // MLA_Q_RANK)),
            pl.BlockSpec((tm, MLA_KV_RANK), lambda i: (i, F_CKV // MLA_KV_RANK)),
            pl.BlockSpec((tm, LANES), lambda i: (i, F_KROPE // LANES)),
            pl.BlockSpec((1, MLA_Q_RANK), const),
            pl.BlockSpec((1, MLA_KV_RANK), const),
            pl.BlockSpec((hq, MLA_Q_RANK), const),
            pl.BlockSpec((MLA_KV_RANK, MLA_HEADS * MLA_NOPE), const),
            pl.BlockSpec((MLA_WIDTH, MLA_KV_RANK), const),
            pl.BlockSpec((tm, LANES), lambda i: (i % steps_per_seq, 0)),
            pl.BlockSpec((tm, LANES), lambda i: (i % steps_per_seq, 0)),
            pl.BlockSpec((half, tm), lambda i: (0, i % steps_per_seq)),
            pl.BlockSpec((half, tm), lambda i: (0, i % steps_per_seq)),
        ],
        out_specs=[
            pl.BlockSpec((1, hq, tm), lambda i: (i, 0, 0)),
            pl.BlockSpec((tm, hq), lambda i: (i, 0)),
            pl.BlockSpec((1, MLA_HEADS * (MLA_V + SUM_ROWS), tm), lambda i: (i, 0, 0)),
        ],
        out_shape=[
            jax.ShapeDtypeStruct((m // tm, hq, tm), BF16),
            jax.ShapeDtypeStruct((m, hq), BF16),
            jax.ShapeDtypeStruct((m // tm, MLA_HEADS * (MLA_V + SUM_ROWS), tm), BF16),
        ],
        compiler_params=_params("arbitrary"),
        name="mla_up",
    )(hf, hf, hf, gq.reshape(1, -1), gkv.reshape(1, -1), wqt, wk, wvt, cos2, sin2, cos_t, sin_t)


def _mla_attn_kernel(qt_ref, k_ref, vt_ref, z_ref, o_ref, m_ref, acc_ref, s0_ref, *,
                     tile_len, hps):
    t = tile_len
    va = MLA_V + SUM_ROWS
    qi = pl.program_id(2)
    qts = [qt_ref[0, h * HEAD_PAD:(h + 1) * HEAD_PAD, :] for h in range(hps)]

    def scores(j, h):
        kb = pl.multiple_of(j * t, t)
        return jnp.dot(k_ref[0, pl.ds(kb, t), h * HEAD_PAD:(h + 1) * HEAD_PAD], qts[h],
                       preferred_element_type=F32)

    def softmax(h, s, diagonal):
        if diagonal:
            kpos = lax.broadcasted_iota(jnp.int32, (t, t), 0)
            qpos = lax.broadcasted_iota(jnp.int32, (t, t), 1)
            visible = kpos <= qpos
            s = jnp.where(visible, s, MASK_VALUE)
        m = m_ref[h]
        m_new = jnp.maximum(m, jnp.max(s, axis=0, keepdims=True))
        alpha = jnp.exp2(m - m_new)
        p = jnp.exp2(s - m_new)
        m_ref[h] = m_new
        return alpha, p.astype(BF16)

    def accumulate(j, h, alpha, p):
        vt = vt_ref[j, h * va:(h + 1) * va, :]
        acc_ref[h] = alpha * acc_ref[h] + jnp.dot(vt, p, preferred_element_type=F32)

    def tile(j, diagonal):
        alpha, p = softmax(0, s0_ref[...], diagonal)
        for h in range(1, hps):
            s = scores(j, h)
            accumulate(j, h - 1, alpha, p)
            alpha, p = softmax(h, s, diagonal)
        if not diagonal:
            s0_ref[...] = scores(j + 1, 0)
        accumulate(j, hps - 1, alpha, p)

    for h in range(hps):
        m_ref[h] = jnp.full((1, t), MASK_VALUE, F32)
        acc_ref[h] = jnp.zeros((va, t), F32)
    s0_ref[...] = scores(0, 0)

    def body(j, carry):
        tile(j, False)
        return carry

    lax.fori_loop(0, qi, body, 0)

    tile(qi, True)
    for h in range(hps):
        l = acc_ref[h, MLA_V:MLA_V + 1, :]
        o = (acc_ref[h, 0:MLA_V, :] / (l + SOFTMAX_EPS)).T
        cols = slice(h * MLA_V, (h + 1) * MLA_V)
        o_ref[0, :, cols] = (o * _silu(z_ref[0, :, cols])).astype(BF16)


def _mla_attn(qt, k, vt, hf, tile_len, hps):
    b, s, _ = k.shape
    t = tile_len
    tiles = s // t
    kern = functools.partial(_mla_attn_kernel, tile_len=t, hps=hps)
    qw, vw, va = hps * HEAD_PAD, hps * MLA_V, MLA_V + SUM_ROWS
    return pl.pallas_call(
        kern,
        grid=(b, MLA_HEADS // hps, tiles),
        in_specs=[
            pl.BlockSpec((1, qw, t), lambda bi, h, i: (bi * tiles + i, h, 0)),
            pl.BlockSpec((1, s, qw), lambda bi, h, i: (bi, 0, h)),
            pl.BlockSpec((tiles, hps * va, t), lambda bi, h, i: (bi, h, 0)),
            pl.BlockSpec((1, t, vw), lambda bi, h, i: (bi, i, F_Z // vw + h)),
        ],
        out_specs=pl.BlockSpec((1, t, vw), lambda bi, h, i: (bi, i, h)),
        out_shape=jax.ShapeDtypeStruct((b, s, MLA_WIDTH), BF16),
        scratch_shapes=[
            pltpu.VMEM((hps, 1, t), F32),
            pltpu.VMEM((hps, va, t), F32),
            pltpu.VMEM((t, t), F32),
        ],
        compiler_params=_params("arbitrary", "arbitrary", "arbitrary"),
        name="mla_attn",
    )(qt, k, vt, hf)


def _compress_one(x_refs, pe_ref, w1_ref, w2_ref, o_ref):
    chunks = x_refs[0].shape[1] // CMP_STRIDE
    a = b = None
    for i in range(CMP_STRIDE):
        for c, x_ref in enumerate(x_refs):
            lanes = slice(c * LANES, (c + 1) * LANES)
            x = x_ref[0, pl.ds(i, chunks, stride=CMP_STRIDE), :]
            ai = jnp.dot((x + pe_ref[i:i + 1, lanes]).astype(BF16), w1_ref[i, lanes, :],
                         preferred_element_type=F32)
            bi = jnp.dot((x + pe_ref[CMP_STRIDE + i:CMP_STRIDE + i + 1, lanes]).astype(BF16),
                         w1_ref[CMP_STRIDE + i, lanes, :], preferred_element_type=F32)
            a = ai if a is None else a + ai
            b = bi if b is None else b + bi
    h1 = a + pltpu.roll(b, chunks - 1, axis=0)
    o_ref[0] = jnp.dot(_silu(h1).astype(BF16), w2_ref[...], preferred_element_type=F32).astype(BF16)


def _compress_kernel(xk0_ref, xk1_ref, xv_ref, pek_ref, pev_ref, w1k_ref, w2k_ref, w1v_ref, w2v_ref,
                     ok_ref, ov_ref):
    _compress_one((xk0_ref, xk1_ref), pek_ref, w1k_ref, w2k_ref, ok_ref)
    _compress_one((xv_ref,), pev_ref, w1v_ref, w2v_ref, ov_ref)


def _compress(hf, pek, pev, w1k, w2k, w1v, w2v):
    b, s, _ = hf.shape
    chunks = s // CMP_STRIDE
    full = lambda a: pl.BlockSpec(a.shape, lambda bi: (0,) * a.ndim)
    return pl.pallas_call(
        _compress_kernel,
        grid=(b,),
        in_specs=[pl.BlockSpec((1, s, LANES), lambda bi: (bi, 0, F_KC // LANES)),
                  pl.BlockSpec((1, s, LANES), lambda bi: (bi, 0, F_KC // LANES + 1)),
                  pl.BlockSpec((1, s, NSA_DV), lambda bi: (bi, 0, F_VC // NSA_DV)),
                  full(pek), full(pev), full(w1k), full(w2k), full(w1v), full(w2v)],
        out_specs=[
            pl.BlockSpec((1, chunks, HEAD_PAD), lambda bi: (bi, 0, 0)),
            pl.BlockSpec((1, chunks, NSA_DV), lambda bi: (bi, 0, 0)),
        ],
        out_shape=[
            jax.ShapeDtypeStruct((b, chunks, HEAD_PAD), BF16),
            jax.ShapeDtypeStruct((b, chunks, NSA_DV), BF16),
        ],
        compiler_params=_params("arbitrary"),
        name="nsa_compress",
    )(hf, hf, hf, pek, pev, w1k, w2k, w1v, w2v)


def _alibi_slope(h):
    return 2.0 ** (-8.0 * (h + 1) / NSA_HEADS)


def _nsa_attn_kernel(q_ref, g_ref, z_ref, kc_ref, vc_ref, ks_ref, vs_ref, kw_ref, vw_ref,
                     ovt_ref, oh_ref, tabs_ref, tabd_ref, tabw_ref, tabc_ref, slope_ref, o_ref,
                     kaug_ref, kwp_ref, vst_ref, vwt_ref, vct_ref, m_ref, acc_ref, s0_ref,
                     *, tq, tk, n_c, n_s, top_n):
    qi = pl.program_id(1)
    t0 = qi * tq
    heads = NSA_HEADS
    seq = ks_ref.shape[1]
    ncp = kc_ref.shape[1]
    hw = 2 * tq
    aug = HEAD_PAD - NSA_DK
    wlen = WIN + tq
    va = NSA_DV + SUM_ROWS
    sub_tiles = tk // tq

    @pl.when(qi == 0)
    def _():
        kaug_ref[...] = ks_ref[0] + oh_ref[...]
        col = lax.broadcasted_iota(jnp.int32, (WIN, HEAD_PAD), 1)
        kwp_ref[0:WIN, :] = jnp.where(col == NSA_DK, 1.0, 0.0).astype(BF16)
        kwp_ref[WIN:, :] = kw_ref[0]
        for i in range(seq // tq):
            vst_ref[i, 0:NSA_DV, :] = vs_ref[0, i * tq:(i + 1) * tq, :].astype(F32).T.astype(BF16)
            vst_ref[i, NSA_DV:va, :] = jnp.ones((SUM_ROWS, tq), BF16)
        for i in range(WIN // LANES):
            vwt_ref[i] = jnp.zeros((va, LANES), BF16)
        for i in range(seq // LANES):
            vwt_ref[WIN // LANES + i, 0:NSA_DV, :] = (
                vw_ref[0, i * LANES:(i + 1) * LANES, :].astype(F32).T.astype(BF16))
            vwt_ref[WIN // LANES + i, NSA_DV:va, :] = jnp.ones((SUM_ROWS, LANES), BF16)
        vct_ref[...] = vc_ref[0].astype(F32).T.astype(BF16)

    q_all = q_ref[0]
    qt = [q_all[:, h * HEAD_PAD:(h + 1) * HEAD_PAD].astype(F32).T[0:NSA_DK] for h in range(heads)]

    def stacked_qt(extra):
        cols = [jnp.concatenate([qt[h], extra], axis=0) for h in range(heads)]
        return [jnp.concatenate(cols[2 * x:2 * x + 2], axis=1).astype(BF16) for x in range(2)]

    flag_rows = jnp.where(lax.broadcasted_iota(jnp.int32, (aug, tq), 0) == 0, MASK_VALUE, 0.0)
    qt_win = stacked_qt(flag_rows)

    kc = kc_ref[0]
    s_cmp = [jnp.dot(kc, qt_win[x], preferred_element_type=F32) for x in range(2)]
    kwb = kwp_ref[pl.ds(pl.multiple_of(t0, LANES), wlen), :]
    s_win = [jnp.dot(kwb, qt_win[x], preferred_element_type=F32) for x in range(2)]

    first_blk = qi * (tq // CMP_STRIDE)
    tab_c = tabc_ref[pl.ds(pl.multiple_of(ncp - first_blk, CMP_STRIDE), ncp), :]
    o_cmp, p_sum = [], None
    for x in range(2):
        p_x = []
        for hl in range(2):
            h = 2 * x + hl
            s = s_cmp[x][:, hl * tq:(hl + 1) * tq] + tab_c[:, h * tq:(h + 1) * tq]
            m = jnp.max(s, axis=0, keepdims=True)
            e = jnp.exp2(s - m)
            inv = jnp.where(m > 0.5 * MASK_VALUE,
                            1.0 / (jnp.sum(e, axis=0, keepdims=True) + SOFTMAX_EPS), 0.0)
            p = e * inv
            p_sum = p if p_sum is None else p_sum + p
            p_x.append(p.astype(BF16))
        o_cmp.append(jnp.dot(vct_ref[...], jnp.concatenate(p_x, axis=1),
                             preferred_element_type=F32))
    p_hi = p_sum.astype(BF16)
    p_lo = (p_sum - p_hi.astype(F32)).astype(BF16)
    ovt = ovt_ref[...]
    imp = (jnp.dot(ovt, p_hi, preferred_element_type=F32)
           + jnp.dot(ovt, p_lo, preferred_element_type=F32))

    tile0 = qi * (tq // LANES)
    vwb = jnp.concatenate([vwt_ref[tile0 + r] for r in range(wlen // LANES)], axis=1)
    o_win = []
    for x in range(2):
        s = s_win[x] + tabw_ref[:, x * hw:(x + 1) * hw]
        p = jnp.exp2(s - jnp.max(s, axis=0, keepdims=True))
        pv = jnp.dot(vwb, p.astype(BF16), preferred_element_type=F32)
        o_win.append(pv[0:NSA_DV] / (pv[NSA_DV:NSA_DV + 1] + SOFTMAX_EPS))

    blk = lax.broadcasted_iota(jnp.int32, (n_s, tq), 0)
    cur = (t0 + lax.broadcasted_iota(jnp.int32, (n_s, tq), 1)) // SLC_LEN
    forced = (blk == 0) | (blk == cur) | (blk == cur - 1)
    imp = jnp.where(forced, NEG_BIG, imp)
    imp = jnp.where(blk > cur, -NEG_BIG, imp)
    sub = lax.broadcasted_iota(jnp.int32, (8, tq), 0)
    groups = [imp[8 * g:8 * g + 8] for g in range(n_s // 8)]
    ranks = [jnp.zeros((8, tq), F32) for _ in groups]
    for jp in range(n_s):
        row = imp[jp:jp + 1, :]
        for g, grp in enumerate(groups):
            ge = jnp.where(row >= grp, 1.0, 0.0)
            gt = jnp.where(row > grp, 1.0, 0.0)
            if 8 * g > jp:
                beats = ge
            elif 8 * g + 8 <= jp:
                beats = gt
            else:
                beats = jnp.where(sub + 8 * g > jp, ge, gt)
            ranks[g] = ranks[g] + beats
    sel_rows = jnp.where(jnp.concatenate(ranks, axis=0) < top_n, 0.0, MASK_VALUE)
    if n_s < aug:
        sel_rows = jnp.concatenate([sel_rows, jnp.zeros((aug - n_s, tq), F32)], axis=0)
    qt_sel = stacked_qt(sel_rows)

    jd = t0 // tk

    def scores(j, x):
        kb = j * tk if isinstance(j, int) else pl.multiple_of(j * tk, tk)
        return jnp.dot(kaug_ref[pl.ds(kb, tk), :], qt_sel[x], preferred_element_type=F32)

    def softmax(x, s, tab, key0):
        u = s + tab
        off = slope_ref[0:1, x * hw:(x + 1) * hw] * (key0 - t0).astype(F32)
        m_old = m_ref[x]
        m_new = jnp.maximum(m_old, jnp.max(u, axis=0, keepdims=True) + off)
        alpha = jnp.exp2(m_old - m_new)
        p = jnp.exp2(u - (m_new - off))
        m_ref[x] = m_new
        return alpha, p.astype(BF16)

    def accumulate(x, alpha, p, vt):
        acc_ref[x] = alpha * acc_ref[x] + jnp.dot(vt, p, preferred_element_type=F32)

    def full_tile(j):
        vt = jnp.concatenate([vst_ref[j * sub_tiles + r] for r in range(sub_tiles)], axis=1)
        alpha, p = softmax(0, s0_ref[...], tabs_ref[:, 0:hw], j * tk)
        s1 = scores(j, 1)
        accumulate(0, alpha, p, vt)
        alpha, p = softmax(1, s1, tabs_ref[:, hw:2 * hw], j * tk)
        s0_ref[...] = scores(j + 1, 0)
        accumulate(1, alpha, p, vt)

    def sub_tile(r, tab_ref):
        key0 = jd * tk + r * tq
        rows = pl.ds(r * tq if isinstance(r, int) else pl.multiple_of(r * tq, tq), tq)
        vt = vst_ref[jd * sub_tiles + r]
        alpha, p = softmax(0, s0_ref[rows, :], tab_ref[0:tq, 0:hw], key0)
        s1 = jnp.dot(kaug_ref[pl.ds(pl.multiple_of(key0, tq), tq), :], qt_sel[1],
                     preferred_element_type=F32)
        accumulate(0, alpha, p, vt)
        alpha, p = softmax(1, s1, tab_ref[0:tq, hw:2 * hw], key0)
        accumulate(1, alpha, p, vt)

    for x in range(2):
        m_ref[x] = jnp.full((1, hw), MASK_VALUE, F32)
        acc_ref[x] = jnp.zeros((va, hw), F32)
    s0_ref[...] = scores(0, 0)

    def body(j, carry):
        full_tile(j)
        return carry

    lax.fori_loop(0, jd, body, 0)
    own = (t0 - jd * tk) // tq
    for r in range(sub_tiles - 1):
        @pl.when(r < own)
        def _():
            sub_tile(r, tabs_ref)
    sub_tile(own, tabd_ref)
    o_slc = [acc_ref[x, 0:NSA_DV, :] / (acc_ref[x, NSA_DV:NSA_DV + 1, :] + SOFTMAX_EPS)
             for x in range(2)]

    gates = _sigmoid(g_ref[0]).T
    z = z_ref[0]
    for h in range(heads):
        x, lanes = h // 2, slice((h % 2) * tq, (h % 2 + 1) * tq)
        c = NSA_DK + 3 * h
        o = (gates[c:c + 1] * o_cmp[x][:, lanes] + gates[c + 1:c + 2] * o_slc[x][:, lanes]
             + gates[c + 2:c + 3] * o_win[x][:, lanes])
        zh = z[:, h * NSA_DV:(h + 1) * NSA_DV]
        o_ref[0, :, h * NSA_DV:(h + 1) * NSA_DV] = (o.T * _silu(zh)).astype(BF16)


def _nsa_attn(hb, hf, k_cmp, v_cmp, tables, tq, tk):
    ovt, onehot, tab_sel, tab_diag, tab_win, tab_cmp, slope_rows = tables
    b, s, _ = hb.shape
    ncp = k_cmp.shape[1]
    n_s = s // SLC_LEN
    assert n_s <= HEAD_PAD - NSA_DK and n_s % 8 == 0, "selection blocks must fit the spare rows"
    assert tq & (tq - 1) == 0 and tq % LANES == 0 and tk % tq == 0 and s % tk == 0
    kern = functools.partial(_nsa_attn_kernel, tq=tq, tk=tk, n_c=ncp - 1, n_s=n_s,
                             top_n=min(SLC_TOPN, n_s))
    qw = NSA_HEADS * HEAD_PAD
    const2 = lambda a: pl.BlockSpec(a.shape, lambda bi, i: (0, 0), pipeline_mode=pl.Buffered(1))
    return pl.pallas_call(
        kern,
        grid=(b, s // tq),
        in_specs=[
            pl.BlockSpec((1, tq, qw), lambda bi, i: (bi, i, B_QNSA // qw)),
            pl.BlockSpec((1, tq, HEAD_PAD), lambda bi, i: (bi, i, F_KC // HEAD_PAD)),
            pl.BlockSpec((1, tq, NSA_WIDTH), lambda bi, i: (bi, i, F_ZNSA // NSA_WIDTH)),
            pl.BlockSpec((1, ncp, HEAD_PAD), lambda bi, i: (bi, 0, 0)),
            pl.BlockSpec((1, ncp, NSA_DV), lambda bi, i: (bi, 0, 0)),
            pl.BlockSpec((1, s, HEAD_PAD), lambda bi, i: (bi, 0, B_KS // HEAD_PAD)),
            pl.BlockSpec((1, s, NSA_DV), lambda bi, i: (bi, 0, B_VS // NSA_DV)),
            pl.BlockSpec((1, s, HEAD_PAD), lambda bi, i: (bi, 0, B_KW // HEAD_PAD)),
            pl.BlockSpec((1, s, NSA_DV), lambda bi, i: (bi, 0, B_VW // NSA_DV)),
            const2(ovt), const2(onehot), const2(tab_sel), const2(tab_diag), const2(tab_win),
            const2(tab_cmp), const2(slope_rows),
        ],
        out_specs=pl.BlockSpec((1, tq, NSA_WIDTH), lambda bi, i: (bi, i, 0)),
        out_shape=jax.ShapeDtypeStruct((b, s, NSA_WIDTH), BF16),
        scratch_shapes=[
            pltpu.VMEM((s, HEAD_PAD), BF16),
            pltpu.VMEM((s + WIN, HEAD_PAD), BF16),
            pltpu.VMEM((s // tq, NSA_DV + SUM_ROWS, tq), BF16),
            pltpu.VMEM(((s + WIN) // LANES, NSA_DV + SUM_ROWS, LANES), BF16),
            pltpu.VMEM((NSA_DV, ncp), BF16),
            pltpu.VMEM((2, 1, 2 * tq), F32),
            pltpu.VMEM((2, NSA_DV + SUM_ROWS, 2 * tq), F32),
            pltpu.VMEM((tk, 2 * tq), F32),
        ],
        compiler_params=_params("arbitrary", "arbitrary"),
        name="nsa_attn",
    )(hb, hf, hf, k_cmp, v_cmp, hb, hb, hb, hb, ovt, onehot, tab_sel, tab_diag, tab_win, tab_cmp,
      slope_rows)


def _mem_attn_kernel(q_ref, k_ref, v_ref, z_ref, o_ref):
    q = q_ref[0]
    k = k_ref[0]
    v = v_ref[0]
    z = z_ref[0]
    for h in range(MEM_HEADS):
        sl = slice(h * MEM_DH, (h + 1) * MEM_DH)
        s = lax.dot_general(q[:, sl], k[:, sl], _NT, preferred_element_type=F32)
        p = jnp.exp(s - jnp.max(s, axis=-1, keepdims=True))
        o = jnp.dot(p.astype(BF16), v[:, sl], preferred_element_type=F32)
        o = o / jnp.sum(p, axis=-1, keepdims=True)
        o_ref[0, :, sl] = (o * _silu(z[:, sl])).astype(BF16)


def _mem_attn(hb, mem_kv, hf, tq):
    b, s, _ = hb.shape
    mlen = mem_kv.shape[1]
    return pl.pallas_call(
        _mem_attn_kernel,
        grid=(b, s // tq),
        in_specs=[
            pl.BlockSpec((1, tq, MEM_WIDTH), lambda bi, i: (bi, i, B_QMEM // MEM_WIDTH)),
            pl.BlockSpec((1, mlen, MEM_WIDTH), lambda bi, i: (bi, 0, 0)),
            pl.BlockSpec((1, mlen, MEM_WIDTH), lambda bi, i: (bi, 0, 1)),
            pl.BlockSpec((1, tq, MEM_WIDTH), lambda bi, i: (bi, i, F_ZMEM // MEM_WIDTH)),
        ],
        out_specs=pl.BlockSpec((1, tq, MEM_WIDTH), lambda bi, i: (bi, i, 0)),
        out_shape=jax.ShapeDtypeStruct((b, s, MEM_WIDTH), BF16),
        compiler_params=_params("arbitrary", "arbitrary"),
        name="mem_attn",
    )(hb, mem_kv, mem_kv, hf)


def _out_proj_kernel(x_ref, a_ref, n_ref, m_ref, w_ref, g_ref, o_ref, *, final_norm):
    y = x_ref[...]
    y = y + jnp.dot(a_ref[...], w_ref[0:MLA_WIDTH, :], preferred_element_type=F32)
    y = y + jnp.dot(n_ref[...], w_ref[MLA_WIDTH:MLA_WIDTH + NSA_WIDTH, :], preferred_element_type=F32)
    y = y + jnp.dot(m_ref[...], w_ref[MLA_WIDTH + NSA_WIDTH:, :], preferred_element_type=F32)
    if final_norm:
        y = _rmsnorm(y, g_ref[...])
    o_ref[...] = y


def _out_proj(x, o_mla, o_nsa, o_mem, w_out, g, final_norm, tm):
    m, d = x.shape
    kern = functools.partial(_out_proj_kernel, final_norm=final_norm)
    row = lambda width: pl.BlockSpec((tm, width), lambda i: (i, 0))
    return pl.pallas_call(
        kern,
        grid=(m // tm,),
        in_specs=[row(d), row(MLA_WIDTH), row(NSA_WIDTH), row(MEM_WIDTH),
                  pl.BlockSpec(w_out.shape, lambda i: (0, 0)),
                  pl.BlockSpec((1, d), lambda i: (0, 0))],
        out_specs=row(d),
        out_shape=jax.ShapeDtypeStruct((m, d), F32),
        compiler_params=_params("arbitrary"),
        name="out_proj",
    )(x, o_mla, o_nsa, o_mem, w_out, g.reshape(1, d))


def _pad_cols(w, width):
    return jnp.pad(w, ((0, 0), (0, width - w.shape[1])))


def _w_in_moves():
    names = ("c_q", "c_kv", "k_rope", "z_mla", "q_nsa", "k_c", "v_c", "k_s", "v_s", "k_w", "v_w",
             "g_nsa", "z_nsa", "q_mem", "z_mem")
    src, off = {}, 0
    for name, n in zip(names, IN_SPLITS):
        src[name] = (off, n)
        off += n
    dst = {"c_q": (0, F_CQ), "c_kv": (0, F_CKV), "z_mla": (0, F_Z), "z_nsa": (0, F_ZNSA),
           "z_mem": (0, F_ZMEM), "k_c": (0, F_KC), "g_nsa": (0, F_GATE), "v_c": (0, F_VC),
           "k_rope": (0, F_KROPE), "q_mem": (1, B_QMEM), "k_s": (1, B_KS), "k_w": (1, B_KW),
           "v_s": (1, B_VS), "v_w": (1, B_VW)}
    moves = [(src[n][0], src[n][1], dst[n][0], dst[n][1]) for n in dst]
    for h in range(NSA_HEADS):
        moves.append((src["q_nsa"][0] + h * NSA_DK, NSA_DK, 1, B_QNSA + h * HEAD_PAD))
    return moves


def _w_prep_kernel(w_ref, wf_ref, wb_ref):
    outs = (wf_ref, wb_ref)
    wf_ref[...] = jnp.zeros(wf_ref.shape, BF16)
    wb_ref[...] = jnp.zeros(wb_ref.shape, BF16)
    for s0, width, which, d0 in _w_in_moves():
        tn = outs[which].shape[2]
        while width > 0:
            t, c = divmod(d0, tn)
            n = min(width, tn - c)
            outs[which][t, :, c:c + n] = w_ref[:, s0:s0 + n].astype(BF16)
            s0, d0, width = s0 + n, d0 + n, width - n


def _layout_w_in(w_in, tnf, tnb, tr=256):
    d, n = w_in.shape
    nf, nb = F_WIDTH // tnf, B_WIDTH // tnb
    wf, wb = pl.pallas_call(
        _w_prep_kernel,
        grid=(d // tr,),
        in_specs=[pl.BlockSpec((tr, n), lambda i: (i, 0))],
        out_specs=[pl.BlockSpec((nf, tr, tnf), lambda i: (0, i, 0)),
                   pl.BlockSpec((nb, tr, tnb), lambda i: (0, i, 0))],
        out_shape=[jax.ShapeDtypeStruct((nf, d, tnf), BF16), jax.ShapeDtypeStruct((nb, d, tnb), BF16)],
        compiler_params=_params("arbitrary"),
        name="w_in_layout",
    )(w_in)
    scale_b = jnp.concatenate([
        jnp.full((NSA_HEADS * HEAD_PAD,), LOG2E * NSA_DK ** -0.5, F32),
        jnp.full((MEM_WIDTH,), MEM_DH ** -0.5, F32),
        jnp.ones((B_WIDTH - B_QMEM - MEM_WIDTH,), F32)])
    return wf, wb, scale_b


def _rope_tables(seq):
    pos = jnp.arange(seq, dtype=F32)
    inv_freq = ROPE_THETA ** (-jnp.arange(0, MLA_ROPE, 2, dtype=F32) / MLA_ROPE)
    ang = pos[:, None] * inv_freq[None, :]
    zeros = jnp.zeros((seq, LANES - MLA_ROPE), F32)
    cos2 = jnp.concatenate([jnp.cos(ang), jnp.cos(ang), zeros], axis=1)
    sin2 = jnp.concatenate([jnp.sin(ang), jnp.sin(ang), zeros], axis=1)
    return cos2, sin2, jnp.cos(ang).T, jnp.sin(ang).T


def _nsa_tables(seq, tq, tk):
    chunks = seq // CMP_STRIDE
    n_s = seq // SLC_LEN
    c_start = jnp.arange(chunks) * CMP_STRIDE
    s_start = jnp.arange(n_s) * SLC_LEN
    overlap_t = ((c_start[None, :] < s_start[:, None] + SLC_LEN)
                 & (c_start[None, :] + CMP_LEN > s_start[:, None])
                 & (jnp.arange(chunks)[None, :] < chunks - 1))
    key_block = jnp.arange(seq) // SLC_LEN
    onehot = jnp.arange(HEAD_PAD)[None, :] == (NSA_DK + key_block)[:, None]
    slope = jnp.repeat(jnp.array([_alibi_slope(h) for h in range(NSA_HEADS)], F32) * LOG2E, tq)
    q_lane = jnp.tile(jnp.arange(tq), NSA_HEADS)
    tab_sel = jnp.arange(tk, dtype=F32)[:, None] * slope[None, :]
    rel = q_lane[None, :] - jnp.arange(WIN + tq)[:, None] + WIN
    tab_win = jnp.where((rel >= 0) & (rel < WIN), -slope[None, :] * rel.astype(F32), MASK_VALUE)
    slope_rows = jnp.broadcast_to(slope[None, :], (8, NSA_HEADS * tq))
    tab_diag = jnp.where(jnp.arange(tq)[:, None] <= q_lane[None, :], tab_sel[0:tq], MASK_VALUE)
    d_blk = jnp.arange(2 * chunks)[:, None] - chunks
    seen = d_blk * CMP_STRIDE + (CMP_LEN - 1) <= q_lane[None, :]
    dist = (q_lane[None, :] - d_blk * CMP_STRIDE).astype(F32) - (CMP_LEN - 1) / 2.0
    tab_cmp = jnp.where(seen, -slope[None, :] * dist, MASK_VALUE)
    return overlap_t.astype(BF16), onehot.astype(BF16), tab_sel, tab_diag, tab_win, tab_cmp, slope_rows


def _layer(x2, mem2, batch, seq, tables, norm_g, w_in, q_norm_g, w_uq, kv_norm_g, w_ukv,
           cmp_pe_k, cmp_pe_v, cmp_w1k, cmp_w2k, cmp_w1v, cmp_w2v, mem_norm_g, w_mem_kv, w_out,
           final_g, final_norm):
    rope, nsa_tables = tables
    d = x2.shape[1]
    wf, wb, scale_b = _layout_w_in(w_in, F_WIDTH // 2, B_WIDTH)
    hf, hb = _in_proj(x2, norm_g, wf, wb, scale_b, 512)

    wq = jnp.pad(w_uq.reshape(MLA_Q_RANK, MLA_HEADS, MLA_NOPE + MLA_ROPE),
                 ((0, 0), (0, 0), (0, HEAD_PAD - MLA_NOPE - MLA_ROPE)))
    wqt = wq.reshape(MLA_Q_RANK, MLA_HEADS * HEAD_PAD).T.astype(BF16)
    wkv = w_ukv.reshape(MLA_KV_RANK, MLA_HEADS, MLA_NOPE + MLA_V)
    wk = wkv[:, :, :MLA_NOPE].reshape(MLA_KV_RANK, MLA_HEADS * MLA_NOPE).astype(BF16)
    wvt = wkv[:, :, MLA_NOPE:].reshape(MLA_KV_RANK, MLA_WIDTH).T.astype(BF16)
    qt, k, vt = _mla_up(hf, q_norm_g, kv_norm_g, wqt, wk, wvt, rope, seq, MLA_TILE)
    hf3 = hf.reshape(batch, seq, F_WIDTH)
    hb3 = hb.reshape(batch, seq, B_WIDTH)
    o_mla = _mla_attn(qt, k.reshape(batch, seq, -1), vt, hf3, MLA_TILE, 4)

    pad_k = HEAD_PAD - NSA_DK
    k_cmp, v_cmp = _compress(
        hf3, _pad_cols(cmp_pe_k, HEAD_PAD), cmp_pe_v,
        jnp.pad(cmp_w1k.reshape(CMP_LEN, NSA_DK, NSA_DK), ((0, 0), (0, pad_k), (0, pad_k))).astype(BF16),
        jnp.pad(cmp_w2k, ((0, pad_k), (0, pad_k))).astype(BF16),
        cmp_w1v.reshape(CMP_LEN, NSA_DV, NSA_DV).astype(BF16), cmp_w2v.astype(BF16))
    o_nsa = _nsa_attn(hb3, hf3, k_cmp, v_cmp, nsa_tables, NSA_TQ, NSA_TK)

    mem_kv = _norm_proj(mem2, mem_norm_g, w_mem_kv.astype(BF16), jnp.ones((2 * MEM_WIDTH,), F32),
                        BF16, mem2.shape[0] // batch, MEM_WIDTH, "mem_kv_proj")
    o_mem = _mem_attn(hb3, mem_kv.reshape(batch, -1, 2 * MEM_WIDTH), hf3, 512)

    return _out_proj(x2, o_mla.reshape(-1, MLA_WIDTH), o_nsa.reshape(-1, NSA_WIDTH),
                     o_mem.reshape(-1, MEM_WIDTH), w_out.astype(BF16), final_g, final_norm, 512)


def kernel(x, mem, norm_g, w_in, q_norm_g, w_uq, kv_norm_g, w_ukv, cmp_pe_k, cmp_pe_v,
           cmp_w1k, cmp_w2k, cmp_w1v, cmp_w2v, mem_norm_g, w_mem_kv, w_out, final_norm_g):
    batch, seq, d = x.shape
    depth = norm_g.shape[0]
    tables = (_rope_tables(seq), _nsa_tables(seq, NSA_TQ, NSA_TK))
    x2 = x.reshape(batch * seq, d)
    mem2 = mem.reshape(batch * mem.shape[1], d)
    for l in range(depth):
        x2 = _layer(x2, mem2, batch, seq, tables, norm_g[l], w_in[l], q_norm_g[l], w_uq[l],
                    kv_norm_g[l], w_ukv[l], cmp_pe_k[l], cmp_pe_v[l], cmp_w1k[l], cmp_w2k[l],
                    cmp_w1v[l], cmp_w2v[l], mem_norm_g[l], w_mem_kv[l], w_out[l],
                    final_norm_g, l == depth - 1)
    return x2.reshape(batch, seq, d)
```

```python
import functools

import jax
import jax.numpy as jnp
from jax import lax
from jax.experimental import pallas as pl
from jax.experimental.pallas import tpu as pltpu

F32 = jnp.float32
BF16 = jnp.bfloat16

EPS = 1e-6
NEG_BIG = 1e9
MASK_VALUE = -1e30
SOFTMAX_EPS = 1e-20
LOG2E = 1.4426950408889634

MLA_HEADS = 8
MLA_NOPE = 128
MLA_ROPE = 64
MLA_V = 128
MLA_Q_RANK = 512
MLA_KV_RANK = 512
ROPE_THETA = 10000.0

NSA_HEADS = 4
NSA_DK = 192
NSA_DV = 128
CMP_LEN = 32
CMP_STRIDE = 16
SLC_LEN = 64
SLC_TOPN = 16
WIN = 512

MEM_HEADS = 4
MEM_DH = 128

MLA_WIDTH = MLA_HEADS * MLA_V
NSA_WIDTH = NSA_HEADS * NSA_DV
MEM_WIDTH = MEM_HEADS * MEM_DH

IN_SPLITS = (
    MLA_Q_RANK, MLA_KV_RANK, MLA_ROPE, MLA_WIDTH,
    NSA_HEADS * NSA_DK, NSA_DK, NSA_DV, NSA_DK, NSA_DV,
    NSA_DK, NSA_DV, 3 * NSA_HEADS, NSA_WIDTH,
    MEM_WIDTH, MEM_WIDTH,
)

LANES = 128
MXU_COLS = 256
HEAD_PAD = 256
VMEM_LIMIT = 56 * 1024 * 1024
MLA_TILE = 512
SUM_ROWS = 16
NSA_TQ = 256
NSA_TK = 512

F_CQ, F_CKV, F_Z, F_KC, F_VC, F_KROPE = 0, 512, 1024, 3072, 3328, 3456
F_ZNSA, F_ZMEM = F_Z + MLA_WIDTH, F_Z + MLA_WIDTH + NSA_WIDTH
F_GATE = F_KC + NSA_DK
F_WIDTH = 3584
B_QNSA, B_QMEM, B_KS, B_KW, B_VS, B_VW = 0, 1024, 1536, 1792, 2048, 2176
B_WIDTH = 2304

_NT = (((1,), (1,)), ((), ()))


def _params(*sem):
    return pltpu.CompilerParams(dimension_semantics=sem, vmem_limit_bytes=VMEM_LIMIT)


def _sigmoid(x):
    return 1.0 / (1.0 + jnp.exp(-x))


def _silu(x):
    return x * _sigmoid(x)


def _rmsnorm(x, g):
    ms = jnp.mean(x * x, axis=-1, keepdims=True)
    return (x * lax.rsqrt(ms + EPS)) * g


def _norm_proj_kernel(x_ref, g_ref, w_ref, cs_ref, o_ref, xn_ref):
    @pl.when(pl.program_id(1) == 0)
    def _():
        xn_ref[...] = _rmsnorm(x_ref[...], g_ref[...]).astype(BF16)

    acc = jnp.dot(xn_ref[...], w_ref[...], preferred_element_type=F32)
    o_ref[...] = (acc * cs_ref[...]).astype(o_ref.dtype)


def _norm_proj(x, g, w, colscale, out_dtype, tm, tn, name):
    m, k = x.shape
    n = w.shape[1]
    return pl.pallas_call(
        _norm_proj_kernel,
        grid=(m // tm, n // tn),
        in_specs=[
            pl.BlockSpec((tm, k), lambda i, j: (i, 0)),
            pl.BlockSpec((1, k), lambda i, j: (0, 0)),
            pl.BlockSpec((k, tn), lambda i, j: (0, j)),
            pl.BlockSpec((1, tn), lambda i, j: (0, j)),
        ],
        out_specs=pl.BlockSpec((tm, tn), lambda i, j: (i, j)),
        out_shape=jax.ShapeDtypeStruct((m, n), out_dtype),
        scratch_shapes=[pltpu.VMEM((tm, k), BF16)],
        compiler_params=_params("arbitrary", "arbitrary"),
        name=name,
    )(x, g.reshape(1, k), w, colscale.reshape(1, n))


def _in_proj_kernel(x_ref, g_ref, wf_ref, wb_ref, cs_ref, of_ref, ob_ref, xn_ref, *, nf):
    j = pl.program_id(1)

    @pl.when(j == 0)
    def _():
        xn_ref[...] = _rmsnorm(x_ref[...], g_ref[...]).astype(BF16)

    @pl.when(j < nf)
    def _():
        of_ref[...] = jnp.dot(xn_ref[...], wf_ref[jnp.minimum(j, nf - 1)],
                              preferred_element_type=F32)

    @pl.when(j >= nf)
    def _():
        jb = jnp.maximum(j - nf, 0)
        acc = jnp.dot(xn_ref[...], wb_ref[jb], preferred_element_type=F32)
        ob_ref[...] = (acc * cs_ref[jb]).astype(BF16)


def _in_proj(x, g, wf, wb, scale_b, tm):
    m, k = x.shape
    (nf, _, tnf), (nb, _, tnb) = wf.shape, wb.shape
    whole = lambda a: pl.BlockSpec(a.shape, lambda i, j: (0, 0, 0), pipeline_mode=pl.Buffered(1))
    cs = scale_b.reshape(nb, 1, tnb)
    return pl.pallas_call(
        functools.partial(_in_proj_kernel, nf=nf),
        grid=(m // tm, nf + nb),
        in_specs=[
            pl.BlockSpec((tm, k), lambda i, j: (i, 0)),
            pl.BlockSpec((1, k), lambda i, j: (0, 0)),
            whole(wf), whole(wb), whole(cs),
        ],
        out_specs=[pl.BlockSpec((tm, tnf), lambda i, j: (i, jnp.minimum(j, nf - 1))),
                   pl.BlockSpec((tm, tnb), lambda i, j: (i, jnp.maximum(j - nf, 0)))],
        out_shape=[jax.ShapeDtypeStruct((m, nf * tnf), F32),
                   jax.ShapeDtypeStruct((m, nb * tnb), BF16)],
        scratch_shapes=[pltpu.VMEM((tm, k), BF16)],
        compiler_params=_params("arbitrary", "arbitrary"),
        name="in_proj",
    )(x, g.reshape(1, k), wf, wb, cs)


def _rope_half(hi, cos2, sin2):
    up = pltpu.roll(hi, 32, axis=1)
    down = pltpu.roll(hi, 96, axis=1)
    return hi * cos2 + (up - down) * sin2


def _mla_up_kernel(cq_ref, ckv_ref, kr_ref, gq_ref, gkv_ref, wqt_ref, wk_ref, wvt_ref,
                   cos_ref, sin_ref, cost_ref, sint_ref, qt_ref, k_ref, vt_ref):
    scale = LOG2E * (MLA_NOPE + MLA_ROPE) ** -0.5
    half = MLA_ROPE // 2
    cos_t = cost_ref[...]
    sin_t = sint_ref[...]

    cqn = _rmsnorm(cq_ref[...], gq_ref[...]).astype(BF16)
    yt = lax.dot_general(wqt_ref[...], cqn, _NT, preferred_element_type=F32)
    for h in range(MLA_HEADS):
        r = h * HEAD_PAD
        x1 = yt[r + MLA_NOPE:r + MLA_NOPE + half]
        x2 = yt[r + MLA_NOPE + half:r + MLA_NOPE + MLA_ROPE]
        qt_ref[0, r:r + MLA_NOPE, :] = (yt[r:r + MLA_NOPE] * scale).astype(BF16)
        qt_ref[0, r + MLA_NOPE:r + MLA_NOPE + half, :] = (
            (x1 * cos_t - x2 * sin_t) * scale).astype(BF16)
        qt_ref[0, r + MLA_NOPE + half:r + MLA_NOPE + MLA_ROPE, :] = (
            (x1 * sin_t + x2 * cos_t) * scale).astype(BF16)
        qt_ref[0, r + MLA_NOPE + MLA_ROPE:r + HEAD_PAD, :] = jnp.zeros(
            (HEAD_PAD - MLA_NOPE - MLA_ROPE, yt.shape[1]), BF16)

    ckn = _rmsnorm(ckv_ref[...], gkv_ref[...]).astype(BF16)
    kn = jnp.dot(ckn, wk_ref[...], preferred_element_type=F32)
    k_pe = _rope_half(kr_ref[...], cos_ref[...], sin_ref[...]).astype(BF16)
    for h in range(MLA_HEADS):
        k_ref[:, h * HEAD_PAD:h * HEAD_PAD + LANES] = kn[:, h * LANES:(h + 1) * LANES].astype(BF16)
        k_ref[:, h * HEAD_PAD + LANES:(h + 1) * HEAD_PAD] = k_pe
    vt = lax.dot_general(wvt_ref[...], ckn, _NT, preferred_element_type=F32)
    ones = jnp.ones((SUM_ROWS, vt.shape[1]), BF16)
    for h in range(MLA_HEADS):
        r = h * (MLA_V + SUM_ROWS)
        vt_ref[0, r:r + MLA_V, :] = vt[h * MLA_V:(h + 1) * MLA_V].astype(BF16)
        vt_ref[0, r + MLA_V:r + MLA_V + SUM_ROWS, :] = ones


def _mla_up(hf, gq, gkv, wqt, wk, wvt, rope, seq, tm):
    cos2, sin2, cos_t, sin_t = rope
    m = hf.shape[0]
    steps_per_seq = seq // tm
    hq = MLA_HEADS * HEAD_PAD
    half = MLA_ROPE // 2
    const = lambda i: (0, 0)
    return pl.pallas_call(
        _mla_up_kernel,
        grid=(m // tm,),
        in_specs=[
            pl.BlockSpec((tm, MLA_Q_RANK), lambda i: (i, F_CQ // MLA_Q_RANK)),
            pl.BlockSpec((tm, MLA_KV_RANK), lambda i: (i, F_CKV // MLA_KV_RANK)),
            pl.BlockSpec((tm, LANES), lambda i: (i, F_KROPE // LANES)),
            pl.BlockSpec((1, MLA_Q_RANK), const),
            pl.BlockSpec((1, MLA_KV_RANK), const),
            pl.BlockSpec((hq, MLA_Q_RANK), const),
            pl.BlockSpec((MLA_KV_RANK, MLA_HEADS * MLA_NOPE), const),
            pl.BlockSpec((MLA_WIDTH, MLA_KV_RANK), const),
            pl.BlockSpec((tm, LANES), lambda i: (i % steps_per_seq, 0)),
            pl.BlockSpec((tm, LANES), lambda i: (i % steps_per_seq, 0)),
            pl.BlockSpec((half, tm), lambda i: (0, i % steps_per_seq)),
            pl.BlockSpec((half, tm), lambda i: (0, i % steps_per_seq)),
        ],
        out_specs=[
            pl.BlockSpec((1, hq, tm), lambda i: (i, 0, 0)),
            pl.BlockSpec((tm, hq), lambda i: (i, 0)),
            pl.BlockSpec((1, MLA_HEADS * (MLA_V + SUM_ROWS), tm), lambda i: (i, 0, 0)),
        ],
        out_shape=[
            jax.ShapeDtypeStruct((m // tm, hq, tm), BF16),
            jax.ShapeDtypeStruct((m, hq), BF16),
            jax.ShapeDtypeStruct((m // tm, MLA_HEADS * (MLA_V + SUM_ROWS), tm), BF16),
        ],
        compiler_params=_params("arbitrary"),
        name="mla_up",
    )(hf, hf, hf, gq.reshape(1, -1), gkv.reshape(1, -1), wqt, wk, wvt, cos2, sin2, cos_t, sin_t)


def _mla_attn_kernel(qt_ref, k_ref, vt_ref, z_ref, o_ref, m_ref, acc_ref, s0_ref, *,
                     tile_len, hps):
    t = tile_len
    va = MLA_V + SUM_ROWS
    qi = pl.program_id(2)
    qts = [qt_ref[0, h * HEAD_PAD:(h + 1) * HEAD_PAD, :] for h in range(hps)]

    def scores(j, h):
        kb = pl.multiple_of(j * t, t)
        return jnp.dot(k_ref[0, pl.ds(kb, t), h * HEAD_PAD:(h + 1) * HEAD_PAD], qts[h],
                       preferred_element_type=F32)

    def softmax(h, s, diagonal):
        if diagonal:
            kpos = lax.broadcasted_iota(jnp.int32, (t, t), 0)
            qpos = lax.broadcasted_iota(jnp.int32, (t, t), 1)
            visible = kpos <= qpos
            s = jnp.where(visible, s, MASK_VALUE)
        m = m_ref[h]
        m_new = jnp.maximum(m, jnp.max(s, axis=0, keepdims=True))
        alpha = jnp.exp2(m - m_new)
        p = jnp.exp2(s - m_new)
        m_ref[h] = m_new
        return alpha, p.astype(BF16)

    def accumulate(j, h, alpha, p):
        vt = vt_ref[j, h * va:(h + 1) * va, :]
        acc_ref[h] = alpha * acc_ref[h] + jnp.dot(vt, p, preferred_element_type=F32)

    def tile(j, diagonal):
        alpha, p = softmax(0, s0_ref[...], diagonal)
        for h in range(1, hps):
            s = scores(j, h)
            accumulate(j, h - 1, alpha, p)
            alpha, p = softmax(h, s, diagonal)
        if not diagonal:
            s0_ref[...] = scores(j + 1, 0)
        accumulate(j, hps - 1, alpha, p)

    for h in range(hps):
        m_ref[h] = jnp.full((1, t), MASK_VALUE, F32)
        acc_ref[h] = jnp.zeros((va, t), F32)
    s0_ref[...] = scores(0, 0)

    def body(j, carry):
        tile(j, False)
        return carry

    lax.fori_loop(0, qi, body, 0)

    tile(qi, True)
    for h in range(hps):
        l = acc_ref[h, MLA_V:MLA_V + 1, :]
        o = (acc_ref[h, 0:MLA_V, :] / (l + SOFTMAX_EPS)).T
        cols = slice(h * MLA_V, (h + 1) * MLA_V)
        o_ref[0, :, cols] = (o * _silu(z_ref[0, :, cols])).astype(BF16)


def _mla_attn(qt, k, vt, hf, tile_len, hps):
    b, s, _ = k.shape
    t = tile_len
    tiles = s // t
    kern = functools.partial(_mla_attn_kernel, tile_len=t, hps=hps)
    qw, vw, va = hps * HEAD_PAD, hps * MLA_V, MLA_V + SUM_ROWS
    return pl.pallas_call(
        kern,
        grid=(b, MLA_HEADS // hps, tiles),
        in_specs=[
            pl.BlockSpec((1, qw, t), lambda bi, h, i: (bi * tiles + i, h, 0)),
            pl.BlockSpec((1, s, qw), lambda bi, h, i: (bi, 0, h)),
            pl.BlockSpec((tiles, hps * va, t), lambda bi, h, i: (bi, h, 0)),
            pl.BlockSpec((1, t, vw), lambda bi, h, i: (bi, i, F_Z // vw + h)),
        ],
        out_specs=pl.BlockSpec((1, t, vw), lambda bi, h, i: (bi, i, h)),
        out_shape=jax.ShapeDtypeStruct((b, s, MLA_WIDTH), BF16),
        scratch_shapes=[
            pltpu.VMEM((hps, 1, t), F32),
            pltpu.VMEM((hps, va, t), F32),
            pltpu.VMEM((t, t), F32),
        ],
        compiler_params=_params("arbitrary", "arbitrary", "arbitrary"),
        name="mla_attn",
    )(qt, k, vt, hf)


def _compress_one(x_refs, pe_ref, w1_ref, w2_ref, o_ref):
    chunks = x_refs[0].shape[1] // CMP_STRIDE
    a = b = None
    for i in range(CMP_STRIDE):
        for c, x_ref in enumerate(x_refs):
            lanes = slice(c * LANES, (c + 1) * LANES)
            x = x_ref[0, pl.ds(i, chunks, stride=CMP_STRIDE), :]
            ai = jnp.dot((x + pe_ref[i:i + 1, lanes]).astype(BF16), w1_ref[i, lanes, :],
                         preferred_element_type=F32)
            bi = jnp.dot((x + pe_ref[CMP_STRIDE + i:CMP_STRIDE + i + 1, lanes]).astype(BF16),
                         w1_ref[CMP_STRIDE + i, lanes, :], preferred_element_type=F32)
            a = ai if a is None else a + ai
            b = bi if b is None else b + bi
    h1 = a + pltpu.roll(b, chunks - 1, axis=0)
    o_ref[0] = jnp.dot(_silu(h1).astype(BF16), w2_ref[...], preferred_element_type=F32).astype(BF16)


def _compress_kernel(xk0_ref, xk1_ref, xv_ref, pek_ref, pev_ref, w1k_ref, w2k_ref, w1v_ref, w2v_ref,
                     ok_ref, ov_ref):
    _compress_one((xk0_ref, xk1_ref), pek_ref, w1k_ref, w2k_ref, ok_ref)
    _compress_one((xv_ref,), pev_ref, w1v_ref, w2v_ref, ov_ref)


def _compress(hf, pek, pev, w1k, w2k, w1v, w2v):
    b, s, _ = hf.shape
    chunks = s // CMP_STRIDE
    full = lambda a: pl.BlockSpec(a.shape, lambda bi: (0,) * a.ndim)
    return pl.pallas_call(
        _compress_kernel,
        grid=(b,),
        in_specs=[pl.BlockSpec((1, s, LANES), lambda bi: (bi, 0, F_KC // LANES)),
                  pl.BlockSpec((1, s, LANES), lambda bi: (bi, 0, F_KC // LANES + 1)),
                  pl.BlockSpec((1, s, NSA_DV), lambda bi: (bi, 0, F_VC // NSA_DV)),
                  full(pek), full(pev), full(w1k), full(w2k), full(w1v), full(w2v)],
        out_specs=[
            pl.BlockSpec((1, chunks, HEAD_PAD), lambda bi: (bi, 0, 0)),
            pl.BlockSpec((1, chunks, NSA_DV), lambda bi: (bi, 0, 0)),
        ],
        out_shape=[
            jax.ShapeDtypeStruct((b, chunks, HEAD_PAD), BF16),
            jax.ShapeDtypeStruct((b, chunks, NSA_DV), BF16),
        ],
        compiler_params=_params("arbitrary"),
        name="nsa_compress",
    )(hf, hf, hf, pek, pev, w1k, w2k, w1v, w2v)


def _alibi_slope(h):
    return 2.0 ** (-8.0 * (h + 1) / NSA_HEADS)


def _nsa_attn_kernel(q_ref, g_ref, z_ref, kc_ref, vc_ref, ks_ref, vs_ref, kw_ref, vw_ref,
                     ovt_ref, oh_ref, tabs_ref, tabd_ref, tabw_ref, tabc_ref, slope_ref, o_ref,
                     kaug_ref, kwp_ref, vst_ref, vwt_ref, vct_ref, m_ref, acc_ref, s0_ref,
                     *, tq, tk, n_c, n_s, top_n):
    qi = pl.program_id(1)
    t0 = qi * tq
    heads = NSA_HEADS
    seq = ks_ref.shape[1]
    ncp = kc_ref.shape[1]
    hw = 2 * tq
    aug = HEAD_PAD - NSA_DK
    wlen = WIN + tq
    va = NSA_DV + SUM_ROWS
    sub_tiles = tk // tq

    @pl.when(qi == 0)
    def _():
        kaug_ref[...] = ks_ref[0] + oh_ref[...]
        col = lax.broadcasted_iota(jnp.int32, (WIN, HEAD_PAD), 1)
        kwp_ref[0:WIN, :] = jnp.where(col == NSA_DK, 1.0, 0.0).astype(BF16)
        kwp_ref[WIN:, :] = kw_ref[0]
        for i in range(seq // tq):
            vst_ref[i, 0:NSA_DV, :] = vs_ref[0, i * tq:(i + 1) * tq, :].astype(F32).T.astype(BF16)
            vst_ref[i, NSA_DV:va, :] = jnp.ones((SUM_ROWS, tq), BF16)
        for i in range(WIN // LANES):
            vwt_ref[i] = jnp.zeros((va, LANES), BF16)
        for i in range(seq // LANES):
            vwt_ref[WIN // LANES + i, 0:NSA_DV, :] = (
                vw_ref[0, i * LANES:(i + 1) * LANES, :].astype(F32).T.astype(BF16))
            vwt_ref[WIN // LANES + i, NSA_DV:va, :] = jnp.ones((SUM_ROWS, LANES), BF16)
        vct_ref[...] = vc_ref[0].astype(F32).T.astype(BF16)

    q_all = q_ref[0]
    qt = [q_all[:, h * HEAD_PAD:(h + 1) * HEAD_PAD].astype(F32).T[0:NSA_DK] for h in range(heads)]

    def stacked_qt(extra):
        cols = [jnp.concatenate([qt[h], extra], axis=0) for h in range(heads)]
        return [jnp.concatenate(cols[2 * x:2 * x + 2], axis=1).astype(BF16) for x in range(2)]

    flag_rows = jnp.where(lax.broadcasted_iota(jnp.int32, (aug, tq), 0) == 0, MASK_VALUE, 0.0)
    qt_win = stacked_qt(flag_rows)

    kc = kc_ref[0]
    s_cmp = [jnp.dot(kc, qt_win[x], preferred_element_type=F32) for x in range(2)]
    kwb = kwp_ref[pl.ds(pl.multiple_of(t0, LANES), wlen), :]
    s_win = [jnp.dot(kwb, qt_win[x], preferred_element_type=F32) for x in range(2)]

    first_blk = qi * (tq // CMP_STRIDE)
    tab_c = tabc_ref[pl.ds(pl.multiple_of(ncp - first_blk, CMP_STRIDE), ncp), :]
    o_cmp, p_sum = [], None
    for x in range(2):
        p_x = []
        for hl in range(2):
            h = 2 * x + hl
            s = s_cmp[x][:, hl * tq:(hl + 1) * tq] + tab_c[:, h * tq:(h + 1) * tq]
            m = jnp.max(s, axis=0, keepdims=True)
            e = jnp.exp2(s - m)
            inv = jnp.where(m > 0.5 * MASK_VALUE,
                            1.0 / (jnp.sum(e, axis=0, keepdims=True) + SOFTMAX_EPS), 0.0)
            p = e * inv
            p_sum = p if p_sum is None else p_sum + p
            p_x.append(p.astype(BF16))
        o_cmp.append(jnp.dot(vct_ref[...], jnp.concatenate(p_x, axis=1),
                             preferred_element_type=F32))
    p_hi = p_sum.astype(BF16)
    p_lo = (p_sum - p_hi.astype(F32)).astype(BF16)
    ovt = ovt_ref[...]
    imp = (jnp.dot(ovt, p_hi, preferred_element_type=F32)
           + jnp.dot(ovt, p_lo, preferred_element_type=F32))

    tile0 = qi * (tq // LANES)
    vwb = jnp.concatenate([vwt_ref[tile0 + r] for r in range(wlen // LANES)], axis=1)
    o_win = []
    for x in range(2):
        s = s_win[x] + tabw_ref[:, x * hw:(x + 1) * hw]
        p = jnp.exp2(s - jnp.max(s, axis=0, keepdims=True))
        pv = jnp.dot(vwb, p.astype(BF16), preferred_element_type=F32)
        o_win.append(pv[0:NSA_DV] / (pv[NSA_DV:NSA_DV + 1] + SOFTMAX_EPS))

    blk = lax.broadcasted_iota(jnp.int32, (n_s, tq), 0)
    cur = (t0 + lax.broadcasted_iota(jnp.int32, (n_s, tq), 1)) // SLC_LEN
    forced = (blk == 0) | (blk == cur) | (blk == cur - 1)
    imp = jnp.where(forced, NEG_BIG, imp)
    imp = jnp.where(blk > cur, -NEG_BIG, imp)
    sub = lax.broadcasted_iota(jnp.int32, (8, tq), 0)
    groups = [imp[8 * g:8 * g + 8] for g in range(n_s // 8)]
    ranks = [jnp.zeros((8, tq), F32) for _ in groups]
    for jp in range(n_s):
        row = imp[jp:jp + 1, :]
        for g, grp in enumerate(groups):
            ge = jnp.where(row >= grp, 1.0, 0.0)
            gt = jnp.where(row > grp, 1.0, 0.0)
            if 8 * g > jp:
                beats = ge
            elif 8 * g + 8 <= jp:
                beats = gt
            else:
                beats = jnp.where(sub + 8 * g > jp, ge, gt)
            ranks[g] = ranks[g] + beats
    sel_rows = jnp.where(jnp.concatenate(ranks, axis=0) < top_n, 0.0, MASK_VALUE)
    if n_s < aug:
        sel_rows = jnp.concatenate([sel_rows, jnp.zeros((aug - n_s, tq), F32)], axis=0)
    qt_sel = stacked_qt(sel_rows)

    jd = t0 // tk

    def scores(j, x):
        kb = j * tk if isinstance(j, int) else pl.multiple_of(j * tk, tk)
        return jnp.dot(kaug_ref[pl.ds(kb, tk), :], qt_sel[x], preferred_element_type=F32)

    def softmax(x, s, tab, key0):
        u = s + tab
        off = slope_ref[0:1, x * hw:(x + 1) * hw] * (key0 - t0).astype(F32)
        m_old = m_ref[x]
        m_new = jnp.maximum(m_old, jnp.max(u, axis=0, keepdims=True) + off)
        alpha = jnp.exp2(m_old - m_new)
        p = jnp.exp2(u - (m_new - off))
        m_ref[x] = m_new
        return alpha, p.astype(BF16)

    def accumulate(x, alpha, p, vt):
        acc_ref[x] = alpha * acc_ref[x] + jnp.dot(vt, p, preferred_element_type=F32)

    def full_tile(j):
        vt = jnp.concatenate([vst_ref[j * sub_tiles + r] for r in range(sub_tiles)], axis=1)
        alpha, p = softmax(0, s0_ref[...], tabs_ref[:, 0:hw], j * tk)
        s1 = scores(j, 1)
        accumulate(0, alpha, p, vt)
        alpha, p = softmax(1, s1, tabs_ref[:, hw:2 * hw], j * tk)
        s0_ref[...] = scores(j + 1, 0)
        accumulate(1, alpha, p, vt)

    def sub_tile(r, tab_ref):
        key0 = jd * tk + r * tq
        rows = pl.ds(r * tq if isinstance(r, int) else pl.multiple_of(r * tq, tq), tq)
        vt = vst_ref[jd * sub_tiles + r]
        alpha, p = softmax(0, s0_ref[rows, :], tab_ref[0:tq, 0:hw], key0)
        s1 = jnp.dot(kaug_ref[pl.ds(pl.multiple_of(key0, tq), tq), :], qt_sel[1],
                     preferred_element_type=F32)
        accumulate(0, alpha, p, vt)
        alpha, p = softmax(1, s1, tab_ref[0:tq, hw:2 * hw], key0)
        accumulate(1, alpha, p, vt)

    for x in range(2):
        m_ref[x] = jnp.full((1, hw), MASK_VALUE, F32)
        acc_ref[x] = jnp.zeros((va, hw), F32)
    s0_ref[...] = scores(0, 0)

    def body(j, carry):
        full_tile(j)
        return carry

    lax.fori_loop(0, jd, body, 0)
    own = (t0 - jd * tk) // tq
    for r in range(sub_tiles - 1):
        @pl.when(r < own)
        def _():
            sub_tile(r, tabs_ref)
    sub_tile(own, tabd_ref)
    o_slc = [acc_ref[x, 0:NSA_DV, :] / (acc_ref[x, NSA_DV:NSA_DV + 1, :] + SOFTMAX_EPS)
             for x in range(2)]

    gates = _sigmoid(g_ref[0]).T
    z = z_ref[0]
    for h in range(heads):
        x, lanes = h // 2, slice((h % 2) * tq, (h % 2 + 1) * tq)
        c = NSA_DK + 3 * h
        o = (gates[c:c + 1] * o_cmp[x][:, lanes] + gates[c + 1:c + 2] * o_slc[x][:, lanes]
             + gates[c + 2:c + 3] * o_win[x][:, lanes])
        zh = z[:, h * NSA_DV:(h + 1) * NSA_DV]
        o_ref[0, :, h * NSA_DV:(h + 1) * NSA_DV] = (o.T * _silu(zh)).astype(BF16)


def _nsa_attn(hb, hf, k_cmp, v_cmp, tables, tq, tk):
    ovt, onehot, tab_sel, tab_diag, tab_win, tab_cmp, slope_rows = tables
    b, s, _ = hb.shape
    ncp = k_cmp.shape[1]
    n_s = s // SLC_LEN
    assert n_s <= HEAD_PAD - NSA_DK and n_s % 8 == 0, "selection blocks must fit the spare rows"
    assert tq & (tq - 1) == 0 and tq % LANES == 0 and tk % tq == 0 and s % tk == 0
    kern = functools.partial(_nsa_attn_kernel, tq=tq, tk=tk, n_c=ncp - 1, n_s=n_s,
                             top_n=min(SLC_TOPN, n_s))
    qw = NSA_HEADS * HEAD_PAD
    const2 = lambda a: pl.BlockSpec(a.shape, lambda bi, i: (0, 0), pipeline_mode=pl.Buffered(1))
    return pl.pallas_call(
        kern,
        grid=(b, s // tq),
        in_specs=[
            pl.BlockSpec((1, tq, qw), lambda bi, i: (bi, i, B_QNSA // qw)),
            pl.BlockSpec((1, tq, HEAD_PAD), lambda bi, i: (bi, i, F_KC // HEAD_PAD)),
            pl.BlockSpec((1, tq, NSA_WIDTH), lambda bi, i: (bi, i, F_ZNSA // NSA_WIDTH)),
            pl.BlockSpec((1, ncp, HEAD_PAD), lambda bi, i: (bi, 0, 0)),
            pl.BlockSpec((1, ncp, NSA_DV), lambda bi, i: (bi, 0, 0)),
            pl.BlockSpec((1, s, HEAD_PAD), lambda bi, i: (bi, 0, B_KS // HEAD_PAD)),
            pl.BlockSpec((1, s, NSA_DV), lambda bi, i: (bi, 0, B_VS // NSA_DV)),
            pl.BlockSpec((1, s, HEAD_PAD), lambda bi, i: (bi, 0, B_KW // HEAD_PAD)),
            pl.BlockSpec((1, s, NSA_DV), lambda bi, i: (bi, 0, B_VW // NSA_DV)),
            const2(ovt), const2(onehot), const2(tab_sel), const2(tab_diag), const2(tab_win),
            const2(tab_cmp), const2(slope_rows),
        ],
        out_specs=pl.BlockSpec((1, tq, NSA_WIDTH), lambda bi, i: (bi, i, 0)),
        out_shape=jax.ShapeDtypeStruct((b, s, NSA_WIDTH), BF16),
        scratch_shapes=[
            pltpu.VMEM((s, HEAD_PAD), BF16),
            pltpu.VMEM((s + WIN, HEAD_PAD), BF16),
            pltpu.VMEM((s // tq, NSA_DV + SUM_ROWS, tq), BF16),
            pltpu.VMEM(((s + WIN) // LANES, NSA_DV + SUM_ROWS, LANES), BF16),
            pltpu.VMEM((NSA_DV, ncp), BF16),
            pltpu.VMEM((2, 1, 2 * tq), F32),
            pltpu.VMEM((2, NSA_DV + SUM_ROWS, 2 * tq), F32),
            pltpu.VMEM((tk, 2 * tq), F32),
        ],
        compiler_params=_params("arbitrary", "arbitrary"),
        name="nsa_attn",
    )(hb, hf, hf, k_cmp, v_cmp, hb, hb, hb, hb, ovt, onehot, tab_sel, tab_diag, tab_win, tab_cmp,
      slope_rows)


def _mem_attn_kernel(q_ref, k_ref, v_ref, z_ref, o_ref):
    q = q_ref[0]
    k = k_ref[0]
    v = v_ref[0]
    z = z_ref[0]
    for h in range(MEM_HEADS):
        sl = slice(h * MEM_DH, (h + 1) * MEM_DH)
        s = lax.dot_general(q[:, sl], k[:, sl], _NT, preferred_element_type=F32)
        p = jnp.exp(s - jnp.max(s, axis=-1, keepdims=True))
        o = jnp.dot(p.astype(BF16), v[:, sl], preferred_element_type=F32)
        o = o / jnp.sum(p, axis=-1, keepdims=True)
        o_ref[0, :, sl] = (o * _silu(z[:, sl])).astype(BF16)


def _mem_attn(hb, mem_kv, hf, tq):
    b, s, _ = hb.shape
    mlen = mem_kv.shape[1]
    return pl.pallas_call(
        _mem_attn_kernel,
        grid=(b, s // tq),
        in_specs=[
            pl.BlockSpec((1, tq, MEM_WIDTH), lambda bi, i: (bi, i, B_QMEM // MEM_WIDTH)),
            pl.BlockSpec((1, mlen, MEM_WIDTH), lambda bi, i: (bi, 0, 0)),
            pl.BlockSpec((1, mlen, MEM_WIDTH), lambda bi, i: (bi, 0, 1)),
            pl.BlockSpec((1, tq, MEM_WIDTH), lambda bi, i: (bi, i, F_ZMEM // MEM_WIDTH)),
        ],
        out_specs=pl.BlockSpec((1, tq, MEM_WIDTH), lambda bi, i: (bi, i, 0)),
        out_shape=jax.ShapeDtypeStruct((b, s, MEM_WIDTH), BF16),
        compiler_params=_params("arbitrary", "arbitrary"),
        name="mem_attn",
    )(hb, mem_kv, mem_kv, hf)


def _out_proj_kernel(x_ref, a_ref, n_ref, m_ref, w_ref, g_ref, o_ref, *, final_norm):
    y = x_ref[...]
    y = y + jnp.dot(a_ref[...], w_ref[0:MLA_WIDTH, :], preferred_element_type=F32)
    y = y + jnp.dot(n_ref[...], w_ref[MLA_WIDTH:MLA_WIDTH + NSA_WIDTH, :], preferred_element_type=F32)
    y = y + jnp.dot(m_ref[...], w_ref[MLA_WIDTH + NSA_WIDTH:, :], preferred_element_type=F32)
    if final_norm:
        y = _rmsnorm(y, g_ref[...])
    o_ref[...] = y


def _out_proj(x, o_mla, o_nsa, o_mem, w_out, g, final_norm, tm):
    m, d = x.shape
    kern = functools.partial(_out_proj_kernel, final_norm=final_norm)
    row = lambda width: pl.BlockSpec((tm, width), lambda i: (i, 0))
    return pl.pallas_call(
        kern,
        grid=(m // tm,),
        in_specs=[row(d), row(MLA_WIDTH), row(NSA_WIDTH), row(MEM_WIDTH),
                  pl.BlockSpec(w_out.shape, lambda i: (0, 0)),
                  pl.BlockSpec((1, d), lambda i: (0, 0))],
        out_specs=row(d),
        out_shape=jax.ShapeDtypeStruct((m, d), F32),
        compiler_params=_params("arbitrary"),
        name="out_proj",
    )(x, o_mla, o_nsa, o_mem, w_out, g.reshape(1, d))


def _pad_cols(w, width):
    return jnp.pad(w, ((0, 0), (0, width - w.shape[1])))


def _w_in_moves():
    names = ("c_q", "c_kv", "k_rope", "z_mla", "q_nsa", "k_c", "v_c", "k_s", "v_s", "k_w", "v_w",
             "g_nsa", "z_nsa", "q_mem", "z_mem")
    src, off = {}, 0
    for name, n in zip(names, IN_SPLITS):
        src[name] = (off, n)
        off += n
    dst = {"c_q": (0, F_CQ), "c_kv": (0, F_CKV), "z_mla": (0, F_Z), "z_nsa": (0, F_ZNSA),
           "z_mem": (0, F_ZMEM), "k_c": (0, F_KC), "g_nsa": (0, F_GATE), "v_c": (0, F_VC),
           "k_rope": (0, F_KROPE), "q_mem": (1, B_QMEM), "k_s": (1, B_KS), "k_w": (1, B_KW),
           "v_s": (1, B_VS), "v_w": (1, B_VW)}
    moves = [(src[n][0], src[n][1], dst[n][0], dst[n][1]) for n in dst]
    for h in range(NSA_HEADS):
        moves.append((src["q_nsa"][0] + h * NSA_DK, NSA_DK, 1, B_QNSA + h * HEAD_PAD))
    return moves


def _w_prep_kernel(wt_ref, wf_ref, wb_ref):
    outs = (wf_ref, wb_ref)
    wf_ref[...] = jnp.zeros(wf_ref.shape, BF16)
    wb_ref[...] = jnp.zeros(wb_ref.shape, BF16)
    for s0, width, which, d0 in _w_in_moves():
        tn = outs[which].shape[2]
        while width > 0:
            t, c = divmod(d0, tn)
            n = min(width, tn - c)
            rows = -(-n // LANES) * LANES
            start = min(s0, wt_ref.shape[0] - rows)
            slab = wt_ref[start:start + rows, :].T
            outs[which][t, :, c:c + n] = slab[:, s0 - start:s0 - start + n].astype(BF16)
            s0, d0, width = s0 + n, d0 + n, width - n


def _layout_w_in(w_in, tnf, tnb, tr=256):
    d, n = w_in.shape
    nf, nb = F_WIDTH // tnf, B_WIDTH // tnb
    wf, wb = pl.pallas_call(
        _w_prep_kernel,
        grid=(d // tr,),
        in_specs=[pl.BlockSpec((n, tr), lambda i: (0, i))],
        out_specs=[pl.BlockSpec((nf, tr, tnf), lambda i: (0, i, 0)),
                   pl.BlockSpec((nb, tr, tnb), lambda i: (0, i, 0))],
        out_shape=[jax.ShapeDtypeStruct((nf, d, tnf), BF16), jax.ShapeDtypeStruct((nb, d, tnb), BF16)],
        compiler_params=_params("arbitrary"),
        name="w_in_layout",
    )(w_in.T)
    scale_b = jnp.concatenate([
        jnp.full((NSA_HEADS * HEAD_PAD,), LOG2E * NSA_DK ** -0.5, F32),
        jnp.full((MEM_WIDTH,), MEM_DH ** -0.5, F32),
        jnp.ones((B_WIDTH - B_QMEM - MEM_WIDTH,), F32)])
    return wf, wb, scale_b


def _rope_tables(seq):
    pos = jnp.arange(seq, dtype=F32)
    inv_freq = ROPE_THETA ** (-jnp.arange(0, MLA_ROPE, 2, dtype=F32) / MLA_ROPE)
    ang = pos[:, None] * inv_freq[None, :]
    zeros = jnp.zeros((seq, LANES - MLA_ROPE), F32)
    cos2 = jnp.concatenate([jnp.cos(ang), jnp.cos(ang), zeros], axis=1)
    sin2 = jnp.concatenate([jnp.sin(ang), jnp.sin(ang), zeros], axis=1)
    return cos2, sin2, jnp.cos(ang).T, jnp.sin(ang).T


def _nsa_tables(seq, tq, tk):
    chunks = seq // CMP_STRIDE
    n_s = seq // SLC_LEN
    c_start = jnp.arange(chunks) * CMP_STRIDE
    s_start = jnp.arange(n_s) * SLC_LEN
    overlap_t = ((c_start[None, :] < s_start[:, None] + SLC_LEN)
                 & (c_start[None, :] + CMP_LEN > s_start[:, None])
                 & (jnp.arange(chunks)[None, :] < chunks - 1))
    key_block = jnp.arange(seq) // SLC_LEN
    onehot = jnp.arange(HEAD_PAD)[None, :] == (NSA_DK + key_block)[:, None]
    slope = jnp.repeat(jnp.array([_alibi_slope(h) for h in range(NSA_HEADS)], F32) * LOG2E, tq)
    q_lane = jnp.tile(jnp.arange(tq), NSA_HEADS)
    tab_sel = jnp.arange(tk, dtype=F32)[:, None] * slope[None, :]
    rel = q_lane[None, :] - jnp.arange(WIN + tq)[:, None] + WIN
    tab_win = jnp.where((rel >= 0) & (rel < WIN), -slope[None, :] * rel.astype(F32), MASK_VALUE)
    slope_rows = jnp.broadcast_to(slope[None, :], (8, NSA_HEADS * tq))
    tab_diag = jnp.where(jnp.arange(tq)[:, None] <= q_lane[None, :], tab_sel[0:tq], MASK_VALUE)
    d_blk = jnp.arange(2 * chunks)[:, None] - chunks
    seen = d_blk * CMP_STRIDE + (CMP_LEN - 1) <= q_lane[None, :]
    dist = (q_lane[None, :] - d_blk * CMP_STRIDE).astype(F32) - (CMP_LEN - 1) / 2.0
    tab_cmp = jnp.where(seen, -slope[None, :] * dist, MASK_VALUE)
    return overlap_t.astype(BF16), onehot.astype(BF16), tab_sel, tab_diag, tab_win, tab_cmp, slope_rows


def _layer(x2, mem2, batch, seq, tables, norm_g, w_in, q_norm_g, w_uq, kv_norm_g, w_ukv,
           cmp_pe_k, cmp_pe_v, cmp_w1k, cmp_w2k, cmp_w1v, cmp_w2v, mem_norm_g, w_mem_kv, w_out,
           final_g, final_norm):
    rope, nsa_tables = tables
    d = x2.shape[1]
    wf, wb, scale_b = _layout_w_in(w_in, F_WIDTH // 2, B_WIDTH)
    hf, hb = _in_proj(x2, norm_g, wf, wb, scale_b, 512)

    wq = jnp.pad(w_uq.reshape(MLA_Q_RANK, MLA_HEADS, MLA_NOPE + MLA_ROPE),
                 ((0, 0), (0, 0), (0, HEAD_PAD - MLA_NOPE - MLA_ROPE)))
    wqt = wq.reshape(MLA_Q_RANK, MLA_HEADS * HEAD_PAD).T.astype(BF16)
    wkv = w_ukv.reshape(MLA_KV_RANK, MLA_HEADS, MLA_NOPE + MLA_V)
    wk = wkv[:, :, :MLA_NOPE].reshape(MLA_KV_RANK, MLA_HEADS * MLA_NOPE).astype(BF16)
    wvt = wkv[:, :, MLA_NOPE:].reshape(MLA_KV_RANK, MLA_WIDTH).T.astype(BF16)
    qt, k, vt = _mla_up(hf, q_norm_g, kv_norm_g, wqt, wk, wvt, rope, seq, MLA_TILE)
    hf3 = hf.reshape(batch, seq, F_WIDTH)
    hb3 = hb.reshape(batch, seq, B_WIDTH)
    o_mla = _mla_attn(qt, k.reshape(batch, seq, -1), vt, hf3, MLA_TILE, 4)

    pad_k = HEAD_PAD - NSA_DK
    k_cmp, v_cmp = _compress(
        hf3, _pad_cols(cmp_pe_k, HEAD_PAD), cmp_pe_v,
        jnp.pad(cmp_w1k.reshape(CMP_LEN, NSA_DK, NSA_DK), ((0, 0), (0, pad_k), (0, pad_k))).astype(BF16),
        jnp.pad(cmp_w2k, ((0, pad_k), (0, pad_k))).astype(BF16),
        cmp_w1v.reshape(CMP_LEN, NSA_DV, NSA_DV).astype(BF16), cmp_w2v.astype(BF16))
    o_nsa = _nsa_attn(hb3, hf3, k_cmp, v_cmp, nsa_tables, NSA_TQ, NSA_TK)

    mem_kv = _norm_proj(mem2, mem_norm_g, w_mem_kv.astype(BF16), jnp.ones((2 * MEM_WIDTH,), F32),
                        BF16, mem2.shape[0] // batch, MEM_WIDTH, "mem_kv_proj")
    o_mem = _mem_attn(hb3, mem_kv.reshape(batch, -1, 2 * MEM_WIDTH), hf3, 512)

    return _out_proj(x2, o_mla.reshape(-1, MLA_WIDTH), o_nsa.reshape(-1, NSA_WIDTH),
                     o_mem.reshape(-1, MEM_WIDTH), w_out.astype(BF16), final_g, final_norm, 512)


def kernel(x, mem, norm_g, w_in, q_norm_g, w_uq, kv_norm_g, w_ukv, cmp_pe_k, cmp_pe_v,
           cmp_w1k, cmp_w2k, cmp_w1v, cmp_w2v, mem_norm_g, w_mem_kv, w_out, final_norm_g):
    batch, seq, d = x.shape
    depth = norm_g.shape[0]
    tables = (_rope_tables(seq), _nsa_tables(seq, NSA_TQ, NSA_TK))
    x2 = x.reshape(batch * seq, d)
    mem2 = mem.reshape(batch * mem.shape[1], d)
    for l in range(depth):
        x2 = _layer(x2, mem2, batch, seq, tables, norm_g[l], w_in[l], q_norm_g[l], w_uq[l],
                    kv_norm_g[l], w_ukv[l], cmp_pe_k[l], cmp_pe_v[l], cmp_w1k[l], cmp_w2k[l],
                    cmp_w1v[l], cmp_w2v[l], mem_norm_g[l], w_mem_kv[l], w_out[l],
                    final_norm_g, l == depth - 1)
    return x2.reshape(batch, seq, d)
```

```python
import functools

import jax
import jax.numpy as jnp
from jax import lax
from jax.experimental import pallas as pl
from jax.experimental.pallas import tpu as pltpu

F32 = jnp.float32
BF16 = jnp.bfloat16

EPS = 1e-6
NEG_BIG = 1e9
MASK_VALUE = -1e30
SOFTMAX_EPS = 1e-20
LOG2E = 1.4426950408889634

MLA_HEADS = 8
MLA_NOPE = 128
MLA_ROPE = 64
MLA_V = 128
MLA_Q_RANK = 512
MLA_KV_RANK = 512
ROPE_THETA = 10000.0

NSA_HEADS = 4
NSA_DK = 192
NSA_DV = 128
CMP_LEN = 32
CMP_STRIDE = 16
SLC_LEN = 64
SLC_TOPN = 16
WIN = 512

MEM_HEADS = 4
MEM_DH = 128

MLA_WIDTH = MLA_HEADS * MLA_V
NSA_WIDTH = NSA_HEADS * NSA_DV
MEM_WIDTH = MEM_HEADS * MEM_DH

IN_SPLITS = (
    MLA_Q_RANK, MLA_KV_RANK, MLA_ROPE, MLA_WIDTH,
    NSA_HEADS * NSA_DK, NSA_DK, NSA_DV, NSA_DK, NSA_DV,
    NSA_DK, NSA_DV, 3 * NSA_HEADS, NSA_WIDTH,
    MEM_WIDTH, MEM_WIDTH,
)

LANES = 128
MXU_COLS = 256
HEAD_PAD = 256
VMEM_LIMIT = 56 * 1024 * 1024
MLA_TILE = 512
SUM_ROWS = 16
NSA_TQ = 256
NSA_TK = 512

F_CQ, F_CKV, F_KC, F_VC, F_KROPE = 0, 512, 1024, 1280, 1408
F_GATE = F_KC + NSA_DK
F_WIDTH = 1536
G_ZMLA, G_ZNSA, G_ZMEM = 0, MLA_WIDTH, MLA_WIDTH + NSA_WIDTH
G_WIDTH = MLA_WIDTH + NSA_WIDTH + MEM_WIDTH
B_QNSA, B_KS, B_QMEM, B_KW, B_VS, B_VW = 0, 768, 1024, 1536, 1792, 1920
B_WIDTH = 2048

_NT = (((1,), (1,)), ((), ()))


def _params(*sem):
    return pltpu.CompilerParams(dimension_semantics=sem, vmem_limit_bytes=VMEM_LIMIT)


def _sigmoid(x):
    return 1.0 / (1.0 + jnp.exp(-x))


def _silu(x):
    return x * _sigmoid(x)


def _rmsnorm(x, g):
    ms = jnp.mean(x * x, axis=-1, keepdims=True)
    return (x * lax.rsqrt(ms + EPS)) * g


def _norm_proj_kernel(x_ref, g_ref, w_ref, cs_ref, o_ref, xn_ref):
    @pl.when(pl.program_id(1) == 0)
    def _():
        xn_ref[...] = _rmsnorm(x_ref[...], g_ref[...]).astype(BF16)

    acc = jnp.dot(xn_ref[...], w_ref[...], preferred_element_type=F32)
    o_ref[...] = (acc * cs_ref[...]).astype(o_ref.dtype)


def _norm_proj(x, g, w, colscale, out_dtype, tm, tn, name):
    m, k = x.shape
    n = w.shape[1]
    return pl.pallas_call(
        _norm_proj_kernel,
        grid=(m // tm, n // tn),
        in_specs=[
            pl.BlockSpec((tm, k), lambda i, j: (i, 0)),
            pl.BlockSpec((1, k), lambda i, j: (0, 0)),
            pl.BlockSpec((k, tn), lambda i, j: (0, j)),
            pl.BlockSpec((1, tn), lambda i, j: (0, j)),
        ],
        out_specs=pl.BlockSpec((tm, tn), lambda i, j: (i, j)),
        out_shape=jax.ShapeDtypeStruct((m, n), out_dtype),
        scratch_shapes=[pltpu.VMEM((tm, k), BF16)],
        compiler_params=_params("arbitrary", "arbitrary"),
        name=name,
    )(x, g.reshape(1, k), w, colscale.reshape(1, n))


def _in_proj_kernel(x_ref, g_ref, wf_ref, wg_ref, wb_ref, cs_ref, of_ref, og_ref, ob_ref, xn_ref):
    j = pl.program_id(1)

    @pl.when(j == 0)
    def _():
        xn_ref[...] = _rmsnorm(x_ref[...], g_ref[...]).astype(BF16)
        of_ref[...] = jnp.dot(xn_ref[...], wf_ref[...], preferred_element_type=F32)

    @pl.when(j == 1)
    def _():
        og_ref[...] = _silu(jnp.dot(xn_ref[...], wg_ref[...], preferred_element_type=F32)).astype(BF16)

    @pl.when(j == 2)
    def _():
        acc = jnp.dot(xn_ref[...], wb_ref[...], preferred_element_type=F32)
        ob_ref[...] = (acc * cs_ref[...]).astype(BF16)


def _in_proj(x, g, wf, wg, wb, scale_b, tm):
    m, k = x.shape
    whole = lambda a: pl.BlockSpec(a.shape, lambda i, j: (0, 0), pipeline_mode=pl.Buffered(1))
    rows = lambda a: pl.BlockSpec((tm, a.shape[1]), lambda i, j: (i, 0))
    cs = scale_b.reshape(1, -1)
    return pl.pallas_call(
        _in_proj_kernel,
        grid=(m // tm, 3),
        in_specs=[rows(x), pl.BlockSpec((1, k), lambda i, j: (0, 0)),
                  whole(wf), whole(wg), whole(wb), whole(cs)],
        out_specs=[rows(wf), rows(wg), rows(wb)],
        out_shape=[jax.ShapeDtypeStruct((m, wf.shape[1]), F32),
                   jax.ShapeDtypeStruct((m, wg.shape[1]), BF16),
                   jax.ShapeDtypeStruct((m, wb.shape[1]), BF16)],
        scratch_shapes=[pltpu.VMEM((tm, k), BF16)],
        compiler_params=_params("arbitrary", "arbitrary"),
        name="in_proj",
    )(x, g.reshape(1, k), wf, wg, wb, cs)


def _rope_half(hi, cos2, sin2):
    up = pltpu.roll(hi, 32, axis=1)
    down = pltpu.roll(hi, 96, axis=1)
    return hi * cos2 + (up - down) * sin2


def _mla_up_kernel(cq_ref, ckv_ref, kr_ref, gq_ref, gkv_ref, wqt_ref, wk_ref, wvt_ref,
                   cos_ref, sin_ref, cost_ref, sint_ref, qt_ref, k_ref, vt_ref):
    scale = LOG2E * (MLA_NOPE + MLA_ROPE) ** -0.5
    half = MLA_ROPE // 2
    cos_t = cost_ref[...]
    sin_t = sint_ref[...]

    cqn = _rmsnorm(cq_ref[...], gq_ref[...]).astype(BF16)
    yt = lax.dot_general(wqt_ref[...], cqn, _NT, preferred_element_type=F32)
    for h in range(MLA_HEADS):
        r = h * HEAD_PAD
        x1 = yt[r + MLA_NOPE:r + MLA_NOPE + half]
        x2 = yt[r + MLA_NOPE + half:r + MLA_NOPE + MLA_ROPE]
        qt_ref[0, r:r + MLA_NOPE, :] = (yt[r:r + MLA_NOPE] * scale).astype(BF16)
        qt_ref[0, r + MLA_NOPE:r + MLA_NOPE + half, :] = (
            (x1 * cos_t - x2 * sin_t) * scale).astype(BF16)
        qt_ref[0, r + MLA_NOPE + half:r + MLA_NOPE + MLA_ROPE, :] = (
            (x1 * sin_t + x2 * cos_t) * scale).astype(BF16)
        qt_ref[0, r + MLA_NOPE + MLA_ROPE:r + HEAD_PAD, :] = jnp.zeros(
            (HEAD_PAD - MLA_NOPE - MLA_ROPE, yt.shape[1]), BF16)

    ckn = _rmsnorm(ckv_ref[...], gkv_ref[...]).astype(BF16)
    kn = jnp.dot(ckn, wk_ref[...], preferred_element_type=F32)
    k_pe = _rope_half(kr_ref[...], cos_ref[...], sin_ref[...]).astype(BF16)
    for h in range(MLA_HEADS):
        k_ref[:, h * HEAD_PAD:h * HEAD_PAD + LANES] = kn[:, h * LANES:(h + 1) * LANES].astype(BF16)
        k_ref[:, h * HEAD_PAD + LANES:(h + 1) * HEAD_PAD] = k_pe
    vt = lax.dot_general(wvt_ref[...], ckn, _NT, preferred_element_type=F32)
    ones = jnp.ones((SUM_ROWS, vt.shape[1]), BF16)
    for h in range(MLA_HEADS):
        r = h * (MLA_V + SUM_ROWS)
        vt_ref[0, r:r + MLA_V, :] = vt[h * MLA_V:(h + 1) * MLA_V].astype(BF16)
        vt_ref[0, r + MLA_V:r + MLA_V + SUM_ROWS, :] = ones


def _mla_up(hf, gq, gkv, wqt, wk, wvt, rope, seq, tm):
    cos2, sin2, cos_t, sin_t = rope
    m = hf.shape[0]
    steps_per_seq = seq // tm
    hq = MLA_HEADS * HEAD_PAD
    half = MLA_ROPE // 2
    const = lambda i: (0, 0)
    return pl.pallas_call(
        _mla_up_kernel,
        grid=(m // tm,),
        in_specs=[
            pl.BlockSpec((tm, MLA_Q_RANK), lambda i: (i, F_CQ // MLA_Q_RANK)),
            pl.BlockSpec((tm, MLA_KV_RANK), lambda i: (i, F_CKV // MLA_KV_RANK)),
            pl.BlockSpec((tm, LANES), lambda i: (i, F_KROPE // LANES)),
            pl.BlockSpec((1, MLA_Q_RANK), const),
            pl.BlockSpec((1, MLA_KV_RANK), const),
            pl.BlockSpec((hq, MLA_Q_RANK), const),
            pl.BlockSpec((MLA_KV_RANK, MLA_HEADS * MLA_NOPE), const),
            pl.BlockSpec((MLA_WIDTH, MLA_KV_RANK), const),
            pl.BlockSpec((tm, LANES), lambda i: (i % steps_per_seq, 0)),
            pl.BlockSpec((tm, LANES), lambda i: (i % steps_per_seq, 0)),
            pl.BlockSpec((half, tm), lambda i: (0, i % steps_per_seq)),
            pl.BlockSpec((half, tm), lambda i: (0, i % steps_per_seq)),
        ],
        out_specs=[
            pl.BlockSpec((1, hq, tm), lambda i: (i, 0, 0)),
            pl.BlockSpec((tm, hq), lambda i: (i, 0)),
            pl.BlockSpec((1, MLA_HEADS * (MLA_V + SUM_ROWS), tm), lambda i: (i, 0, 0)),
        ],
        out_shape=[
            jax.ShapeDtypeStruct((m // tm, hq, tm), BF16),
            jax.ShapeDtypeStruct((m, hq), BF16),
            jax.ShapeDtypeStruct((m // tm, MLA_HEADS * (MLA_V + SUM_ROWS), tm), BF16),
        ],
        compiler_params=_params("arbitrary"),
        name="mla_up",
    )(hf, hf, hf, gq.reshape(1, -1), gkv.reshape(1, -1), wqt, wk, wvt, cos2, sin2, cos_t, sin_t)


def _mla_attn_kernel(qt_ref, k_ref, vt_ref, z_ref, o_ref, m_ref, acc_ref, s0_ref, *,
                     tile_len, hps):
    t = tile_len
    va = MLA_V + SUM_ROWS
    qi = pl.program_id(2)
    qts = [qt_ref[0, h * HEAD_PAD:(h + 1) * HEAD_PAD, :] for h in range(hps)]

    def scores(j, h):
        kb = pl.multiple_of(j * t, t)
        return jnp.dot(k_ref[0, pl.ds(kb, t), h * HEAD_PAD:(h + 1) * HEAD_PAD], qts[h],
                       preferred_element_type=F32)

    def softmax(h, s, diagonal):
        if diagonal:
            kpos = lax.broadcasted_iota(jnp.int32, (t, t), 0)
            qpos = lax.broadcasted_iota(jnp.int32, (t, t), 1)
            visible = kpos <= qpos
            s = jnp.where(visible, s, MASK_VALUE)
        m = m_ref[h]
        m_new = jnp.maximum(m, jnp.max(s, axis=0, keepdims=True))
        alpha = jnp.exp2(m - m_new)
        p = jnp.exp2(s - m_new)
        m_ref[h] = m_new
        return alpha, p.astype(BF16)

    def accumulate(j, h, alpha, p):
        vt = vt_ref[j, h * va:(h + 1) * va, :]
        acc_ref[h] = alpha * acc_ref[h] + jnp.dot(vt, p, preferred_element_type=F32)

    def tile(j, diagonal):
        alpha, p = softmax(0, s0_ref[...], diagonal)
        for h in range(1, hps):
            s = scores(j, h)
            accumulate(j, h - 1, alpha, p)
            alpha, p = softmax(h, s, diagonal)
        if not diagonal:
            s0_ref[...] = scores(j + 1, 0)
        accumulate(j, hps - 1, alpha, p)

    for h in range(hps):
        m_ref[h] = jnp.full((1, t), MASK_VALUE, F32)
        acc_ref[h] = jnp.zeros((va, t), F32)
    s0_ref[...] = scores(0, 0)

    def body(j, carry):
        tile(j, False)
        return carry

    lax.fori_loop(0, qi, body, 0)

    tile(qi, True)
    for h in range(hps):
        l = acc_ref[h, MLA_V:MLA_V + 1, :]
        o = (acc_ref[h, 0:MLA_V, :] / (l + SOFTMAX_EPS)).T
        cols = slice(h * MLA_V, (h + 1) * MLA_V)
        o_ref[0, :, cols] = (o * z_ref[0, :, cols].astype(F32)).astype(BF16)


def _mla_attn(qt, k, vt, hg, tile_len, hps):
    b, s, _ = k.shape
    t = tile_len
    tiles = s // t
    kern = functools.partial(_mla_attn_kernel, tile_len=t, hps=hps)
    qw, vw, va = hps * HEAD_PAD, hps * MLA_V, MLA_V + SUM_ROWS
    return pl.pallas_call(
        kern,
        grid=(b, MLA_HEADS // hps, tiles),
        in_specs=[
            pl.BlockSpec((1, qw, t), lambda bi, h, i: (bi * tiles + i, h, 0)),
            pl.BlockSpec((1, s, qw), lambda bi, h, i: (bi, 0, h)),
            pl.BlockSpec((tiles, hps * va, t), lambda bi, h, i: (bi, h, 0)),
            pl.BlockSpec((1, t, vw), lambda bi, h, i: (bi, i, G_ZMLA // vw + h)),
        ],
        out_specs=pl.BlockSpec((1, t, vw), lambda bi, h, i: (bi, i, h)),
        out_shape=jax.ShapeDtypeStruct((b, s, MLA_WIDTH), BF16),
        scratch_shapes=[
            pltpu.VMEM((hps, 1, t), F32),
            pltpu.VMEM((hps, va, t), F32),
            pltpu.VMEM((t, t), F32),
        ],
        compiler_params=_params("arbitrary", "arbitrary", "arbitrary"),
        name="mla_attn",
    )(qt, k, vt, hg)


def _compress_one(x_refs, pe_ref, w1_ref, w2_ref, o_ref):
    chunks = x_refs[0].shape[1] // CMP_STRIDE
    a = b = None
    for i in range(CMP_STRIDE):
        for c, x_ref in enumerate(x_refs):
            lanes = slice(c * LANES, (c + 1) * LANES)
            x = x_ref[0, pl.ds(i, chunks, stride=CMP_STRIDE), :]
            ai = jnp.dot((x + pe_ref[i:i + 1, lanes]).astype(BF16), w1_ref[i, lanes, :],
                         preferred_element_type=F32)
            bi = jnp.dot((x + pe_ref[CMP_STRIDE + i:CMP_STRIDE + i + 1, lanes]).astype(BF16),
                         w1_ref[CMP_STRIDE + i, lanes, :], preferred_element_type=F32)
            a = ai if a is None else a + ai
            b = bi if b is None else b + bi
    h1 = a + pltpu.roll(b, chunks - 1, axis=0)
    o_ref[0] = jnp.dot(_silu(h1).astype(BF16), w2_ref[...], preferred_element_type=F32).astype(BF16)


def _compress_kernel(xk0_ref, xk1_ref, xv_ref, pek_ref, pev_ref, w1k_ref, w2k_ref, w1v_ref, w2v_ref,
                     ok_ref, ov_ref):
    _compress_one((xk0_ref, xk1_ref), pek_ref, w1k_ref, w2k_ref, ok_ref)
    _compress_one((xv_ref,), pev_ref, w1v_ref, w2v_ref, ov_ref)


def _compress(hf, pek, pev, w1k, w2k, w1v, w2v):
    b, s, _ = hf.shape
    chunks = s // CMP_STRIDE
    full = lambda a: pl.BlockSpec(a.shape, lambda bi: (0,) * a.ndim)
    return pl.pallas_call(
        _compress_kernel,
        grid=(b,),
        in_specs=[pl.BlockSpec((1, s, LANES), lambda bi: (bi, 0, F_KC // LANES)),
                  pl.BlockSpec((1, s, LANES), lambda bi: (bi, 0, F_KC // LANES + 1)),
                  pl.BlockSpec((1, s, NSA_DV), lambda bi: (bi, 0, F_VC // NSA_DV)),
                  full(pek), full(pev), full(w1k), full(w2k), full(w1v), full(w2v)],
        out_specs=[
            pl.BlockSpec((1, chunks, HEAD_PAD), lambda bi: (bi, 0, 0)),
            pl.BlockSpec((1, chunks, NSA_DV), lambda bi: (bi, 0, 0)),
        ],
        out_shape=[
            jax.ShapeDtypeStruct((b, chunks, HEAD_PAD), BF16),
            jax.ShapeDtypeStruct((b, chunks, NSA_DV), BF16),
        ],
        compiler_params=_params("arbitrary"),
        name="nsa_compress",
    )(hf, hf, hf, pek, pev, w1k, w2k, w1v, w2v)


def _alibi_slope(h):
    return 2.0 ** (-8.0 * (h + 1) / NSA_HEADS)


def _nsa_attn_kernel(q_ref, g_ref, z_ref, kc_ref, vc_ref, ks_ref, vs_ref, kw_ref, vw_ref,
                     ovt_ref, oh_ref, tabs_ref, tabd_ref, tabw_ref, tabc_ref, slope_ref, o_ref,
                     kaug_ref, kwp_ref, vst_ref, vwt_ref, vct_ref, m_ref, acc_ref, s0_ref,
                     *, tq, tk, n_c, n_s, top_n):
    qi = pl.program_id(1)
    t0 = qi * tq
    heads = NSA_HEADS
    seq = ks_ref.shape[1]
    ncp = kc_ref.shape[1]
    hw = 2 * tq
    aug = HEAD_PAD - NSA_DK
    wlen = WIN + tq
    va = NSA_DV + SUM_ROWS
    sub_tiles = tk // tq

    @pl.when(qi == 0)
    def _():
        kaug_ref[...] = ks_ref[0] + oh_ref[...]
        col = lax.broadcasted_iota(jnp.int32, (WIN, HEAD_PAD), 1)
        kwp_ref[0:WIN, :] = jnp.where(col == NSA_DK, 1.0, 0.0).astype(BF16)
        kwp_ref[WIN:, :] = kw_ref[0]
        for i in range(seq // tq):
            vst_ref[i, 0:NSA_DV, :] = vs_ref[0, i * tq:(i + 1) * tq, :].astype(F32).T.astype(BF16)
            vst_ref[i, NSA_DV:va, :] = jnp.ones((SUM_ROWS, tq), BF16)
        for i in range(WIN // LANES):
            vwt_ref[i] = jnp.zeros((va, LANES), BF16)
        for i in range(seq // LANES):
            vwt_ref[WIN // LANES + i, 0:NSA_DV, :] = (
                vw_ref[0, i * LANES:(i + 1) * LANES, :].astype(F32).T.astype(BF16))
            vwt_ref[WIN // LANES + i, NSA_DV:va, :] = jnp.ones((SUM_ROWS, LANES), BF16)
        vct_ref[...] = vc_ref[0].astype(F32).T.astype(BF16)

    q_t = q_ref[0].astype(F32).T
    qt = [q_t[h * NSA_DK:(h + 1) * NSA_DK] for h in range(heads)]

    def stacked_qt(extra):
        cols = [jnp.concatenate([qt[h], extra], axis=0) for h in range(heads)]
        return [jnp.concatenate(cols[2 * x:2 * x + 2], axis=1).astype(BF16) for x in range(2)]

    flag_rows = jnp.where(lax.broadcasted_iota(jnp.int32, (aug, tq), 0) == 0, MASK_VALUE, 0.0)
    qt_win = stacked_qt(flag_rows)

    kc = kc_ref[0]
    s_cmp = [jnp.dot(kc, qt_win[x], preferred_element_type=F32) for x in range(2)]
    kwb = kwp_ref[pl.ds(pl.multiple_of(t0, LANES), wlen), :]
    s_win = [jnp.dot(kwb, qt_win[x], preferred_element_type=F32) for x in range(2)]

    first_blk = qi * (tq // CMP_STRIDE)
    tab_c = tabc_ref[pl.ds(pl.multiple_of(ncp - first_blk, CMP_STRIDE), ncp), :]
    o_cmp, p_sum = [], None
    for x in range(2):
        p_x = []
        for hl in range(2):
            h = 2 * x + hl
            s = s_cmp[x][:, hl * tq:(hl + 1) * tq] + tab_c[:, h * tq:(h + 1) * tq]
            m = jnp.max(s, axis=0, keepdims=True)
            e = jnp.exp2(s - m)
            inv = jnp.where(m > 0.5 * MASK_VALUE,
                            1.0 / (jnp.sum(e, axis=0, keepdims=True) + SOFTMAX_EPS), 0.0)
            p = e * inv
            p_sum = p if p_sum is None else p_sum + p
            p_x.append(p.astype(BF16))
        o_cmp.append(jnp.dot(vct_ref[...], jnp.concatenate(p_x, axis=1),
                             preferred_element_type=F32))
    p_hi = p_sum.astype(BF16)
    p_lo = (p_sum - p_hi.astype(F32)).astype(BF16)
    ovt = ovt_ref[...]
    imp = (jnp.dot(ovt, p_hi, preferred_element_type=F32)
           + jnp.dot(ovt, p_lo, preferred_element_type=F32))

    tile0 = qi * (tq // LANES)
    vwb = jnp.concatenate([vwt_ref[tile0 + r] for r in range(wlen // LANES)], axis=1)
    o_win = []
    for x in range(2):
        s = s_win[x] + tabw_ref[:, x * hw:(x + 1) * hw]
        p = jnp.exp2(s - jnp.max(s, axis=0, keepdims=True))
        pv = jnp.dot(vwb, p.astype(BF16), preferred_element_type=F32)
        o_win.append(pv[0:NSA_DV] / (pv[NSA_DV:NSA_DV + 1] + SOFTMAX_EPS))

    blk = lax.broadcasted_iota(jnp.int32, (n_s, tq), 0)
    cur = (t0 + lax.broadcasted_iota(jnp.int32, (n_s, tq), 1)) // SLC_LEN
    forced = (blk == 0) | (blk == cur) | (blk == cur - 1)
    imp = jnp.where(forced, NEG_BIG, imp)
    imp = jnp.where(blk > cur, -NEG_BIG, imp)
    sub = lax.broadcasted_iota(jnp.int32, (8, tq), 0)
    groups = [imp[8 * g:8 * g + 8] for g in range(n_s // 8)]
    ranks = [jnp.zeros((8, tq), F32) for _ in groups]
    for jp in range(n_s):
        row = imp[jp:jp + 1, :]
        for g, grp in enumerate(groups):
            ge = jnp.where(row >= grp, 1.0, 0.0)
            gt = jnp.where(row > grp, 1.0, 0.0)
            if 8 * g > jp:
                beats = ge
            elif 8 * g + 8 <= jp:
                beats = gt
            else:
                beats = jnp.where(sub + 8 * g > jp, ge, gt)
            ranks[g] = ranks[g] + beats
    sel_rows = jnp.where(jnp.concatenate(ranks, axis=0) < top_n, 0.0, MASK_VALUE)
    if n_s < aug:
        sel_rows = jnp.concatenate([sel_rows, jnp.zeros((aug - n_s, tq), F32)], axis=0)
    qt_sel = stacked_qt(sel_rows)

    jd = t0 // tk

    def scores(j, x):
        kb = j * tk if isinstance(j, int) else pl.multiple_of(j * tk, tk)
        return jnp.dot(kaug_ref[pl.ds(kb, tk), :], qt_sel[x], preferred_element_type=F32)

    def softmax(x, s, tab, key0):
        u = s + tab
        off = slope_ref[0:1, x * hw:(x + 1) * hw] * (key0 - t0).astype(F32)
        m_old = m_ref[x]
        m_new = jnp.maximum(m_old, jnp.max(u, axis=0, keepdims=True) + off)
        alpha = jnp.exp2(m_old - m_new)
        p = jnp.exp2(u - (m_new - off))
        m_ref[x] = m_new
        return alpha, p.astype(BF16)

    def accumulate(x, alpha, p, vt):
        acc_ref[x] = alpha * acc_ref[x] + jnp.dot(vt, p, preferred_element_type=F32)

    def full_tile(j):
        vt = jnp.concatenate([vst_ref[j * sub_tiles + r] for r in range(sub_tiles)], axis=1)
        alpha, p = softmax(0, s0_ref[...], tabs_ref[:, 0:hw], j * tk)
        s1 = scores(j, 1)
        accumulate(0, alpha, p, vt)
        alpha, p = softmax(1, s1, tabs_ref[:, hw:2 * hw], j * tk)
        s0_ref[...] = scores(j + 1, 0)
        accumulate(1, alpha, p, vt)

    def sub_tile(r, tab_ref):
        key0 = jd * tk + r * tq
        rows = pl.ds(r * tq if isinstance(r, int) else pl.multiple_of(r * tq, tq), tq)
        vt = vst_ref[jd * sub_tiles + r]
        alpha, p = softmax(0, s0_ref[rows, :], tab_ref[0:tq, 0:hw], key0)
        s1 = jnp.dot(kaug_ref[pl.ds(pl.multiple_of(key0, tq), tq), :], qt_sel[1],
                     preferred_element_type=F32)
        accumulate(0, alpha, p, vt)
        alpha, p = softmax(1, s1, tab_ref[0:tq, hw:2 * hw], key0)
        accumulate(1, alpha, p, vt)

    for x in range(2):
        m_ref[x] = jnp.full((1, hw), MASK_VALUE, F32)
        acc_ref[x] = jnp.zeros((va, hw), F32)
    s0_ref[...] = scores(0, 0)

    def body(j, carry):
        full_tile(j)
        return carry

    lax.fori_loop(0, jd, body, 0)
    own = (t0 - jd * tk) // tq
    for r in range(sub_tiles - 1):
        @pl.when(r < own)
        def _():
            sub_tile(r, tabs_ref)
    sub_tile(own, tabd_ref)
    o_slc = [acc_ref[x, 0:NSA_DV, :] / (acc_ref[x, NSA_DV:NSA_DV + 1, :] + SOFTMAX_EPS)
             for x in range(2)]

    gates = _sigmoid(g_ref[0]).T
    z = z_ref[0]
    for h in range(heads):
        x, lanes = h // 2, slice((h % 2) * tq, (h % 2 + 1) * tq)
        c = NSA_DK + 3 * h
        o = (gates[c:c + 1] * o_cmp[x][:, lanes] + gates[c + 1:c + 2] * o_slc[x][:, lanes]
             + gates[c + 2:c + 3] * o_win[x][:, lanes])
        zh = z[:, h * NSA_DV:(h + 1) * NSA_DV]
        o_ref[0, :, h * NSA_DV:(h + 1) * NSA_DV] = (o.T * zh.astype(F32)).astype(BF16)


def _nsa_attn(hb, hf, hg, k_cmp, v_cmp, tables, tq, tk):
    ovt, onehot, tab_sel, tab_diag, tab_win, tab_cmp, slope_rows = tables
    b, s, _ = hb.shape
    ncp = k_cmp.shape[1]
    n_s = s // SLC_LEN
    assert n_s <= HEAD_PAD - NSA_DK and n_s % 8 == 0, "selection blocks must fit the spare rows"
    assert tq & (tq - 1) == 0 and tq % LANES == 0 and tk % tq == 0 and s % tk == 0
    kern = functools.partial(_nsa_attn_kernel, tq=tq, tk=tk, n_c=ncp - 1, n_s=n_s,
                             top_n=min(SLC_TOPN, n_s))
    qw = NSA_HEADS * NSA_DK
    const2 = lambda a: pl.BlockSpec(a.shape, lambda bi, i: (0, 0), pipeline_mode=pl.Buffered(1))
    return pl.pallas_call(
        kern,
        grid=(b, s // tq),
        in_specs=[
            pl.BlockSpec((1, tq, qw), lambda bi, i: (bi, i, B_QNSA // qw)),
            pl.BlockSpec((1, tq, HEAD_PAD), lambda bi, i: (bi, i, F_KC // HEAD_PAD)),
            pl.BlockSpec((1, tq, NSA_WIDTH), lambda bi, i: (bi, i, G_ZNSA // NSA_WIDTH)),
            pl.BlockSpec((1, ncp, HEAD_PAD), lambda bi, i: (bi, 0, 0)),
            pl.BlockSpec((1, ncp, NSA_DV), lambda bi, i: (bi, 0, 0)),
            pl.BlockSpec((1, s, HEAD_PAD), lambda bi, i: (bi, 0, B_KS // HEAD_PAD)),
            pl.BlockSpec((1, s, NSA_DV), lambda bi, i: (bi, 0, B_VS // NSA_DV)),
            pl.BlockSpec((1, s, HEAD_PAD), lambda bi, i: (bi, 0, B_KW // HEAD_PAD)),
            pl.BlockSpec((1, s, NSA_DV), lambda bi, i: (bi, 0, B_VW // NSA_DV)),
            const2(ovt), const2(onehot), const2(tab_sel), const2(tab_diag), const2(tab_win),
            const2(tab_cmp), const2(slope_rows),
        ],
        out_specs=pl.BlockSpec((1, tq, NSA_WIDTH), lambda bi, i: (bi, i, 0)),
        out_shape=jax.ShapeDtypeStruct((b, s, NSA_WIDTH), BF16),
        scratch_shapes=[
            pltpu.VMEM((s, HEAD_PAD), BF16),
            pltpu.VMEM((s + WIN, HEAD_PAD), BF16),
            pltpu.VMEM((s // tq, NSA_DV + SUM_ROWS, tq), BF16),
            pltpu.VMEM(((s + WIN) // LANES, NSA_DV + SUM_ROWS, LANES), BF16),
            pltpu.VMEM((NSA_DV, ncp), BF16),
            pltpu.VMEM((2, 1, 2 * tq), F32),
            pltpu.VMEM((2, NSA_DV + SUM_ROWS, 2 * tq), F32),
            pltpu.VMEM((tk, 2 * tq), F32),
        ],
        compiler_params=_params("arbitrary", "arbitrary"),
        name="nsa_attn",
    )(hb, hf, hg, k_cmp, v_cmp, hb, hb, hb, hb, ovt, onehot, tab_sel, tab_diag, tab_win, tab_cmp,
      slope_rows)


def _mem_attn_kernel(q_ref, k_ref, v_ref, z_ref, o_ref):
    q = q_ref[0]
    k = k_ref[0]
    v = v_ref[0]
    z = z_ref[0]
    for h in range(MEM_HEADS):
        sl = slice(h * MEM_DH, (h + 1) * MEM_DH)
        s = lax.dot_general(q[:, sl], k[:, sl], _NT, preferred_element_type=F32)
        p = jnp.exp(s - jnp.max(s, axis=-1, keepdims=True))
        o = jnp.dot(p.astype(BF16), v[:, sl], preferred_element_type=F32)
        o = o / jnp.sum(p, axis=-1, keepdims=True)
        o_ref[0, :, sl] = (o * z[:, sl].astype(F32)).astype(BF16)


def _mem_attn(hb, mem_kv, hg, tq):
    b, s, _ = hb.shape
    mlen = mem_kv.shape[1]
    return pl.pallas_call(
        _mem_attn_kernel,
        grid=(b, s // tq),
        in_specs=[
            pl.BlockSpec((1, tq, MEM_WIDTH), lambda bi, i: (bi, i, B_QMEM // MEM_WIDTH)),
            pl.BlockSpec((1, mlen, MEM_WIDTH), lambda bi, i: (bi, 0, 0)),
            pl.BlockSpec((1, mlen, MEM_WIDTH), lambda bi, i: (bi, 0, 1)),
            pl.BlockSpec((1, tq, MEM_WIDTH), lambda bi, i: (bi, i, G_ZMEM // MEM_WIDTH)),
        ],
        out_specs=pl.BlockSpec((1, tq, MEM_WIDTH), lambda bi, i: (bi, i, 0)),
        out_shape=jax.ShapeDtypeStruct((b, s, MEM_WIDTH), BF16),
        compiler_params=_params("arbitrary", "arbitrary"),
        name="mem_attn",
    )(hb, mem_kv, mem_kv, hg)


def _out_proj_kernel(x_ref, a_ref, n_ref, m_ref, w_ref, g_ref, o_ref, *, final_norm):
    y = x_ref[...]
    y = y + jnp.dot(a_ref[...], w_ref[0:MLA_WIDTH, :], preferred_element_type=F32)
    y = y + jnp.dot(n_ref[...], w_ref[MLA_WIDTH:MLA_WIDTH + NSA_WIDTH, :], preferred_element_type=F32)
    y = y + jnp.dot(m_ref[...], w_ref[MLA_WIDTH + NSA_WIDTH:, :], preferred_element_type=F32)
    if final_norm:
        y = _rmsnorm(y, g_ref[...])
    o_ref[...] = y


def _out_proj(x, o_mla, o_nsa, o_mem, w_out, g, final_norm, tm):
    m, d = x.shape
    kern = functools.partial(_out_proj_kernel, final_norm=final_norm)
    row = lambda width: pl.BlockSpec((tm, width), lambda i: (i, 0))
    return pl.pallas_call(
        kern,
        grid=(m // tm,),
        in_specs=[row(d), row(MLA_WIDTH), row(NSA_WIDTH), row(MEM_WIDTH),
                  pl.BlockSpec(w_out.shape, lambda i: (0, 0)),
                  pl.BlockSpec((1, d), lambda i: (0, 0))],
        out_specs=row(d),
        out_shape=jax.ShapeDtypeStruct((m, d), F32),
        compiler_params=_params("arbitrary"),
        name="out_proj",
    )(x, o_mla, o_nsa, o_mem, w_out, g.reshape(1, d))


def _pad_cols(w, width):
    return jnp.pad(w, ((0, 0), (0, width - w.shape[1])))


def _w_in_moves():
    names = ("c_q", "c_kv", "k_rope", "z_mla", "q_nsa", "k_c", "v_c", "k_s", "v_s", "k_w", "v_w",
             "g_nsa", "z_nsa", "q_mem", "z_mem")
    src, off = {}, 0
    for name, n in zip(names, IN_SPLITS):
        src[name] = (off, n)
        off += n
    dst = {"c_q": (0, F_CQ), "c_kv": (0, F_CKV), "k_c": (0, F_KC), "g_nsa": (0, F_GATE),
           "v_c": (0, F_VC), "k_rope": (0, F_KROPE),
           "z_mla": (1, G_ZMLA), "z_nsa": (1, G_ZNSA), "z_mem": (1, G_ZMEM),
           "q_nsa": (2, B_QNSA), "q_mem": (2, B_QMEM), "k_s": (2, B_KS), "k_w": (2, B_KW),
           "v_s": (2, B_VS), "v_w": (2, B_VW)}
    return [(src[n][0], src[n][1], dst[n][0], dst[n][1]) for n in dst]


def _w_prep_kernel(wt_ref, wf_ref, wg_ref, wb_ref):
    outs = (wf_ref, wg_ref, wb_ref)
    for o_ref in outs:
        o_ref[...] = jnp.zeros(o_ref.shape, BF16)
    for s0, width, which, d0 in _w_in_moves():
        rows = -(-width // LANES) * LANES
        start = min(s0, wt_ref.shape[0] - rows)
        slab = wt_ref[start:start + rows, :].T
        outs[which][:, d0:d0 + width] = slab[:, s0 - start:s0 - start + width].astype(BF16)


def _layout_w_in(w_in, tr=256):
    d, n = w_in.shape
    widths = (F_WIDTH, G_WIDTH, B_WIDTH)
    wf, wg, wb = pl.pallas_call(
        _w_prep_kernel,
        grid=(d // tr,),
        in_specs=[pl.BlockSpec((n, tr), lambda i: (0, i))],
        out_specs=[pl.BlockSpec((tr, w), lambda i: (i, 0)) for w in widths],
        out_shape=[jax.ShapeDtypeStruct((d, w), BF16) for w in widths],
        compiler_params=_params("arbitrary"),
        name="w_in_layout",
    )(w_in.T)
    scale_b = jnp.ones((B_WIDTH,), F32)
    scale_b = scale_b.at[B_QNSA:B_QNSA + NSA_HEADS * NSA_DK].set(LOG2E * NSA_DK ** -0.5)
    scale_b = scale_b.at[B_QMEM:B_QMEM + MEM_WIDTH].set(MEM_DH ** -0.5)
    return wf, wg, wb, scale_b


def _rope_tables(seq):
    pos = jnp.arange(seq, dtype=F32)
    inv_freq = ROPE_THETA ** (-jnp.arange(0, MLA_ROPE, 2, dtype=F32) / MLA_ROPE)
    ang = pos[:, None] * inv_freq[None, :]
    zeros = jnp.zeros((seq, LANES - MLA_ROPE), F32)
    cos2 = jnp.concatenate([jnp.cos(ang), jnp.cos(ang), zeros], axis=1)
    sin2 = jnp.concatenate([jnp.sin(ang), jnp.sin(ang), zeros], axis=1)
    return cos2, sin2, jnp.cos(ang).T, jnp.sin(ang).T


def _nsa_tables(seq, tq, tk):
    chunks = seq // CMP_STRIDE
    n_s = seq // SLC_LEN
    c_start = jnp.arange(chunks) * CMP_STRIDE
    s_start = jnp.arange(n_s) * SLC_LEN
    overlap_t = ((c_start[None, :] < s_start[:, None] + SLC_LEN)
                 & (c_start[None, :] + CMP_LEN > s_start[:, None])
                 & (jnp.arange(chunks)[None, :] < chunks - 1))
    key_block = jnp.arange(seq) // SLC_LEN
    onehot = jnp.arange(HEAD_PAD)[None, :] == (NSA_DK + key_block)[:, None]
    slope = jnp.repeat(jnp.array([_alibi_slope(h) for h in range(NSA_HEADS)], F32) * LOG2E, tq)
    q_lane = jnp.tile(jnp.arange(tq), NSA_HEADS)
    tab_sel = jnp.arange(tk, dtype=F32)[:, None] * slope[None, :]
    rel = q_lane[None, :] - jnp.arange(WIN + tq)[:, None] + WIN
    tab_win = jnp.where((rel >= 0) & (rel < WIN), -slope[None, :] * rel.astype(F32), MASK_VALUE)
    slope_rows = jnp.broadcast_to(slope[None, :], (8, NSA_HEADS * tq))
    tab_diag = jnp.where(jnp.arange(tq)[:, None] <= q_lane[None, :], tab_sel[0:tq], MASK_VALUE)
    d_blk = jnp.arange(2 * chunks)[:, None] - chunks
    seen = d_blk * CMP_STRIDE + (CMP_LEN - 1) <= q_lane[None, :]
    dist = (q_lane[None, :] - d_blk * CMP_STRIDE).astype(F32) - (CMP_LEN - 1) / 2.0
    tab_cmp = jnp.where(seen, -slope[None, :] * dist, MASK_VALUE)
    return overlap_t.astype(BF16), onehot.astype(BF16), tab_sel, tab_diag, tab_win, tab_cmp, slope_rows


def _layer(x2, mem2, batch, seq, tables, norm_g, w_in, q_norm_g, w_uq, kv_norm_g, w_ukv,
           cmp_pe_k, cmp_pe_v, cmp_w1k, cmp_w2k, cmp_w1v, cmp_w2v, mem_norm_g, w_mem_kv, w_out,
           final_g, final_norm):
    rope, nsa_tables = tables
    d = x2.shape[1]
    wf, wg, wb, scale_b = _layout_w_in(w_in)
    hf, hg, hb = _in_proj(x2, norm_g, wf, wg, wb, scale_b, 512)

    wq = jnp.pad(w_uq.reshape(MLA_Q_RANK, MLA_HEADS, MLA_NOPE + MLA_ROPE),
                 ((0, 0), (0, 0), (0, HEAD_PAD - MLA_NOPE - MLA_ROPE)))
    wqt = wq.reshape(MLA_Q_RANK, MLA_HEADS * HEAD_PAD).T.astype(BF16)
    wkv = w_ukv.reshape(MLA_KV_RANK, MLA_HEADS, MLA_NOPE + MLA_V)
    wk = wkv[:, :, :MLA_NOPE].reshape(MLA_KV_RANK, MLA_HEADS * MLA_NOPE).astype(BF16)
    wvt = wkv[:, :, MLA_NOPE:].reshape(MLA_KV_RANK, MLA_WIDTH).T.astype(BF16)
    qt, k, vt = _mla_up(hf, q_norm_g, kv_norm_g, wqt, wk, wvt, rope, seq, MLA_TILE)
    hf3 = hf.reshape(batch, seq, F_WIDTH)
    hg3 = hg.reshape(batch, seq, G_WIDTH)
    hb3 = hb.reshape(batch, seq, B_WIDTH)
    o_mla = _mla_attn(qt, k.reshape(batch, seq, -1), vt, hg3, MLA_TILE, 4)

    pad_k = HEAD_PAD - NSA_DK
    k_cmp, v_cmp = _compress(
        hf3, _pad_cols(cmp_pe_k, HEAD_PAD), cmp_pe_v,
        jnp.pad(cmp_w1k.reshape(CMP_LEN, NSA_DK, NSA_DK), ((0, 0), (0, pad_k), (0, pad_k))).astype(BF16),
        jnp.pad(cmp_w2k, ((0, pad_k), (0, pad_k))).astype(BF16),
        cmp_w1v.reshape(CMP_LEN, NSA_DV, NSA_DV).astype(BF16), cmp_w2v.astype(BF16))
    o_nsa = _nsa_attn(hb3, hf3, hg3, k_cmp, v_cmp, nsa_tables, NSA_TQ, NSA_TK)

    mem_kv = _norm_proj(mem2, mem_norm_g, w_mem_kv.astype(BF16), jnp.ones((2 * MEM_WIDTH,), F32),
                        BF16, mem2.shape[0] // batch, MEM_WIDTH, "mem_kv_proj")
    o_mem = _mem_attn(hb3, mem_kv.reshape(batch, -1, 2 * MEM_WIDTH), hg3, 512)

    return _out_proj(x2, o_mla.reshape(-1, MLA_WIDTH), o_nsa.reshape(-1, NSA_WIDTH),
                     o_mem.reshape(-1, MEM_WIDTH), w_out.astype(BF16), final_g, final_norm, 512)


def kernel(x, mem, norm_g, w_in, q_norm_g, w_uq, kv_norm_g, w_ukv, cmp_pe_k, cmp_pe_v,
           cmp_w1k, cmp_w2k, cmp_w1v, cmp_w2v, mem_norm_g, w_mem_kv, w_out, final_norm_g):
    batch, seq, d = x.shape
    depth = norm_g.shape[0]
    tables = (_rope_tables(seq), _nsa_tables(seq, NSA_TQ, NSA_TK))
    x2 = x.reshape(batch * seq, d)
    mem2 = mem.reshape(batch * mem.shape[1], d)
    for l in range(depth):
        x2 = _layer(x2, mem2, batch, seq, tables, norm_g[l], w_in[l], q_norm_g[l], w_uq[l],
                    kv_norm_g[l], w_ukv[l], cmp_pe_k[l], cmp_pe_v[l], cmp_w1k[l], cmp_w2k[l],
                    cmp_w1v[l], cmp_w2v[l], mem_norm_g[l], w_mem_kv[l], w_out[l],
                    final_norm_g, l == depth - 1)
    return x2.reshape(batch, seq, d)
```

```python
import functools

import jax
import jax.numpy as jnp
from jax import lax
from jax.experimental import pallas as pl
from jax.experimental.pallas import tpu as pltpu

F32 = jnp.float32
BF16 = jnp.bfloat16

EPS = 1e-6
NEG_BIG = 1e9
MASK_VALUE = -1e30
SOFTMAX_EPS = 1e-20
LOG2E = 1.4426950408889634

MLA_HEADS = 8
MLA_NOPE = 128
MLA_ROPE = 64
MLA_V = 128
MLA_Q_RANK = 512
MLA_KV_RANK = 512
ROPE_THETA = 10000.0

NSA_HEADS = 4
NSA_DK = 192
NSA_DV = 128
CMP_LEN = 32
CMP_STRIDE = 16
SLC_LEN = 64
SLC_TOPN = 16
WIN = 512

MEM_HEADS = 4
MEM_DH = 128

MLA_WIDTH = MLA_HEADS * MLA_V
NSA_WIDTH = NSA_HEADS * NSA_DV
MEM_WIDTH = MEM_HEADS * MEM_DH

IN_SPLITS = (
    MLA_Q_RANK, MLA_KV_RANK, MLA_ROPE, MLA_WIDTH,
    NSA_HEADS * NSA_DK, NSA_DK, NSA_DV, NSA_DK, NSA_DV,
    NSA_DK, NSA_DV, 3 * NSA_HEADS, NSA_WIDTH,
    MEM_WIDTH, MEM_WIDTH,
)

LANES = 128
MXU_COLS = 256
HEAD_PAD = 256
VMEM_LIMIT = 56 * 1024 * 1024
MLA_TILE = 512
SUM_ROWS = 16
NSA_TQ = 256
NSA_TK = 512

F_CQ, F_CKV, F_KC, F_VC, F_KROPE = 0, 512, 1024, 1280, 1408
F_GATE = F_KC + NSA_DK
F_WIDTH = 1536
G_ZMLA, G_ZNSA, G_ZMEM = 0, MLA_WIDTH, MLA_WIDTH + NSA_WIDTH
G_WIDTH = MLA_WIDTH + NSA_WIDTH + MEM_WIDTH
B_QNSA, B_KS, B_QMEM, B_KW, B_VS, B_VW = 0, 768, 1024, 1536, 1792, 1920
B_WIDTH = 2048

_NT = (((1,), (1,)), ((), ()))


def _params(*sem):
    return pltpu.CompilerParams(dimension_semantics=sem, vmem_limit_bytes=VMEM_LIMIT)


def _sigmoid(x):
    return 1.0 / (1.0 + jnp.exp(-x))


def _silu(x):
    return x * _sigmoid(x)


def _rmsnorm(x, g):
    ms = jnp.mean(x * x, axis=-1, keepdims=True)
    return (x * lax.rsqrt(ms + EPS)) * g


def _norm_proj_kernel(x_ref, g_ref, w_ref, cs_ref, o_ref, xn_ref):
    @pl.when(pl.program_id(1) == 0)
    def _():
        xn_ref[...] = _rmsnorm(x_ref[...], g_ref[...]).astype(BF16)

    acc = jnp.dot(xn_ref[...], w_ref[...], preferred_element_type=F32)
    o_ref[...] = (acc * cs_ref[...]).astype(o_ref.dtype)


def _norm_proj(x, g, w, colscale, out_dtype, tm, tn, name):
    m, k = x.shape
    n = w.shape[1]
    return pl.pallas_call(
        _norm_proj_kernel,
        grid=(m // tm, n // tn),
        in_specs=[
            pl.BlockSpec((tm, k), lambda i, j: (i, 0)),
            pl.BlockSpec((1, k), lambda i, j: (0, 0)),
            pl.BlockSpec((k, tn), lambda i, j: (0, j)),
            pl.BlockSpec((1, tn), lambda i, j: (0, j)),
        ],
        out_specs=pl.BlockSpec((tm, tn), lambda i, j: (i, j)),
        out_shape=jax.ShapeDtypeStruct((m, n), out_dtype),
        scratch_shapes=[pltpu.VMEM((tm, k), BF16)],
        compiler_params=_params("arbitrary", "arbitrary"),
        name=name,
    )(x, g.reshape(1, k), w, colscale.reshape(1, n))


def _in_proj_kernel(x_ref, g_ref, wf_ref, wg_ref, wb_ref, cs_ref, of_ref, og_ref, ob_ref, xn_ref):
    j = pl.program_id(1)

    @pl.when(j == 0)
    def _():
        xn_ref[...] = _rmsnorm(x_ref[...], g_ref[...]).astype(BF16)
        of_ref[...] = jnp.dot(xn_ref[...], wf_ref[...], preferred_element_type=F32)

    @pl.when(j == 1)
    def _():
        og_ref[...] = _silu(jnp.dot(xn_ref[...], wg_ref[...], preferred_element_type=F32)).astype(BF16)

    @pl.when(j == 2)
    def _():
        acc = jnp.dot(xn_ref[...], wb_ref[...], preferred_element_type=F32)
        ob_ref[...] = (acc * cs_ref[...]).astype(BF16)


def _in_proj(x, g, wf, wg, wb, scale_b, tm):
    m, k = x.shape
    whole = lambda a: pl.BlockSpec(a.shape, lambda i, j: (0, 0), pipeline_mode=pl.Buffered(1))
    rows = lambda a: pl.BlockSpec((tm, a.shape[1]), lambda i, j: (i, 0))
    cs = scale_b.reshape(1, -1)
    return pl.pallas_call(
        _in_proj_kernel,
        grid=(m // tm, 3),
        in_specs=[rows(x), pl.BlockSpec((1, k), lambda i, j: (0, 0)),
                  whole(wf), whole(wg), whole(wb), whole(cs)],
        out_specs=[rows(wf), rows(wg), rows(wb)],
        out_shape=[jax.ShapeDtypeStruct((m, wf.shape[1]), F32),
                   jax.ShapeDtypeStruct((m, wg.shape[1]), BF16),
                   jax.ShapeDtypeStruct((m, wb.shape[1]), BF16)],
        scratch_shapes=[pltpu.VMEM((tm, k), BF16)],
        compiler_params=_params("arbitrary", "arbitrary"),
        name="in_proj",
    )(x, g.reshape(1, k), wf, wg, wb, cs)


def _rope_half(hi, cos2, sin2):
    up = pltpu.roll(hi, 32, axis=1)
    down = pltpu.roll(hi, 96, axis=1)
    return hi * cos2 + (up - down) * sin2


def _mla_up_kernel(cq_ref, ckv_ref, kr_ref, gq_ref, gkv_ref, wqt_ref, wk_ref, wvt_ref,
                   cos_ref, sin_ref, cost_ref, sint_ref, qt_ref, k_ref, kpe_ref, vt_ref):
    scale = LOG2E * (MLA_NOPE + MLA_ROPE) ** -0.5
    half = MLA_ROPE // 2
    cos_t = cost_ref[...]
    sin_t = sint_ref[...]

    cqn = _rmsnorm(cq_ref[...], gq_ref[...]).astype(BF16)
    yt = lax.dot_general(wqt_ref[...], cqn, _NT, preferred_element_type=F32)
    for h in range(MLA_HEADS):
        r = h * HEAD_PAD
        x1 = yt[r + MLA_NOPE:r + MLA_NOPE + half]
        x2 = yt[r + MLA_NOPE + half:r + MLA_NOPE + MLA_ROPE]
        qt_ref[0, r:r + MLA_NOPE, :] = (yt[r:r + MLA_NOPE] * scale).astype(BF16)
        qt_ref[0, r + MLA_NOPE:r + MLA_NOPE + half, :] = (
            (x1 * cos_t - x2 * sin_t) * scale).astype(BF16)
        qt_ref[0, r + MLA_NOPE + half:r + MLA_NOPE + MLA_ROPE, :] = (
            (x1 * sin_t + x2 * cos_t) * scale).astype(BF16)
        qt_ref[0, r + MLA_NOPE + MLA_ROPE:r + HEAD_PAD, :] = jnp.zeros(
            (HEAD_PAD - MLA_NOPE - MLA_ROPE, yt.shape[1]), BF16)

    ckn = _rmsnorm(ckv_ref[...], gkv_ref[...]).astype(BF16)
    kn = jnp.dot(ckn, wk_ref[...], preferred_element_type=F32)
    k_pe = _rope_half(kr_ref[...], cos_ref[...], sin_ref[...]).astype(BF16)
    k_ref[...] = kn.astype(BF16)
    kpe_ref[...] = k_pe
    vt = lax.dot_general(wvt_ref[...], ckn, _NT, preferred_element_type=F32)
    ones = jnp.ones((SUM_ROWS, vt.shape[1]), BF16)
    for h in range(MLA_HEADS):
        r = h * (MLA_V + SUM_ROWS)
        vt_ref[0, r:r + MLA_V, :] = vt[h * MLA_V:(h + 1) * MLA_V].astype(BF16)
        vt_ref[0, r + MLA_V:r + MLA_V + SUM_ROWS, :] = ones


def _mla_up(hf, gq, gkv, wqt, wk, wvt, rope, seq, tm):
    cos2, sin2, cos_t, sin_t = rope
    m = hf.shape[0]
    steps_per_seq = seq // tm
    hq = MLA_HEADS * HEAD_PAD
    half = MLA_ROPE // 2
    const = lambda i: (0, 0)
    return pl.pallas_call(
        _mla_up_kernel,
        grid=(m // tm,),
        in_specs=[
            pl.BlockSpec((tm, MLA_Q_RANK), lambda i: (i, F_CQ // MLA_Q_RANK)),
            pl.BlockSpec((tm, MLA_KV_RANK), lambda i: (i, F_CKV // MLA_KV_RANK)),
            pl.BlockSpec((tm, LANES), lambda i: (i, F_KROPE // LANES)),
            pl.BlockSpec((1, MLA_Q_RANK), const),
            pl.BlockSpec((1, MLA_KV_RANK), const),
            pl.BlockSpec((hq, MLA_Q_RANK), const),
            pl.BlockSpec((MLA_KV_RANK, MLA_HEADS * MLA_NOPE), const),
            pl.BlockSpec((MLA_WIDTH, MLA_KV_RANK), const),
            pl.BlockSpec((tm, LANES), lambda i: (i % steps_per_seq, 0)),
            pl.BlockSpec((tm, LANES), lambda i: (i % steps_per_seq, 0)),
            pl.BlockSpec((half, tm), lambda i: (0, i % steps_per_seq)),
            pl.BlockSpec((half, tm), lambda i: (0, i % steps_per_seq)),
        ],
        out_specs=[
            pl.BlockSpec((1, hq, tm), lambda i: (i, 0, 0)),
            pl.BlockSpec((tm, MLA_HEADS * MLA_NOPE), lambda i: (i, 0)),
            pl.BlockSpec((tm, LANES), lambda i: (i, 0)),
            pl.BlockSpec((1, MLA_HEADS * (MLA_V + SUM_ROWS), tm), lambda i: (i, 0, 0)),
        ],
        out_shape=[
            jax.ShapeDtypeStruct((m // tm, hq, tm), BF16),
            jax.ShapeDtypeStruct((m, MLA_HEADS * MLA_NOPE), BF16),
            jax.ShapeDtypeStruct((m, LANES), BF16),
            jax.ShapeDtypeStruct((m // tm, MLA_HEADS * (MLA_V + SUM_ROWS), tm), BF16),
        ],
        compiler_params=_params("arbitrary"),
        name="mla_up",
    )(hf, hf, hf, gq.reshape(1, -1), gkv.reshape(1, -1), wqt, wk, wvt, cos2, sin2, cos_t, sin_t)


def _mla_attn_kernel(qt_ref, k_ref, kpe_ref, vt_ref, z_ref, o_ref, m_ref, acc_ref, s0_ref, *,
                     tile_len, hps):
    t = tile_len
    va = MLA_V + SUM_ROWS
    qi = pl.program_id(2)
    qts = [qt_ref[0, h * HEAD_PAD:(h + 1) * HEAD_PAD, :] for h in range(hps)]

    def scores(j, h):
        kb = pl.multiple_of(j * t, t)
        k = jnp.concatenate([k_ref[0, pl.ds(kb, t), h * MLA_NOPE:(h + 1) * MLA_NOPE],
                             kpe_ref[0, pl.ds(kb, t), :]], axis=1)
        return jnp.dot(k, qts[h], preferred_element_type=F32)

    def softmax(h, s, diagonal):
        if diagonal:
            kpos = lax.broadcasted_iota(jnp.int32, (t, t), 0)
            qpos = lax.broadcasted_iota(jnp.int32, (t, t), 1)
            visible = kpos <= qpos
            s = jnp.where(visible, s, MASK_VALUE)
        m = m_ref[h]
        m_new = jnp.maximum(m, jnp.max(s, axis=0, keepdims=True))
        alpha = jnp.exp2(m - m_new)
        p = jnp.exp2(s - m_new)
        m_ref[h] = m_new
        return alpha, p.astype(BF16)

    def accumulate(j, h, alpha, p):
        vt = vt_ref[j, h * va:(h + 1) * va, :]
        acc_ref[h] = alpha * acc_ref[h] + jnp.dot(vt, p, preferred_element_type=F32)

    def tile(j, diagonal):
        for h in range(hps):
            alpha, p = softmax(h, s0_ref[h], diagonal)
            accumulate(j, h, alpha, p)
            if not diagonal:
                s0_ref[h] = scores(j + 1, h)

    for h in range(hps):
        m_ref[h] = jnp.full((1, t), MASK_VALUE, F32)
        acc_ref[h] = jnp.zeros((va, t), F32)
        s0_ref[h] = scores(0, h)

    def body(j, carry):
        tile(j, False)
        return carry

    lax.fori_loop(0, qi, body, 0)

    tile(qi, True)
    for h in range(hps):
        l = acc_ref[h, MLA_V:MLA_V + 1, :]
        o = (acc_ref[h, 0:MLA_V, :] / (l + SOFTMAX_EPS)).T
        cols = slice(h * MLA_V, (h + 1) * MLA_V)
        o_ref[0, :, cols] = (o * z_ref[0, :, cols].astype(F32)).astype(BF16)


def _mla_attn(qt, k, kpe, vt, hg, tile_len, hps):
    b, s, _ = k.shape
    t = tile_len
    tiles = s // t
    kern = functools.partial(_mla_attn_kernel, tile_len=t, hps=hps)
    qw, vw, va = hps * HEAD_PAD, hps * MLA_V, MLA_V + SUM_ROWS
    return pl.pallas_call(
        kern,
        grid=(b, MLA_HEADS // hps, tiles),
        in_specs=[
            pl.BlockSpec((1, qw, t), lambda bi, h, i: (bi * tiles + i, h, 0)),
            pl.BlockSpec((1, s, hps * MLA_NOPE), lambda bi, h, i: (bi, 0, h)),
            pl.BlockSpec((1, s, LANES), lambda bi, h, i: (bi, 0, 0)),
            pl.BlockSpec((tiles, hps * va, t), lambda bi, h, i: (bi, h, 0)),
            pl.BlockSpec((1, t, vw), lambda bi, h, i: (bi, i, G_ZMLA // vw + h)),
        ],
        out_specs=pl.BlockSpec((1, t, vw), lambda bi, h, i: (bi, i, h)),
        out_shape=jax.ShapeDtypeStruct((b, s, MLA_WIDTH), BF16),
        scratch_shapes=[
            pltpu.VMEM((hps, 1, t), F32),
            pltpu.VMEM((hps, va, t), F32),
            pltpu.VMEM((hps, t, t), F32),
        ],
        compiler_params=_params("arbitrary", "arbitrary", "arbitrary"),
        name="mla_attn",
    )(qt, k, kpe, vt, hg)


def _compress_one(x_refs, pe_ref, w1_ref, w2_ref, o_ref):
    chunks = x_refs[0].shape[1] // CMP_STRIDE
    a = b = None
    for i in range(CMP_STRIDE):
        for c, x_ref in enumerate(x_refs):
            lanes = slice(c * LANES, (c + 1) * LANES)
            x = x_ref[0, pl.ds(i, chunks, stride=CMP_STRIDE), :]
            ai = jnp.dot((x + pe_ref[i:i + 1, lanes]).astype(BF16), w1_ref[i, lanes, :],
                         preferred_element_type=F32)
            bi = jnp.dot((x + pe_ref[CMP_STRIDE + i:CMP_STRIDE + i + 1, lanes]).astype(BF16),
                         w1_ref[CMP_STRIDE + i, lanes, :], preferred_element_type=F32)
            a = ai if a is None else a + ai
            b = bi if b is None else b + bi
    h1 = a + pltpu.roll(b, chunks - 1, axis=0)
    o_ref[0] = jnp.dot(_silu(h1).astype(BF16), w2_ref[...], preferred_element_type=F32).astype(BF16)


def _compress_kernel(xk0_ref, xk1_ref, xv_ref, pek_ref, pev_ref, w1k_ref, w2k_ref, w1v_ref, w2v_ref,
                     ok_ref, ov_ref):
    _compress_one((xk0_ref, xk1_ref), pek_ref, w1k_ref, w2k_ref, ok_ref)
    _compress_one((xv_ref,), pev_ref, w1v_ref, w2v_ref, ov_ref)


def _compress(hf, pek, pev, w1k, w2k, w1v, w2v):
    b, s, _ = hf.shape
    chunks = s // CMP_STRIDE
    full = lambda a: pl.BlockSpec(a.shape, lambda bi: (0,) * a.ndim)
    return pl.pallas_call(
        _compress_kernel,
        grid=(b,),
        in_specs=[pl.BlockSpec((1, s, LANES), lambda bi: (bi, 0, F_KC // LANES)),
                  pl.BlockSpec((1, s, LANES), lambda bi: (bi, 0, F_KC // LANES + 1)),
                  pl.BlockSpec((1, s, NSA_DV), lambda bi: (bi, 0, F_VC // NSA_DV)),
                  full(pek), full(pev), full(w1k), full(w2k), full(w1v), full(w2v)],
        out_specs=[
            pl.BlockSpec((1, chunks, HEAD_PAD), lambda bi: (bi, 0, 0)),
            pl.BlockSpec((1, chunks, NSA_DV), lambda bi: (bi, 0, 0)),
        ],
        out_shape=[
            jax.ShapeDtypeStruct((b, chunks, HEAD_PAD), BF16),
            jax.ShapeDtypeStruct((b, chunks, NSA_DV), BF16),
        ],
        compiler_params=_params("arbitrary"),
        name="nsa_compress",
    )(hf, hf, hf, pek, pev, w1k, w2k, w1v, w2v)


def _alibi_slope(h):
    return 2.0 ** (-8.0 * (h + 1) / NSA_HEADS)


def _nsa_attn_kernel(q_ref, g_ref, z_ref, kc_ref, vc_ref, ks_ref, vs_ref, kw_ref, vw_ref,
                     ovt_ref, oh_ref, tabs_ref, tabd_ref, tabw_ref, tabc_ref, slope_ref, o_ref,
                     kaug_ref, kwp_ref, vst_ref, vwt_ref, vct_ref, m_ref, acc_ref, s0_ref,
                     *, tq, tk, n_c, n_s, top_n):
    qi = pl.program_id(1)
    t0 = qi * tq
    heads = NSA_HEADS
    seq = ks_ref.shape[1]
    ncp = kc_ref.shape[1]
    hw = 2 * tq
    aug = HEAD_PAD - NSA_DK
    wlen = WIN + tq
    va = NSA_DV + SUM_ROWS
    sub_tiles = tk // tq

    @pl.when(qi == 0)
    def _():
        kaug_ref[...] = ks_ref[0] + oh_ref[...]
        col = lax.broadcasted_iota(jnp.int32, (WIN, HEAD_PAD), 1)
        kwp_ref[0:WIN, :] = jnp.where(col == NSA_DK, 1.0, 0.0).astype(BF16)
        kwp_ref[WIN:, :] = kw_ref[0]
        for i in range(seq // tq):
            vst_ref[i, 0:NSA_DV, :] = vs_ref[0, i * tq:(i + 1) * tq, :].astype(F32).T.astype(BF16)
            vst_ref[i, NSA_DV:va, :] = jnp.ones((SUM_ROWS, tq), BF16)
        for i in range(WIN // LANES):
            vwt_ref[i] = jnp.zeros((va, LANES), BF16)
        for i in range(seq // LANES):
            vwt_ref[WIN // LANES + i, 0:NSA_DV, :] = (
                vw_ref[0, i * LANES:(i + 1) * LANES, :].astype(F32).T.astype(BF16))
            vwt_ref[WIN // LANES + i, NSA_DV:va, :] = jnp.ones((SUM_ROWS, LANES), BF16)
        vct_ref[...] = vc_ref[0].astype(F32).T.astype(BF16)

    q_t = q_ref[0].astype(F32).T
    qt = [q_t[h * NSA_DK:(h + 1) * NSA_DK] for h in range(heads)]

    def stacked_qt(extra):
        cols = [jnp.concatenate([qt[h], extra], axis=0) for h in range(heads)]
        return [jnp.concatenate(cols[2 * x:2 * x + 2], axis=1).astype(BF16) for x in range(2)]

    flag_rows = jnp.where(lax.broadcasted_iota(jnp.int32, (aug, tq), 0) == 0, MASK_VALUE, 0.0)
    qt_win = stacked_qt(flag_rows)

    kc = kc_ref[0]
    s_cmp = [jnp.dot(kc, qt_win[x], preferred_element_type=F32) for x in range(2)]
    kwb = kwp_ref[pl.ds(pl.multiple_of(t0, LANES), wlen), :]
    s_win = [jnp.dot(kwb, qt_win[x], preferred_element_type=F32) for x in range(2)]

    first_blk = qi * (tq // CMP_STRIDE)
    tab_c = tabc_ref[pl.ds(pl.multiple_of(ncp - first_blk, CMP_STRIDE), ncp), :]
    o_cmp, p_sum = [], None
    for x in range(2):
        p_x = []
        for hl in range(2):
            h = 2 * x + hl
            s = s_cmp[x][:, hl * tq:(hl + 1) * tq] + tab_c[:, h * tq:(h + 1) * tq]
            m = jnp.max(s, axis=0, keepdims=True)
            e = jnp.exp2(s - m)
            inv = jnp.where(m > 0.5 * MASK_VALUE,
                            1.0 / (jnp.sum(e, axis=0, keepdims=True) + SOFTMAX_EPS), 0.0)
            p = e * inv
            p_sum = p if p_sum is None else p_sum + p
            p_x.append(p.astype(BF16))
        o_cmp.append(jnp.dot(vct_ref[...], jnp.concatenate(p_x, axis=1),
                             preferred_element_type=F32))
    p_hi = p_sum.astype(BF16)
    p_lo = (p_sum - p_hi.astype(F32)).astype(BF16)
    ovt = ovt_ref[...]
    imp = (jnp.dot(ovt, p_hi, preferred_element_type=F32)
           + jnp.dot(ovt, p_lo, preferred_element_type=F32))

    tile0 = qi * (tq // LANES)
    vwb = jnp.concatenate([vwt_ref[tile0 + r] for r in range(wlen // LANES)], axis=1)
    o_win = []
    for x in range(2):
        s = s_win[x] + tabw_ref[:, x * hw:(x + 1) * hw]
        p = jnp.exp2(s - jnp.max(s, axis=0, keepdims=True))
        pv = jnp.dot(vwb, p.astype(BF16), preferred_element_type=F32)
        o_win.append(pv[0:NSA_DV] / (pv[NSA_DV:NSA_DV + 1] + SOFTMAX_EPS))

    blk = lax.broadcasted_iota(jnp.int32, (n_s, tq), 0)
    cur = (t0 + lax.broadcasted_iota(jnp.int32, (n_s, tq), 1)) // SLC_LEN
    forced = (blk == 0) | (blk == cur) | (blk == cur - 1)
    imp = jnp.where(forced, NEG_BIG, imp)
    imp = jnp.where(blk > cur, -NEG_BIG, imp)
    sub = lax.broadcasted_iota(jnp.int32, (8, tq), 0)
    groups = [imp[8 * g:8 * g + 8] for g in range(n_s // 8)]
    ranks = [jnp.zeros((8, tq), F32) for _ in groups]
    for jp in range(n_s):
        row = imp[jp:jp + 1, :]
        for g, grp in enumerate(groups):
            ge = jnp.where(row >= grp, 1.0, 0.0)
            gt = jnp.where(row > grp, 1.0, 0.0)
            if 8 * g > jp:
                beats = ge
            elif 8 * g + 8 <= jp:
                beats = gt
            else:
                beats = jnp.where(sub + 8 * g > jp, ge, gt)
            ranks[g] = ranks[g] + beats
    sel_rows = jnp.where(jnp.concatenate(ranks, axis=0) < top_n, 0.0, MASK_VALUE)
    if n_s < aug:
        sel_rows = jnp.concatenate([sel_rows, jnp.zeros((aug - n_s, tq), F32)], axis=0)
    qt_sel = stacked_qt(sel_rows)

    jd = t0 // tk

    def scores(j, x):
        kb = j * tk if isinstance(j, int) else pl.multiple_of(j * tk, tk)
        return jnp.dot(kaug_ref[pl.ds(kb, tk), :], qt_sel[x], preferred_element_type=F32)

    def softmax(x, s, tab, key0):
        u = s + tab
        off = slope_ref[0:1, x * hw:(x + 1) * hw] * (key0 - t0).astype(F32)
        m_old = m_ref[x]
        m_new = jnp.maximum(m_old, jnp.max(u, axis=0, keepdims=True) + off)
        alpha = jnp.exp2(m_old - m_new)
        p = jnp.exp2(u - (m_new - off))
        m_ref[x] = m_new
        return alpha, p.astype(BF16)

    def accumulate(x, alpha, p, vt):
        acc_ref[x] = alpha * acc_ref[x] + jnp.dot(vt, p, preferred_element_type=F32)

    def full_tile(j):
        vt = jnp.concatenate([vst_ref[j * sub_tiles + r] for r in range(sub_tiles)], axis=1)
        for x in range(2):
            alpha, p = softmax(x, s0_ref[x], tabs_ref[:, x * hw:(x + 1) * hw], j * tk)
            accumulate(x, alpha, p, vt)
            s0_ref[x] = scores(j + 1, x)

    def sub_tile(r, tab_ref):
        key0 = jd * tk + r * tq
        rows = pl.ds(r * tq if isinstance(r, int) else pl.multiple_of(r * tq, tq), tq)
        vt = vst_ref[jd * sub_tiles + r]
        for x in range(2):
            alpha, p = softmax(x, s0_ref[x, rows, :], tab_ref[0:tq, x * hw:(x + 1) * hw], key0)
            accumulate(x, alpha, p, vt)

    for x in range(2):
        m_ref[x] = jnp.full((1, hw), MASK_VALUE, F32)
        acc_ref[x] = jnp.zeros((va, hw), F32)
        s0_ref[x] = scores(0, x)

    def body(j, carry):
        full_tile(j)
        return carry

    lax.fori_loop(0, jd, body, 0)
    own = (t0 - jd * tk) // tq
    for r in range(sub_tiles - 1):
        @pl.when(r < own)
        def _():
            sub_tile(r, tabs_ref)
    sub_tile(own, tabd_ref)
    o_slc = [acc_ref[x, 0:NSA_DV, :] / (acc_ref[x, NSA_DV:NSA_DV + 1, :] + SOFTMAX_EPS)
             for x in range(2)]

    gates = _sigmoid(g_ref[0]).T
    z = z_ref[0]
    for h in range(heads):
        x, lanes = h // 2, slice((h % 2) * tq, (h % 2 + 1) * tq)
        c = NSA_DK + 3 * h
        o = (gates[c:c + 1] * o_cmp[x][:, lanes] + gates[c + 1:c + 2] * o_slc[x][:, lanes]
             + gates[c + 2:c + 3] * o_win[x][:, lanes])
        zh = z[:, h * NSA_DV:(h + 1) * NSA_DV]
        o_ref[0, :, h * NSA_DV:(h + 1) * NSA_DV] = (o.T * zh.astype(F32)).astype(BF16)


def _nsa_attn(hb, hf, hg, k_cmp, v_cmp, tables, tq, tk):
    ovt, onehot, tab_sel, tab_diag, tab_win, tab_cmp, slope_rows = tables
    b, s, _ = hb.shape
    ncp = k_cmp.shape[1]
    n_s = s // SLC_LEN
    assert n_s <= HEAD_PAD - NSA_DK and n_s % 8 == 0, "selection blocks must fit the spare rows"
    assert tq & (tq - 1) == 0 and tq % LANES == 0 and tk % tq == 0 and s % tk == 0
    kern = functools.partial(_nsa_attn_kernel, tq=tq, tk=tk, n_c=ncp - 1, n_s=n_s,
                             top_n=min(SLC_TOPN, n_s))
    qw = NSA_HEADS * NSA_DK
    const2 = lambda a: pl.BlockSpec(a.shape, lambda bi, i: (0, 0), pipeline_mode=pl.Buffered(1))
    return pl.pallas_call(
        kern,
        grid=(b, s // tq),
        in_specs=[
            pl.BlockSpec((1, tq, qw), lambda bi, i: (bi, i, B_QNSA // qw)),
            pl.BlockSpec((1, tq, HEAD_PAD), lambda bi, i: (bi, i, F_KC // HEAD_PAD)),
            pl.BlockSpec((1, tq, NSA_WIDTH), lambda bi, i: (bi, i, G_ZNSA // NSA_WIDTH)),
            pl.BlockSpec((1, ncp, HEAD_PAD), lambda bi, i: (bi, 0, 0)),
            pl.BlockSpec((1, ncp, NSA_DV), lambda bi, i: (bi, 0, 0)),
            pl.BlockSpec((1, s, HEAD_PAD), lambda bi, i: (bi, 0, B_KS // HEAD_PAD)),
            pl.BlockSpec((1, s, NSA_DV), lambda bi, i: (bi, 0, B_VS // NSA_DV)),
            pl.BlockSpec((1, s, HEAD_PAD), lambda bi, i: (bi, 0, B_KW // HEAD_PAD)),
            pl.BlockSpec((1, s, NSA_DV), lambda bi, i: (bi, 0, B_VW // NSA_DV)),
            const2(ovt), const2(onehot), const2(tab_sel), const2(tab_diag), const2(tab_win),
            const2(tab_cmp), const2(slope_rows),
        ],
        out_specs=pl.BlockSpec((1, tq, NSA_WIDTH), lambda bi, i: (bi, i, 0)),
        out_shape=jax.ShapeDtypeStruct((b, s, NSA_WIDTH), BF16),
        scratch_shapes=[
            pltpu.VMEM((s, HEAD_PAD), BF16),
            pltpu.VMEM((s + WIN, HEAD_PAD), BF16),
            pltpu.VMEM((s // tq, NSA_DV + SUM_ROWS, tq), BF16),
            pltpu.VMEM(((s + WIN) // LANES, NSA_DV + SUM_ROWS, LANES), BF16),
            pltpu.VMEM((NSA_DV, ncp), BF16),
            pltpu.VMEM((2, 1, 2 * tq), F32),
            pltpu.VMEM((2, NSA_DV + SUM_ROWS, 2 * tq), F32),
            pltpu.VMEM((2, tk, 2 * tq), F32),
        ],
        compiler_params=_params("arbitrary", "arbitrary"),
        name="nsa_attn",
    )(hb, hf, hg, k_cmp, v_cmp, hb, hb, hb, hb, ovt, onehot, tab_sel, tab_diag, tab_win, tab_cmp,
      slope_rows)


def _mem_attn_kernel(q_ref, k_ref, v_ref, z_ref, o_ref):
    q = q_ref[0]
    k = k_ref[0]
    v = v_ref[0]
    z = z_ref[0]
    for h in range(MEM_HEADS):
        sl = slice(h * MEM_DH, (h + 1) * MEM_DH)
        s = lax.dot_general(q[:, sl], k[:, sl], _NT, preferred_element_type=F32)
        p = jnp.exp(s - jnp.max(s, axis=-1, keepdims=True))
        o = jnp.dot(p.astype(BF16), v[:, sl], preferred_element_type=F32)
        o = o / jnp.sum(p, axis=-1, keepdims=True)
        o_ref[0, :, sl] = (o * z[:, sl].astype(F32)).astype(BF16)


def _mem_attn(hb, mem_kv, hg, tq):
    b, s, _ = hb.shape
    mlen = mem_kv.shape[1]
    return pl.pallas_call(
        _mem_attn_kernel,
        grid=(b, s // tq),
        in_specs=[
            pl.BlockSpec((1, tq, MEM_WIDTH), lambda bi, i: (bi, i, B_QMEM // MEM_WIDTH)),
            pl.BlockSpec((1, mlen, MEM_WIDTH), lambda bi, i: (bi, 0, 0)),
            pl.BlockSpec((1, mlen, MEM_WIDTH), lambda bi, i: (bi, 0, 1)),
            pl.BlockSpec((1, tq, MEM_WIDTH), lambda bi, i: (bi, i, G_ZMEM // MEM_WIDTH)),
        ],
        out_specs=pl.BlockSpec((1, tq, MEM_WIDTH), lambda bi, i: (bi, i, 0)),
        out_shape=jax.ShapeDtypeStruct((b, s, MEM_WIDTH), BF16),
        compiler_params=_params("arbitrary", "arbitrary"),
        name="mem_attn",
    )(hb, mem_kv, mem_kv, hg)


def _out_proj_kernel(x_ref, a_ref, n_ref, m_ref, w_ref, g_ref, o_ref, *, final_norm):
    y = x_ref[...]
    y = y + jnp.dot(a_ref[...], w_ref[0:MLA_WIDTH, :], preferred_element_type=F32)
    y = y + jnp.dot(n_ref[...], w_ref[MLA_WIDTH:MLA_WIDTH + NSA_WIDTH, :], preferred_element_type=F32)
    y = y + jnp.dot(m_ref[...], w_ref[MLA_WIDTH + NSA_WIDTH:, :], preferred_element_type=F32)
    if final_norm:
        y = _rmsnorm(y, g_ref[...])
    o_ref[...] = y


def _out_proj(x, o_mla, o_nsa, o_mem, w_out, g, final_norm, tm):
    m, d = x.shape
    kern = functools.partial(_out_proj_kernel, final_norm=final_norm)
    row = lambda width: pl.BlockSpec((tm, width), lambda i: (i, 0))
    return pl.pallas_call(
        kern,
        grid=(m // tm,),
        in_specs=[row(d), row(MLA_WIDTH), row(NSA_WIDTH), row(MEM_WIDTH),
                  pl.BlockSpec(w_out.shape, lambda i: (0, 0)),
                  pl.BlockSpec((1, d), lambda i: (0, 0))],
        out_specs=row(d),
        out_shape=jax.ShapeDtypeStruct((m, d), F32),
        compiler_params=_params("arbitrary"),
        name="out_proj",
    )(x, o_mla, o_nsa, o_mem, w_out, g.reshape(1, d))


def _pad_cols(w, width):
    return jnp.pad(w, ((0, 0), (0, width - w.shape[1])))


def _w_in_moves():
    names = ("c_q", "c_kv", "k_rope", "z_mla", "q_nsa", "k_c", "v_c", "k_s", "v_s", "k_w", "v_w",
             "g_nsa", "z_nsa", "q_mem", "z_mem")
    src, off = {}, 0
    for name, n in zip(names, IN_SPLITS):
        src[name] = (off, n)
        off += n
    dst = {"c_q": (0, F_CQ), "c_kv": (0, F_CKV), "k_c": (0, F_KC), "g_nsa": (0, F_GATE),
           "v_c": (0, F_VC), "k_rope": (0, F_KROPE),
           "z_mla": (1, G_ZMLA), "z_nsa": (1, G_ZNSA), "z_mem": (1, G_ZMEM),
           "q_nsa": (2, B_QNSA), "q_mem": (2, B_QMEM), "k_s": (2, B_KS), "k_w": (2, B_KW),
           "v_s": (2, B_VS), "v_w": (2, B_VW)}
    return [(src[n][0], src[n][1], dst[n][0], dst[n][1]) for n in dst]


def _w_prep_kernel(wt_ref, wf_ref, wg_ref, wb_ref):
    outs = (wf_ref, wg_ref, wb_ref)
    for o_ref in outs:
        o_ref[...] = jnp.zeros(o_ref.shape, BF16)
    for s0, width, which, d0 in _w_in_moves():
        rows = -(-width // LANES) * LANES
        start = min(s0, wt_ref.shape[0] - rows)
        slab = wt_ref[start:start + rows, :].T
        outs[which][:, d0:d0 + width] = slab[:, s0 - start:s0 - start + width].astype(BF16)


def _layout_w_in(w_in, tr=256):
    d, n = w_in.shape
    widths = (F_WIDTH, G_WIDTH, B_WIDTH)
    wf, wg, wb = pl.pallas_call(
        _w_prep_kernel,
        grid=(d // tr,),
        in_specs=[pl.BlockSpec((n, tr), lambda i: (0, i))],
        out_specs=[pl.BlockSpec((tr, w), lambda i: (i, 0)) for w in widths],
        out_shape=[jax.ShapeDtypeStruct((d, w), BF16) for w in widths],
        compiler_params=_params("arbitrary"),
        name="w_in_layout",
    )(w_in.T)
    scale_b = jnp.ones((B_WIDTH,), F32)
    scale_b = scale_b.at[B_QNSA:B_QNSA + NSA_HEADS * NSA_DK].set(LOG2E * NSA_DK ** -0.5)
    scale_b = scale_b.at[B_QMEM:B_QMEM + MEM_WIDTH].set(MEM_DH ** -0.5)
    return wf, wg, wb, scale_b


def _rope_tables(seq):
    pos = jnp.arange(seq, dtype=F32)
    inv_freq = ROPE_THETA ** (-jnp.arange(0, MLA_ROPE, 2, dtype=F32) / MLA_ROPE)
    ang = pos[:, None] * inv_freq[None, :]
    zeros = jnp.zeros((seq, LANES - MLA_ROPE), F32)
    cos2 = jnp.concatenate([jnp.cos(ang), jnp.cos(ang), zeros], axis=1)
    sin2 = jnp.concatenate([jnp.sin(ang), jnp.sin(ang), zeros], axis=1)
    return cos2, sin2, jnp.cos(ang).T, jnp.sin(ang).T


def _nsa_tables(seq, tq, tk):
    chunks = seq // CMP_STRIDE
    n_s = seq // SLC_LEN
    c_start = jnp.arange(chunks) * CMP_STRIDE
    s_start = jnp.arange(n_s) * SLC_LEN
    overlap_t = ((c_start[None, :] < s_start[:, None] + SLC_LEN)
                 & (c_start[None, :] + CMP_LEN > s_start[:, None])
                 & (jnp.arange(chunks)[None, :] < chunks - 1))
    key_block = jnp.arange(seq) // SLC_LEN
    onehot = jnp.arange(HEAD_PAD)[None, :] == (NSA_DK + key_block)[:, None]
    slope = jnp.repeat(jnp.array([_alibi_slope(h) for h in range(NSA_HEADS)], F32) * LOG2E, tq)
    q_lane = jnp.tile(jnp.arange(tq), NSA_HEADS)
    tab_sel = jnp.arange(tk, dtype=F32)[:, None] * slope[None, :]
    rel = q_lane[None, :] - jnp.arange(WIN + tq)[:, None] + WIN
    tab_win = jnp.where((rel >= 0) & (rel < WIN), -slope[None, :] * rel.astype(F32), MASK_VALUE)
    slope_rows = jnp.broadcast_to(slope[None, :], (8, NSA_HEADS * tq))
    tab_diag = jnp.where(jnp.arange(tq)[:, None] <= q_lane[None, :], tab_sel[0:tq], MASK_VALUE)
    d_blk = jnp.arange(2 * chunks)[:, None] - chunks
    seen = d_blk * CMP_STRIDE + (CMP_LEN - 1) <= q_lane[None, :]
    dist = (q_lane[None, :] - d_blk * CMP_STRIDE).astype(F32) - (CMP_LEN - 1) / 2.0
    tab_cmp = jnp.where(seen, -slope[None, :] * dist, MASK_VALUE)
    return overlap_t.astype(BF16), onehot.astype(BF16), tab_sel, tab_diag, tab_win, tab_cmp, slope_rows


def _layer(x2, mem2, batch, seq, tables, norm_g, w_in, q_norm_g, w_uq, kv_norm_g, w_ukv,
           cmp_pe_k, cmp_pe_v, cmp_w1k, cmp_w2k, cmp_w1v, cmp_w2v, mem_norm_g, w_mem_kv, w_out,
           final_g, final_norm):
    rope, nsa_tables = tables
    d = x2.shape[1]
    wf, wg, wb, scale_b = _layout_w_in(w_in)
    hf, hg, hb = _in_proj(x2, norm_g, wf, wg, wb, scale_b, 512)

    wq = jnp.pad(w_uq.reshape(MLA_Q_RANK, MLA_HEADS, MLA_NOPE + MLA_ROPE),
                 ((0, 0), (0, 0), (0, HEAD_PAD - MLA_NOPE - MLA_ROPE)))
    wqt = wq.reshape(MLA_Q_RANK, MLA_HEADS * HEAD_PAD).T.astype(BF16)
    wkv = w_ukv.reshape(MLA_KV_RANK, MLA_HEADS, MLA_NOPE + MLA_V)
    wk = wkv[:, :, :MLA_NOPE].reshape(MLA_KV_RANK, MLA_HEADS * MLA_NOPE).astype(BF16)
    wvt = wkv[:, :, MLA_NOPE:].reshape(MLA_KV_RANK, MLA_WIDTH).T.astype(BF16)
    qt, k, kpe, vt = _mla_up(hf, q_norm_g, kv_norm_g, wqt, wk, wvt, rope, seq, MLA_TILE)
    hf3 = hf.reshape(batch, seq, F_WIDTH)
    hg3 = hg.reshape(batch, seq, G_WIDTH)
    hb3 = hb.reshape(batch, seq, B_WIDTH)
    o_mla = _mla_attn(qt, k.reshape(batch, seq, -1), kpe.reshape(batch, seq, -1), vt, hg3,
                      MLA_TILE, 4)

    pad_k = HEAD_PAD - NSA_DK
    k_cmp, v_cmp = _compress(
        hf3, _pad_cols(cmp_pe_k, HEAD_PAD), cmp_pe_v,
        jnp.pad(cmp_w1k.reshape(CMP_LEN, NSA_DK, NSA_DK), ((0, 0), (0, pad_k), (0, pad_k))).astype(BF16),
        jnp.pad(cmp_w2k, ((0, pad_k), (0, pad_k))).astype(BF16),
        cmp_w1v.reshape(CMP_LEN, NSA_DV, NSA_DV).astype(BF16), cmp_w2v.astype(BF16))
    o_nsa = _nsa_attn(hb3, hf3, hg3, k_cmp, v_cmp, nsa_tables, NSA_TQ, NSA_TK)

    mem_kv = _norm_proj(mem2, mem_norm_g, w_mem_kv.astype(BF16), jnp.ones((2 * MEM_WIDTH,), F32),
                        BF16, mem2.shape[0] // batch, MEM_WIDTH, "mem_kv_proj")
    o_mem = _mem_attn(hb3, mem_kv.reshape(batch, -1, 2 * MEM_WIDTH), hg3, 512)

    return _out_proj(x2, o_mla.reshape(-1, MLA_WIDTH), o_nsa.reshape(-1, NSA_WIDTH),
                     o_mem.reshape(-1, MEM_WIDTH), w_out.astype(BF16), final_g, final_norm, 512)


def kernel(x, mem, norm_g, w_in, q_norm_g, w_uq, kv_norm_g, w_ukv, cmp_pe_k, cmp_pe_v,
           cmp_w1k, cmp_w2k, cmp_w1v, cmp_w2v, mem_norm_g, w_mem_kv, w_out, final_norm_g):
    batch, seq, d = x.shape
    depth = norm_g.shape[0]
    tables = (_rope_tables(seq), _nsa_tables(seq, NSA_TQ, NSA_TK))
    x2 = x.reshape(batch * seq, d)
    mem2 = mem.reshape(batch * mem.shape[1], d)
    for l in range(depth):
        x2 = _layer(x2, mem2, batch, seq, tables, norm_g[l], w_in[l], q_norm_g[l], w_uq[l],
                    kv_norm_g[l], w_ukv[l], cmp_pe_k[l], cmp_pe_v[l], cmp_w1k[l], cmp_w2k[l],
                    cmp_w1v[l], cmp_w2v[l], mem_norm_g[l], w_mem_kv[l], w_out[l],
                    final_norm_g, l == depth - 1)
    return x2.reshape(batch, seq, d)
```

```python
import functools

import jax
import jax.numpy as jnp
from jax import lax
from jax.experimental import pallas as pl
from jax.experimental.pallas import tpu as pltpu

F32 = jnp.float32
BF16 = jnp.bfloat16

EPS = 1e-6
NEG_BIG = 1e9
MASK_VALUE = -1e30
SOFTMAX_EPS = 1e-20
LOG2E = 1.4426950408889634

MLA_HEADS = 8
MLA_NOPE = 128
MLA_ROPE = 64
MLA_V = 128
MLA_Q_RANK = 512
MLA_KV_RANK = 512
ROPE_THETA = 10000.0

NSA_HEADS = 4
NSA_DK = 192
NSA_DV = 128
CMP_LEN = 32
CMP_STRIDE = 16
SLC_LEN = 64
SLC_TOPN = 16
WIN = 512

MEM_HEADS = 4
MEM_DH = 128

MLA_WIDTH = MLA_HEADS * MLA_V
NSA_WIDTH = NSA_HEADS * NSA_DV
MEM_WIDTH = MEM_HEADS * MEM_DH

IN_SPLITS = (
    MLA_Q_RANK, MLA_KV_RANK, MLA_ROPE, MLA_WIDTH,
    NSA_HEADS * NSA_DK, NSA_DK, NSA_DV, NSA_DK, NSA_DV,
    NSA_DK, NSA_DV, 3 * NSA_HEADS, NSA_WIDTH,
    MEM_WIDTH, MEM_WIDTH,
)

LANES = 128
MXU_COLS = 256
HEAD_PAD = 256
VMEM_LIMIT = 56 * 1024 * 1024
MLA_TILE = 512
SUM_ROWS = 16
NSA_TQ = 256
NSA_TK = 512

F_CQ, F_CKV, F_KC, F_VC, F_KROPE = 0, 512, 1024, 1280, 1408
F_GATE = F_KC + NSA_DK
F_WIDTH = 1536
G_ZMLA, G_ZNSA, G_ZMEM = 0, MLA_WIDTH, MLA_WIDTH + NSA_WIDTH
G_WIDTH = MLA_WIDTH + NSA_WIDTH + MEM_WIDTH
B_QNSA, B_KS, B_QMEM, B_KW, B_VS, B_VW = 0, 768, 1024, 1536, 1792, 1920
B_WIDTH = 2048

_NT = (((1,), (1,)), ((), ()))


def _params(*sem):
    return pltpu.CompilerParams(dimension_semantics=sem, vmem_limit_bytes=VMEM_LIMIT)


def _sigmoid(x):
    return 1.0 / (1.0 + jnp.exp(-x))


def _silu(x):
    return x * _sigmoid(x)


def _rmsnorm(x, g):
    ms = jnp.mean(x * x, axis=-1, keepdims=True)
    return (x * lax.rsqrt(ms + EPS)) * g


def _norm_proj_kernel(x_ref, g_ref, w_ref, cs_ref, o_ref, xn_ref):
    @pl.when(pl.program_id(1) == 0)
    def _():
        xn_ref[...] = _rmsnorm(x_ref[...], g_ref[...]).astype(BF16)

    acc = jnp.dot(xn_ref[...], w_ref[...], preferred_element_type=F32)
    o_ref[...] = (acc * cs_ref[...]).astype(o_ref.dtype)


def _norm_proj(x, g, w, colscale, out_dtype, tm, tn, name):
    m, k = x.shape
    n = w.shape[1]
    return pl.pallas_call(
        _norm_proj_kernel,
        grid=(m // tm, n // tn),
        in_specs=[
            pl.BlockSpec((tm, k), lambda i, j: (i, 0)),
            pl.BlockSpec((1, k), lambda i, j: (0, 0)),
            pl.BlockSpec((k, tn), lambda i, j: (0, j)),
            pl.BlockSpec((1, tn), lambda i, j: (0, j)),
        ],
        out_specs=pl.BlockSpec((tm, tn), lambda i, j: (i, j)),
        out_shape=jax.ShapeDtypeStruct((m, n), out_dtype),
        scratch_shapes=[pltpu.VMEM((tm, k), BF16)],
        compiler_params=_params("arbitrary", "arbitrary"),
        name=name,
    )(x, g.reshape(1, k), w, colscale.reshape(1, n))


def _in_proj_kernel(x_ref, g_ref, wf_ref, wg_ref, wb_ref, cs_ref, of_ref, og_ref, ob_ref, xn_ref):
    j = pl.program_id(1)

    @pl.when(j == 0)
    def _():
        xn_ref[...] = _rmsnorm(x_ref[...], g_ref[...]).astype(BF16)
        of_ref[...] = jnp.dot(xn_ref[...], wf_ref[...], preferred_element_type=F32)

    @pl.when(j == 1)
    def _():
        og_ref[...] = _silu(jnp.dot(xn_ref[...], wg_ref[...], preferred_element_type=F32)).astype(BF16)

    @pl.when(j == 2)
    def _():
        acc = jnp.dot(xn_ref[...], wb_ref[...], preferred_element_type=F32)
        ob_ref[...] = (acc * cs_ref[...]).astype(BF16)


def _in_proj(x, g, wf, wg, wb, scale_b, tm):
    m, k = x.shape
    whole = lambda a: pl.BlockSpec(a.shape, lambda i, j: (0, 0), pipeline_mode=pl.Buffered(1))
    rows = lambda a: pl.BlockSpec((tm, a.shape[1]), lambda i, j: (i, 0))
    cs = scale_b.reshape(1, -1)
    return pl.pallas_call(
        _in_proj_kernel,
        grid=(m // tm, 3),
        in_specs=[rows(x), pl.BlockSpec((1, k), lambda i, j: (0, 0)),
                  whole(wf), whole(wg), whole(wb), whole(cs)],
        out_specs=[rows(wf), rows(wg), rows(wb)],
        out_shape=[jax.ShapeDtypeStruct((m, wf.shape[1]), F32),
                   jax.ShapeDtypeStruct((m, wg.shape[1]), BF16),
                   jax.ShapeDtypeStruct((m, wb.shape[1]), BF16)],
        scratch_shapes=[pltpu.VMEM((tm, k), BF16)],
        compiler_params=_params("arbitrary", "arbitrary"),
        name="in_proj",
    )(x, g.reshape(1, k), wf, wg, wb, cs)


def _rope_half(hi, cos2, sin2):
    up = pltpu.roll(hi, 32, axis=1)
    down = pltpu.roll(hi, 96, axis=1)
    return hi * cos2 + (up - down) * sin2


def _mla_up_kernel(cq_ref, ckv_ref, kr_ref, gq_ref, gkv_ref, wqt_ref, wk_ref, wvt_ref,
                   cos_ref, sin_ref, cost_ref, sint_ref, qt_ref, k_ref, kpe_ref, vt_ref):
    scale = LOG2E * (MLA_NOPE + MLA_ROPE) ** -0.5
    half = MLA_ROPE // 2
    cos_t = cost_ref[...]
    sin_t = sint_ref[...]

    cqn = _rmsnorm(cq_ref[...], gq_ref[...]).astype(BF16)
    yt = lax.dot_general(wqt_ref[...], cqn, _NT, preferred_element_type=F32)
    for h in range(MLA_HEADS):
        r = h * HEAD_PAD
        x1 = yt[r + MLA_NOPE:r + MLA_NOPE + half]
        x2 = yt[r + MLA_NOPE + half:r + MLA_NOPE + MLA_ROPE]
        qt_ref[0, r:r + MLA_NOPE, :] = (yt[r:r + MLA_NOPE] * scale).astype(BF16)
        qt_ref[0, r + MLA_NOPE:r + MLA_NOPE + half, :] = (
            (x1 * cos_t - x2 * sin_t) * scale).astype(BF16)
        qt_ref[0, r + MLA_NOPE + half:r + MLA_NOPE + MLA_ROPE, :] = (
            (x1 * sin_t + x2 * cos_t) * scale).astype(BF16)
        qt_ref[0, r + MLA_NOPE + MLA_ROPE:r + HEAD_PAD, :] = jnp.zeros(
            (HEAD_PAD - MLA_NOPE - MLA_ROPE, yt.shape[1]), BF16)

    ckn = _rmsnorm(ckv_ref[...], gkv_ref[...]).astype(BF16)
    kn = jnp.dot(ckn, wk_ref[...], preferred_element_type=F32)
    k_pe = _rope_half(kr_ref[...], cos_ref[...], sin_ref[...]).astype(BF16)
    k_ref[...] = kn.astype(BF16)
    kpe_ref[...] = k_pe
    vt = lax.dot_general(wvt_ref[...], ckn, _NT, preferred_element_type=F32)
    ones = jnp.ones((SUM_ROWS, vt.shape[1]), BF16)
    for h in range(MLA_HEADS):
        r = h * (MLA_V + SUM_ROWS)
        vt_ref[0, r:r + MLA_V, :] = vt[h * MLA_V:(h + 1) * MLA_V].astype(BF16)
        vt_ref[0, r + MLA_V:r + MLA_V + SUM_ROWS, :] = ones


def _mla_up(hf, gq, gkv, wqt, wk, wvt, rope, seq, tm):
    cos2, sin2, cos_t, sin_t = rope
    m = hf.shape[0]
    steps_per_seq = seq // tm
    hq = MLA_HEADS * HEAD_PAD
    half = MLA_ROPE // 2
    const = lambda i: (0, 0)
    return pl.pallas_call(
        _mla_up_kernel,
        grid=(m // tm,),
        in_specs=[
            pl.BlockSpec((tm, MLA_Q_RANK), lambda i: (i, F_CQ // MLA_Q_RANK)),
            pl.BlockSpec((tm, MLA_KV_RANK), lambda i: (i, F_CKV // MLA_KV_RANK)),
            pl.BlockSpec((tm, LANES), lambda i: (i, F_KROPE // LANES)),
            pl.BlockSpec((1, MLA_Q_RANK), const),
            pl.BlockSpec((1, MLA_KV_RANK), const),
            pl.BlockSpec((hq, MLA_Q_RANK), const),
            pl.BlockSpec((MLA_KV_RANK, MLA_HEADS * MLA_NOPE), const),
            pl.BlockSpec((MLA_WIDTH, MLA_KV_RANK), const),
            pl.BlockSpec((tm, LANES), lambda i: (i % steps_per_seq, 0)),
            pl.BlockSpec((tm, LANES), lambda i: (i % steps_per_seq, 0)),
            pl.BlockSpec((half, tm), lambda i: (0, i % steps_per_seq)),
            pl.BlockSpec((half, tm), lambda i: (0, i % steps_per_seq)),
        ],
        out_specs=[
            pl.BlockSpec((1, hq, tm), lambda i: (i, 0, 0)),
            pl.BlockSpec((tm, MLA_HEADS * MLA_NOPE), lambda i: (i, 0)),
            pl.BlockSpec((tm, LANES), lambda i: (i, 0)),
            pl.BlockSpec((1, MLA_HEADS * (MLA_V + SUM_ROWS), tm), lambda i: (i, 0, 0)),
        ],
        out_shape=[
            jax.ShapeDtypeStruct((m // tm, hq, tm), BF16),
            jax.ShapeDtypeStruct((m, MLA_HEADS * MLA_NOPE), BF16),
            jax.ShapeDtypeStruct((m, LANES), BF16),
            jax.ShapeDtypeStruct((m // tm, MLA_HEADS * (MLA_V + SUM_ROWS), tm), BF16),
        ],
        compiler_params=_params("arbitrary"),
        name="mla_up",
    )(hf, hf, hf, gq.reshape(1, -1), gkv.reshape(1, -1), wqt, wk, wvt, cos2, sin2, cos_t, sin_t)


def _mla_attn_kernel(qt_ref, k_ref, kpe_ref, vt_ref, z_ref, o_ref, m_ref, acc_ref, s0_ref, *,
                     tile_len, hps):
    t = tile_len
    va = MLA_V + SUM_ROWS
    qi = pl.program_id(2)

    def scores(j, h):
        kb = pl.multiple_of(j * t, t)
        k = jnp.concatenate([k_ref[0, pl.ds(kb, t), h * MLA_NOPE:(h + 1) * MLA_NOPE],
                             kpe_ref[0, pl.ds(kb, t), :]], axis=1)
        q_t = qt_ref[0, h * HEAD_PAD:(h + 1) * HEAD_PAD, :]
        return jnp.dot(k, q_t, preferred_element_type=F32)

    def softmax(h, s, diagonal):
        if diagonal:
            kpos = lax.broadcasted_iota(jnp.int32, (t, t), 0)
            qpos = lax.broadcasted_iota(jnp.int32, (t, t), 1)
            s = jnp.where(kpos <= qpos, s, MASK_VALUE)
        m = m_ref[h]
        m_new = jnp.maximum(m, jnp.max(s, axis=0, keepdims=True))
        alpha = jnp.exp2(m - m_new)
        p = jnp.exp2(s - m_new)
        m_ref[h] = m_new
        return alpha, p.astype(BF16)

    def accumulate(j, h, alpha, p):
        vt = vt_ref[j, h * va:(h + 1) * va, :]
        acc_ref[h] = alpha * acc_ref[h] + jnp.dot(vt, p, preferred_element_type=F32)

    def tile(j, diagonal):
        for h in range(hps):
            alpha, p = softmax(h, s0_ref[h], diagonal)
            accumulate(j, h, alpha, p)
            if not diagonal:
                s0_ref[h] = scores(j + 1, h)

    for h in range(hps):
        m_ref[h] = jnp.full((1, t), MASK_VALUE, F32)
        acc_ref[h] = jnp.zeros((va, t), F32)
        s0_ref[h] = scores(0, h)

    def body(j, carry):
        tile(j, False)
        return carry

    lax.fori_loop(0, qi, body, 0)

    tile(qi, True)
    for h in range(hps):
        l = acc_ref[h, MLA_V:MLA_V + 1, :]
        o = (acc_ref[h, 0:MLA_V, :] / (l + SOFTMAX_EPS)).T
        cols = slice(h * MLA_V, (h + 1) * MLA_V)
        o_ref[0, :, cols] = (o * z_ref[0, :, cols].astype(F32)).astype(BF16)


def _mla_attn(qt, k, kpe, vt, hg, tile_len, hps):
    b, s, _ = k.shape
    t = tile_len
    tiles = s // t
    kern = functools.partial(_mla_attn_kernel, tile_len=t, hps=hps)
    qw, vw, va = hps * HEAD_PAD, hps * MLA_V, MLA_V + SUM_ROWS
    return pl.pallas_call(
        kern,
        grid=(b, MLA_HEADS // hps, tiles),
        in_specs=[
            pl.BlockSpec((1, qw, t), lambda bi, h, i: (bi * tiles + i, h, 0)),
            pl.BlockSpec((1, s, hps * MLA_NOPE), lambda bi, h, i: (bi, 0, h)),
            pl.BlockSpec((1, s, LANES), lambda bi, h, i: (bi, 0, 0)),
            pl.BlockSpec((tiles, hps * va, t), lambda bi, h, i: (bi, h, 0)),
            pl.BlockSpec((1, t, vw), lambda bi, h, i: (bi, i, G_ZMLA // vw + h)),
        ],
        out_specs=pl.BlockSpec((1, t, vw), lambda bi, h, i: (bi, i, h)),
        out_shape=jax.ShapeDtypeStruct((b, s, MLA_WIDTH), BF16),
        scratch_shapes=[
            pltpu.VMEM((hps, 1, t), F32),
            pltpu.VMEM((hps, va, t), F32),
            pltpu.VMEM((hps, t, t), F32),
        ],
        compiler_params=_params("arbitrary", "arbitrary", "arbitrary"),
        name="mla_attn",
    )(qt, k, kpe, vt, hg)


def _compress_one(x_refs, pe_ref, w1_ref, w2_ref, o_ref):
    chunks = x_refs[0].shape[1] // CMP_STRIDE
    a = b = None
    for i in range(CMP_STRIDE):
        for c, x_ref in enumerate(x_refs):
            lanes = slice(c * LANES, (c + 1) * LANES)
            x = x_ref[0, pl.ds(i, chunks, stride=CMP_STRIDE), :]
            ai = jnp.dot((x + pe_ref[i:i + 1, lanes]).astype(BF16), w1_ref[i, lanes, :],
                         preferred_element_type=F32)
            bi = jnp.dot((x + pe_ref[CMP_STRIDE + i:CMP_STRIDE + i + 1, lanes]).astype(BF16),
                         w1_ref[CMP_STRIDE + i, lanes, :], preferred_element_type=F32)
            a = ai if a is None else a + ai
            b = bi if b is None else b + bi
    h1 = a + pltpu.roll(b, chunks - 1, axis=0)
    o_ref[0] = jnp.dot(_silu(h1).astype(BF16), w2_ref[...], preferred_element_type=F32).astype(BF16)


def _compress_kernel(xk0_ref, xk1_ref, xv_ref, pek_ref, pev_ref, w1k_ref, w2k_ref, w1v_ref, w2v_ref,
                     ok_ref, ov_ref):
    _compress_one((xk0_ref, xk1_ref), pek_ref, w1k_ref, w2k_ref, ok_ref)
    _compress_one((xv_ref,), pev_ref, w1v_ref, w2v_ref, ov_ref)


def _compress(hf, pek, pev, w1k, w2k, w1v, w2v):
    b, s, _ = hf.shape
    chunks = s // CMP_STRIDE
    full = lambda a: pl.BlockSpec(a.shape, lambda bi: (0,) * a.ndim)
    return pl.pallas_call(
        _compress_kernel,
        grid=(b,),
        in_specs=[pl.BlockSpec((1, s, LANES), lambda bi: (bi, 0, F_KC // LANES)),
                  pl.BlockSpec((1, s, LANES), lambda bi: (bi, 0, F_KC // LANES + 1)),
                  pl.BlockSpec((1, s, NSA_DV), lambda bi: (bi, 0, F_VC // NSA_DV)),
                  full(pek), full(pev), full(w1k), full(w2k), full(w1v), full(w2v)],
        out_specs=[
            pl.BlockSpec((1, chunks, HEAD_PAD), lambda bi: (bi, 0, 0)),
            pl.BlockSpec((1, chunks, NSA_DV), lambda bi: (bi, 0, 0)),
        ],
        out_shape=[
            jax.ShapeDtypeStruct((b, chunks, HEAD_PAD), BF16),
            jax.ShapeDtypeStruct((b, chunks, NSA_DV), BF16),
        ],
        compiler_params=_params("arbitrary"),
        name="nsa_compress",
    )(hf, hf, hf, pek, pev, w1k, w2k, w1v, w2v)


def _alibi_slope(h):
    return 2.0 ** (-8.0 * (h + 1) / NSA_HEADS)


def _nsa_attn_kernel(q_ref, g_ref, z_ref, kc_ref, vc_ref, ks_ref, vs_ref, kw_ref, vw_ref,
                     ovt_ref, oh_ref, tabs_ref, tabd_ref, tabw_ref, tabc_ref, slope_ref, o_ref,
                     kaug_ref, kwp_ref, vst_ref, vwt_ref, vct_ref, m_ref, acc_ref, s0_ref,
                     sc_ref, sw_ref, qwin_ref, qsel_ref,
                     *, tq, tk, n_c, n_s, top_n):
    qi = pl.program_id(1)
    t0 = qi * tq
    heads = NSA_HEADS
    seq = ks_ref.shape[1]
    ncp = kc_ref.shape[1]
    hw = 2 * tq
    aug = HEAD_PAD - NSA_DK
    wlen = WIN + tq
    va = NSA_DV + SUM_ROWS
    sub_tiles = tk // tq

    @pl.when(qi == 0)
    def _():
        kaug_ref[...] = ks_ref[0] + oh_ref[...]
        col = lax.broadcasted_iota(jnp.int32, (WIN, HEAD_PAD), 1)
        kwp_ref[0:WIN, :] = jnp.where(col == NSA_DK, 1.0, 0.0).astype(BF16)
        kwp_ref[WIN:, :] = kw_ref[0]
        for i in range(seq // tq):
            vst_ref[i, 0:NSA_DV, :] = vs_ref[0, i * tq:(i + 1) * tq, :].astype(F32).T.astype(BF16)
            vst_ref[i, NSA_DV:va, :] = jnp.ones((SUM_ROWS, tq), BF16)
        for i in range(WIN // LANES):
            vwt_ref[i] = jnp.zeros((va, LANES), BF16)
        for i in range(seq // LANES):
            vwt_ref[WIN // LANES + i, 0:NSA_DV, :] = (
                vw_ref[0, i * LANES:(i + 1) * LANES, :].astype(F32).T.astype(BF16))
            vwt_ref[WIN // LANES + i, NSA_DV:va, :] = jnp.ones((SUM_ROWS, LANES), BF16)
        vct_ref[...] = vc_ref[0].astype(F32).T.astype(BF16)

    q_t = q_ref[0].astype(F32).T
    flag_rows = jnp.where(lax.broadcasted_iota(jnp.int32, (aug, tq), 0) == 0,
                          MASK_VALUE, 0.0).astype(BF16)
    head_lanes = [(h // 2, slice((h % 2) * tq, (h % 2 + 1) * tq)) for h in range(heads)]
    for h, (x, lanes) in enumerate(head_lanes):
        rows = q_t[h * NSA_DK:(h + 1) * NSA_DK].astype(BF16)
        qwin_ref[x, 0:NSA_DK, lanes] = rows
        qsel_ref[x, 0:NSA_DK, lanes] = rows
        qwin_ref[x, NSA_DK:HEAD_PAD, lanes] = flag_rows

    kc = kc_ref[0]
    kwb = kwp_ref[pl.ds(pl.multiple_of(t0, LANES), wlen), :]
    for x in range(2):
        sc_ref[x] = jnp.dot(kc, qwin_ref[x], preferred_element_type=F32)
    for x in range(2):
        sw_ref[x] = (jnp.dot(kwb, qwin_ref[x], preferred_element_type=F32)
                     + tabw_ref[:, x * hw:(x + 1) * hw])

    first_blk = qi * (tq // CMP_STRIDE)
    tab_c = tabc_ref[pl.ds(pl.multiple_of(ncp - first_blk, CMP_STRIDE), ncp), :]
    o_cmp, p_sum = [], None
    for x in range(2):
        p_x = []
        for hl in range(2):
            h = 2 * x + hl
            s = sc_ref[x, :, hl * tq:(hl + 1) * tq] + tab_c[:, h * tq:(h + 1) * tq]
            m = jnp.max(s, axis=0, keepdims=True)
            e = jnp.exp2(s - m)
            inv = jnp.where(m > 0.5 * MASK_VALUE,
                            1.0 / (jnp.sum(e, axis=0, keepdims=True) + SOFTMAX_EPS), 0.0)
            p = e * inv
            p_sum = p if p_sum is None else p_sum + p
            p_x.append(p.astype(BF16))
        o_cmp.append(jnp.dot(vct_ref[...], jnp.concatenate(p_x, axis=1),
                             preferred_element_type=F32))
    p_hi = p_sum.astype(BF16)
    p_lo = (p_sum - p_hi.astype(F32)).astype(BF16)
    ovt = ovt_ref[...]
    imp = (jnp.dot(ovt, p_hi, preferred_element_type=F32)
           + jnp.dot(ovt, p_lo, preferred_element_type=F32))

    tile0 = qi * (tq // LANES)
    vwb = jnp.concatenate([vwt_ref[tile0 + r] for r in range(wlen // LANES)], axis=1)
    o_win = []
    for x in range(2):
        s = sw_ref[x]
        p = jnp.exp2(s - jnp.max(s, axis=0, keepdims=True))
        pv = jnp.dot(vwb, p.astype(BF16), preferred_element_type=F32)
        o_win.append(pv[0:NSA_DV] / (pv[NSA_DV:NSA_DV + 1] + SOFTMAX_EPS))

    blk = lax.broadcasted_iota(jnp.int32, (n_s, tq), 0)
    cur = (t0 + lax.broadcasted_iota(jnp.int32, (n_s, tq), 1)) // SLC_LEN
    forced = (blk == 0) | (blk == cur) | (blk == cur - 1)
    imp = jnp.where(forced, NEG_BIG, imp)
    imp = jnp.where(blk > cur, -NEG_BIG, imp)
    sub = lax.broadcasted_iota(jnp.int32, (8, tq), 0)
    groups = [imp[8 * g:8 * g + 8] for g in range(n_s // 8)]
    ranks = [jnp.zeros((8, tq), F32) for _ in groups]
    for jp in range(n_s):
        row = imp[jp:jp + 1, :]
        for g, grp in enumerate(groups):
            ge = jnp.where(row >= grp, 1.0, 0.0)
            gt = jnp.where(row > grp, 1.0, 0.0)
            if 8 * g > jp:
                beats = ge
            elif 8 * g + 8 <= jp:
                beats = gt
            else:
                beats = jnp.where(sub + 8 * g > jp, ge, gt)
            ranks[g] = ranks[g] + beats
    sel_rows = jnp.where(jnp.concatenate(ranks, axis=0) < top_n, 0.0, MASK_VALUE)
    if n_s < aug:
        sel_rows = jnp.concatenate([sel_rows, jnp.zeros((aug - n_s, tq), F32)], axis=0)
    for x, lanes in head_lanes:
        qsel_ref[x, NSA_DK:HEAD_PAD, lanes] = sel_rows.astype(BF16)

    jd = t0 // tk

    def scores(j, x):
        kb = j * tk if isinstance(j, int) else pl.multiple_of(j * tk, tk)
        return jnp.dot(kaug_ref[pl.ds(kb, tk), :], qsel_ref[x], preferred_element_type=F32)

    def softmax(x, s, tab, key0):
        u = s + tab
        off = slope_ref[0:1, x * hw:(x + 1) * hw] * (key0 - t0).astype(F32)
        m_old = m_ref[x]
        m_new = jnp.maximum(m_old, jnp.max(u, axis=0, keepdims=True) + off)
        alpha = jnp.exp2(m_old - m_new)
        p = jnp.exp2(u - (m_new - off))
        m_ref[x] = m_new
        return alpha, p.astype(BF16)

    def accumulate(x, alpha, p, vt):
        acc_ref[x] = alpha * acc_ref[x] + jnp.dot(vt, p, preferred_element_type=F32)

    def full_tile(j):
        vt = jnp.concatenate([vst_ref[j * sub_tiles + r] for r in range(sub_tiles)], axis=1)
        for x in range(2):
            alpha, p = softmax(x, s0_ref[x], tabs_ref[:, x * hw:(x + 1) * hw], j * tk)
            accumulate(x, alpha, p, vt)
            s0_ref[x] = scores(j + 1, x)

    def sub_tile(r, tab_ref):
        key0 = jd * tk + r * tq
        rows = pl.ds(r * tq if isinstance(r, int) else pl.multiple_of(r * tq, tq), tq)
        vt = vst_ref[jd * sub_tiles + r]
        for x in range(2):
            alpha, p = softmax(x, s0_ref[x, rows, :], tab_ref[0:tq, x * hw:(x + 1) * hw], key0)
            accumulate(x, alpha, p, vt)

    for x in range(2):
        m_ref[x] = jnp.full((1, hw), MASK_VALUE, F32)
        acc_ref[x] = jnp.zeros((va, hw), F32)
        s0_ref[x] = scores(0, x)

    def body(j, carry):
        full_tile(j)
        return carry

    lax.fori_loop(0, jd, body, 0)
    own = (t0 - jd * tk) // tq
    for r in range(sub_tiles - 1):
        @pl.when(r < own)
        def _():
            sub_tile(r, tabs_ref)
    sub_tile(own, tabd_ref)
    o_slc = [acc_ref[x, 0:NSA_DV, :] / (acc_ref[x, NSA_DV:NSA_DV + 1, :] + SOFTMAX_EPS)
             for x in range(2)]

    gates = _sigmoid(g_ref[0]).T
    z = z_ref[0]
    for h in range(heads):
        x, lanes = h // 2, slice((h % 2) * tq, (h % 2 + 1) * tq)
        c = NSA_DK + 3 * h
        o = (gates[c:c + 1] * o_cmp[x][:, lanes] + gates[c + 1:c + 2] * o_slc[x][:, lanes]
             + gates[c + 2:c + 3] * o_win[x][:, lanes])
        zh = z[:, h * NSA_DV:(h + 1) * NSA_DV]
        o_ref[0, :, h * NSA_DV:(h + 1) * NSA_DV] = (o.T * zh.astype(F32)).astype(BF16)


def _nsa_attn(hb, hf, hg, k_cmp, v_cmp, tables, tq, tk):
    ovt, onehot, tab_sel, tab_diag, tab_win, tab_cmp, slope_rows = tables
    b, s, _ = hb.shape
    ncp = k_cmp.shape[1]
    n_s = s // SLC_LEN
    assert n_s <= HEAD_PAD - NSA_DK and n_s % 8 == 0, "selection blocks must fit the spare rows"
    assert tq & (tq - 1) == 0 and tq % LANES == 0 and tk % tq == 0 and s % tk == 0
    kern = functools.partial(_nsa_attn_kernel, tq=tq, tk=tk, n_c=ncp - 1, n_s=n_s,
                             top_n=min(SLC_TOPN, n_s))
    qw = NSA_HEADS * NSA_DK
    const2 = lambda a: pl.BlockSpec(a.shape, lambda bi, i: (0, 0), pipeline_mode=pl.Buffered(1))
    return pl.pallas_call(
        kern,
        grid=(b, s // tq),
        in_specs=[
            pl.BlockSpec((1, tq, qw), lambda bi, i: (bi, i, B_QNSA // qw)),
            pl.BlockSpec((1, tq, HEAD_PAD), lambda bi, i: (bi, i, F_KC // HEAD_PAD)),
            pl.BlockSpec((1, tq, NSA_WIDTH), lambda bi, i: (bi, i, G_ZNSA // NSA_WIDTH)),
            pl.BlockSpec((1, ncp, HEAD_PAD), lambda bi, i: (bi, 0, 0)),
            pl.BlockSpec((1, ncp, NSA_DV), lambda bi, i: (bi, 0, 0)),
            pl.BlockSpec((1, s, HEAD_PAD), lambda bi, i: (bi, 0, B_KS // HEAD_PAD)),
            pl.BlockSpec((1, s, NSA_DV), lambda bi, i: (bi, 0, B_VS // NSA_DV)),
            pl.BlockSpec((1, s, HEAD_PAD), lambda bi, i: (bi, 0, B_KW // HEAD_PAD)),
            pl.BlockSpec((1, s, NSA_DV), lambda bi, i: (bi, 0, B_VW // NSA_DV)),
            const2(ovt), const2(onehot), const2(tab_sel), const2(tab_diag), const2(tab_win),
            const2(tab_cmp), const2(slope_rows),
        ],
        out_specs=pl.BlockSpec((1, tq, NSA_WIDTH), lambda bi, i: (bi, i, 0)),
        out_shape=jax.ShapeDtypeStruct((b, s, NSA_WIDTH), BF16),
        scratch_shapes=[
            pltpu.VMEM((s, HEAD_PAD), BF16),
            pltpu.VMEM((s + WIN, HEAD_PAD), BF16),
            pltpu.VMEM((s // tq, NSA_DV + SUM_ROWS, tq), BF16),
            pltpu.VMEM(((s + WIN) // LANES, NSA_DV + SUM_ROWS, LANES), BF16),
            pltpu.VMEM((NSA_DV, ncp), BF16),
            pltpu.VMEM((2, 1, 2 * tq), F32),
            pltpu.VMEM((2, NSA_DV + SUM_ROWS, 2 * tq), F32),
            pltpu.VMEM((2, tk, 2 * tq), F32),
            pltpu.VMEM((2, ncp, 2 * tq), F32),
            pltpu.VMEM((2, WIN + tq, 2 * tq), F32),
            pltpu.VMEM((2, HEAD_PAD, 2 * tq), BF16),
            pltpu.VMEM((2, HEAD_PAD, 2 * tq), BF16),
        ],
        compiler_params=_params("arbitrary", "arbitrary"),
        name="nsa_attn",
    )(hb, hf, hg, k_cmp, v_cmp, hb, hb, hb, hb, ovt, onehot, tab_sel, tab_diag, tab_win, tab_cmp,
      slope_rows)


def _mem_attn_kernel(q_ref, k_ref, v_ref, z_ref, o_ref):
    q = q_ref[0]
    k = k_ref[0]
    v = v_ref[0]
    z = z_ref[0]
    for h in range(MEM_HEADS):
        sl = slice(h * MEM_DH, (h + 1) * MEM_DH)
        s = lax.dot_general(q[:, sl], k[:, sl], _NT, preferred_element_type=F32)
        p = jnp.exp(s - jnp.max(s, axis=-1, keepdims=True))
        o = jnp.dot(p.astype(BF16), v[:, sl], preferred_element_type=F32)
        o = o / jnp.sum(p, axis=-1, keepdims=True)
        o_ref[0, :, sl] = (o * z[:, sl].astype(F32)).astype(BF16)


def _mem_attn(hb, mem_kv, hg, tq):
    b, s, _ = hb.shape
    mlen = mem_kv.shape[1]
    return pl.pallas_call(
        _mem_attn_kernel,
        grid=(b, s // tq),
        in_specs=[
            pl.BlockSpec((1, tq, MEM_WIDTH), lambda bi, i: (bi, i, B_QMEM // MEM_WIDTH)),
            pl.BlockSpec((1, mlen, MEM_WIDTH), lambda bi, i: (bi, 0, 0)),
            pl.BlockSpec((1, mlen, MEM_WIDTH), lambda bi, i: (bi, 0, 1)),
            pl.BlockSpec((1, tq, MEM_WIDTH), lambda bi, i: (bi, i, G_ZMEM // MEM_WIDTH)),
        ],
        out_specs=pl.BlockSpec((1, tq, MEM_WIDTH), lambda bi, i: (bi, i, 0)),
        out_shape=jax.ShapeDtypeStruct((b, s, MEM_WIDTH), BF16),
        compiler_params=_params("arbitrary", "arbitrary"),
        name="mem_attn",
    )(hb, mem_kv, mem_kv, hg)


def _out_proj_kernel(x_ref, a_ref, n_ref, m_ref, w_ref, g_ref, o_ref, *, final_norm):
    y = x_ref[...]
    y = y + jnp.dot(a_ref[...], w_ref[0:MLA_WIDTH, :], preferred_element_type=F32)
    y = y + jnp.dot(n_ref[...], w_ref[MLA_WIDTH:MLA_WIDTH + NSA_WIDTH, :], preferred_element_type=F32)
    y = y + jnp.dot(m_ref[...], w_ref[MLA_WIDTH + NSA_WIDTH:, :], preferred_element_type=F32)
    if final_norm:
        y = _rmsnorm(y, g_ref[...])
    o_ref[...] = y


def _out_proj(x, o_mla, o_nsa, o_mem, w_out, g, final_norm, tm):
    m, d = x.shape
    kern = functools.partial(_out_proj_kernel, final_norm=final_norm)
    row = lambda width: pl.BlockSpec((tm, width), lambda i: (i, 0))
    return pl.pallas_call(
        kern,
        grid=(m // tm,),
        in_specs=[row(d), row(MLA_WIDTH), row(NSA_WIDTH), row(MEM_WIDTH),
                  pl.BlockSpec(w_out.shape, lambda i: (0, 0)),
                  pl.BlockSpec((1, d), lambda i: (0, 0))],
        out_specs=row(d),
        out_shape=jax.ShapeDtypeStruct((m, d), F32),
        compiler_params=_params("arbitrary"),
        name="out_proj",
    )(x, o_mla, o_nsa, o_mem, w_out, g.reshape(1, d))


def _pad_cols(w, width):
    return jnp.pad(w, ((0, 0), (0, width - w.shape[1])))


def _w_in_moves():
    names = ("c_q", "c_kv", "k_rope", "z_mla", "q_nsa", "k_c", "v_c", "k_s", "v_s", "k_w", "v_w",
             "g_nsa", "z_nsa", "q_mem", "z_mem")
    src, off = {}, 0
    for name, n in zip(names, IN_SPLITS):
        src[name] = (off, n)
        off += n
    dst = {"c_q": (0, F_CQ), "c_kv": (0, F_CKV), "k_c": (0, F_KC), "g_nsa": (0, F_GATE),
           "v_c": (0, F_VC), "k_rope": (0, F_KROPE),
           "z_mla": (1, G_ZMLA), "z_nsa": (1, G_ZNSA), "z_mem": (1, G_ZMEM),
           "q_nsa": (2, B_QNSA), "q_mem": (2, B_QMEM), "k_s": (2, B_KS), "k_w": (2, B_KW),
           "v_s": (2, B_VS), "v_w": (2, B_VW)}
    return [(src[n][0], src[n][1], dst[n][0], dst[n][1]) for n in dst]


def _w_prep_kernel(wt_ref, wf_ref, wg_ref, wb_ref):
    outs = (wf_ref, wg_ref, wb_ref)
    for o_ref in outs:
        o_ref[...] = jnp.zeros(o_ref.shape, BF16)
    for s0, width, which, d0 in _w_in_moves():
        rows = -(-width // LANES) * LANES
        start = min(s0, wt_ref.shape[0] - rows)
        slab = wt_ref[start:start + rows, :].T
        outs[which][:, d0:d0 + width] = slab[:, s0 - start:s0 - start + width].astype(BF16)


def _layout_w_in(w_in, tr=256):
    d, n = w_in.shape
    widths = (F_WIDTH, G_WIDTH, B_WIDTH)
    wf, wg, wb = pl.pallas_call(
        _w_prep_kernel,
        grid=(d // tr,),
        in_specs=[pl.BlockSpec((n, tr), lambda i: (0, i))],
        out_specs=[pl.BlockSpec((tr, w), lambda i: (i, 0)) for w in widths],
        out_shape=[jax.ShapeDtypeStruct((d, w), BF16) for w in widths],
        compiler_params=_params("arbitrary"),
        name="w_in_layout",
    )(w_in.T)
    scale_b = jnp.ones((B_WIDTH,), F32)
    scale_b = scale_b.at[B_QNSA:B_QNSA + NSA_HEADS * NSA_DK].set(LOG2E * NSA_DK ** -0.5)
    scale_b = scale_b.at[B_QMEM:B_QMEM + MEM_WIDTH].set(MEM_DH ** -0.5)
    return wf, wg, wb, scale_b


def _rope_tables(seq):
    pos = jnp.arange(seq, dtype=F32)
    inv_freq = ROPE_THETA ** (-jnp.arange(0, MLA_ROPE, 2, dtype=F32) / MLA_ROPE)
    ang = pos[:, None] * inv_freq[None, :]
    zeros = jnp.zeros((seq, LANES - MLA_ROPE), F32)
    cos2 = jnp.concatenate([jnp.cos(ang), jnp.cos(ang), zeros], axis=1)
    sin2 = jnp.concatenate([jnp.sin(ang), jnp.sin(ang), zeros], axis=1)
    return cos2, sin2, jnp.cos(ang).T, jnp.sin(ang).T


def _nsa_tables(seq, tq, tk):
    chunks = seq // CMP_STRIDE
    n_s = seq // SLC_LEN
    c_start = jnp.arange(chunks) * CMP_STRIDE
    s_start = jnp.arange(n_s) * SLC_LEN
    overlap_t = ((c_start[None, :] < s_start[:, None] + SLC_LEN)
                 & (c_start[None, :] + CMP_LEN > s_start[:, None])
                 & (jnp.arange(chunks)[None, :] < chunks - 1))
    key_block = jnp.arange(seq) // SLC_LEN
    onehot = jnp.arange(HEAD_PAD)[None, :] == (NSA_DK + key_block)[:, None]
    slope = jnp.repeat(jnp.array([_alibi_slope(h) for h in range(NSA_HEADS)], F32) * LOG2E, tq)
    q_lane = jnp.tile(jnp.arange(tq), NSA_HEADS)
    tab_sel = jnp.arange(tk, dtype=F32)[:, None] * slope[None, :]
    rel = q_lane[None, :] - jnp.arange(WIN + tq)[:, None] + WIN
    tab_win = jnp.where((rel >= 0) & (rel < WIN), -slope[None, :] * rel.astype(F32), MASK_VALUE)
    slope_rows = jnp.broadcast_to(slope[None, :], (8, NSA_HEADS * tq))
    tab_diag = jnp.where(jnp.arange(tq)[:, None] <= q_lane[None, :], tab_sel[0:tq], MASK_VALUE)
    d_blk = jnp.arange(2 * chunks)[:, None] - chunks
    seen = d_blk * CMP_STRIDE + (CMP_LEN - 1) <= q_lane[None, :]
    dist = (q_lane[None, :] - d_blk * CMP_STRIDE).astype(F32) - (CMP_LEN - 1) / 2.0
    tab_cmp = jnp.where(seen, -slope[None, :] * dist, MASK_VALUE)
    return overlap_t.astype(BF16), onehot.astype(BF16), tab_sel, tab_diag, tab_win, tab_cmp, slope_rows


def _layer(x2, mem2, batch, seq, tables, norm_g, w_in, q_norm_g, w_uq, kv_norm_g, w_ukv,
           cmp_pe_k, cmp_pe_v, cmp_w1k, cmp_w2k, cmp_w1v, cmp_w2v, mem_norm_g, w_mem_kv, w_out,
           final_g, final_norm):
    rope, nsa_tables = tables
    d = x2.shape[1]
    wf, wg, wb, scale_b = _layout_w_in(w_in)
    hf, hg, hb = _in_proj(x2, norm_g, wf, wg, wb, scale_b, 512)

    wq = jnp.pad(w_uq.reshape(MLA_Q_RANK, MLA_HEADS, MLA_NOPE + MLA_ROPE),
                 ((0, 0), (0, 0), (0, HEAD_PAD - MLA_NOPE - MLA_ROPE)))
    wqt = wq.reshape(MLA_Q_RANK, MLA_HEADS * HEAD_PAD).T.astype(BF16)
    wkv = w_ukv.reshape(MLA_KV_RANK, MLA_HEADS, MLA_NOPE + MLA_V)
    wk = wkv[:, :, :MLA_NOPE].reshape(MLA_KV_RANK, MLA_HEADS * MLA_NOPE).astype(BF16)
    wvt = wkv[:, :, MLA_NOPE:].reshape(MLA_KV_RANK, MLA_WIDTH).T.astype(BF16)
    qt, k, kpe, vt = _mla_up(hf, q_norm_g, kv_norm_g, wqt, wk, wvt, rope, seq, MLA_TILE)
    hf3 = hf.reshape(batch, seq, F_WIDTH)
    hg3 = hg.reshape(batch, seq, G_WIDTH)
    hb3 = hb.reshape(batch, seq, B_WIDTH)
    o_mla = _mla_attn(qt, k.reshape(batch, seq, -1), kpe.reshape(batch, seq, -1), vt, hg3,
                      MLA_TILE, 4)

    pad_k = HEAD_PAD - NSA_DK
    k_cmp, v_cmp = _compress(
        hf3, _pad_cols(cmp_pe_k, HEAD_PAD), cmp_pe_v,
        jnp.pad(cmp_w1k.reshape(CMP_LEN, NSA_DK, NSA_DK), ((0, 0), (0, pad_k), (0, pad_k))).astype(BF16),
        jnp.pad(cmp_w2k, ((0, pad_k), (0, pad_k))).astype(BF16),
        cmp_w1v.reshape(CMP_LEN, NSA_DV, NSA_DV).astype(BF16), cmp_w2v.astype(BF16))
    o_nsa = _nsa_attn(hb3, hf3, hg3, k_cmp, v_cmp, nsa_tables, NSA_TQ, NSA_TK)

    mem_kv = _norm_proj(mem2, mem_norm_g, w_mem_kv.astype(BF16), jnp.ones((2 * MEM_WIDTH,), F32),
                        BF16, mem2.shape[0] // batch, MEM_WIDTH, "mem_kv_proj")
    o_mem = _mem_attn(hb3, mem_kv.reshape(batch, -1, 2 * MEM_WIDTH), hg3, 512)

    return _out_proj(x2, o_mla.reshape(-1, MLA_WIDTH), o_nsa.reshape(-1, NSA_WIDTH),
                     o_mem.reshape(-1, MEM_WIDTH), w_out.astype(BF16), final_g, final_norm, 512)


def kernel(x, mem, norm_g, w_in, q_norm_g, w_uq, kv_norm_g, w_ukv, cmp_pe_k, cmp_pe_v,
           cmp_w1k, cmp_w2k, cmp_w1v, cmp_w2v, mem_norm_g, w_mem_kv, w_out, final_norm_g):
    batch, seq, d = x.shape
    depth = norm_g.shape[0]
    tables = (_rope_tables(seq), _nsa_tables(seq, NSA_TQ, NSA_TK))
    x2 = x.reshape(batch * seq, d)
    mem2 = mem.reshape(batch * mem.shape[1], d)
    for l in range(depth):
        x2 = _layer(x2, mem2, batch, seq, tables, norm_g[l], w_in[l], q_norm_g[l], w_uq[l],
                    kv_norm_g[l], w_ukv[l], cmp_pe_k[l], cmp_pe_v[l], cmp_w1k[l], cmp_w2k[l],
                    cmp_w1v[l], cmp_w2v[l], mem_norm_g[l], w_mem_kv[l], w_out[l],
                    final_norm_g, l == depth - 1)
    return x2.reshape(batch, seq, d)
```

```python
import functools

import jax
import jax.numpy as jnp
from jax import lax
from jax.experimental import pallas as pl
from jax.experimental.pallas import tpu as pltpu

F32 = jnp.float32
BF16 = jnp.bfloat16

EPS = 1e-6
NEG_BIG = 1e9
MASK_VALUE = -1e30
SOFTMAX_EPS = 1e-20
LOG2E = 1.4426950408889634

MLA_HEADS = 8
MLA_NOPE = 128
MLA_ROPE = 64
MLA_V = 128
MLA_Q_RANK = 512
MLA_KV_RANK = 512
ROPE_THETA = 10000.0

NSA_HEADS = 4
NSA_DK = 192
NSA_DV = 128
CMP_LEN = 32
CMP_STRIDE = 16
SLC_LEN = 64
SLC_TOPN = 16
WIN = 512

MEM_HEADS = 4
MEM_DH = 128

MLA_WIDTH = MLA_HEADS * MLA_V
NSA_WIDTH = NSA_HEADS * NSA_DV
MEM_WIDTH = MEM_HEADS * MEM_DH

IN_SPLITS = (
    MLA_Q_RANK, MLA_KV_RANK, MLA_ROPE, MLA_WIDTH,
    NSA_HEADS * NSA_DK, NSA_DK, NSA_DV, NSA_DK, NSA_DV,
    NSA_DK, NSA_DV, 3 * NSA_HEADS, NSA_WIDTH,
    MEM_WIDTH, MEM_WIDTH,
)

LANES = 128
MXU_COLS = 256
HEAD_PAD = 256
VMEM_LIMIT = 56 * 1024 * 1024
MLA_TILE = 512
SUM_ROWS = 16
NSA_TQ = 256
NSA_TK = 512

F_CQ, F_CKV, F_KC, F_VC, F_KROPE = 0, 512, 1024, 1280, 1408
F_GATE = F_KC + NSA_DK
F_WIDTH = 1536
G_ZMLA, G_ZNSA, G_ZMEM = 0, MLA_WIDTH, MLA_WIDTH + NSA_WIDTH
G_WIDTH = MLA_WIDTH + NSA_WIDTH + MEM_WIDTH
B_QNSA, B_KS, B_QMEM, B_KW, B_VS, B_VW = 0, 768, 1024, 1536, 1792, 1920
B_WIDTH = 2048

_NT = (((1,), (1,)), ((), ()))


def _params(*sem):
    return pltpu.CompilerParams(dimension_semantics=sem, vmem_limit_bytes=VMEM_LIMIT)


def _sigmoid(x):
    return 1.0 / (1.0 + jnp.exp(-x))


def _silu(x):
    return x * _sigmoid(x)


def _rmsnorm(x, g):
    ms = jnp.mean(x * x, axis=-1, keepdims=True)
    return (x * lax.rsqrt(ms + EPS)) * g


def _norm_proj_kernel(x_ref, g_ref, w_ref, cs_ref, o_ref, xn_ref):
    @pl.when(pl.program_id(1) == 0)
    def _():
        xn_ref[...] = _rmsnorm(x_ref[...], g_ref[...]).astype(BF16)

    acc = jnp.dot(xn_ref[...], w_ref[...], preferred_element_type=F32)
    o_ref[...] = (acc * cs_ref[...]).astype(o_ref.dtype)


def _norm_proj(x, g, w, colscale, out_dtype, tm, tn, name):
    m, k = x.shape
    n = w.shape[1]
    return pl.pallas_call(
        _norm_proj_kernel,
        grid=(m // tm, n // tn),
        in_specs=[
            pl.BlockSpec((tm, k), lambda i, j: (i, 0)),
            pl.BlockSpec((1, k), lambda i, j: (0, 0)),
            pl.BlockSpec((k, tn), lambda i, j: (0, j)),
            pl.BlockSpec((1, tn), lambda i, j: (0, j)),
        ],
        out_specs=pl.BlockSpec((tm, tn), lambda i, j: (i, j)),
        out_shape=jax.ShapeDtypeStruct((m, n), out_dtype),
        scratch_shapes=[pltpu.VMEM((tm, k), BF16)],
        compiler_params=_params("arbitrary", "arbitrary"),
        name=name,
    )(x, g.reshape(1, k), w, colscale.reshape(1, n))


def _in_proj_kernel(x_ref, g_ref, wf_ref, wg_ref, wb_ref, cs_ref, of_ref, og_ref, ob_ref, xn_ref):
    j = pl.program_id(1)

    @pl.when(j == 0)
    def _():
        xn_ref[...] = _rmsnorm(x_ref[...], g_ref[...]).astype(BF16)
        of_ref[...] = jnp.dot(xn_ref[...], wf_ref[...], preferred_element_type=F32)

    @pl.when(j == 1)
    def _():
        og_ref[...] = _silu(jnp.dot(xn_ref[...], wg_ref[...], preferred_element_type=F32)).astype(BF16)

    @pl.when(j == 2)
    def _():
        acc = jnp.dot(xn_ref[...], wb_ref[...], preferred_element_type=F32)
        ob_ref[...] = (acc * cs_ref[...]).astype(BF16)


def _in_proj(x, g, wf, wg, wb, scale_b, tm):
    m, k = x.shape
    whole = lambda a: pl.BlockSpec(a.shape, lambda i, j: (0, 0), pipeline_mode=pl.Buffered(1))
    rows = lambda a: pl.BlockSpec((tm, a.shape[1]), lambda i, j: (i, 0))
    cs = scale_b.reshape(1, -1)
    return pl.pallas_call(
        _in_proj_kernel,
        grid=(m // tm, 3),
        in_specs=[rows(x), pl.BlockSpec((1, k), lambda i, j: (0, 0)),
                  whole(wf), whole(wg), whole(wb), whole(cs)],
        out_specs=[rows(wf), rows(wg), rows(wb)],
        out_shape=[jax.ShapeDtypeStruct((m, wf.shape[1]), F32),
                   jax.ShapeDtypeStruct((m, wg.shape[1]), BF16),
                   jax.ShapeDtypeStruct((m, wb.shape[1]), BF16)],
        scratch_shapes=[pltpu.VMEM((tm, k), BF16)],
        compiler_params=_params("arbitrary", "arbitrary"),
        name="in_proj",
    )(x, g.reshape(1, k), wf, wg, wb, cs)


def _rope_half(hi, cos2, sin2):
    up = pltpu.roll(hi, 32, axis=1)
    down = pltpu.roll(hi, 96, axis=1)
    return hi * cos2 + (up - down) * sin2


def _mla_up_kernel(cq_ref, ckv_ref, kr_ref, gq_ref, gkv_ref, wqt_ref, wk_ref, wvt_ref,
                   cos_ref, sin_ref, cost_ref, sint_ref, qt_ref, k_ref, kpe_ref, vt_ref):
    scale = LOG2E * (MLA_NOPE + MLA_ROPE) ** -0.5
    half = MLA_ROPE // 2
    cos_t = cost_ref[...]
    sin_t = sint_ref[...]

    cqn = _rmsnorm(cq_ref[...], gq_ref[...]).astype(BF16)
    yt = lax.dot_general(wqt_ref[...], cqn, _NT, preferred_element_type=F32)
    for h in range(MLA_HEADS):
        r = h * HEAD_PAD
        x1 = yt[r + MLA_NOPE:r + MLA_NOPE + half]
        x2 = yt[r + MLA_NOPE + half:r + MLA_NOPE + MLA_ROPE]
        qt_ref[0, r:r + MLA_NOPE, :] = (yt[r:r + MLA_NOPE] * scale).astype(BF16)
        qt_ref[0, r + MLA_NOPE:r + MLA_NOPE + half, :] = (
            (x1 * cos_t - x2 * sin_t) * scale).astype(BF16)
        qt_ref[0, r + MLA_NOPE + half:r + MLA_NOPE + MLA_ROPE, :] = (
            (x1 * sin_t + x2 * cos_t) * scale).astype(BF16)
        qt_ref[0, r + MLA_NOPE + MLA_ROPE:r + HEAD_PAD, :] = jnp.zeros(
            (HEAD_PAD - MLA_NOPE - MLA_ROPE, yt.shape[1]), BF16)

    ckn = _rmsnorm(ckv_ref[...], gkv_ref[...]).astype(BF16)
    kn = jnp.dot(ckn, wk_ref[...], preferred_element_type=F32)
    k_pe = _rope_half(kr_ref[...], cos_ref[...], sin_ref[...]).astype(BF16)
    k_ref[...] = kn.astype(BF16)
    kpe_ref[...] = k_pe
    vt = lax.dot_general(wvt_ref[...], ckn, _NT, preferred_element_type=F32)
    ones = jnp.ones((SUM_ROWS, vt.shape[1]), BF16)
    for h in range(MLA_HEADS):
        r = h * (MLA_V + SUM_ROWS)
        vt_ref[0, r:r + MLA_V, :] = vt[h * MLA_V:(h + 1) * MLA_V].astype(BF16)
        vt_ref[0, r + MLA_V:r + MLA_V + SUM_ROWS, :] = ones


def _mla_up(hf, gq, gkv, wqt, wk, wvt, rope, seq, tm):
    cos2, sin2, cos_t, sin_t = rope
    m = hf.shape[0]
    steps_per_seq = seq // tm
    hq = MLA_HEADS * HEAD_PAD
    half = MLA_ROPE // 2
    const = lambda i: (0, 0)
    return pl.pallas_call(
        _mla_up_kernel,
        grid=(m // tm,),
        in_specs=[
            pl.BlockSpec((tm, MLA_Q_RANK), lambda i: (i, F_CQ // MLA_Q_RANK)),
            pl.BlockSpec((tm, MLA_KV_RANK), lambda i: (i, F_CKV // MLA_KV_RANK)),
            pl.BlockSpec((tm, LANES), lambda i: (i, F_KROPE // LANES)),
            pl.BlockSpec((1, MLA_Q_RANK), const),
            pl.BlockSpec((1, MLA_KV_RANK), const),
            pl.BlockSpec((hq, MLA_Q_RANK), const),
            pl.BlockSpec((MLA_KV_RANK, MLA_HEADS * MLA_NOPE), const),
            pl.BlockSpec((MLA_WIDTH, MLA_KV_RANK), const),
            pl.BlockSpec((tm, LANES), lambda i: (i % steps_per_seq, 0)),
            pl.BlockSpec((tm, LANES), lambda i: (i % steps_per_seq, 0)),
            pl.BlockSpec((half, tm), lambda i: (0, i % steps_per_seq)),
            pl.BlockSpec((half, tm), lambda i: (0, i % steps_per_seq)),
        ],
        out_specs=[
            pl.BlockSpec((1, hq, tm), lambda i: (i, 0, 0)),
            pl.BlockSpec((tm, MLA_HEADS * MLA_NOPE), lambda i: (i, 0)),
            pl.BlockSpec((tm, LANES), lambda i: (i, 0)),
            pl.BlockSpec((1, MLA_HEADS * (MLA_V + SUM_ROWS), tm), lambda i: (i, 0, 0)),
        ],
        out_shape=[
            jax.ShapeDtypeStruct((m // tm, hq, tm), BF16),
            jax.ShapeDtypeStruct((m, MLA_HEADS * MLA_NOPE), BF16),
            jax.ShapeDtypeStruct((m, LANES), BF16),
            jax.ShapeDtypeStruct((m // tm, MLA_HEADS * (MLA_V + SUM_ROWS), tm), BF16),
        ],
        compiler_params=_params("arbitrary"),
        name="mla_up",
    )(hf, hf, hf, gq.reshape(1, -1), gkv.reshape(1, -1), wqt, wk, wvt, cos2, sin2, cos_t, sin_t)


def _mla_attn_kernel(qt_ref, k_ref, kpe_ref, vt_ref, z_ref, o_ref, m_ref, acc_ref, s0_ref, *,
                     tile_len, hps):
    t = tile_len
    va = MLA_V + SUM_ROWS
    qi = pl.program_id(2)

    def scores(j, h):
        kb = pl.multiple_of(j * t, t)
        k = jnp.concatenate([k_ref[0, pl.ds(kb, t), h * MLA_NOPE:(h + 1) * MLA_NOPE],
                             kpe_ref[0, pl.ds(kb, t), :]], axis=1)
        q_t = qt_ref[0, h * HEAD_PAD:(h + 1) * HEAD_PAD, :]
        return jnp.dot(k, q_t, preferred_element_type=F32)

    def softmax(h, s, diagonal):
        if diagonal:
            kpos = lax.broadcasted_iota(jnp.int32, (t, t), 0)
            qpos = lax.broadcasted_iota(jnp.int32, (t, t), 1)
            s = jnp.where(kpos <= qpos, s, MASK_VALUE)
        m = m_ref[h]
        m_new = jnp.maximum(m, jnp.max(s, axis=0, keepdims=True))
        alpha = jnp.exp2(m - m_new)
        p = jnp.exp2(s - m_new)
        m_ref[h] = m_new
        return alpha, p.astype(BF16)

    def accumulate(j, h, alpha, p):
        vt = vt_ref[j, h * va:(h + 1) * va, :]
        acc_ref[h] = alpha * acc_ref[h] + jnp.dot(vt, p, preferred_element_type=F32)

    ahead = hps // 2

    def tile(j, diagonal):
        for h in range(ahead, hps):
            s0_ref[h] = scores(j, h)
        for h in range(hps):
            alpha, p = softmax(h, s0_ref[h], diagonal)
            accumulate(j, h, alpha, p)
            if h < ahead and not diagonal:
                s0_ref[h] = scores(j + 1, h)

    for h in range(hps):
        m_ref[h] = jnp.full((1, t), MASK_VALUE, F32)
        acc_ref[h] = jnp.zeros((va, t), F32)
    for h in range(ahead):
        s0_ref[h] = scores(0, h)

    def body(j, carry):
        tile(j, False)
        return carry

    lax.fori_loop(0, qi, body, 0)

    tile(qi, True)
    for h in range(hps):
        l = acc_ref[h, MLA_V:MLA_V + 1, :]
        o = (acc_ref[h, 0:MLA_V, :] / (l + SOFTMAX_EPS)).T
        cols = slice(h * MLA_V, (h + 1) * MLA_V)
        o_ref[0, :, cols] = (o * z_ref[0, :, cols].astype(F32)).astype(BF16)


def _mla_attn(qt, k, kpe, vt, hg, tile_len, hps):
    b, s, _ = k.shape
    t = tile_len
    tiles = s // t
    kern = functools.partial(_mla_attn_kernel, tile_len=t, hps=hps)
    qw, vw, va = hps * HEAD_PAD, hps * MLA_V, MLA_V + SUM_ROWS
    return pl.pallas_call(
        kern,
        grid=(b, MLA_HEADS // hps, tiles),
        in_specs=[
            pl.BlockSpec((1, qw, t), lambda bi, h, i: (bi * tiles + i, h, 0)),
            pl.BlockSpec((1, s, hps * MLA_NOPE), lambda bi, h, i: (bi, 0, h)),
            pl.BlockSpec((1, s, LANES), lambda bi, h, i: (bi, 0, 0)),
            pl.BlockSpec((tiles, hps * va, t), lambda bi, h, i: (bi, h, 0)),
            pl.BlockSpec((1, t, vw), lambda bi, h, i: (bi, i, G_ZMLA // vw + h)),
        ],
        out_specs=pl.BlockSpec((1, t, vw), lambda bi, h, i: (bi, i, h)),
        out_shape=jax.ShapeDtypeStruct((b, s, MLA_WIDTH), BF16),
        scratch_shapes=[
            pltpu.VMEM((hps, 1, t), F32),
            pltpu.VMEM((hps, va, t), F32),
            pltpu.VMEM((hps, t, t), F32),
        ],
        compiler_params=_params("arbitrary", "arbitrary", "arbitrary"),
        name="mla_attn",
    )(qt, k, kpe, vt, hg)


def _compress_one(x_refs, pe_ref, w1_ref, w2_ref, o_ref):
    chunks = x_refs[0].shape[1] // CMP_STRIDE
    a = b = None
    for i in range(CMP_STRIDE):
        for c, x_ref in enumerate(x_refs):
            lanes = slice(c * LANES, (c + 1) * LANES)
            x = x_ref[0, pl.ds(i, chunks, stride=CMP_STRIDE), :]
            ai = jnp.dot((x + pe_ref[i:i + 1, lanes]).astype(BF16), w1_ref[i, lanes, :],
                         preferred_element_type=F32)
            bi = jnp.dot((x + pe_ref[CMP_STRIDE + i:CMP_STRIDE + i + 1, lanes]).astype(BF16),
                         w1_ref[CMP_STRIDE + i, lanes, :], preferred_element_type=F32)
            a = ai if a is None else a + ai
            b = bi if b is None else b + bi
    h1 = a + pltpu.roll(b, chunks - 1, axis=0)
    o_ref[0] = jnp.dot(_silu(h1).astype(BF16), w2_ref[...], preferred_element_type=F32).astype(BF16)


def _compress_kernel(xk0_ref, xk1_ref, xv_ref, pek_ref, pev_ref, w1k_ref, w2k_ref, w1v_ref, w2v_ref,
                     ok_ref, ov_ref):
    _compress_one((xk0_ref, xk1_ref), pek_ref, w1k_ref, w2k_ref, ok_ref)
    _compress_one((xv_ref,), pev_ref, w1v_ref, w2v_ref, ov_ref)


def _compress(hf, pek, pev, w1k, w2k, w1v, w2v):
    b, s, _ = hf.shape
    chunks = s // CMP_STRIDE
    full = lambda a: pl.BlockSpec(a.shape, lambda bi: (0,) * a.ndim)
    return pl.pallas_call(
        _compress_kernel,
        grid=(b,),
        in_specs=[pl.BlockSpec((1, s, LANES), lambda bi: (bi, 0, F_KC // LANES)),
                  pl.BlockSpec((1, s, LANES), lambda bi: (bi, 0, F_KC // LANES + 1)),
                  pl.BlockSpec((1, s, NSA_DV), lambda bi: (bi, 0, F_VC // NSA_DV)),
                  full(pek), full(pev), full(w1k), full(w2k), full(w1v), full(w2v)],
        out_specs=[
            pl.BlockSpec((1, chunks, HEAD_PAD), lambda bi: (bi, 0, 0)),
            pl.BlockSpec((1, chunks, NSA_DV), lambda bi: (bi, 0, 0)),
        ],
        out_shape=[
            jax.ShapeDtypeStruct((b, chunks, HEAD_PAD), BF16),
            jax.ShapeDtypeStruct((b, chunks, NSA_DV), BF16),
        ],
        compiler_params=_params("arbitrary"),
        name="nsa_compress",
    )(hf, hf, hf, pek, pev, w1k, w2k, w1v, w2v)


def _alibi_slope(h):
    return 2.0 ** (-8.0 * (h + 1) / NSA_HEADS)


def _nsa_attn_kernel(q_ref, g_ref, z_ref, kc_ref, vc_ref, ks_ref, vs_ref, kw_ref, vw_ref,
                     ovt_ref, oh_ref, tabs_ref, tabd_ref, tabw_ref, tabc_ref, slope_ref, o_ref,
                     kaug_ref, kwp_ref, vst_ref, vwt_ref, vct_ref, m_ref, acc_ref, s0_ref,
                     sc_ref, sw_ref, qwin_ref, qsel_ref,
                     *, tq, tk, n_c, n_s, top_n):
    qi = pl.program_id(1)
    t0 = qi * tq
    heads = NSA_HEADS
    seq = ks_ref.shape[1]
    ncp = kc_ref.shape[1]
    hw = 2 * tq
    aug = HEAD_PAD - NSA_DK
    wlen = WIN + tq
    va = NSA_DV + SUM_ROWS
    sub_tiles = tk // tq

    @pl.when(qi == 0)
    def _():
        kaug_ref[...] = ks_ref[0] + oh_ref[...]
        col = lax.broadcasted_iota(jnp.int32, (WIN, HEAD_PAD), 1)
        kwp_ref[0:WIN, :] = jnp.where(col == NSA_DK, 1.0, 0.0).astype(BF16)
        kwp_ref[WIN:, :] = kw_ref[0]
        for i in range(seq // tq):
            vst_ref[i, 0:NSA_DV, :] = vs_ref[0, i * tq:(i + 1) * tq, :].astype(F32).T.astype(BF16)
            vst_ref[i, NSA_DV:va, :] = jnp.ones((SUM_ROWS, tq), BF16)
        for i in range(WIN // LANES):
            vwt_ref[i] = jnp.zeros((va, LANES), BF16)
        for i in range(seq // LANES):
            vwt_ref[WIN // LANES + i, 0:NSA_DV, :] = (
                vw_ref[0, i * LANES:(i + 1) * LANES, :].astype(F32).T.astype(BF16))
            vwt_ref[WIN // LANES + i, NSA_DV:va, :] = jnp.ones((SUM_ROWS, LANES), BF16)
        vct_ref[...] = vc_ref[0].astype(F32).T.astype(BF16)

    q_t = q_ref[0].astype(F32).T
    flag_rows = jnp.where(lax.broadcasted_iota(jnp.int32, (aug, tq), 0) == 0,
                          MASK_VALUE, 0.0).astype(BF16)
    head_lanes = [(h // 2, slice((h % 2) * tq, (h % 2 + 1) * tq)) for h in range(heads)]
    for h, (x, lanes) in enumerate(head_lanes):
        rows = q_t[h * NSA_DK:(h + 1) * NSA_DK].astype(BF16)
        qwin_ref[x, 0:NSA_DK, lanes] = rows
        qsel_ref[x, 0:NSA_DK, lanes] = rows
        qwin_ref[x, NSA_DK:HEAD_PAD, lanes] = flag_rows

    kc = kc_ref[0]
    kwb = kwp_ref[pl.ds(pl.multiple_of(t0, LANES), wlen), :]
    for x in range(2):
        sc_ref[x] = jnp.dot(kc, qwin_ref[x], preferred_element_type=F32)
    for x in range(2):
        sw_ref[x] = (jnp.dot(kwb, qwin_ref[x], preferred_element_type=F32)
                     + tabw_ref[:, x * hw:(x + 1) * hw])

    first_blk = qi * (tq // CMP_STRIDE)
    tab_c = tabc_ref[pl.ds(pl.multiple_of(ncp - first_blk, CMP_STRIDE), ncp), :]
    o_cmp, p_sum = [], None
    for x in range(2):
        p_x = []
        for hl in range(2):
            h = 2 * x + hl
            s = sc_ref[x, :, hl * tq:(hl + 1) * tq] + tab_c[:, h * tq:(h + 1) * tq]
            m = jnp.max(s, axis=0, keepdims=True)
            e = jnp.exp2(s - m)
            inv = jnp.where(m > 0.5 * MASK_VALUE,
                            1.0 / (jnp.sum(e, axis=0, keepdims=True) + SOFTMAX_EPS), 0.0)
            p = e * inv
            p_sum = p if p_sum is None else p_sum + p
            p_x.append(p.astype(BF16))
        o_cmp.append(jnp.dot(vct_ref[...], jnp.concatenate(p_x, axis=1),
                             preferred_element_type=F32))
    p_hi = p_sum.astype(BF16)
    p_lo = (p_sum - p_hi.astype(F32)).astype(BF16)
    ovt = ovt_ref[...]
    imp = (jnp.dot(ovt, p_hi, preferred_element_type=F32)
           + jnp.dot(ovt, p_lo, preferred_element_type=F32))

    tile0 = qi * (tq // LANES)
    vwb = jnp.concatenate([vwt_ref[tile0 + r] for r in range(wlen // LANES)], axis=1)
    o_win = []
    for x in range(2):
        s = sw_ref[x]
        p = jnp.exp2(s - jnp.max(s, axis=0, keepdims=True))
        pv = jnp.dot(vwb, p.astype(BF16), preferred_element_type=F32)
        o_win.append(pv[0:NSA_DV] / (pv[NSA_DV:NSA_DV + 1] + SOFTMAX_EPS))

    blk = lax.broadcasted_iota(jnp.int32, (n_s, tq), 0)
    cur = (t0 + lax.broadcasted_iota(jnp.int32, (n_s, tq), 1)) // SLC_LEN
    forced = (blk == 0) | (blk == cur) | (blk == cur - 1)
    imp = jnp.where(forced, NEG_BIG, imp)
    imp = jnp.where(blk > cur, -NEG_BIG, imp)
    sub = lax.broadcasted_iota(jnp.int32, (8, tq), 0)
    groups = [imp[8 * g:8 * g + 8] for g in range(n_s // 8)]
    ranks = [jnp.zeros((8, tq), F32) for _ in groups]
    for jp in range(n_s):
        row = imp[jp:jp + 1, :]
        for g, grp in enumerate(groups):
            ge = jnp.where(row >= grp, 1.0, 0.0)
            gt = jnp.where(row > grp, 1.0, 0.0)
            if 8 * g > jp:
                beats = ge
            elif 8 * g + 8 <= jp:
                beats = gt
            else:
                beats = jnp.where(sub + 8 * g > jp, ge, gt)
            ranks[g] = ranks[g] + beats
    sel_rows = jnp.where(jnp.concatenate(ranks, axis=0) < top_n, 0.0, MASK_VALUE)
    if n_s < aug:
        sel_rows = jnp.concatenate([sel_rows, jnp.zeros((aug - n_s, tq), F32)], axis=0)
    for x, lanes in head_lanes:
        qsel_ref[x, NSA_DK:HEAD_PAD, lanes] = sel_rows.astype(BF16)

    jd = t0 // tk

    def scores(j, x):
        kb = j * tk if isinstance(j, int) else pl.multiple_of(j * tk, tk)
        return jnp.dot(kaug_ref[pl.ds(kb, tk), :], qsel_ref[x], preferred_element_type=F32)

    def softmax(x, s, tab, key0):
        u = s + tab
        off = slope_ref[0:1, x * hw:(x + 1) * hw] * (key0 - t0).astype(F32)
        m_old = m_ref[x]
        m_new = jnp.maximum(m_old, jnp.max(u, axis=0, keepdims=True) + off)
        alpha = jnp.exp2(m_old - m_new)
        p = jnp.exp2(u - (m_new - off))
        m_ref[x] = m_new
        return alpha, p.astype(BF16)

    def accumulate(x, alpha, p, vt):
        acc_ref[x] = alpha * acc_ref[x] + jnp.dot(vt, p, preferred_element_type=F32)

    def full_tile(j):
        vt = jnp.concatenate([vst_ref[j * sub_tiles + r] for r in range(sub_tiles)], axis=1)
        s0_ref[1] = scores(j, 1)
        for x in range(2):
            alpha, p = softmax(x, s0_ref[x], tabs_ref[:, x * hw:(x + 1) * hw], j * tk)
            accumulate(x, alpha, p, vt)
            if x == 0:
                s0_ref[0] = scores(j + 1, 0)

    def sub_tile(r, tab_ref):
        key0 = jd * tk + r * tq
        rows = pl.ds(r * tq if isinstance(r, int) else pl.multiple_of(r * tq, tq), tq)
        vt = vst_ref[jd * sub_tiles + r]
        for x in range(2):
            alpha, p = softmax(x, s0_ref[x, rows, :], tab_ref[0:tq, x * hw:(x + 1) * hw], key0)
            accumulate(x, alpha, p, vt)

    for x in range(2):
        m_ref[x] = jnp.full((1, hw), MASK_VALUE, F32)
        acc_ref[x] = jnp.zeros((va, hw), F32)
    s0_ref[0] = scores(0, 0)

    def body(j, carry):
        full_tile(j)
        return carry

    lax.fori_loop(0, jd, body, 0)
    s0_ref[1] = scores(jd, 1)
    own = (t0 - jd * tk) // tq
    for r in range(sub_tiles - 1):
        @pl.when(r < own)
        def _():
            sub_tile(r, tabs_ref)
    sub_tile(own, tabd_ref)
    o_slc = [acc_ref[x, 0:NSA_DV, :] / (acc_ref[x, NSA_DV:NSA_DV + 1, :] + SOFTMAX_EPS)
             for x in range(2)]

    gates = _sigmoid(g_ref[0]).T
    z = z_ref[0]
    for h in range(heads):
        x, lanes = h // 2, slice((h % 2) * tq, (h % 2 + 1) * tq)
        c = NSA_DK + 3 * h
        o = (gates[c:c + 1] * o_cmp[x][:, lanes] + gates[c + 1:c + 2] * o_slc[x][:, lanes]
             + gates[c + 2:c + 3] * o_win[x][:, lanes])
        zh = z[:, h * NSA_DV:(h + 1) * NSA_DV]
        o_ref[0, :, h * NSA_DV:(h + 1) * NSA_DV] = (o.T * zh.astype(F32)).astype(BF16)


def _nsa_attn(hb, hf, hg, k_cmp, v_cmp, tables, tq, tk):
    ovt, onehot, tab_sel, tab_diag, tab_win, tab_cmp, slope_rows = tables
    b, s, _ = hb.shape
    ncp = k_cmp.shape[1]
    n_s = s // SLC_LEN
    assert n_s <= HEAD_PAD - NSA_DK and n_s % 8 == 0, "selection blocks must fit the spare rows"
    assert tq & (tq - 1) == 0 and tq % LANES == 0 and tk % tq == 0 and s % tk == 0
    kern = functools.partial(_nsa_attn_kernel, tq=tq, tk=tk, n_c=ncp - 1, n_s=n_s,
                             top_n=min(SLC_TOPN, n_s))
    qw = NSA_HEADS * NSA_DK
    const2 = lambda a: pl.BlockSpec(a.shape, lambda bi, i: (0, 0), pipeline_mode=pl.Buffered(1))
    return pl.pallas_call(
        kern,
        grid=(b, s // tq),
        in_specs=[
            pl.BlockSpec((1, tq, qw), lambda bi, i: (bi, i, B_QNSA // qw)),
            pl.BlockSpec((1, tq, HEAD_PAD), lambda bi, i: (bi, i, F_KC // HEAD_PAD)),
            pl.BlockSpec((1, tq, NSA_WIDTH), lambda bi, i: (bi, i, G_ZNSA // NSA_WIDTH)),
            pl.BlockSpec((1, ncp, HEAD_PAD), lambda bi, i: (bi, 0, 0)),
            pl.BlockSpec((1, ncp, NSA_DV), lambda bi, i: (bi, 0, 0)),
            pl.BlockSpec((1, s, HEAD_PAD), lambda bi, i: (bi, 0, B_KS // HEAD_PAD)),
            pl.BlockSpec((1, s, NSA_DV), lambda bi, i: (bi, 0, B_VS // NSA_DV)),
            pl.BlockSpec((1, s, HEAD_PAD), lambda bi, i: (bi, 0, B_KW // HEAD_PAD)),
            pl.BlockSpec((1, s, NSA_DV), lambda bi, i: (bi, 0, B_VW // NSA_DV)),
            const2(ovt), const2(onehot), const2(tab_sel), const2(tab_diag), const2(tab_win),
            const2(tab_cmp), const2(slope_rows),
        ],
        out_specs=pl.BlockSpec((1, tq, NSA_WIDTH), lambda bi, i: (bi, i, 0)),
        out_shape=jax.ShapeDtypeStruct((b, s, NSA_WIDTH), BF16),
        scratch_shapes=[
            pltpu.VMEM((s, HEAD_PAD), BF16),
            pltpu.VMEM((s + WIN, HEAD_PAD), BF16),
            pltpu.VMEM((s // tq, NSA_DV + SUM_ROWS, tq), BF16),
            pltpu.VMEM(((s + WIN) // LANES, NSA_DV + SUM_ROWS, LANES), BF16),
            pltpu.VMEM((NSA_DV, ncp), BF16),
            pltpu.VMEM((2, 1, 2 * tq), F32),
            pltpu.VMEM((2, NSA_DV + SUM_ROWS, 2 * tq), F32),
            pltpu.VMEM((2, tk, 2 * tq), F32),
            pltpu.VMEM((2, ncp, 2 * tq), F32),
            pltpu.VMEM((2, WIN + tq, 2 * tq), F32),
            pltpu.VMEM((2, HEAD_PAD, 2 * tq), BF16),
            pltpu.VMEM((2, HEAD_PAD, 2 * tq), BF16),
        ],
        compiler_params=_params("arbitrary", "arbitrary"),
        name="nsa_attn",
    )(hb, hf, hg, k_cmp, v_cmp, hb, hb, hb, hb, ovt, onehot, tab_sel, tab_diag, tab_win, tab_cmp,
      slope_rows)


def _mem_attn_kernel(q_ref, k_ref, v_ref, z_ref, o_ref):
    q = q_ref[0]
    k = k_ref[0]
    v = v_ref[0]
    z = z_ref[0]
    for h in range(MEM_HEADS):
        sl = slice(h * MEM_DH, (h + 1) * MEM_DH)
        s = lax.dot_general(q[:, sl], k[:, sl], _NT, preferred_element_type=F32)
        p = jnp.exp(s - jnp.max(s, axis=-1, keepdims=True))
        o = jnp.dot(p.astype(BF16), v[:, sl], preferred_element_type=F32)
        o = o / jnp.sum(p, axis=-1, keepdims=True)
        o_ref[0, :, sl] = (o * z[:, sl].astype(F32)).astype(BF16)


def _mem_attn(hb, mem_kv, hg, tq):
    b, s, _ = hb.shape
    mlen = mem_kv.shape[1]
    return pl.pallas_call(
        _mem_attn_kernel,
        grid=(b, s // tq),
        in_specs=[
            pl.BlockSpec((1, tq, MEM_WIDTH), lambda bi, i: (bi, i, B_QMEM // MEM_WIDTH)),
            pl.BlockSpec((1, mlen, MEM_WIDTH), lambda bi, i: (bi, 0, 0)),
            pl.BlockSpec((1, mlen, MEM_WIDTH), lambda bi, i: (bi, 0, 1)),
            pl.BlockSpec((1, tq, MEM_WIDTH), lambda bi, i: (bi, i, G_ZMEM // MEM_WIDTH)),
        ],
        out_specs=pl.BlockSpec((1, tq, MEM_WIDTH), lambda bi, i: (bi, i, 0)),
        out_shape=jax.ShapeDtypeStruct((b, s, MEM_WIDTH), BF16),
        compiler_params=_params("arbitrary", "arbitrary"),
        name="mem_attn",
    )(hb, mem_kv, mem_kv, hg)


def _out_proj_kernel(x_ref, a_ref, n_ref, m_ref, w_ref, g_ref, o_ref, *, final_norm):
    y = x_ref[...]
    y = y + jnp.dot(a_ref[...], w_ref[0:MLA_WIDTH, :], preferred_element_type=F32)
    y = y + jnp.dot(n_ref[...], w_ref[MLA_WIDTH:MLA_WIDTH + NSA_WIDTH, :], preferred_element_type=F32)
    y = y + jnp.dot(m_ref[...], w_ref[MLA_WIDTH + NSA_WIDTH:, :], preferred_element_type=F32)
    if final_norm:
        y = _rmsnorm(y, g_ref[...])
    o_ref[...] = y


def _out_proj(x, o_mla, o_nsa, o_mem, w_out, g, final_norm, tm):
    m, d = x.shape
    kern = functools.partial(_out_proj_kernel, final_norm=final_norm)
    row = lambda width: pl.BlockSpec((tm, width), lambda i: (i, 0))
    return pl.pallas_call(
        kern,
        grid=(m // tm,),
        in_specs=[row(d), row(MLA_WIDTH), row(NSA_WIDTH), row(MEM_WIDTH),
                  pl.BlockSpec(w_out.shape, lambda i: (0, 0)),
                  pl.BlockSpec((1, d), lambda i: (0, 0))],
        out_specs=row(d),
        out_shape=jax.ShapeDtypeStruct((m, d), F32),
        compiler_params=_params("arbitrary"),
        name="out_proj",
    )(x, o_mla, o_nsa, o_mem, w_out, g.reshape(1, d))


def _pad_cols(w, width):
    return jnp.pad(w, ((0, 0), (0, width - w.shape[1])))


def _w_in_moves():
    names = ("c_q", "c_kv", "k_rope", "z_mla", "q_nsa", "k_c", "v_c", "k_s", "v_s", "k_w", "v_w",
             "g_nsa", "z_nsa", "q_mem", "z_mem")
    src, off = {}, 0
    for name, n in zip(names, IN_SPLITS):
        src[name] = (off, n)
        off += n
    dst = {"c_q": (0, F_CQ), "c_kv": (0, F_CKV), "k_c": (0, F_KC), "g_nsa": (0, F_GATE),
           "v_c": (0, F_VC), "k_rope": (0, F_KROPE),
           "z_mla": (1, G_ZMLA), "z_nsa": (1, G_ZNSA), "z_mem": (1, G_ZMEM),
           "q_nsa": (2, B_QNSA), "q_mem": (2, B_QMEM), "k_s": (2, B_KS), "k_w": (2, B_KW),
           "v_s": (2, B_VS), "v_w": (2, B_VW)}
    return [(src[n][0], src[n][1], dst[n][0], dst[n][1]) for n in dst]


def _w_prep_kernel(wt_ref, wf_ref, wg_ref, wb_ref):
    outs = (wf_ref, wg_ref, wb_ref)
    for o_ref in outs:
        o_ref[...] = jnp.zeros(o_ref.shape, BF16)
    for s0, width, which, d0 in _w_in_moves():
        rows = -(-width // LANES) * LANES
        start = min(s0, wt_ref.shape[0] - rows)
        slab = wt_ref[start:start + rows, :].T
        outs[which][:, d0:d0 + width] = slab[:, s0 - start:s0 - start + width].astype(BF16)


def _layout_w_in(w_in, tr=256):
    d, n = w_in.shape
    widths = (F_WIDTH, G_WIDTH, B_WIDTH)
    wf, wg, wb = pl.pallas_call(
        _w_prep_kernel,
        grid=(d // tr,),
        in_specs=[pl.BlockSpec((n, tr), lambda i: (0, i))],
        out_specs=[pl.BlockSpec((tr, w), lambda i: (i, 0)) for w in widths],
        out_shape=[jax.ShapeDtypeStruct((d, w), BF16) for w in widths],
        compiler_params=_params("arbitrary"),
        name="w_in_layout",
    )(w_in.T)
    scale_b = jnp.ones((B_WIDTH,), F32)
    scale_b = scale_b.at[B_QNSA:B_QNSA + NSA_HEADS * NSA_DK].set(LOG2E * NSA_DK ** -0.5)
    scale_b = scale_b.at[B_QMEM:B_QMEM + MEM_WIDTH].set(MEM_DH ** -0.5)
    return wf, wg, wb, scale_b


def _rope_tables(seq):
    pos = jnp.arange(seq, dtype=F32)
    inv_freq = ROPE_THETA ** (-jnp.arange(0, MLA_ROPE, 2, dtype=F32) / MLA_ROPE)
    ang = pos[:, None] * inv_freq[None, :]
    zeros = jnp.zeros((seq, LANES - MLA_ROPE), F32)
    cos2 = jnp.concatenate([jnp.cos(ang), jnp.cos(ang), zeros], axis=1)
    sin2 = jnp.concatenate([jnp.sin(ang), jnp.sin(ang), zeros], axis=1)
    return cos2, sin2, jnp.cos(ang).T, jnp.sin(ang).T


def _nsa_tables(seq, tq, tk):
    chunks = seq // CMP_STRIDE
    n_s = seq // SLC_LEN
    c_start = jnp.arange(chunks) * CMP_STRIDE
    s_start = jnp.arange(n_s) * SLC_LEN
    overlap_t = ((c_start[None, :] < s_start[:, None] + SLC_LEN)
                 & (c_start[None, :] + CMP_LEN > s_start[:, None])
                 & (jnp.arange(chunks)[None, :] < chunks - 1))
    key_block = jnp.arange(seq) // SLC_LEN
    onehot = jnp.arange(HEAD_PAD)[None, :] == (NSA_DK + key_block)[:, None]
    slope = jnp.repeat(jnp.array([_alibi_slope(h) for h in range(NSA_HEADS)], F32) * LOG2E, tq)
    q_lane = jnp.tile(jnp.arange(tq), NSA_HEADS)
    tab_sel = jnp.arange(tk, dtype=F32)[:, None] * slope[None, :]
    rel = q_lane[None, :] - jnp.arange(WIN + tq)[:, None] + WIN
    tab_win = jnp.where((rel >= 0) & (rel < WIN), -slope[None, :] * rel.astype(F32), MASK_VALUE)
    slope_rows = jnp.broadcast_to(slope[None, :], (8, NSA_HEADS * tq))
    tab_diag = jnp.where(jnp.arange(tq)[:, None] <= q_lane[None, :], tab_sel[0:tq], MASK_VALUE)
    d_blk = jnp.arange(2 * chunks)[:, None] - chunks
    seen = d_blk * CMP_STRIDE + (CMP_LEN - 1) <= q_lane[None, :]
    dist = (q_lane[None, :] - d_blk * CMP_STRIDE).astype(F32) - (CMP_LEN - 1) / 2.0
    tab_cmp = jnp.where(seen, -slope[None, :] * dist, MASK_VALUE)
    return overlap_t.astype(BF16), onehot.astype(BF16), tab_sel, tab_diag, tab_win, tab_cmp, slope_rows


def _layer(x2, mem2, batch, seq, tables, norm_g, w_in, q_norm_g, w_uq, kv_norm_g, w_ukv,
           cmp_pe_k, cmp_pe_v, cmp_w1k, cmp_w2k, cmp_w1v, cmp_w2v, mem_norm_g, w_mem_kv, w_out,
           final_g, final_norm):
    rope, nsa_tables = tables
    d = x2.shape[1]
    wf, wg, wb, scale_b = _layout_w_in(w_in)
    hf, hg, hb = _in_proj(x2, norm_g, wf, wg, wb, scale_b, 512)

    wq = jnp.pad(w_uq.reshape(MLA_Q_RANK, MLA_HEADS, MLA_NOPE + MLA_ROPE),
                 ((0, 0), (0, 0), (0, HEAD_PAD - MLA_NOPE - MLA_ROPE)))
    wqt = wq.reshape(MLA_Q_RANK, MLA_HEADS * HEAD_PAD).T.astype(BF16)
    wkv = w_ukv.reshape(MLA_KV_RANK, MLA_HEADS, MLA_NOPE + MLA_V)
    wk = wkv[:, :, :MLA_NOPE].reshape(MLA_KV_RANK, MLA_HEADS * MLA_NOPE).astype(BF16)
    wvt = wkv[:, :, MLA_NOPE:].reshape(MLA_KV_RANK, MLA_WIDTH).T.astype(BF16)
    qt, k, kpe, vt = _mla_up(hf, q_norm_g, kv_norm_g, wqt, wk, wvt, rope, seq, MLA_TILE)
    hf3 = hf.reshape(batch, seq, F_WIDTH)
    hg3 = hg.reshape(batch, seq, G_WIDTH)
    hb3 = hb.reshape(batch, seq, B_WIDTH)
    o_mla = _mla_attn(qt, k.reshape(batch, seq, -1), kpe.reshape(batch, seq, -1), vt, hg3,
                      MLA_TILE, 4)

    pad_k = HEAD_PAD - NSA_DK
    k_cmp, v_cmp = _compress(
        hf3, _pad_cols(cmp_pe_k, HEAD_PAD), cmp_pe_v,
        jnp.pad(cmp_w1k.reshape(CMP_LEN, NSA_DK, NSA_DK), ((0, 0), (0, pad_k), (0, pad_k))).astype(BF16),
        jnp.pad(cmp_w2k, ((0, pad_k), (0, pad_k))).astype(BF16),
        cmp_w1v.reshape(CMP_LEN, NSA_DV, NSA_DV).astype(BF16), cmp_w2v.astype(BF16))
    o_nsa = _nsa_attn(hb3, hf3, hg3, k_cmp, v_cmp, nsa_tables, NSA_TQ, NSA_TK)

    mem_kv = _norm_proj(mem2, mem_norm_g, w_mem_kv.astype(BF16), jnp.ones((2 * MEM_WIDTH,), F32),
                        BF16, mem2.shape[0] // batch, MEM_WIDTH, "mem_kv_proj")
    o_mem = _mem_attn(hb3, mem_kv.reshape(batch, -1, 2 * MEM_WIDTH), hg3, 512)

    return _out_proj(x2, o_mla.reshape(-1, MLA_WIDTH), o_nsa.reshape(-1, NSA_WIDTH),
                     o_mem.reshape(-1, MEM_WIDTH), w_out.astype(BF16), final_g, final_norm, 512)


def kernel(x, mem, norm_g, w_in, q_norm_g, w_uq, kv_norm_g, w_ukv, cmp_pe_k, cmp_pe_v,
           cmp_w1k, cmp_w2k, cmp_w1v, cmp_w2v, mem_norm_g, w_mem_kv, w_out, final_norm_g):
    batch, seq, d = x.shape
    depth = norm_g.shape[0]
    tables = (_rope_tables(seq), _nsa_tables(seq, NSA_TQ, NSA_TK))
    x2 = x.reshape(batch * seq, d)
    mem2 = mem.reshape(batch * mem.shape[1], d)
    for l in range(depth):
        x2 = _layer(x2, mem2, batch, seq, tables, norm_g[l], w_in[l], q_norm_g[l], w_uq[l],
                    kv_norm_g[l], w_ukv[l], cmp_pe_k[l], cmp_pe_v[l], cmp_w1k[l], cmp_w2k[l],
                    cmp_w1v[l], cmp_w2v[l], mem_norm_g[l], w_mem_kv[l], w_out[l],
                    final_norm_g, l == depth - 1)
    return x2.reshape(batch, seq, d)
```

```python
import functools

import jax
import jax.numpy as jnp
from jax import lax
from jax.experimental import pallas as pl
from jax.experimental.pallas import tpu as pltpu

F32 = jnp.float32
BF16 = jnp.bfloat16

EPS = 1e-6
NEG_BIG = 1e9
MASK_VALUE = -1e30
SOFTMAX_EPS = 1e-20
LOG2E = 1.4426950408889634

MLA_HEADS = 8
MLA_NOPE = 128
MLA_ROPE = 64
MLA_V = 128
MLA_Q_RANK = 512
MLA_KV_RANK = 512
ROPE_THETA = 10000.0

NSA_HEADS = 4
NSA_DK = 192
NSA_DV = 128
CMP_LEN = 32
CMP_STRIDE = 16
SLC_LEN = 64
SLC_TOPN = 16
WIN = 512

MEM_HEADS = 4
MEM_DH = 128

MLA_WIDTH = MLA_HEADS * MLA_V
NSA_WIDTH = NSA_HEADS * NSA_DV
MEM_WIDTH = MEM_HEADS * MEM_DH

IN_SPLITS = (
    MLA_Q_RANK, MLA_KV_RANK, MLA_ROPE, MLA_WIDTH,
    NSA_HEADS * NSA_DK, NSA_DK, NSA_DV, NSA_DK, NSA_DV,
    NSA_DK, NSA_DV, 3 * NSA_HEADS, NSA_WIDTH,
    MEM_WIDTH, MEM_WIDTH,
)

LANES = 128
MXU_COLS = 256
HEAD_PAD = 256
VMEM_LIMIT = 56 * 1024 * 1024
MLA_TILE = 512
SUM_ROWS = 16
NSA_TQ = 256
NSA_TK = 512

F_CQ, F_CKV, F_KC, F_VC, F_KROPE = 0, 512, 1024, 1280, 1408
F_GATE = F_KC + NSA_DK
F_WIDTH = 1536
G_ZMLA, G_ZNSA, G_ZMEM = 0, MLA_WIDTH, MLA_WIDTH + NSA_WIDTH
G_WIDTH = MLA_WIDTH + NSA_WIDTH + MEM_WIDTH
B_QNSA, B_KS, B_QMEM, B_KW, B_VS, B_VW = 0, 768, 1024, 1536, 1792, 1920
B_WIDTH = 2048

_NT = (((1,), (1,)), ((), ()))


def _params(*sem):
    return pltpu.CompilerParams(dimension_semantics=sem, vmem_limit_bytes=VMEM_LIMIT)


def _sigmoid(x):
    return 1.0 / (1.0 + jnp.exp(-x))


def _silu(x):
    return x * _sigmoid(x)


def _rmsnorm(x, g):
    ms = jnp.mean(x * x, axis=-1, keepdims=True)
    return (x * lax.rsqrt(ms + EPS)) * g


def _norm_proj_kernel(x_ref, g_ref, w_ref, cs_ref, o_ref, xn_ref):
    @pl.when(pl.program_id(1) == 0)
    def _():
        xn_ref[...] = _rmsnorm(x_ref[...], g_ref[...]).astype(BF16)

    acc = jnp.dot(xn_ref[...], w_ref[...], preferred_element_type=F32)
    o_ref[...] = (acc * cs_ref[...]).astype(o_ref.dtype)


def _norm_proj(x, g, w, colscale, out_dtype, tm, tn, name):
    m, k = x.shape
    n = w.shape[1]
    return pl.pallas_call(
        _norm_proj_kernel,
        grid=(m // tm, n // tn),
        in_specs=[
            pl.BlockSpec((tm, k), lambda i, j: (i, 0)),
            pl.BlockSpec((1, k), lambda i, j: (0, 0)),
            pl.BlockSpec((k, tn), lambda i, j: (0, j)),
            pl.BlockSpec((1, tn), lambda i, j: (0, j)),
        ],
        out_specs=pl.BlockSpec((tm, tn), lambda i, j: (i, j)),
        out_shape=jax.ShapeDtypeStruct((m, n), out_dtype),
        scratch_shapes=[pltpu.VMEM((tm, k), BF16)],
        compiler_params=_params("arbitrary", "arbitrary"),
        name=name,
    )(x, g.reshape(1, k), w, colscale.reshape(1, n))


def _in_proj_kernel(x_ref, g_ref, wf_ref, wg_ref, wb_ref, cs_ref, of_ref, og_ref, ob_ref, xn_ref):
    j = pl.program_id(1)

    @pl.when(j == 0)
    def _():
        xn_ref[...] = _rmsnorm(x_ref[...], g_ref[...]).astype(BF16)
        of_ref[...] = jnp.dot(xn_ref[...], wf_ref[...], preferred_element_type=F32)

    @pl.when(j == 1)
    def _():
        og_ref[...] = _silu(jnp.dot(xn_ref[...], wg_ref[...], preferred_element_type=F32)).astype(BF16)

    @pl.when(j == 2)
    def _():
        acc = jnp.dot(xn_ref[...], wb_ref[...], preferred_element_type=F32)
        ob_ref[...] = (acc * cs_ref[...]).astype(BF16)


def _in_proj(x, g, wf, wg, wb, scale_b, tm):
    m, k = x.shape
    whole = lambda a: pl.BlockSpec(a.shape, lambda i, j: (0, 0), pipeline_mode=pl.Buffered(1))
    rows = lambda a: pl.BlockSpec((tm, a.shape[1]), lambda i, j: (i, 0))
    cs = scale_b.reshape(1, -1)
    return pl.pallas_call(
        _in_proj_kernel,
        grid=(m // tm, 3),
        in_specs=[rows(x), pl.BlockSpec((1, k), lambda i, j: (0, 0)),
                  whole(wf), whole(wg), whole(wb), whole(cs)],
        out_specs=[rows(wf), rows(wg), rows(wb)],
        out_shape=[jax.ShapeDtypeStruct((m, wf.shape[1]), F32),
                   jax.ShapeDtypeStruct((m, wg.shape[1]), BF16),
                   jax.ShapeDtypeStruct((m, wb.shape[1]), BF16)],
        scratch_shapes=[pltpu.VMEM((tm, k), BF16)],
        compiler_params=_params("arbitrary", "arbitrary"),
        name="in_proj",
    )(x, g.reshape(1, k), wf, wg, wb, cs)


def _rope_half(hi, cos2, sin2):
    up = pltpu.roll(hi, 32, axis=1)
    down = pltpu.roll(hi, 96, axis=1)
    return hi * cos2 + (up - down) * sin2


def _mla_up_kernel(cq_ref, ckv_ref, kr_ref, gq_ref, gkv_ref, wqt_ref, wk_ref, wvt_ref,
                   cos_ref, sin_ref, cost_ref, sint_ref, qt_ref, k_ref, kpe_ref, vt_ref):
    scale = LOG2E * (MLA_NOPE + MLA_ROPE) ** -0.5
    half = MLA_ROPE // 2
    cos_t = cost_ref[...]
    sin_t = sint_ref[...]

    cqn = _rmsnorm(cq_ref[...], gq_ref[...]).astype(BF16)
    yt = lax.dot_general(wqt_ref[...], cqn, _NT, preferred_element_type=F32)
    for h in range(MLA_HEADS):
        r = h * HEAD_PAD
        x1 = yt[r + MLA_NOPE:r + MLA_NOPE + half]
        x2 = yt[r + MLA_NOPE + half:r + MLA_NOPE + MLA_ROPE]
        qt_ref[0, r:r + MLA_NOPE, :] = (yt[r:r + MLA_NOPE] * scale).astype(BF16)
        qt_ref[0, r + MLA_NOPE:r + MLA_NOPE + half, :] = (
            (x1 * cos_t - x2 * sin_t) * scale).astype(BF16)
        qt_ref[0, r + MLA_NOPE + half:r + MLA_NOPE + MLA_ROPE, :] = (
            (x1 * sin_t + x2 * cos_t) * scale).astype(BF16)
        qt_ref[0, r + MLA_NOPE + MLA_ROPE:r + HEAD_PAD, :] = jnp.zeros(
            (HEAD_PAD - MLA_NOPE - MLA_ROPE, yt.shape[1]), BF16)

    ckn = _rmsnorm(ckv_ref[...], gkv_ref[...]).astype(BF16)
    kn = jnp.dot(ckn, wk_ref[...], preferred_element_type=F32)
    k_pe = _rope_half(kr_ref[...], cos_ref[...], sin_ref[...]).astype(BF16)
    k_ref[...] = kn.astype(BF16)
    kpe_ref[...] = k_pe
    vt = lax.dot_general(wvt_ref[...], ckn, _NT, preferred_element_type=F32)
    ones = jnp.ones((SUM_ROWS, vt.shape[1]), BF16)
    for h in range(MLA_HEADS):
        r = h * (MLA_V + SUM_ROWS)
        vt_ref[0, r:r + MLA_V, :] = vt[h * MLA_V:(h + 1) * MLA_V].astype(BF16)
        vt_ref[0, r + MLA_V:r + MLA_V + SUM_ROWS, :] = ones


def _mla_up(hf, gq, gkv, wqt, wk, wvt, rope, seq, tm):
    cos2, sin2, cos_t, sin_t = rope
    m = hf.shape[0]
    steps_per_seq = seq // tm
    hq = MLA_HEADS * HEAD_PAD
    half = MLA_ROPE // 2
    const = lambda i: (0, 0)
    return pl.pallas_call(
        _mla_up_kernel,
        grid=(m // tm,),
        in_specs=[
            pl.BlockSpec((tm, MLA_Q_RANK), lambda i: (i, F_CQ // MLA_Q_RANK)),
            pl.BlockSpec((tm, MLA_KV_RANK), lambda i: (i, F_CKV // MLA_KV_RANK)),
            pl.BlockSpec((tm, LANES), lambda i: (i, F_KROPE // LANES)),
            pl.BlockSpec((1, MLA_Q_RANK), const),
            pl.BlockSpec((1, MLA_KV_RANK), const),
            pl.BlockSpec((hq, MLA_Q_RANK), const),
            pl.BlockSpec((MLA_KV_RANK, MLA_HEADS * MLA_NOPE), const),
            pl.BlockSpec((MLA_WIDTH, MLA_KV_RANK), const),
            pl.BlockSpec((tm, LANES), lambda i: (i % steps_per_seq, 0)),
            pl.BlockSpec((tm, LANES), lambda i: (i % steps_per_seq, 0)),
            pl.BlockSpec((half, tm), lambda i: (0, i % steps_per_seq)),
            pl.BlockSpec((half, tm), lambda i: (0, i % steps_per_seq)),
        ],
        out_specs=[
            pl.BlockSpec((1, hq, tm), lambda i: (i, 0, 0)),
            pl.BlockSpec((tm, MLA_HEADS * MLA_NOPE), lambda i: (i, 0)),
            pl.BlockSpec((tm, LANES), lambda i: (i, 0)),
            pl.BlockSpec((1, MLA_HEADS * (MLA_V + SUM_ROWS), tm), lambda i: (i, 0, 0)),
        ],
        out_shape=[
            jax.ShapeDtypeStruct((m // tm, hq, tm), BF16),
            jax.ShapeDtypeStruct((m, MLA_HEADS * MLA_NOPE), BF16),
            jax.ShapeDtypeStruct((m, LANES), BF16),
            jax.ShapeDtypeStruct((m // tm, MLA_HEADS * (MLA_V + SUM_ROWS), tm), BF16),
        ],
        compiler_params=_params("arbitrary"),
        name="mla_up",
    )(hf, hf, hf, gq.reshape(1, -1), gkv.reshape(1, -1), wqt, wk, wvt, cos2, sin2, cos_t, sin_t)


def _mla_attn_kernel(qt_ref, k_ref, kpe_ref, vt_ref, z_ref, o_ref, m_ref, acc_ref, s0_ref, *,
                     tile_len, hps):
    t = tile_len
    va = MLA_V + SUM_ROWS
    qi = pl.program_id(2)

    def scores(j, h):
        kb = pl.multiple_of(j * t, t)
        k = jnp.concatenate([k_ref[0, pl.ds(kb, t), h * MLA_NOPE:(h + 1) * MLA_NOPE],
                             kpe_ref[0, pl.ds(kb, t), :]], axis=1)
        q_t = qt_ref[0, h * HEAD_PAD:(h + 1) * HEAD_PAD, :]
        return jnp.dot(k, q_t, preferred_element_type=F32)

    def softmax(h, s, diagonal):
        if diagonal:
            kpos = lax.broadcasted_iota(jnp.int32, (t, t), 0)
            qpos = lax.broadcasted_iota(jnp.int32, (t, t), 1)
            s = jnp.where(kpos <= qpos, s, MASK_VALUE)
        m = m_ref[h]
        m_new = jnp.maximum(m, jnp.max(s, axis=0, keepdims=True))
        alpha = jnp.exp2(m - m_new)
        p = jnp.exp2(s - m_new)
        m_ref[h] = m_new
        return alpha, p.astype(BF16)

    def accumulate(j, h, alpha, p):
        vt = vt_ref[j, h * va:(h + 1) * va, :]
        acc_ref[h] = alpha * acc_ref[h] + jnp.dot(vt, p, preferred_element_type=F32)

    ahead = 0

    def tile(j, diagonal):
        for h in range(ahead, hps):
            s0_ref[h] = scores(j, h)
        for h in range(hps):
            alpha, p = softmax(h, s0_ref[h], diagonal)
            accumulate(j, h, alpha, p)
            if h < ahead and not diagonal:
                s0_ref[h] = scores(j + 1, h)

    for h in range(hps):
        m_ref[h] = jnp.full((1, t), MASK_VALUE, F32)
        acc_ref[h] = jnp.zeros((va, t), F32)
    for h in range(ahead):
        s0_ref[h] = scores(0, h)

    def body(j, carry):
        tile(j, False)
        return carry

    lax.fori_loop(0, qi, body, 0)

    tile(qi, True)
    for h in range(hps):
        l = acc_ref[h, MLA_V:MLA_V + 1, :]
        o = (acc_ref[h, 0:MLA_V, :] / (l + SOFTMAX_EPS)).T
        cols = slice(h * MLA_V, (h + 1) * MLA_V)
        o_ref[0, :, cols] = (o * z_ref[0, :, cols].astype(F32)).astype(BF16)


def _mla_attn(qt, k, kpe, vt, hg, tile_len, hps):
    b, s, _ = k.shape
    t = tile_len
    tiles = s // t
    kern = functools.partial(_mla_attn_kernel, tile_len=t, hps=hps)
    qw, vw, va = hps * HEAD_PAD, hps * MLA_V, MLA_V + SUM_ROWS
    return pl.pallas_call(
        kern,
        grid=(b, MLA_HEADS // hps, tiles),
        in_specs=[
            pl.BlockSpec((1, qw, t), lambda bi, h, i: (bi * tiles + i, h, 0)),
            pl.BlockSpec((1, s, hps * MLA_NOPE), lambda bi, h, i: (bi, 0, h)),
            pl.BlockSpec((1, s, LANES), lambda bi, h, i: (bi, 0, 0)),
            pl.BlockSpec((tiles, hps * va, t), lambda bi, h, i: (bi, h, 0)),
            pl.BlockSpec((1, t, vw), lambda bi, h, i: (bi, i, G_ZMLA // vw + h)),
        ],
        out_specs=pl.BlockSpec((1, t, vw), lambda bi, h, i: (bi, i, h)),
        out_shape=jax.ShapeDtypeStruct((b, s, MLA_WIDTH), BF16),
        scratch_shapes=[
            pltpu.VMEM((hps, 1, t), F32),
            pltpu.VMEM((hps, va, t), F32),
            pltpu.VMEM((hps, t, t), F32),
        ],
        compiler_params=_params("arbitrary", "arbitrary", "arbitrary"),
        name="mla_attn",
    )(qt, k, kpe, vt, hg)


def _compress_one(x_refs, pe_ref, w1_ref, w2_ref, o_ref):
    chunks = x_refs[0].shape[1] // CMP_STRIDE
    a = b = None
    for i in range(CMP_STRIDE):
        for c, x_ref in enumerate(x_refs):
            lanes = slice(c * LANES, (c + 1) * LANES)
            x = x_ref[0, pl.ds(i, chunks, stride=CMP_STRIDE), :]
            ai = jnp.dot((x + pe_ref[i:i + 1, lanes]).astype(BF16), w1_ref[i, lanes, :],
                         preferred_element_type=F32)
            bi = jnp.dot((x + pe_ref[CMP_STRIDE + i:CMP_STRIDE + i + 1, lanes]).astype(BF16),
                         w1_ref[CMP_STRIDE + i, lanes, :], preferred_element_type=F32)
            a = ai if a is None else a + ai
            b = bi if b is None else b + bi
    h1 = a + pltpu.roll(b, chunks - 1, axis=0)
    o_ref[0] = jnp.dot(_silu(h1).astype(BF16), w2_ref[...], preferred_element_type=F32).astype(BF16)


def _compress_kernel(xk0_ref, xk1_ref, xv_ref, pek_ref, pev_ref, w1k_ref, w2k_ref, w1v_ref, w2v_ref,
                     ok_ref, ov_ref):
    _compress_one((xk0_ref, xk1_ref), pek_ref, w1k_ref, w2k_ref, ok_ref)
    _compress_one((xv_ref,), pev_ref, w1v_ref, w2v_ref, ov_ref)


def _compress(hf, pek, pev, w1k, w2k, w1v, w2v):
    b, s, _ = hf.shape
    chunks = s // CMP_STRIDE
    full = lambda a: pl.BlockSpec(a.shape, lambda bi: (0,) * a.ndim)
    return pl.pallas_call(
        _compress_kernel,
        grid=(b,),
        in_specs=[pl.BlockSpec((1, s, LANES), lambda bi: (bi, 0, F_KC // LANES)),
                  pl.BlockSpec((1, s, LANES), lambda bi: (bi, 0, F_KC // LANES + 1)),
                  pl.BlockSpec((1, s, NSA_DV), lambda bi: (bi, 0, F_VC // NSA_DV)),
                  full(pek), full(pev), full(w1k), full(w2k), full(w1v), full(w2v)],
        out_specs=[
            pl.BlockSpec((1, chunks, HEAD_PAD), lambda bi: (bi, 0, 0)),
            pl.BlockSpec((1, chunks, NSA_DV), lambda bi: (bi, 0, 0)),
        ],
        out_shape=[
            jax.ShapeDtypeStruct((b, chunks, HEAD_PAD), BF16),
            jax.ShapeDtypeStruct((b, chunks, NSA_DV), BF16),
        ],
        compiler_params=_params("arbitrary"),
        name="nsa_compress",
    )(hf, hf, hf, pek, pev, w1k, w2k, w1v, w2v)


def _alibi_slope(h):
    return 2.0 ** (-8.0 * (h + 1) / NSA_HEADS)


def _nsa_attn_kernel(q_ref, g_ref, z_ref, kc_ref, vc_ref, ks_ref, vs_ref, kw_ref, vw_ref,
                     ovt_ref, oh_ref, tabs_ref, tabd_ref, tabw_ref, tabc_ref, slope_ref, o_ref,
                     kaug_ref, kwp_ref, vst_ref, vwt_ref, vct_ref, m_ref, acc_ref, s0_ref,
                     sc_ref, sw_ref, qwin_ref, qsel_ref,
                     *, tq, tk, n_c, n_s, top_n):
    qi = pl.program_id(1)
    t0 = qi * tq
    heads = NSA_HEADS
    seq = ks_ref.shape[1]
    ncp = kc_ref.shape[1]
    hw = 2 * tq
    aug = HEAD_PAD - NSA_DK
    wlen = WIN + tq
    va = NSA_DV + SUM_ROWS
    sub_tiles = tk // tq

    @pl.when(qi == 0)
    def _():
        kaug_ref[...] = ks_ref[0] + oh_ref[...]
        col = lax.broadcasted_iota(jnp.int32, (WIN, HEAD_PAD), 1)
        kwp_ref[0:WIN, :] = jnp.where(col == NSA_DK, 1.0, 0.0).astype(BF16)
        kwp_ref[WIN:, :] = kw_ref[0]
        for i in range(seq // tq):
            vst_ref[i, 0:NSA_DV, :] = vs_ref[0, i * tq:(i + 1) * tq, :].astype(F32).T.astype(BF16)
            vst_ref[i, NSA_DV:va, :] = jnp.ones((SUM_ROWS, tq), BF16)
        for i in range(WIN // LANES):
            vwt_ref[i] = jnp.zeros((va, LANES), BF16)
        for i in range(seq // LANES):
            vwt_ref[WIN // LANES + i, 0:NSA_DV, :] = (
                vw_ref[0, i * LANES:(i + 1) * LANES, :].astype(F32).T.astype(BF16))
            vwt_ref[WIN // LANES + i, NSA_DV:va, :] = jnp.ones((SUM_ROWS, LANES), BF16)
        vct_ref[...] = vc_ref[0].astype(F32).T.astype(BF16)

    q_t = q_ref[0].astype(F32).T
    flag_rows = jnp.where(lax.broadcasted_iota(jnp.int32, (aug, tq), 0) == 0,
                          MASK_VALUE, 0.0).astype(BF16)
    head_lanes = [(h // 2, slice((h % 2) * tq, (h % 2 + 1) * tq)) for h in range(heads)]
    for h, (x, lanes) in enumerate(head_lanes):
        rows = q_t[h * NSA_DK:(h + 1) * NSA_DK].astype(BF16)
        qwin_ref[x, 0:NSA_DK, lanes] = rows
        qsel_ref[x, 0:NSA_DK, lanes] = rows
        qwin_ref[x, NSA_DK:HEAD_PAD, lanes] = flag_rows

    kc = kc_ref[0]
    kwb = kwp_ref[pl.ds(pl.multiple_of(t0, LANES), wlen), :]
    for x in range(2):
        sc_ref[x] = jnp.dot(kc, qwin_ref[x], preferred_element_type=F32)
    for x in range(2):
        sw_ref[x] = (jnp.dot(kwb, qwin_ref[x], preferred_element_type=F32)
                     + tabw_ref[:, x * hw:(x + 1) * hw])

    first_blk = qi * (tq // CMP_STRIDE)
    tab_c = tabc_ref[pl.ds(pl.multiple_of(ncp - first_blk, CMP_STRIDE), ncp), :]
    o_cmp, p_sum = [], None
    for x in range(2):
        p_x = []
        for hl in range(2):
            h = 2 * x + hl
            s = sc_ref[x, :, hl * tq:(hl + 1) * tq] + tab_c[:, h * tq:(h + 1) * tq]
            m = jnp.max(s, axis=0, keepdims=True)
            e = jnp.exp2(s - m)
            inv = jnp.where(m > 0.5 * MASK_VALUE,
                            1.0 / (jnp.sum(e, axis=0, keepdims=True) + SOFTMAX_EPS), 0.0)
            p = e * inv
            p_sum = p if p_sum is None else p_sum + p
            p_x.append(p.astype(BF16))
        o_cmp.append(jnp.dot(vct_ref[...], jnp.concatenate(p_x, axis=1),
                             preferred_element_type=F32))
    p_hi = p_sum.astype(BF16)
    p_lo = (p_sum - p_hi.astype(F32)).astype(BF16)
    ovt = ovt_ref[...]
    imp = (jnp.dot(ovt, p_hi, preferred_element_type=F32)
           + jnp.dot(ovt, p_lo, preferred_element_type=F32))

    tile0 = qi * (tq // LANES)
    vwb = jnp.concatenate([vwt_ref[tile0 + r] for r in range(wlen // LANES)], axis=1)
    o_win = []
    for x in range(2):
        s = sw_ref[x]
        p = jnp.exp2(s - jnp.max(s, axis=0, keepdims=True))
        pv = jnp.dot(vwb, p.astype(BF16), preferred_element_type=F32)
        o_win.append(pv[0:NSA_DV] / (pv[NSA_DV:NSA_DV + 1] + SOFTMAX_EPS))

    blk = lax.broadcasted_iota(jnp.int32, (n_s, tq), 0)
    cur = (t0 + lax.broadcasted_iota(jnp.int32, (n_s, tq), 1)) // SLC_LEN
    forced = (blk == 0) | (blk == cur) | (blk == cur - 1)
    imp = jnp.where(forced, NEG_BIG, imp)
    imp = jnp.where(blk > cur, -NEG_BIG, imp)
    sub = lax.broadcasted_iota(jnp.int32, (8, tq), 0)
    groups = [imp[8 * g:8 * g + 8] for g in range(n_s // 8)]
    ranks = [jnp.zeros((8, tq), F32) for _ in groups]
    for jp in range(n_s):
        row = imp[jp:jp + 1, :]
        for g, grp in enumerate(groups):
            ge = jnp.where(row >= grp, 1.0, 0.0)
            gt = jnp.where(row > grp, 1.0, 0.0)
            if 8 * g > jp:
                beats = ge
            elif 8 * g + 8 <= jp:
                beats = gt
            else:
                beats = jnp.where(sub + 8 * g > jp, ge, gt)
            ranks[g] = ranks[g] + beats
    sel_rows = jnp.where(jnp.concatenate(ranks, axis=0) < top_n, 0.0, MASK_VALUE)
    if n_s < aug:
        sel_rows = jnp.concatenate([sel_rows, jnp.zeros((aug - n_s, tq), F32)], axis=0)
    for x, lanes in head_lanes:
        qsel_ref[x, NSA_DK:HEAD_PAD, lanes] = sel_rows.astype(BF16)

    jd = t0 // tk

    def scores(j, x):
        kb = j * tk if isinstance(j, int) else pl.multiple_of(j * tk, tk)
        return jnp.dot(kaug_ref[pl.ds(kb, tk), :], qsel_ref[x], preferred_element_type=F32)

    def softmax(x, s, tab, key0):
        u = s + tab
        off = slope_ref[0:1, x * hw:(x + 1) * hw] * (key0 - t0).astype(F32)
        m_old = m_ref[x]
        m_new = jnp.maximum(m_old, jnp.max(u, axis=0, keepdims=True) + off)
        alpha = jnp.exp2(m_old - m_new)
        p = jnp.exp2(u - (m_new - off))
        m_ref[x] = m_new
        return alpha, p.astype(BF16)

    def accumulate(x, alpha, p, vt):
        acc_ref[x] = alpha * acc_ref[x] + jnp.dot(vt, p, preferred_element_type=F32)

    def full_tile(j):
        vt = jnp.concatenate([vst_ref[j * sub_tiles + r] for r in range(sub_tiles)], axis=1)
        for x in range(2):
            s0_ref[x] = scores(j, x)
        for x in range(2):
            alpha, p = softmax(x, s0_ref[x], tabs_ref[:, x * hw:(x + 1) * hw], j * tk)
            accumulate(x, alpha, p, vt)

    def sub_tile(r, tab_ref):
        key0 = jd * tk + r * tq
        rows = pl.ds(r * tq if isinstance(r, int) else pl.multiple_of(r * tq, tq), tq)
        vt = vst_ref[jd * sub_tiles + r]
        for x in range(2):
            alpha, p = softmax(x, s0_ref[x, rows, :], tab_ref[0:tq, x * hw:(x + 1) * hw], key0)
            accumulate(x, alpha, p, vt)

    for x in range(2):
        m_ref[x] = jnp.full((1, hw), MASK_VALUE, F32)
        acc_ref[x] = jnp.zeros((va, hw), F32)

    def body(j, carry):
        full_tile(j)
        return carry

    lax.fori_loop(0, jd, body, 0)
    for x in range(2):
        s0_ref[x] = scores(jd, x)
    own = (t0 - jd * tk) // tq
    for r in range(sub_tiles - 1):
        @pl.when(r < own)
        def _():
            sub_tile(r, tabs_ref)
    sub_tile(own, tabd_ref)
    o_slc = [acc_ref[x, 0:NSA_DV, :] / (acc_ref[x, NSA_DV:NSA_DV + 1, :] + SOFTMAX_EPS)
             for x in range(2)]

    gates = _sigmoid(g_ref[0]).T
    z = z_ref[0]
    for h in range(heads):
        x, lanes = h // 2, slice((h % 2) * tq, (h % 2 + 1) * tq)
        c = NSA_DK + 3 * h
        o = (gates[c:c + 1] * o_cmp[x][:, lanes] + gates[c + 1:c + 2] * o_slc[x][:, lanes]
             + gates[c + 2:c + 3] * o_win[x][:, lanes])
        zh = z[:, h * NSA_DV:(h + 1) * NSA_DV]
        o_ref[0, :, h * NSA_DV:(h + 1) * NSA_DV] = (o.T * zh.astype(F32)).astype(BF16)


def _nsa_attn(hb, hf, hg, k_cmp, v_cmp, tables, tq, tk):
    ovt, onehot, tab_sel, tab_diag, tab_win, tab_cmp, slope_rows = tables
    b, s, _ = hb.shape
    ncp = k_cmp.shape[1]
    n_s = s // SLC_LEN
    assert n_s <= HEAD_PAD - NSA_DK and n_s % 8 == 0, "selection blocks must fit the spare rows"
    assert tq & (tq - 1) == 0 and tq % LANES == 0 and tk % tq == 0 and s % tk == 0
    kern = functools.partial(_nsa_attn_kernel, tq=tq, tk=tk, n_c=ncp - 1, n_s=n_s,
                             top_n=min(SLC_TOPN, n_s))
    qw = NSA_HEADS * NSA_DK
    const2 = lambda a: pl.BlockSpec(a.shape, lambda bi, i: (0, 0), pipeline_mode=pl.Buffered(1))
    return pl.pallas_call(
        kern,
        grid=(b, s // tq),
        in_specs=[
            pl.BlockSpec((1, tq, qw), lambda bi, i: (bi, i, B_QNSA // qw)),
            pl.BlockSpec((1, tq, HEAD_PAD), lambda bi, i: (bi, i, F_KC // HEAD_PAD)),
            pl.BlockSpec((1, tq, NSA_WIDTH), lambda bi, i: (bi, i, G_ZNSA // NSA_WIDTH)),
            pl.BlockSpec((1, ncp, HEAD_PAD), lambda bi, i: (bi, 0, 0)),
            pl.BlockSpec((1, ncp, NSA_DV), lambda bi, i: (bi, 0, 0)),
            pl.BlockSpec((1, s, HEAD_PAD), lambda bi, i: (bi, 0, B_KS // HEAD_PAD)),
            pl.BlockSpec((1, s, NSA_DV), lambda bi, i: (bi, 0, B_VS // NSA_DV)),
            pl.BlockSpec((1, s, HEAD_PAD), lambda bi, i: (bi, 0, B_KW // HEAD_PAD)),
            pl.BlockSpec((1, s, NSA_DV), lambda bi, i: (bi, 0, B_VW // NSA_DV)),
            const2(ovt), const2(onehot), const2(tab_sel), const2(tab_diag), const2(tab_win),
            const2(tab_cmp), const2(slope_rows),
        ],
        out_specs=pl.BlockSpec((1, tq, NSA_WIDTH), lambda bi, i: (bi, i, 0)),
        out_shape=jax.ShapeDtypeStruct((b, s, NSA_WIDTH), BF16),
        scratch_shapes=[
            pltpu.VMEM((s, HEAD_PAD), BF16),
            pltpu.VMEM((s + WIN, HEAD_PAD), BF16),
            pltpu.VMEM((s // tq, NSA_DV + SUM_ROWS, tq), BF16),
            pltpu.VMEM(((s + WIN) // LANES, NSA_DV + SUM_ROWS, LANES), BF16),
            pltpu.VMEM((NSA_DV, ncp), BF16),
            pltpu.VMEM((2, 1, 2 * tq), F32),
            pltpu.VMEM((2, NSA_DV + SUM_ROWS, 2 * tq), F32),
            pltpu.VMEM((2, tk, 2 * tq), F32),
            pltpu.VMEM((2, ncp, 2 * tq), F32),
            pltpu.VMEM((2, WIN + tq, 2 * tq), F32),
            pltpu.VMEM((2, HEAD_PAD, 2 * tq), BF16),
            pltpu.VMEM((2, HEAD_PAD, 2 * tq), BF16),
        ],
        compiler_params=_params("arbitrary", "arbitrary"),
        name="nsa_attn",
    )(hb, hf, hg, k_cmp, v_cmp, hb, hb, hb, hb, ovt, onehot, tab_sel, tab_diag, tab_win, tab_cmp,
      slope_rows)


def _mem_attn_kernel(q_ref, k_ref, v_ref, z_ref, o_ref):
    q = q_ref[0]
    k = k_ref[0]
    v = v_ref[0]
    z = z_ref[0]
    for h in range(MEM_HEADS):
        sl = slice(h * MEM_DH, (h + 1) * MEM_DH)
        s = lax.dot_general(q[:, sl], k[:, sl], _NT, preferred_element_type=F32)
        p = jnp.exp(s - jnp.max(s, axis=-1, keepdims=True))
        o = jnp.dot(p.astype(BF16), v[:, sl], preferred_element_type=F32)
        o = o / jnp.sum(p, axis=-1, keepdims=True)
        o_ref[0, :, sl] = (o * z[:, sl].astype(F32)).astype(BF16)


def _mem_attn(hb, mem_kv, hg, tq):
    b, s, _ = hb.shape
    mlen = mem_kv.shape[1]
    return pl.pallas_call(
        _mem_attn_kernel,
        grid=(b, s // tq),
        in_specs=[
            pl.BlockSpec((1, tq, MEM_WIDTH), lambda bi, i: (bi, i, B_QMEM // MEM_WIDTH)),
            pl.BlockSpec((1, mlen, MEM_WIDTH), lambda bi, i: (bi, 0, 0)),
            pl.BlockSpec((1, mlen, MEM_WIDTH), lambda bi, i: (bi, 0, 1)),
            pl.BlockSpec((1, tq, MEM_WIDTH), lambda bi, i: (bi, i, G_ZMEM // MEM_WIDTH)),
        ],
        out_specs=pl.BlockSpec((1, tq, MEM_WIDTH), lambda bi, i: (bi, i, 0)),
        out_shape=jax.ShapeDtypeStruct((b, s, MEM_WIDTH), BF16),
        compiler_params=_params("arbitrary", "arbitrary"),
        name="mem_attn",
    )(hb, mem_kv, mem_kv, hg)


def _out_proj_kernel(x_ref, a_ref, n_ref, m_ref, w_ref, g_ref, o_ref, *, final_norm):
    y = x_ref[...]
    y = y + jnp.dot(a_ref[...], w_ref[0:MLA_WIDTH, :], preferred_element_type=F32)
    y = y + jnp.dot(n_ref[...], w_ref[MLA_WIDTH:MLA_WIDTH + NSA_WIDTH, :], preferred_element_type=F32)
    y = y + jnp.dot(m_ref[...], w_ref[MLA_WIDTH + NSA_WIDTH:, :], preferred_element_type=F32)
    if final_norm:
        y = _rmsnorm(y, g_ref[...])
    o_ref[...] = y


def _out_proj(x, o_mla, o_nsa, o_mem, w_out, g, final_norm, tm):
    m, d = x.shape
    kern = functools.partial(_out_proj_kernel, final_norm=final_norm)
    row = lambda width: pl.BlockSpec((tm, width), lambda i: (i, 0))
    return pl.pallas_call(
        kern,
        grid=(m // tm,),
        in_specs=[row(d), row(MLA_WIDTH), row(NSA_WIDTH), row(MEM_WIDTH),
                  pl.BlockSpec(w_out.shape, lambda i: (0, 0)),
                  pl.BlockSpec((1, d), lambda i: (0, 0))],
        out_specs=row(d),
        out_shape=jax.ShapeDtypeStruct((m, d), F32),
        compiler_params=_params("arbitrary"),
        name="out_proj",
    )(x, o_mla, o_nsa, o_mem, w_out, g.reshape(1, d))


def _pad_cols(w, width):
    return jnp.pad(w, ((0, 0), (0, width - w.shape[1])))


def _w_in_moves():
    names = ("c_q", "c_kv", "k_rope", "z_mla", "q_nsa", "k_c", "v_c", "k_s", "v_s", "k_w", "v_w",
             "g_nsa", "z_nsa", "q_mem", "z_mem")
    src, off = {}, 0
    for name, n in zip(names, IN_SPLITS):
        src[name] = (off, n)
        off += n
    dst = {"c_q": (0, F_CQ), "c_kv": (0, F_CKV), "k_c": (0, F_KC), "g_nsa": (0, F_GATE),
           "v_c": (0, F_VC), "k_rope": (0, F_KROPE),
           "z_mla": (1, G_ZMLA), "z_nsa": (1, G_ZNSA), "z_mem": (1, G_ZMEM),
           "q_nsa": (2, B_QNSA), "q_mem": (2, B_QMEM), "k_s": (2, B_KS), "k_w": (2, B_KW),
           "v_s": (2, B_VS), "v_w": (2, B_VW)}
    return [(src[n][0], src[n][1], dst[n][0], dst[n][1]) for n in dst]


def _w_prep_kernel(wt_ref, wf_ref, wg_ref, wb_ref):
    outs = (wf_ref, wg_ref, wb_ref)
    for o_ref in outs:
        o_ref[...] = jnp.zeros(o_ref.shape, BF16)
    for s0, width, which, d0 in _w_in_moves():
        rows = -(-width // LANES) * LANES
        start = min(s0, wt_ref.shape[0] - rows)
        slab = wt_ref[start:start + rows, :].T
        outs[which][:, d0:d0 + width] = slab[:, s0 - start:s0 - start + width].astype(BF16)


def _layout_w_in(w_in, tr=256):
    d, n = w_in.shape
    widths = (F_WIDTH, G_WIDTH, B_WIDTH)
    wf, wg, wb = pl.pallas_call(
        _w_prep_kernel,
        grid=(d // tr,),
        in_specs=[pl.BlockSpec((n, tr), lambda i: (0, i))],
        out_specs=[pl.BlockSpec((tr, w), lambda i: (i, 0)) for w in widths],
        out_shape=[jax.ShapeDtypeStruct((d, w), BF16) for w in widths],
        compiler_params=_params("arbitrary"),
        name="w_in_layout",
    )(w_in.T)
    scale_b = jnp.ones((B_WIDTH,), F32)
    scale_b = scale_b.at[B_QNSA:B_QNSA + NSA_HEADS * NSA_DK].set(LOG2E * NSA_DK ** -0.5)
    scale_b = scale_b.at[B_QMEM:B_QMEM + MEM_WIDTH].set(MEM_DH ** -0.5)
    return wf, wg, wb, scale_b


def _rope_tables(seq):
    pos = jnp.arange(seq, dtype=F32)
    inv_freq = ROPE_THETA ** (-jnp.arange(0, MLA_ROPE, 2, dtype=F32) / MLA_ROPE)
    ang = pos[:, None] * inv_freq[None, :]
    zeros = jnp.zeros((seq, LANES - MLA_ROPE), F32)
    cos2 = jnp.concatenate([jnp.cos(ang), jnp.cos(ang), zeros], axis=1)
    sin2 = jnp.concatenate([jnp.sin(ang), jnp.sin(ang), zeros], axis=1)
    return cos2, sin2, jnp.cos(ang).T, jnp.sin(ang).T


def _nsa_tables(seq, tq, tk):
    chunks = seq // CMP_STRIDE
    n_s = seq // SLC_LEN
    c_start = jnp.arange(chunks) * CMP_STRIDE
    s_start = jnp.arange(n_s) * SLC_LEN
    overlap_t = ((c_start[None, :] < s_start[:, None] + SLC_LEN)
                 & (c_start[None, :] + CMP_LEN > s_start[:, None])
                 & (jnp.arange(chunks)[None, :] < chunks - 1))
    key_block = jnp.arange(seq) // SLC_LEN
    onehot = jnp.arange(HEAD_PAD)[None, :] == (NSA_DK + key_block)[:, None]
    slope = jnp.repeat(jnp.array([_alibi_slope(h) for h in range(NSA_HEADS)], F32) * LOG2E, tq)
    q_lane = jnp.tile(jnp.arange(tq), NSA_HEADS)
    tab_sel = jnp.arange(tk, dtype=F32)[:, None] * slope[None, :]
    rel = q_lane[None, :] - jnp.arange(WIN + tq)[:, None] + WIN
    tab_win = jnp.where((rel >= 0) & (rel < WIN), -slope[None, :] * rel.astype(F32), MASK_VALUE)
    slope_rows = jnp.broadcast_to(slope[None, :], (8, NSA_HEADS * tq))
    tab_diag = jnp.where(jnp.arange(tq)[:, None] <= q_lane[None, :], tab_sel[0:tq], MASK_VALUE)
    d_blk = jnp.arange(2 * chunks)[:, None] - chunks
    seen = d_blk * CMP_STRIDE + (CMP_LEN - 1) <= q_lane[None, :]
    dist = (q_lane[None, :] - d_blk * CMP_STRIDE).astype(F32) - (CMP_LEN - 1) / 2.0
    tab_cmp = jnp.where(seen, -slope[None, :] * dist, MASK_VALUE)
    return overlap_t.astype(BF16), onehot.astype(BF16), tab_sel, tab_diag, tab_win, tab_cmp, slope_rows


def _layer(x2, mem2, batch, seq, tables, norm_g, w_in, q_norm_g, w_uq, kv_norm_g, w_ukv,
           cmp_pe_k, cmp_pe_v, cmp_w1k, cmp_w2k, cmp_w1v, cmp_w2v, mem_norm_g, w_mem_kv, w_out,
           final_g, final_norm):
    rope, nsa_tables = tables
    d = x2.shape[1]
    wf, wg, wb, scale_b = _layout_w_in(w_in)
    hf, hg, hb = _in_proj(x2, norm_g, wf, wg, wb, scale_b, 512)

    wq = jnp.pad(w_uq.reshape(MLA_Q_RANK, MLA_HEADS, MLA_NOPE + MLA_ROPE),
                 ((0, 0), (0, 0), (0, HEAD_PAD - MLA_NOPE - MLA_ROPE)))
    wqt = wq.reshape(MLA_Q_RANK, MLA_HEADS * HEAD_PAD).T.astype(BF16)
    wkv = w_ukv.reshape(MLA_KV_RANK, MLA_HEADS, MLA_NOPE + MLA_V)
    wk = wkv[:, :, :MLA_NOPE].reshape(MLA_KV_RANK, MLA_HEADS * MLA_NOPE).astype(BF16)
    wvt = wkv[:, :, MLA_NOPE:].reshape(MLA_KV_RANK, MLA_WIDTH).T.astype(BF16)
    qt, k, kpe, vt = _mla_up(hf, q_norm_g, kv_norm_g, wqt, wk, wvt, rope, seq, MLA_TILE)
    hf3 = hf.reshape(batch, seq, F_WIDTH)
    hg3 = hg.reshape(batch, seq, G_WIDTH)
    hb3 = hb.reshape(batch, seq, B_WIDTH)
    o_mla = _mla_attn(qt, k.reshape(batch, seq, -1), kpe.reshape(batch, seq, -1), vt, hg3,
                      MLA_TILE, 4)

    pad_k = HEAD_PAD - NSA_DK
    k_cmp, v_cmp = _compress(
        hf3, _pad_cols(cmp_pe_k, HEAD_PAD), cmp_pe_v,
        jnp.pad(cmp_w1k.reshape(CMP_LEN, NSA_DK, NSA_DK), ((0, 0), (0, pad_k), (0, pad_k))).astype(BF16),
        jnp.pad(cmp_w2k, ((0, pad_k), (0, pad_k))).astype(BF16),
        cmp_w1v.reshape(CMP_LEN, NSA_DV, NSA_DV).astype(BF16), cmp_w2v.astype(BF16))
    o_nsa = _nsa_attn(hb3, hf3, hg3, k_cmp, v_cmp, nsa_tables, NSA_TQ, NSA_TK)

    mem_kv = _norm_proj(mem2, mem_norm_g, w_mem_kv.astype(BF16), jnp.ones((2 * MEM_WIDTH,), F32),
                        BF16, mem2.shape[0] // batch, MEM_WIDTH, "mem_kv_proj")
    o_mem = _mem_attn(hb3, mem_kv.reshape(batch, -1, 2 * MEM_WIDTH), hg3, 512)

    return _out_proj(x2, o_mla.reshape(-1, MLA_WIDTH), o_nsa.reshape(-1, NSA_WIDTH),
                     o_mem.reshape(-1, MEM_WIDTH), w_out.astype(BF16), final_g, final_norm, 512)


def kernel(x, mem, norm_g, w_in, q_norm_g, w_uq, kv_norm_g, w_ukv, cmp_pe_k, cmp_pe_v,
           cmp_w1k, cmp_w2k, cmp_w1v, cmp_w2v, mem_norm_g, w_mem_kv, w_out, final_norm_g):
    batch, seq, d = x.shape
    depth = norm_g.shape[0]
    tables = (_rope_tables(seq), _nsa_tables(seq, NSA_TQ, NSA_TK))
    x2 = x.reshape(batch * seq, d)
    mem2 = mem.reshape(batch * mem.shape[1], d)
    for l in range(depth):
        x2 = _layer(x2, mem2, batch, seq, tables, norm_g[l], w_in[l], q_norm_g[l], w_uq[l],
                    kv_norm_g[l], w_ukv[l], cmp_pe_k[l], cmp_pe_v[l], cmp_w1k[l], cmp_w2k[l],
                    cmp_w1v[l], cmp_w2v[l], mem_norm_g[l], w_mem_kv[l], w_out[l],
                    final_norm_g, l == depth - 1)
    return x2.reshape(batch, seq, d)
```

```python
import functools

import jax
import jax.numpy as jnp
from jax import lax
from jax.experimental import pallas as pl
from jax.experimental.pallas import tpu as pltpu

F32 = jnp.float32
BF16 = jnp.bfloat16

EPS = 1e-6
NEG_BIG = 1e9
MASK_VALUE = -1e30
SOFTMAX_EPS = 1e-20
LOG2E = 1.4426950408889634

MLA_HEADS = 8
MLA_NOPE = 128
MLA_ROPE = 64
MLA_V = 128
MLA_Q_RANK = 512
MLA_KV_RANK = 512
ROPE_THETA = 10000.0

NSA_HEADS = 4
NSA_DK = 192
NSA_DV = 128
CMP_LEN = 32
CMP_STRIDE = 16
SLC_LEN = 64
SLC_TOPN = 16
WIN = 512

MEM_HEADS = 4
MEM_DH = 128

MLA_WIDTH = MLA_HEADS * MLA_V
NSA_WIDTH = NSA_HEADS * NSA_DV
MEM_WIDTH = MEM_HEADS * MEM_DH

IN_SPLITS = (
    MLA_Q_RANK, MLA_KV_RANK, MLA_ROPE, MLA_WIDTH,
    NSA_HEADS * NSA_DK, NSA_DK, NSA_DV, NSA_DK, NSA_DV,
    NSA_DK, NSA_DV, 3 * NSA_HEADS, NSA_WIDTH,
    MEM_WIDTH, MEM_WIDTH,
)

LANES = 128
MXU_COLS = 256
HEAD_PAD = 256
VMEM_LIMIT = 56 * 1024 * 1024
MLA_TILE = 512
SUM_ROWS = 16
NSA_TQ = 256
NSA_TK = 512

F_CQ, F_CKV, F_KC, F_VC, F_KROPE = 0, 512, 1024, 1280, 1408
F_GATE = F_KC + NSA_DK
F_WIDTH = 1536
G_ZMLA, G_ZNSA, G_ZMEM = 0, MLA_WIDTH, MLA_WIDTH + NSA_WIDTH
G_WIDTH = MLA_WIDTH + NSA_WIDTH + MEM_WIDTH
B_QNSA, B_KS, B_QMEM, B_KW, B_VS, B_VW = 0, 768, 1024, 1536, 1792, 1920
B_WIDTH = 2048

_NT = (((1,), (1,)), ((), ()))


def _params(*sem):
    return pltpu.CompilerParams(dimension_semantics=sem, vmem_limit_bytes=VMEM_LIMIT)


def _sigmoid(x):
    return 1.0 / (1.0 + jnp.exp(-x))


def _silu(x):
    return x * _sigmoid(x)


def _rmsnorm(x, g):
    ms = jnp.mean(x * x, axis=-1, keepdims=True)
    return (x * lax.rsqrt(ms + EPS)) * g


def _norm_proj_kernel(x_ref, g_ref, w_ref, cs_ref, o_ref, xn_ref):
    @pl.when(pl.program_id(1) == 0)
    def _():
        xn_ref[...] = _rmsnorm(x_ref[...], g_ref[...]).astype(BF16)

    acc = jnp.dot(xn_ref[...], w_ref[...], preferred_element_type=F32)
    o_ref[...] = (acc * cs_ref[...]).astype(o_ref.dtype)


def _norm_proj(x, g, w, colscale, out_dtype, tm, tn, name):
    m, k = x.shape
    n = w.shape[1]
    return pl.pallas_call(
        _norm_proj_kernel,
        grid=(m // tm, n // tn),
        in_specs=[
            pl.BlockSpec((tm, k), lambda i, j: (i, 0)),
            pl.BlockSpec((1, k), lambda i, j: (0, 0)),
            pl.BlockSpec((k, tn), lambda i, j: (0, j)),
            pl.BlockSpec((1, tn), lambda i, j: (0, j)),
        ],
        out_specs=pl.BlockSpec((tm, tn), lambda i, j: (i, j)),
        out_shape=jax.ShapeDtypeStruct((m, n), out_dtype),
        scratch_shapes=[pltpu.VMEM((tm, k), BF16)],
        compiler_params=_params("arbitrary", "arbitrary"),
        name=name,
    )(x, g.reshape(1, k), w, colscale.reshape(1, n))


def _in_proj_kernel(x_ref, g_ref, wf_ref, wg_ref, wb_ref, cs_ref, of_ref, og_ref, ob_ref, xn_ref):
    j = pl.program_id(1)

    @pl.when(j == 0)
    def _():
        xn_ref[...] = _rmsnorm(x_ref[...], g_ref[...]).astype(BF16)
        of_ref[...] = jnp.dot(xn_ref[...], wf_ref[...], preferred_element_type=F32)

    @pl.when(j == 1)
    def _():
        og_ref[...] = _silu(jnp.dot(xn_ref[...], wg_ref[...], preferred_element_type=F32)).astype(BF16)

    @pl.when(j == 2)
    def _():
        acc = jnp.dot(xn_ref[...], wb_ref[...], preferred_element_type=F32)
        ob_ref[...] = (acc * cs_ref[...]).astype(BF16)


def _in_proj(x, g, wf, wg, wb, scale_b, tm):
    m, k = x.shape
    whole = lambda a: pl.BlockSpec(a.shape, lambda i, j: (0, 0), pipeline_mode=pl.Buffered(1))
    rows = lambda a: pl.BlockSpec((tm, a.shape[1]), lambda i, j: (i, 0))
    cs = scale_b.reshape(1, -1)
    return pl.pallas_call(
        _in_proj_kernel,
        grid=(m // tm, 3),
        in_specs=[rows(x), pl.BlockSpec((1, k), lambda i, j: (0, 0)),
                  whole(wf), whole(wg), whole(wb), whole(cs)],
        out_specs=[rows(wf), rows(wg), rows(wb)],
        out_shape=[jax.ShapeDtypeStruct((m, wf.shape[1]), F32),
                   jax.ShapeDtypeStruct((m, wg.shape[1]), BF16),
                   jax.ShapeDtypeStruct((m, wb.shape[1]), BF16)],
        scratch_shapes=[pltpu.VMEM((tm, k), BF16)],
        compiler_params=_params("arbitrary", "arbitrary"),
        name="in_proj",
    )(x, g.reshape(1, k), wf, wg, wb, cs)


def _rope_half(hi, cos2, sin2):
    up = pltpu.roll(hi, 32, axis=1)
    down = pltpu.roll(hi, 96, axis=1)
    return hi * cos2 + (up - down) * sin2


def _mla_up_kernel(cq_ref, ckv_ref, kr_ref, gq_ref, gkv_ref, wqt_ref, wk_ref, wvt_ref,
                   cos_ref, sin_ref, cost_ref, sint_ref, qt_ref, k_ref, kpe_ref, vt_ref):
    scale = LOG2E * (MLA_NOPE + MLA_ROPE) ** -0.5
    half = MLA_ROPE // 2
    cos_t = cost_ref[...]
    sin_t = sint_ref[...]

    cqn = _rmsnorm(cq_ref[...], gq_ref[...]).astype(BF16)
    yt = lax.dot_general(wqt_ref[...], cqn, _NT, preferred_element_type=F32)
    for h in range(MLA_HEADS):
        r = h * HEAD_PAD
        x1 = yt[r + MLA_NOPE:r + MLA_NOPE + half]
        x2 = yt[r + MLA_NOPE + half:r + MLA_NOPE + MLA_ROPE]
        qt_ref[0, r:r + MLA_NOPE, :] = (yt[r:r + MLA_NOPE] * scale).astype(BF16)
        qt_ref[0, r + MLA_NOPE:r + MLA_NOPE + half, :] = (
            (x1 * cos_t - x2 * sin_t) * scale).astype(BF16)
        qt_ref[0, r + MLA_NOPE + half:r + MLA_NOPE + MLA_ROPE, :] = (
            (x1 * sin_t + x2 * cos_t) * scale).astype(BF16)
        qt_ref[0, r + MLA_NOPE + MLA_ROPE:r + HEAD_PAD, :] = jnp.zeros(
            (HEAD_PAD - MLA_NOPE - MLA_ROPE, yt.shape[1]), BF16)

    ckn = _rmsnorm(ckv_ref[...], gkv_ref[...]).astype(BF16)
    kn = jnp.dot(ckn, wk_ref[...], preferred_element_type=F32)
    k_pe = _rope_half(kr_ref[...], cos_ref[...], sin_ref[...]).astype(BF16)
    k_ref[...] = kn.astype(BF16)
    kpe_ref[...] = k_pe
    vt = lax.dot_general(wvt_ref[...], ckn, _NT, preferred_element_type=F32)
    ones = jnp.ones((SUM_ROWS, vt.shape[1]), BF16)
    for h in range(MLA_HEADS):
        r = h * (MLA_V + SUM_ROWS)
        vt_ref[0, r:r + MLA_V, :] = vt[h * MLA_V:(h + 1) * MLA_V].astype(BF16)
        vt_ref[0, r + MLA_V:r + MLA_V + SUM_ROWS, :] = ones


def _mla_up(hf, gq, gkv, wqt, wk, wvt, rope, seq, tm):
    cos2, sin2, cos_t, sin_t = rope
    m = hf.shape[0]
    steps_per_seq = seq // tm
    hq = MLA_HEADS * HEAD_PAD
    half = MLA_ROPE // 2
    const = lambda i: (0, 0)
    return pl.pallas_call(
        _mla_up_kernel,
        grid=(m // tm,),
        in_specs=[
            pl.BlockSpec((tm, MLA_Q_RANK), lambda i: (i, F_CQ // MLA_Q_RANK)),
            pl.BlockSpec((tm, MLA_KV_RANK), lambda i: (i, F_CKV // MLA_KV_RANK)),
            pl.BlockSpec((tm, LANES), lambda i: (i, F_KROPE // LANES)),
            pl.BlockSpec((1, MLA_Q_RANK), const),
            pl.BlockSpec((1, MLA_KV_RANK), const),
            pl.BlockSpec((hq, MLA_Q_RANK), const),
            pl.BlockSpec((MLA_KV_RANK, MLA_HEADS * MLA_NOPE), const),
            pl.BlockSpec((MLA_WIDTH, MLA_KV_RANK), const),
            pl.BlockSpec((tm, LANES), lambda i: (i % steps_per_seq, 0)),
            pl.BlockSpec((tm, LANES), lambda i: (i % steps_per_seq, 0)),
            pl.BlockSpec((half, tm), lambda i: (0, i % steps_per_seq)),
            pl.BlockSpec((half, tm), lambda i: (0, i % steps_per_seq)),
        ],
        out_specs=[
            pl.BlockSpec((1, hq, tm), lambda i: (i, 0, 0)),
            pl.BlockSpec((tm, MLA_HEADS * MLA_NOPE), lambda i: (i, 0)),
            pl.BlockSpec((tm, LANES), lambda i: (i, 0)),
            pl.BlockSpec((1, MLA_HEADS * (MLA_V + SUM_ROWS), tm), lambda i: (i, 0, 0)),
        ],
        out_shape=[
            jax.ShapeDtypeStruct((m // tm, hq, tm), BF16),
            jax.ShapeDtypeStruct((m, MLA_HEADS * MLA_NOPE), BF16),
            jax.ShapeDtypeStruct((m, LANES), BF16),
            jax.ShapeDtypeStruct((m // tm, MLA_HEADS * (MLA_V + SUM_ROWS), tm), BF16),
        ],
        compiler_params=_params("arbitrary"),
        name="mla_up",
    )(hf, hf, hf, gq.reshape(1, -1), gkv.reshape(1, -1), wqt, wk, wvt, cos2, sin2, cos_t, sin_t)


def _mla_attn_kernel(qt_ref, k_ref, kpe_ref, vt_ref, z_ref, o_ref, m_ref, acc_ref, s_ref, *,
                     tile_len, hps):
    t = tile_len
    va = MLA_V + SUM_ROWS
    qi = pl.program_id(2)

    def scores(j, h):
        kb = pl.multiple_of(j * t, t)
        k = jnp.concatenate([k_ref[0, pl.ds(kb, t), h * MLA_NOPE:(h + 1) * MLA_NOPE],
                             kpe_ref[0, pl.ds(kb, t), :]], axis=1)
        q_t = qt_ref[0, h * HEAD_PAD:(h + 1) * HEAD_PAD, :]
        return jnp.dot(k, q_t, preferred_element_type=F32)

    def softmax(h, s, diagonal):
        if diagonal:
            kpos = lax.broadcasted_iota(jnp.int32, (t, t), 0)
            qpos = lax.broadcasted_iota(jnp.int32, (t, t), 1)
            s = jnp.where(kpos <= qpos, s, MASK_VALUE)
        m = m_ref[h]
        m_new = jnp.maximum(m, jnp.max(s, axis=0, keepdims=True))
        alpha = jnp.exp2(m - m_new)
        p = jnp.exp2(s - m_new)
        m_ref[h] = m_new
        return alpha, p.astype(BF16)

    def accumulate(j, h, alpha, p):
        vt = vt_ref[j, h * va:(h + 1) * va, :]
        acc_ref[h] = alpha * acc_ref[h] + jnp.dot(vt, p, preferred_element_type=F32)

    def tile(j, diagonal):
        for h in range(hps):
            s_ref[h] = scores(j, h)
        for h in range(hps):
            alpha, p = softmax(h, s_ref[h], diagonal)
            accumulate(j, h, alpha, p)

    for h in range(hps):
        m_ref[h] = jnp.full((1, t), MASK_VALUE, F32)
        acc_ref[h] = jnp.zeros((va, t), F32)

    def body(j, carry):
        tile(j, False)
        return carry

    lax.fori_loop(0, qi, body, 0)

    tile(qi, True)
    for h in range(hps):
        l = acc_ref[h, MLA_V:MLA_V + 1, :]
        o = (acc_ref[h, 0:MLA_V, :] / (l + SOFTMAX_EPS)).T
        cols = slice(h * MLA_V, (h + 1) * MLA_V)
        o_ref[0, :, cols] = (o * z_ref[0, :, cols].astype(F32)).astype(BF16)


def _mla_attn(qt, k, kpe, vt, hg, tile_len, hps):
    b, s, _ = k.shape
    t = tile_len
    tiles = s // t
    kern = functools.partial(_mla_attn_kernel, tile_len=t, hps=hps)
    qw, vw, va = hps * HEAD_PAD, hps * MLA_V, MLA_V + SUM_ROWS
    return pl.pallas_call(
        kern,
        grid=(b, MLA_HEADS // hps, tiles),
        in_specs=[
            pl.BlockSpec((1, qw, t), lambda bi, h, i: (bi * tiles + i, h, 0)),
            pl.BlockSpec((1, s, hps * MLA_NOPE), lambda bi, h, i: (bi, 0, h)),
            pl.BlockSpec((1, s, LANES), lambda bi, h, i: (bi, 0, 0)),
            pl.BlockSpec((tiles, hps * va, t), lambda bi, h, i: (bi, h, 0)),
            pl.BlockSpec((1, t, vw), lambda bi, h, i: (bi, i, G_ZMLA // vw + h)),
        ],
        out_specs=pl.BlockSpec((1, t, vw), lambda bi, h, i: (bi, i, h)),
        out_shape=jax.ShapeDtypeStruct((b, s, MLA_WIDTH), BF16),
        scratch_shapes=[
            pltpu.VMEM((hps, 1, t), F32),
            pltpu.VMEM((hps, va, t), F32),
            pltpu.VMEM((hps, t, t), F32),
        ],
        compiler_params=_params("arbitrary", "arbitrary", "arbitrary"),
        name="mla_attn",
    )(qt, k, kpe, vt, hg)


def _compress_one(x_refs, pe_ref, w1_ref, w2_ref, o_ref):
    chunks = x_refs[0].shape[1] // CMP_STRIDE
    a = b = None
    for i in range(CMP_STRIDE):
        for c, x_ref in enumerate(x_refs):
            lanes = slice(c * LANES, (c + 1) * LANES)
            x = x_ref[0, pl.ds(i, chunks, stride=CMP_STRIDE), :]
            ai = jnp.dot((x + pe_ref[i:i + 1, lanes]).astype(BF16), w1_ref[i, lanes, :],
                         preferred_element_type=F32)
            bi = jnp.dot((x + pe_ref[CMP_STRIDE + i:CMP_STRIDE + i + 1, lanes]).astype(BF16),
                         w1_ref[CMP_STRIDE + i, lanes, :], preferred_element_type=F32)
            a = ai if a is None else a + ai
            b = bi if b is None else b + bi
    h1 = a + pltpu.roll(b, chunks - 1, axis=0)
    o_ref[0] = jnp.dot(_silu(h1).astype(BF16), w2_ref[...], preferred_element_type=F32).astype(BF16)


def _compress_kernel(xk0_ref, xk1_ref, xv_ref, pek_ref, pev_ref, w1k_ref, w2k_ref, w1v_ref, w2v_ref,
                     ok_ref, ov_ref):
    _compress_one((xk0_ref, xk1_ref), pek_ref, w1k_ref, w2k_ref, ok_ref)
    _compress_one((xv_ref,), pev_ref, w1v_ref, w2v_ref, ov_ref)


def _compress(hf, pek, pev, w1k, w2k, w1v, w2v):
    b, s, _ = hf.shape
    chunks = s // CMP_STRIDE
    full = lambda a: pl.BlockSpec(a.shape, lambda bi: (0,) * a.ndim)
    return pl.pallas_call(
        _compress_kernel,
        grid=(b,),
        in_specs=[pl.BlockSpec((1, s, LANES), lambda bi: (bi, 0, F_KC // LANES)),
                  pl.BlockSpec((1, s, LANES), lambda bi: (bi, 0, F_KC // LANES + 1)),
                  pl.BlockSpec((1, s, NSA_DV), lambda bi: (bi, 0, F_VC // NSA_DV)),
                  full(pek), full(pev), full(w1k), full(w2k), full(w1v), full(w2v)],
        out_specs=[
            pl.BlockSpec((1, chunks, HEAD_PAD), lambda bi: (bi, 0, 0)),
            pl.BlockSpec((1, chunks, NSA_DV), lambda bi: (bi, 0, 0)),
        ],
        out_shape=[
            jax.ShapeDtypeStruct((b, chunks, HEAD_PAD), BF16),
            jax.ShapeDtypeStruct((b, chunks, NSA_DV), BF16),
        ],
        compiler_params=_params("arbitrary"),
        name="nsa_compress",
    )(hf, hf, hf, pek, pev, w1k, w2k, w1v, w2v)


def _alibi_slope(h):
    return 2.0 ** (-8.0 * (h + 1) / NSA_HEADS)


def _nsa_attn_kernel(q_ref, g_ref, z_ref, kc_ref, vc_ref, ks_ref, vs_ref, kw_ref, vw_ref,
                     ovt_ref, oh_ref, tabs_ref, tabd_ref, tabw_ref, tabc_ref, slope_ref, o_ref,
                     kaug_ref, kwp_ref, vst_ref, vwt_ref, vct_ref, m_ref, acc_ref, s0_ref,
                     sc_ref, sw_ref, qwin_ref, qsel_ref,
                     *, tq, tk, n_c, n_s, top_n):
    qi = pl.program_id(1)
    t0 = qi * tq
    heads = NSA_HEADS
    seq = ks_ref.shape[1]
    ncp = kc_ref.shape[1]
    hw = 2 * tq
    aug = HEAD_PAD - NSA_DK
    wlen = WIN + tq
    va = NSA_DV + SUM_ROWS
    sub_tiles = tk // tq

    @pl.when(qi == 0)
    def _():
        kaug_ref[...] = ks_ref[0] + oh_ref[...]
        col = lax.broadcasted_iota(jnp.int32, (WIN, HEAD_PAD), 1)
        kwp_ref[0:WIN, :] = jnp.where(col == NSA_DK, 1.0, 0.0).astype(BF16)
        kwp_ref[WIN:, :] = kw_ref[0]
        for i in range(seq // tq):
            vst_ref[i, 0:NSA_DV, :] = vs_ref[0, i * tq:(i + 1) * tq, :].astype(F32).T.astype(BF16)
            vst_ref[i, NSA_DV:va, :] = jnp.ones((SUM_ROWS, tq), BF16)
        for i in range(WIN // LANES):
            vwt_ref[i] = jnp.zeros((va, LANES), BF16)
        for i in range(seq // LANES):
            vwt_ref[WIN // LANES + i, 0:NSA_DV, :] = (
                vw_ref[0, i * LANES:(i + 1) * LANES, :].astype(F32).T.astype(BF16))
            vwt_ref[WIN // LANES + i, NSA_DV:va, :] = jnp.ones((SUM_ROWS, LANES), BF16)
        vct_ref[...] = vc_ref[0].astype(F32).T.astype(BF16)

    q_t = q_ref[0].astype(F32).T
    flag_rows = jnp.where(lax.broadcasted_iota(jnp.int32, (aug, tq), 0) == 0,
                          MASK_VALUE, 0.0).astype(BF16)
    head_lanes = [(h // 2, slice((h % 2) * tq, (h % 2 + 1) * tq)) for h in range(heads)]
    for h, (x, lanes) in enumerate(head_lanes):
        rows = q_t[h * NSA_DK:(h + 1) * NSA_DK].astype(BF16)
        qwin_ref[x, 0:NSA_DK, lanes] = rows
        qsel_ref[x, 0:NSA_DK, lanes] = rows
        qwin_ref[x, NSA_DK:HEAD_PAD, lanes] = flag_rows

    kc = kc_ref[0]
    kwb = kwp_ref[pl.ds(pl.multiple_of(t0, LANES), wlen), :]
    for x in range(2):
        sc_ref[x] = jnp.dot(kc, qwin_ref[x], preferred_element_type=F32)
    for x in range(2):
        sw_ref[x] = (jnp.dot(kwb, qwin_ref[x], preferred_element_type=F32)
                     + tabw_ref[:, x * hw:(x + 1) * hw])

    first_blk = qi * (tq // CMP_STRIDE)
    tab_c = tabc_ref[pl.ds(pl.multiple_of(ncp - first_blk, CMP_STRIDE), ncp), :]
    o_cmp, p_sum = [], None
    for x in range(2):
        p_x = []
        for hl in range(2):
            h = 2 * x + hl
            s = sc_ref[x, :, hl * tq:(hl + 1) * tq] + tab_c[:, h * tq:(h + 1) * tq]
            m = jnp.max(s, axis=0, keepdims=True)
            e = jnp.exp2(s - m)
            inv = jnp.where(m > 0.5 * MASK_VALUE,
                            1.0 / (jnp.sum(e, axis=0, keepdims=True) + SOFTMAX_EPS), 0.0)
            p = e * inv
            p_sum = p if p_sum is None else p_sum + p
            p_x.append(p.astype(BF16))
        o_cmp.append(jnp.dot(vct_ref[...], jnp.concatenate(p_x, axis=1),
                             preferred_element_type=F32))
    p_hi = p_sum.astype(BF16)
    p_lo = (p_sum - p_hi.astype(F32)).astype(BF16)
    ovt = ovt_ref[...]
    imp = (jnp.dot(ovt, p_hi, preferred_element_type=F32)
           + jnp.dot(ovt, p_lo, preferred_element_type=F32))

    tile0 = qi * (tq // LANES)
    vwb = jnp.concatenate([vwt_ref[tile0 + r] for r in range(wlen // LANES)], axis=1)
    o_win = []
    for x in range(2):
        s = sw_ref[x]
        p = jnp.exp2(s - jnp.max(s, axis=0, keepdims=True))
        pv = jnp.dot(vwb, p.astype(BF16), preferred_element_type=F32)
        o_win.append(pv[0:NSA_DV] / (pv[NSA_DV:NSA_DV + 1] + SOFTMAX_EPS))

    blk = lax.broadcasted_iota(jnp.int32, (n_s, tq), 0)
    cur = (t0 + lax.broadcasted_iota(jnp.int32, (n_s, tq), 1)) // SLC_LEN
    forced = (blk == 0) | (blk == cur) | (blk == cur - 1)
    imp = jnp.where(forced, NEG_BIG, imp)
    imp = jnp.where(blk > cur, -NEG_BIG, imp)
    sub = lax.broadcasted_iota(jnp.int32, (8, tq), 0)
    groups = [imp[8 * g:8 * g + 8] for g in range(n_s // 8)]
    ranks = [jnp.zeros((8, tq), F32) for _ in groups]
    for jp in range(n_s):
        row = imp[jp:jp + 1, :]
        for g, grp in enumerate(groups):
            ge = jnp.where(row >= grp, 1.0, 0.0)
            gt = jnp.where(row > grp, 1.0, 0.0)
            if 8 * g > jp:
                beats = ge
            elif 8 * g + 8 <= jp:
                beats = gt
            else:
                beats = jnp.where(sub + 8 * g > jp, ge, gt)
            ranks[g] = ranks[g] + beats
    sel_rows = jnp.where(jnp.concatenate(ranks, axis=0) < top_n, 0.0, MASK_VALUE)
    if n_s < aug:
        sel_rows = jnp.concatenate([sel_rows, jnp.zeros((aug - n_s, tq), F32)], axis=0)
    for x, lanes in head_lanes:
        qsel_ref[x, NSA_DK:HEAD_PAD, lanes] = sel_rows.astype(BF16)

    jd = t0 // tk

    def scores(j, x):
        kb = j * tk if isinstance(j, int) else pl.multiple_of(j * tk, tk)
        return jnp.dot(kaug_ref[pl.ds(kb, tk), :], qsel_ref[x], preferred_element_type=F32)

    def softmax(x, s, tab, key0):
        u = s + tab
        off = slope_ref[0:1, x * hw:(x + 1) * hw] * (key0 - t0).astype(F32)
        m_old = m_ref[x]
        m_new = jnp.maximum(m_old, jnp.max(u, axis=0, keepdims=True) + off)
        alpha = jnp.exp2(m_old - m_new)
        p = jnp.exp2(u - (m_new - off))
        m_ref[x] = m_new
        return alpha, p.astype(BF16)

    def accumulate(x, alpha, p, vt):
        acc_ref[x] = alpha * acc_ref[x] + jnp.dot(vt, p, preferred_element_type=F32)

    def full_tile(j):
        vt = jnp.concatenate([vst_ref[j * sub_tiles + r] for r in range(sub_tiles)], axis=1)
        for x in range(2):
            alpha, p = softmax(x, s0_ref[x], tabs_ref[:, x * hw:(x + 1) * hw], j * tk)
            accumulate(x, alpha, p, vt)
            s0_ref[x] = scores(j + 1, x)

    def sub_tile(r, tab_ref):
        key0 = jd * tk + r * tq
        rows = pl.ds(r * tq if isinstance(r, int) else pl.multiple_of(r * tq, tq), tq)
        vt = vst_ref[jd * sub_tiles + r]
        for x in range(2):
            alpha, p = softmax(x, s0_ref[x, rows, :], tab_ref[0:tq, x * hw:(x + 1) * hw], key0)
            accumulate(x, alpha, p, vt)

    for x in range(2):
        m_ref[x] = jnp.full((1, hw), MASK_VALUE, F32)
        acc_ref[x] = jnp.zeros((va, hw), F32)
        s0_ref[x] = scores(0, x)

    def body(j, carry):
        full_tile(j)
        return carry

    lax.fori_loop(0, jd, body, 0)
    own = (t0 - jd * tk) // tq
    for r in range(sub_tiles - 1):
        @pl.when(r < own)
        def _():
            sub_tile(r, tabs_ref)
    sub_tile(own, tabd_ref)
    o_slc = [acc_ref[x, 0:NSA_DV, :] / (acc_ref[x, NSA_DV:NSA_DV + 1, :] + SOFTMAX_EPS)
             for x in range(2)]

    gates = _sigmoid(g_ref[0]).T
    z = z_ref[0]
    for h in range(heads):
        x, lanes = h // 2, slice((h % 2) * tq, (h % 2 + 1) * tq)
        c = NSA_DK + 3 * h
        o = (gates[c:c + 1] * o_cmp[x][:, lanes] + gates[c + 1:c + 2] * o_slc[x][:, lanes]
             + gates[c + 2:c + 3] * o_win[x][:, lanes])
        zh = z[:, h * NSA_DV:(h + 1) * NSA_DV]
        o_ref[0, :, h * NSA_DV:(h + 1) * NSA_DV] = (o.T * zh.astype(F32)).astype(BF16)


def _nsa_attn(hb, hf, hg, k_cmp, v_cmp, tables, tq, tk):
    ovt, onehot, tab_sel, tab_diag, tab_win, tab_cmp, slope_rows = tables
    b, s, _ = hb.shape
    ncp = k_cmp.shape[1]
    n_s = s // SLC_LEN
    assert n_s <= HEAD_PAD - NSA_DK and n_s % 8 == 0, "selection blocks must fit the spare rows"
    assert tq & (tq - 1) == 0 and tq % LANES == 0 and tk % tq == 0 and s % tk == 0
    kern = functools.partial(_nsa_attn_kernel, tq=tq, tk=tk, n_c=ncp - 1, n_s=n_s,
                             top_n=min(SLC_TOPN, n_s))
    qw = NSA_HEADS * NSA_DK
    const2 = lambda a: pl.BlockSpec(a.shape, lambda bi, i: (0, 0), pipeline_mode=pl.Buffered(1))
    return pl.pallas_call(
        kern,
        grid=(b, s // tq),
        in_specs=[
            pl.BlockSpec((1, tq, qw), lambda bi, i: (bi, i, B_QNSA // qw)),
            pl.BlockSpec((1, tq, HEAD_PAD), lambda bi, i: (bi, i, F_KC // HEAD_PAD)),
            pl.BlockSpec((1, tq, NSA_WIDTH), lambda bi, i: (bi, i, G_ZNSA // NSA_WIDTH)),
            pl.BlockSpec((1, ncp, HEAD_PAD), lambda bi, i: (bi, 0, 0)),
            pl.BlockSpec((1, ncp, NSA_DV), lambda bi, i: (bi, 0, 0)),
            pl.BlockSpec((1, s, HEAD_PAD), lambda bi, i: (bi, 0, B_KS // HEAD_PAD)),
            pl.BlockSpec((1, s, NSA_DV), lambda bi, i: (bi, 0, B_VS // NSA_DV)),
            pl.BlockSpec((1, s, HEAD_PAD), lambda bi, i: (bi, 0, B_KW // HEAD_PAD)),
            pl.BlockSpec((1, s, NSA_DV), lambda bi, i: (bi, 0, B_VW // NSA_DV)),
            const2(ovt), const2(onehot), const2(tab_sel), const2(tab_diag), const2(tab_win),
            const2(tab_cmp), const2(slope_rows),
        ],
        out_specs=pl.BlockSpec((1, tq, NSA_WIDTH), lambda bi, i: (bi, i, 0)),
        out_shape=jax.ShapeDtypeStruct((b, s, NSA_WIDTH), BF16),
        scratch_shapes=[
            pltpu.VMEM((s, HEAD_PAD), BF16),
            pltpu.VMEM((s + WIN, HEAD_PAD), BF16),
            pltpu.VMEM((s // tq, NSA_DV + SUM_ROWS, tq), BF16),
            pltpu.VMEM(((s + WIN) // LANES, NSA_DV + SUM_ROWS, LANES), BF16),
            pltpu.VMEM((NSA_DV, ncp), BF16),
            pltpu.VMEM((2, 1, 2 * tq), F32),
            pltpu.VMEM((2, NSA_DV + SUM_ROWS, 2 * tq), F32),
            pltpu.VMEM((2, tk, 2 * tq), F32),
            pltpu.VMEM((2, ncp, 2 * tq), F32),
            pltpu.VMEM((2, WIN + tq, 2 * tq), F32),
            pltpu.VMEM((2, HEAD_PAD, 2 * tq), BF16),
            pltpu.VMEM((2, HEAD_PAD, 2 * tq), BF16),
        ],
        compiler_params=_params("arbitrary", "arbitrary"),
        name="nsa_attn",
    )(hb, hf, hg, k_cmp, v_cmp, hb, hb, hb, hb, ovt, onehot, tab_sel, tab_diag, tab_win, tab_cmp,
      slope_rows)


def _mem_attn_kernel(q_ref, k_ref, v_ref, z_ref, o_ref):
    q = q_ref[0]
    k = k_ref[0]
    v = v_ref[0]
    z = z_ref[0]
    for h in range(MEM_HEADS):
        sl = slice(h * MEM_DH, (h + 1) * MEM_DH)
        s = lax.dot_general(q[:, sl], k[:, sl], _NT, preferred_element_type=F32)
        p = jnp.exp(s - jnp.max(s, axis=-1, keepdims=True))
        o = jnp.dot(p.astype(BF16), v[:, sl], preferred_element_type=F32)
        o = o / jnp.sum(p, axis=-1, keepdims=True)
        o_ref[0, :, sl] = (o * z[:, sl].astype(F32)).astype(BF16)


def _mem_attn(hb, mem_kv, hg, tq):
    b, s, _ = hb.shape
    mlen = mem_kv.shape[1]
    return pl.pallas_call(
        _mem_attn_kernel,
        grid=(b, s // tq),
        in_specs=[
            pl.BlockSpec((1, tq, MEM_WIDTH), lambda bi, i: (bi, i, B_QMEM // MEM_WIDTH)),
            pl.BlockSpec((1, mlen, MEM_WIDTH), lambda bi, i: (bi, 0, 0)),
            pl.BlockSpec((1, mlen, MEM_WIDTH), lambda bi, i: (bi, 0, 1)),
            pl.BlockSpec((1, tq, MEM_WIDTH), lambda bi, i: (bi, i, G_ZMEM // MEM_WIDTH)),
        ],
        out_specs=pl.BlockSpec((1, tq, MEM_WIDTH), lambda bi, i: (bi, i, 0)),
        out_shape=jax.ShapeDtypeStruct((b, s, MEM_WIDTH), BF16),
        compiler_params=_params("arbitrary", "arbitrary"),
        name="mem_attn",
    )(hb, mem_kv, mem_kv, hg)


def _out_proj_kernel(x_ref, a_ref, n_ref, m_ref, w_ref, g_ref, o_ref, *, final_norm):
    y = x_ref[...]
    y = y + jnp.dot(a_ref[...], w_ref[0:MLA_WIDTH, :], preferred_element_type=F32)
    y = y + jnp.dot(n_ref[...], w_ref[MLA_WIDTH:MLA_WIDTH + NSA_WIDTH, :], preferred_element_type=F32)
    y = y + jnp.dot(m_ref[...], w_ref[MLA_WIDTH + NSA_WIDTH:, :], preferred_element_type=F32)
    if final_norm:
        y = _rmsnorm(y, g_ref[...])
    o_ref[...] = y


def _out_proj(x, o_mla, o_nsa, o_mem, w_out, g, final_norm, tm):
    m, d = x.shape
    kern = functools.partial(_out_proj_kernel, final_norm=final_norm)
    row = lambda width: pl.BlockSpec((tm, width), lambda i: (i, 0))
    return pl.pallas_call(
        kern,
        grid=(m // tm,),
        in_specs=[row(d), row(MLA_WIDTH), row(NSA_WIDTH), row(MEM_WIDTH),
                  pl.BlockSpec(w_out.shape, lambda i: (0, 0)),
                  pl.BlockSpec((1, d), lambda i: (0, 0))],
        out_specs=row(d),
        out_shape=jax.ShapeDtypeStruct((m, d), F32),
        compiler_params=_params("arbitrary"),
        name="out_proj",
    )(x, o_mla, o_nsa, o_mem, w_out, g.reshape(1, d))


def _pad_cols(w, width):
    return jnp.pad(w, ((0, 0), (0, width - w.shape[1])))


def _w_in_moves():
    names = ("c_q", "c_kv", "k_rope", "z_mla", "q_nsa", "k_c", "v_c", "k_s", "v_s", "k_w", "v_w",
             "g_nsa", "z_nsa", "q_mem", "z_mem")
    src, off = {}, 0
    for name, n in zip(names, IN_SPLITS):
        src[name] = (off, n)
        off += n
    dst = {"c_q": (0, F_CQ), "c_kv": (0, F_CKV), "k_c": (0, F_KC), "g_nsa": (0, F_GATE),
           "v_c": (0, F_VC), "k_rope": (0, F_KROPE),
           "z_mla": (1, G_ZMLA), "z_nsa": (1, G_ZNSA), "z_mem": (1, G_ZMEM),
           "q_nsa": (2, B_QNSA), "q_mem": (2, B_QMEM), "k_s": (2, B_KS), "k_w": (2, B_KW),
           "v_s": (2, B_VS), "v_w": (2, B_VW)}
    return [(src[n][0], src[n][1], dst[n][0], dst[n][1]) for n in dst]


def _w_prep_kernel(wt_ref, wf_ref, wg_ref, wb_ref):
    outs = (wf_ref, wg_ref, wb_ref)
    for o_ref in outs:
        o_ref[...] = jnp.zeros(o_ref.shape, BF16)
    for s0, width, which, d0 in _w_in_moves():
        rows = -(-width // LANES) * LANES
        start = min(s0, wt_ref.shape[0] - rows)
        slab = wt_ref[start:start + rows, :].T
        outs[which][:, d0:d0 + width] = slab[:, s0 - start:s0 - start + width].astype(BF16)


def _layout_w_in(w_in, tr=256):
    d, n = w_in.shape
    widths = (F_WIDTH, G_WIDTH, B_WIDTH)
    wf, wg, wb = pl.pallas_call(
        _w_prep_kernel,
        grid=(d // tr,),
        in_specs=[pl.BlockSpec((n, tr), lambda i: (0, i))],
        out_specs=[pl.BlockSpec((tr, w), lambda i: (i, 0)) for w in widths],
        out_shape=[jax.ShapeDtypeStruct((d, w), BF16) for w in widths],
        compiler_params=_params("arbitrary"),
        name="w_in_layout",
    )(w_in.T)
    scale_b = jnp.ones((B_WIDTH,), F32)
    scale_b = scale_b.at[B_QNSA:B_QNSA + NSA_HEADS * NSA_DK].set(LOG2E * NSA_DK ** -0.5)
    scale_b = scale_b.at[B_QMEM:B_QMEM + MEM_WIDTH].set(MEM_DH ** -0.5)
    return wf, wg, wb, scale_b


def _rope_tables(seq):
    pos = jnp.arange(seq, dtype=F32)
    inv_freq = ROPE_THETA ** (-jnp.arange(0, MLA_ROPE, 2, dtype=F32) / MLA_ROPE)
    ang = pos[:, None] * inv_freq[None, :]
    zeros = jnp.zeros((seq, LANES - MLA_ROPE), F32)
    cos2 = jnp.concatenate([jnp.cos(ang), jnp.cos(ang), zeros], axis=1)
    sin2 = jnp.concatenate([jnp.sin(ang), jnp.sin(ang), zeros], axis=1)
    return cos2, sin2, jnp.cos(ang).T, jnp.sin(ang).T


def _nsa_tables(seq, tq, tk):
    chunks = seq // CMP_STRIDE
    n_s = seq // SLC_LEN
    c_start = jnp.arange(chunks) * CMP_STRIDE
    s_start = jnp.arange(n_s) * SLC_LEN
    overlap_t = ((c_start[None, :] < s_start[:, None] + SLC_LEN)
                 & (c_start[None, :] + CMP_LEN > s_start[:, None])
                 & (jnp.arange(chunks)[None, :] < chunks - 1))
    key_block = jnp.arange(seq) // SLC_LEN
    onehot = jnp.arange(HEAD_PAD)[None, :] == (NSA_DK + key_block)[:, None]
    slope = jnp.repeat(jnp.array([_alibi_slope(h) for h in range(NSA_HEADS)], F32) * LOG2E, tq)
    q_lane = jnp.tile(jnp.arange(tq), NSA_HEADS)
    tab_sel = jnp.arange(tk, dtype=F32)[:, None] * slope[None, :]
    rel = q_lane[None, :] - jnp.arange(WIN + tq)[:, None] + WIN
    tab_win = jnp.where((rel >= 0) & (rel < WIN), -slope[None, :] * rel.astype(F32), MASK_VALUE)
    slope_rows = jnp.broadcast_to(slope[None, :], (8, NSA_HEADS * tq))
    tab_diag = jnp.where(jnp.arange(tq)[:, None] <= q_lane[None, :], tab_sel[0:tq], MASK_VALUE)
    d_blk = jnp.arange(2 * chunks)[:, None] - chunks
    seen = d_blk * CMP_STRIDE + (CMP_LEN - 1) <= q_lane[None, :]
    dist = (q_lane[None, :] - d_blk * CMP_STRIDE).astype(F32) - (CMP_LEN - 1) / 2.0
    tab_cmp = jnp.where(seen, -slope[None, :] * dist, MASK_VALUE)
    return overlap_t.astype(BF16), onehot.astype(BF16), tab_sel, tab_diag, tab_win, tab_cmp, slope_rows


def _layer(x2, mem2, batch, seq, tables, norm_g, w_in, q_norm_g, w_uq, kv_norm_g, w_ukv,
           cmp_pe_k, cmp_pe_v, cmp_w1k, cmp_w2k, cmp_w1v, cmp_w2v, mem_norm_g, w_mem_kv, w_out,
           final_g, final_norm):
    rope, nsa_tables = tables
    d = x2.shape[1]
    wf, wg, wb, scale_b = _layout_w_in(w_in)
    hf, hg, hb = _in_proj(x2, norm_g, wf, wg, wb, scale_b, 512)

    wq = jnp.pad(w_uq.reshape(MLA_Q_RANK, MLA_HEADS, MLA_NOPE + MLA_ROPE),
                 ((0, 0), (0, 0), (0, HEAD_PAD - MLA_NOPE - MLA_ROPE)))
    wqt = wq.reshape(MLA_Q_RANK, MLA_HEADS * HEAD_PAD).T.astype(BF16)
    wkv = w_ukv.reshape(MLA_KV_RANK, MLA_HEADS, MLA_NOPE + MLA_V)
    wk = wkv[:, :, :MLA_NOPE].reshape(MLA_KV_RANK, MLA_HEADS * MLA_NOPE).astype(BF16)
    wvt = wkv[:, :, MLA_NOPE:].reshape(MLA_KV_RANK, MLA_WIDTH).T.astype(BF16)
    qt, k, kpe, vt = _mla_up(hf, q_norm_g, kv_norm_g, wqt, wk, wvt, rope, seq, MLA_TILE)
    hf3 = hf.reshape(batch, seq, F_WIDTH)
    hg3 = hg.reshape(batch, seq, G_WIDTH)
    hb3 = hb.reshape(batch, seq, B_WIDTH)
    o_mla = _mla_attn(qt, k.reshape(batch, seq, -1), kpe.reshape(batch, seq, -1), vt, hg3,
                      MLA_TILE, 4)

    pad_k = HEAD_PAD - NSA_DK
    k_cmp, v_cmp = _compress(
        hf3, _pad_cols(cmp_pe_k, HEAD_PAD), cmp_pe_v,
        jnp.pad(cmp_w1k.reshape(CMP_LEN, NSA_DK, NSA_DK), ((0, 0), (0, pad_k), (0, pad_k))).astype(BF16),
        jnp.pad(cmp_w2k, ((0, pad_k), (0, pad_k))).astype(BF16),
        cmp_w1v.reshape(CMP_LEN, NSA_DV, NSA_DV).astype(BF16), cmp_w2v.astype(BF16))
    o_nsa = _nsa_attn(hb3, hf3, hg3, k_cmp, v_cmp, nsa_tables, NSA_TQ, NSA_TK)

    mem_kv = _norm_proj(mem2, mem_norm_g, w_mem_kv.astype(BF16), jnp.ones((2 * MEM_WIDTH,), F32),
                        BF16, mem2.shape[0] // batch, MEM_WIDTH, "mem_kv_proj")
    o_mem = _mem_attn(hb3, mem_kv.reshape(batch, -1, 2 * MEM_WIDTH), hg3, 512)

    return _out_proj(x2, o_mla.reshape(-1, MLA_WIDTH), o_nsa.reshape(-1, NSA_WIDTH),
                     o_mem.reshape(-1, MEM_WIDTH), w_out.astype(BF16), final_g, final_norm, 512)


def kernel(x, mem, norm_g, w_in, q_norm_g, w_uq, kv_norm_g, w_ukv, cmp_pe_k, cmp_pe_v,
           cmp_w1k, cmp_w2k, cmp_w1v, cmp_w2v, mem_norm_g, w_mem_kv, w_out, final_norm_g):
    batch, seq, d = x.shape
    depth = norm_g.shape[0]
    tables = (_rope_tables(seq), _nsa_tables(seq, NSA_TQ, NSA_TK))
    x2 = x.reshape(batch * seq, d)
    mem2 = mem.reshape(batch * mem.shape[1], d)
    for l in range(depth):
        x2 = _layer(x2, mem2, batch, seq, tables, norm_g[l], w_in[l], q_norm_g[l], w_uq[l],
                    kv_norm_g[l], w_ukv[l], cmp_pe_k[l], cmp_pe_v[l], cmp_w1k[l], cmp_w2k[l],
                    cmp_w1v[l], cmp_w2v[l], mem_norm_g[l], w_mem_kv[l], w_out[l],
                    final_norm_g, l == depth - 1)
    return x2.reshape(batch, seq, d)
```

```python
import functools

import jax
import jax.numpy as jnp
from jax import lax
from jax.experimental import pallas as pl
from jax.experimental.pallas import tpu as pltpu

F32 = jnp.float32
BF16 = jnp.bfloat16

EPS = 1e-6
NEG_BIG = 1e9
MASK_VALUE = -1e30
SOFTMAX_EPS = 1e-20
LOG2E = 1.4426950408889634

MLA_HEADS = 8
MLA_NOPE = 128
MLA_ROPE = 64
MLA_V = 128
MLA_Q_RANK = 512
MLA_KV_RANK = 512
ROPE_THETA = 10000.0

NSA_HEADS = 4
NSA_DK = 192
NSA_DV = 128
CMP_LEN = 32
CMP_STRIDE = 16
SLC_LEN = 64
SLC_TOPN = 16
WIN = 512

MEM_HEADS = 4
MEM_DH = 128

MLA_WIDTH = MLA_HEADS * MLA_V
NSA_WIDTH = NSA_HEADS * NSA_DV
MEM_WIDTH = MEM_HEADS * MEM_DH

IN_SPLITS = (
    MLA_Q_RANK, MLA_KV_RANK, MLA_ROPE, MLA_WIDTH,
    NSA_HEADS * NSA_DK, NSA_DK, NSA_DV, NSA_DK, NSA_DV,
    NSA_DK, NSA_DV, 3 * NSA_HEADS, NSA_WIDTH,
    MEM_WIDTH, MEM_WIDTH,
)

LANES = 128
HEAD_PAD = 256
VMEM_LIMIT = 56 * 1024 * 1024
SUM_ROWS = 16
PROJ_ROWS = 512
MLA_TILE = 512
MLA_HEADS_PER_STEP = 4
NSA_TQ = 256
NSA_TK = 512

F_CQ, F_CKV, F_KC, F_VC, F_KROPE = 0, 512, 1024, 1280, 1408
F_GATE = F_KC + NSA_DK
F_WIDTH = 1536
G_ZMLA, G_ZNSA, G_ZMEM = 0, MLA_WIDTH, MLA_WIDTH + NSA_WIDTH
G_WIDTH = MLA_WIDTH + NSA_WIDTH + MEM_WIDTH
B_QNSA, B_KS, B_QMEM, B_KW, B_VS, B_VW = 0, 768, 1024, 1536, 1792, 1920
B_WIDTH = 2048

_NT = (((1,), (1,)), ((), ()))


def _params(*sem):
    return pltpu.CompilerParams(dimension_semantics=sem, vmem_limit_bytes=VMEM_LIMIT)


def _sigmoid(x):
    return 1.0 / (1.0 + jnp.exp(-x))


def _silu(x):
    return x * _sigmoid(x)


def _rmsnorm(x, g):
    ms = jnp.mean(x * x, axis=-1, keepdims=True)
    return (x * lax.rsqrt(ms + EPS)) * g


def _norm_proj_kernel(x_ref, g_ref, w_ref, cs_ref, o_ref, xn_ref):
    @pl.when(pl.program_id(1) == 0)
    def _():
        xn_ref[...] = _rmsnorm(x_ref[...], g_ref[...]).astype(BF16)

    acc = jnp.dot(xn_ref[...], w_ref[...], preferred_element_type=F32)
    o_ref[...] = (acc * cs_ref[...]).astype(o_ref.dtype)


def _norm_proj(x, g, w, colscale, out_dtype, tm, tn, name):
    m, k = x.shape
    n = w.shape[1]
    return pl.pallas_call(
        _norm_proj_kernel,
        grid=(m // tm, n // tn),
        in_specs=[
            pl.BlockSpec((tm, k), lambda i, j: (i, 0)),
            pl.BlockSpec((1, k), lambda i, j: (0, 0)),
            pl.BlockSpec((k, tn), lambda i, j: (0, j)),
            pl.BlockSpec((1, tn), lambda i, j: (0, j)),
        ],
        out_specs=pl.BlockSpec((tm, tn), lambda i, j: (i, j)),
        out_shape=jax.ShapeDtypeStruct((m, n), out_dtype),
        scratch_shapes=[pltpu.VMEM((tm, k), BF16)],
        compiler_params=_params("arbitrary", "arbitrary"),
        name=name,
    )(x, g.reshape(1, k), w, colscale.reshape(1, n))


def _in_proj_kernel(x_ref, g_ref, wf_ref, wg_ref, wb_ref, cs_ref, of_ref, og_ref, ob_ref, xn_ref):
    j = pl.program_id(1)

    @pl.when(j == 0)
    def _():
        xn_ref[...] = _rmsnorm(x_ref[...], g_ref[...]).astype(BF16)
        of_ref[...] = jnp.dot(xn_ref[...], wf_ref[...], preferred_element_type=F32)

    @pl.when(j == 1)
    def _():
        og_ref[...] = _silu(jnp.dot(xn_ref[...], wg_ref[...], preferred_element_type=F32)).astype(BF16)

    @pl.when(j == 2)
    def _():
        acc = jnp.dot(xn_ref[...], wb_ref[...], preferred_element_type=F32)
        ob_ref[...] = (acc * cs_ref[...]).astype(BF16)


def _in_proj(x, g, wf, wg, wb, scale_b, tm):
    m, k = x.shape
    whole = lambda a: pl.BlockSpec(a.shape, lambda i, j: (0, 0), pipeline_mode=pl.Buffered(1))
    rows = lambda a: pl.BlockSpec((tm, a.shape[1]), lambda i, j: (i, 0))
    cs = scale_b.reshape(1, -1)
    return pl.pallas_call(
        _in_proj_kernel,
        grid=(m // tm, 3),
        in_specs=[rows(x), pl.BlockSpec((1, k), lambda i, j: (0, 0)),
                  whole(wf), whole(wg), whole(wb), whole(cs)],
        out_specs=[rows(wf), rows(wg), rows(wb)],
        out_shape=[jax.ShapeDtypeStruct((m, wf.shape[1]), F32),
                   jax.ShapeDtypeStruct((m, wg.shape[1]), BF16),
                   jax.ShapeDtypeStruct((m, wb.shape[1]), BF16)],
        scratch_shapes=[pltpu.VMEM((tm, k), BF16)],
        compiler_params=_params("arbitrary", "arbitrary"),
        name="in_proj",
    )(x, g.reshape(1, k), wf, wg, wb, cs)


def _rope_half(hi, cos2, sin2):
    up = pltpu.roll(hi, 32, axis=1)
    down = pltpu.roll(hi, 96, axis=1)
    return hi * cos2 + (up - down) * sin2


def _mla_up_kernel(cq_ref, ckv_ref, kr_ref, gq_ref, gkv_ref, wqt_ref, wk_ref, wvt_ref,
                   cos_ref, sin_ref, cost_ref, sint_ref, qt_ref, k_ref, kpe_ref, vt_ref):
    scale = LOG2E * (MLA_NOPE + MLA_ROPE) ** -0.5
    half = MLA_ROPE // 2
    cos_t = cost_ref[...]
    sin_t = sint_ref[...]

    cqn = _rmsnorm(cq_ref[...], gq_ref[...]).astype(BF16)
    yt = lax.dot_general(wqt_ref[...], cqn, _NT, preferred_element_type=F32)
    for h in range(MLA_HEADS):
        r = h * HEAD_PAD
        x1 = yt[r + MLA_NOPE:r + MLA_NOPE + half]
        x2 = yt[r + MLA_NOPE + half:r + MLA_NOPE + MLA_ROPE]
        qt_ref[0, r:r + MLA_NOPE, :] = (yt[r:r + MLA_NOPE] * scale).astype(BF16)
        qt_ref[0, r + MLA_NOPE:r + MLA_NOPE + half, :] = (
            (x1 * cos_t - x2 * sin_t) * scale).astype(BF16)
        qt_ref[0, r + MLA_NOPE + half:r + MLA_NOPE + MLA_ROPE, :] = (
            (x1 * sin_t + x2 * cos_t) * scale).astype(BF16)
        qt_ref[0, r + MLA_NOPE + MLA_ROPE:r + HEAD_PAD, :] = jnp.zeros(
            (HEAD_PAD - MLA_NOPE - MLA_ROPE, yt.shape[1]), BF16)

    ckn = _rmsnorm(ckv_ref[...], gkv_ref[...]).astype(BF16)
    kn = jnp.dot(ckn, wk_ref[...], preferred_element_type=F32)
    k_pe = _rope_half(kr_ref[...], cos_ref[...], sin_ref[...]).astype(BF16)
    k_ref[...] = kn.astype(BF16)
    kpe_ref[...] = k_pe
    vt = lax.dot_general(wvt_ref[...], ckn, _NT, preferred_element_type=F32)
    ones = jnp.ones((SUM_ROWS, vt.shape[1]), BF16)
    for h in range(MLA_HEADS):
        r = h * (MLA_V + SUM_ROWS)
        vt_ref[0, r:r + MLA_V, :] = vt[h * MLA_V:(h + 1) * MLA_V].astype(BF16)
        vt_ref[0, r + MLA_V:r + MLA_V + SUM_ROWS, :] = ones


def _mla_up(hf, gq, gkv, wqt, wk, wvt, rope, seq, tm):
    cos2, sin2, cos_t, sin_t = rope
    m = hf.shape[0]
    steps_per_seq = seq // tm
    hq = MLA_HEADS * HEAD_PAD
    half = MLA_ROPE // 2
    const = lambda i: (0, 0)
    return pl.pallas_call(
        _mla_up_kernel,
        grid=(m // tm,),
        in_specs=[
            pl.BlockSpec((tm, MLA_Q_RANK), lambda i: (i, F_CQ // MLA_Q_RANK)),
            pl.BlockSpec((tm, MLA_KV_RANK), lambda i: (i, F_CKV // MLA_KV_RANK)),
            pl.BlockSpec((tm, LANES), lambda i: (i, F_KROPE // LANES)),
            pl.BlockSpec((1, MLA_Q_RANK), const),
            pl.BlockSpec((1, MLA_KV_RANK), const),
            pl.BlockSpec((hq, MLA_Q_RANK), const),
            pl.BlockSpec((MLA_KV_RANK, MLA_HEADS * MLA_NOPE), const),
            pl.BlockSpec((MLA_WIDTH, MLA_KV_RANK), const),
            pl.BlockSpec((tm, LANES), lambda i: (i % steps_per_seq, 0)),
            pl.BlockSpec((tm, LANES), lambda i: (i % steps_per_seq, 0)),
            pl.BlockSpec((half, tm), lambda i: (0, i % steps_per_seq)),
            pl.BlockSpec((half, tm), lambda i: (0, i % steps_per_seq)),
        ],
        out_specs=[
            pl.BlockSpec((1, hq, tm), lambda i: (i, 0, 0)),
            pl.BlockSpec((tm, MLA_HEADS * MLA_NOPE), lambda i: (i, 0)),
            pl.BlockSpec((tm, LANES), lambda i: (i, 0)),
            pl.BlockSpec((1, MLA_HEADS * (MLA_V + SUM_ROWS), tm), lambda i: (i, 0, 0)),
        ],
        out_shape=[
            jax.ShapeDtypeStruct((m // tm, hq, tm), BF16),
            jax.ShapeDtypeStruct((m, MLA_HEADS * MLA_NOPE), BF16),
            jax.ShapeDtypeStruct((m, LANES), BF16),
            jax.ShapeDtypeStruct((m // tm, MLA_HEADS * (MLA_V + SUM_ROWS), tm), BF16),
        ],
        compiler_params=_params("arbitrary"),
        name="mla_up",
    )(hf, hf, hf, gq.reshape(1, -1), gkv.reshape(1, -1), wqt, wk, wvt, cos2, sin2, cos_t, sin_t)


def _mla_attn_kernel(qt_ref, k_ref, kpe_ref, vt_ref, z_ref, o_ref, m_ref, acc_ref, s_ref, *,
                     tile_len, hps):
    t = tile_len
    va = MLA_V + SUM_ROWS
    qi = pl.program_id(2)

    def scores(j, h):
        kb = pl.multiple_of(j * t, t)
        k = jnp.concatenate([k_ref[0, pl.ds(kb, t), h * MLA_NOPE:(h + 1) * MLA_NOPE],
                             kpe_ref[0, pl.ds(kb, t), :]], axis=1)
        q_t = qt_ref[0, h * HEAD_PAD:(h + 1) * HEAD_PAD, :]
        return jnp.dot(k, q_t, preferred_element_type=F32)

    def softmax(h, s, diagonal):
        if diagonal:
            kpos = lax.broadcasted_iota(jnp.int32, (t, t), 0)
            qpos = lax.broadcasted_iota(jnp.int32, (t, t), 1)
            s = jnp.where(kpos <= qpos, s, MASK_VALUE)
        m = m_ref[h]
        m_new = jnp.maximum(m, jnp.max(s, axis=0, keepdims=True))
        alpha = jnp.exp2(m - m_new)
        p = jnp.exp2(s - m_new)
        m_ref[h] = m_new
        return alpha, p.astype(BF16)

    def accumulate(j, h, alpha, p):
        vt = vt_ref[j, h * va:(h + 1) * va, :]
        acc_ref[h] = alpha * acc_ref[h] + jnp.dot(vt, p, preferred_element_type=F32)

    def tile(j, diagonal):
        for h in range(hps):
            s_ref[h] = scores(j, h)
        for h in range(hps):
            alpha, p = softmax(h, s_ref[h], diagonal)
            accumulate(j, h, alpha, p)

    for h in range(hps):
        m_ref[h] = jnp.full((1, t), MASK_VALUE, F32)
        acc_ref[h] = jnp.zeros((va, t), F32)

    def body(j, carry):
        tile(j, False)
        return carry

    lax.fori_loop(0, qi, body, 0)

    tile(qi, True)
    for h in range(hps):
        l = acc_ref[h, MLA_V:MLA_V + 1, :]
        o = (acc_ref[h, 0:MLA_V, :] / (l + SOFTMAX_EPS)).T
        cols = slice(h * MLA_V, (h + 1) * MLA_V)
        o_ref[0, :, cols] = (o * z_ref[0, :, cols].astype(F32)).astype(BF16)


def _mla_attn(qt, k, kpe, vt, hg, tile_len, hps):
    b, s, _ = k.shape
    t = tile_len
    tiles = s // t
    kern = functools.partial(_mla_attn_kernel, tile_len=t, hps=hps)
    qw, vw, va = hps * HEAD_PAD, hps * MLA_V, MLA_V + SUM_ROWS
    return pl.pallas_call(
        kern,
        grid=(b, MLA_HEADS // hps, tiles),
        in_specs=[
            pl.BlockSpec((1, qw, t), lambda bi, h, i: (bi * tiles + i, h, 0)),
            pl.BlockSpec((1, s, hps * MLA_NOPE), lambda bi, h, i: (bi, 0, h)),
            pl.BlockSpec((1, s, LANES), lambda bi, h, i: (bi, 0, 0)),
            pl.BlockSpec((tiles, hps * va, t), lambda bi, h, i: (bi, h, 0)),
            pl.BlockSpec((1, t, vw), lambda bi, h, i: (bi, i, G_ZMLA // vw + h)),
        ],
        out_specs=pl.BlockSpec((1, t, vw), lambda bi, h, i: (bi, i, h)),
        out_shape=jax.ShapeDtypeStruct((b, s, MLA_WIDTH), BF16),
        scratch_shapes=[
            pltpu.VMEM((hps, 1, t), F32),
            pltpu.VMEM((hps, va, t), F32),
            pltpu.VMEM((hps, t, t), F32),
        ],
        compiler_params=_params("arbitrary", "arbitrary", "arbitrary"),
        name="mla_attn",
    )(qt, k, kpe, vt, hg)


def _compress_one(x_refs, pe_ref, w1_ref, w2_ref, o_ref):
    chunks = x_refs[0].shape[1] // CMP_STRIDE
    a = b = None
    for i in range(CMP_STRIDE):
        for c, x_ref in enumerate(x_refs):
            lanes = slice(c * LANES, (c + 1) * LANES)
            x = x_ref[0, pl.ds(i, chunks, stride=CMP_STRIDE), :]
            ai = jnp.dot((x + pe_ref[i:i + 1, lanes]).astype(BF16), w1_ref[i, lanes, :],
                         preferred_element_type=F32)
            bi = jnp.dot((x + pe_ref[CMP_STRIDE + i:CMP_STRIDE + i + 1, lanes]).astype(BF16),
                         w1_ref[CMP_STRIDE + i, lanes, :], preferred_element_type=F32)
            a = ai if a is None else a + ai
            b = bi if b is None else b + bi
    h1 = a + pltpu.roll(b, chunks - 1, axis=0)
    o_ref[0] = jnp.dot(_silu(h1).astype(BF16), w2_ref[...], preferred_element_type=F32).astype(BF16)


def _compress_kernel(xk0_ref, xk1_ref, xv_ref, pek_ref, pev_ref, w1k_ref, w2k_ref, w1v_ref, w2v_ref,
                     ok_ref, ov_ref):
    _compress_one((xk0_ref, xk1_ref), pek_ref, w1k_ref, w2k_ref, ok_ref)
    _compress_one((xv_ref,), pev_ref, w1v_ref, w2v_ref, ov_ref)


def _compress(hf, pek, pev, w1k, w2k, w1v, w2v):
    b, s, _ = hf.shape
    chunks = s // CMP_STRIDE
    full = lambda a: pl.BlockSpec(a.shape, lambda bi: (0,) * a.ndim)
    return pl.pallas_call(
        _compress_kernel,
        grid=(b,),
        in_specs=[pl.BlockSpec((1, s, LANES), lambda bi: (bi, 0, F_KC // LANES)),
                  pl.BlockSpec((1, s, LANES), lambda bi: (bi, 0, F_KC // LANES + 1)),
                  pl.BlockSpec((1, s, NSA_DV), lambda bi: (bi, 0, F_VC // NSA_DV)),
                  full(pek), full(pev), full(w1k), full(w2k), full(w1v), full(w2v)],
        out_specs=[
            pl.BlockSpec((1, chunks, HEAD_PAD), lambda bi: (bi, 0, 0)),
            pl.BlockSpec((1, chunks, NSA_DV), lambda bi: (bi, 0, 0)),
        ],
        out_shape=[
            jax.ShapeDtypeStruct((b, chunks, HEAD_PAD), BF16),
            jax.ShapeDtypeStruct((b, chunks, NSA_DV), BF16),
        ],
        compiler_params=_params("arbitrary"),
        name="nsa_compress",
    )(hf, hf, hf, pek, pev, w1k, w2k, w1v, w2v)


def _alibi_slope(h):
    return 2.0 ** (-8.0 * (h + 1) / NSA_HEADS)


def _nsa_attn_kernel(q_ref, g_ref, z_ref, kc_ref, vc_ref, ks_ref, vs_ref, kw_ref, vw_ref,
                     ovt_ref, oh_ref, tabs_ref, tabd_ref, tabw_ref, tabc_ref, slope_ref, o_ref,
                     kaug_ref, kwp_ref, vst_ref, vwt_ref, vct_ref, m_ref, acc_ref, s0_ref,
                     sc_ref, sw_ref, qwin_ref, qsel_ref,
                     *, tq, tk, n_s, top_n):
    qi = pl.program_id(1)
    t0 = qi * tq
    heads = NSA_HEADS
    seq = ks_ref.shape[1]
    ncp = kc_ref.shape[1]
    hw = 2 * tq
    aug = HEAD_PAD - NSA_DK
    wlen = WIN + tq
    va = NSA_DV + SUM_ROWS
    sub_tiles = tk // tq

    @pl.when(qi == 0)
    def _():
        kaug_ref[...] = ks_ref[0] + oh_ref[...]
        col = lax.broadcasted_iota(jnp.int32, (WIN, HEAD_PAD), 1)
        kwp_ref[0:WIN, :] = jnp.where(col == NSA_DK, 1.0, 0.0).astype(BF16)
        kwp_ref[WIN:, :] = kw_ref[0]
        for i in range(seq // tq):
            vst_ref[i, 0:NSA_DV, :] = vs_ref[0, i * tq:(i + 1) * tq, :].astype(F32).T.astype(BF16)
            vst_ref[i, NSA_DV:va, :] = jnp.ones((SUM_ROWS, tq), BF16)
        for i in range(WIN // LANES):
            vwt_ref[i] = jnp.zeros((va, LANES), BF16)
        for i in range(seq // LANES):
            vwt_ref[WIN // LANES + i, 0:NSA_DV, :] = (
                vw_ref[0, i * LANES:(i + 1) * LANES, :].astype(F32).T.astype(BF16))
            vwt_ref[WIN // LANES + i, NSA_DV:va, :] = jnp.ones((SUM_ROWS, LANES), BF16)
        vct_ref[...] = vc_ref[0].astype(F32).T.astype(BF16)

    q_t = q_ref[0].astype(F32).T
    flag_rows = jnp.where(lax.broadcasted_iota(jnp.int32, (aug, tq), 0) == 0,
                          MASK_VALUE, 0.0).astype(BF16)
    head_lanes = [(h // 2, slice((h % 2) * tq, (h % 2 + 1) * tq)) for h in range(heads)]
    for h, (x, lanes) in enumerate(head_lanes):
        rows = q_t[h * NSA_DK:(h + 1) * NSA_DK].astype(BF16)
        qwin_ref[x, 0:NSA_DK, lanes] = rows
        qsel_ref[x, 0:NSA_DK, lanes] = rows
        qwin_ref[x, NSA_DK:HEAD_PAD, lanes] = flag_rows

    kc = kc_ref[0]
    kwb = kwp_ref[pl.ds(pl.multiple_of(t0, LANES), wlen), :]
    for x in range(2):
        sc_ref[x] = jnp.dot(kc, qwin_ref[x], preferred_element_type=F32)
    for x in range(2):
        sw_ref[x] = (jnp.dot(kwb, qwin_ref[x], preferred_element_type=F32)
                     + tabw_ref[:, x * hw:(x + 1) * hw])

    first_blk = qi * (tq // CMP_STRIDE)
    tab_c = tabc_ref[pl.ds(pl.multiple_of(ncp - first_blk, CMP_STRIDE), ncp), :]
    o_cmp, p_sum = [], None
    for x in range(2):
        p_x = []
        for hl in range(2):
            h = 2 * x + hl
            s = sc_ref[x, :, hl * tq:(hl + 1) * tq] + tab_c[:, h * tq:(h + 1) * tq]
            m = jnp.max(s, axis=0, keepdims=True)
            e = jnp.exp2(s - m)
            inv = jnp.where(m > 0.5 * MASK_VALUE,
                            1.0 / (jnp.sum(e, axis=0, keepdims=True) + SOFTMAX_EPS), 0.0)
            p = e * inv
            p_sum = p if p_sum is None else p_sum + p
            p_x.append(p.astype(BF16))
        o_cmp.append(jnp.dot(vct_ref[...], jnp.concatenate(p_x, axis=1),
                             preferred_element_type=F32))
    p_hi = p_sum.astype(BF16)
    p_lo = (p_sum - p_hi.astype(F32)).astype(BF16)
    ovt = ovt_ref[...]
    imp = (jnp.dot(ovt, p_hi, preferred_element_type=F32)
           + jnp.dot(ovt, p_lo, preferred_element_type=F32))

    tile0 = qi * (tq // LANES)
    vwb = jnp.concatenate([vwt_ref[tile0 + r] for r in range(wlen // LANES)], axis=1)
    o_win = []
    for x in range(2):
        s = sw_ref[x]
        p = jnp.exp2(s - jnp.max(s, axis=0, keepdims=True))
        pv = jnp.dot(vwb, p.astype(BF16), preferred_element_type=F32)
        o_win.append(pv[0:NSA_DV] / (pv[NSA_DV:NSA_DV + 1] + SOFTMAX_EPS))

    blk = lax.broadcasted_iota(jnp.int32, (n_s, tq), 0)
    cur = (t0 + lax.broadcasted_iota(jnp.int32, (n_s, tq), 1)) // SLC_LEN
    forced = (blk == 0) | (blk == cur) | (blk == cur - 1)
    imp = jnp.where(forced, NEG_BIG, imp)
    imp = jnp.where(blk > cur, -NEG_BIG, imp)
    sub = lax.broadcasted_iota(jnp.int32, (8, tq), 0)
    groups = [imp[8 * g:8 * g + 8] for g in range(n_s // 8)]
    ranks = [jnp.zeros((8, tq), F32) for _ in groups]
    for jp in range(n_s):
        row = imp[jp:jp + 1, :]
        for g, grp in enumerate(groups):
            ge = jnp.where(row >= grp, 1.0, 0.0)
            gt = jnp.where(row > grp, 1.0, 0.0)
            if 8 * g > jp:
                beats = ge
            elif 8 * g + 8 <= jp:
                beats = gt
            else:
                beats = jnp.where(sub + 8 * g > jp, ge, gt)
            ranks[g] = ranks[g] + beats
    sel_rows = jnp.where(jnp.concatenate(ranks, axis=0) < top_n, 0.0, MASK_VALUE)
    if n_s < aug:
        sel_rows = jnp.concatenate([sel_rows, jnp.zeros((aug - n_s, tq), F32)], axis=0)
    for x, lanes in head_lanes:
        qsel_ref[x, NSA_DK:HEAD_PAD, lanes] = sel_rows.astype(BF16)

    jd = t0 // tk

    def scores(j, x):
        kb = j * tk if isinstance(j, int) else pl.multiple_of(j * tk, tk)
        return jnp.dot(kaug_ref[pl.ds(kb, tk), :], qsel_ref[x], preferred_element_type=F32)

    def softmax(x, s, tab, key0):
        u = s + tab
        off = slope_ref[0:1, x * hw:(x + 1) * hw] * (key0 - t0).astype(F32)
        m_old = m_ref[x]
        m_new = jnp.maximum(m_old, jnp.max(u, axis=0, keepdims=True) + off)
        alpha = jnp.exp2(m_old - m_new)
        p = jnp.exp2(u - (m_new - off))
        m_ref[x] = m_new
        return alpha, p.astype(BF16)

    def accumulate(x, alpha, p, vt):
        acc_ref[x] = alpha * acc_ref[x] + jnp.dot(vt, p, preferred_element_type=F32)

    def full_tile(j):
        vt = jnp.concatenate([vst_ref[j * sub_tiles + r] for r in range(sub_tiles)], axis=1)
        for x in range(2):
            alpha, p = softmax(x, s0_ref[x], tabs_ref[:, x * hw:(x + 1) * hw], j * tk)
            accumulate(x, alpha, p, vt)
            s0_ref[x] = scores(j + 1, x)

    def sub_tile(r, tab_ref):
        key0 = jd * tk + r * tq
        rows = pl.ds(r * tq if isinstance(r, int) else pl.multiple_of(r * tq, tq), tq)
        vt = vst_ref[jd * sub_tiles + r]
        for x in range(2):
            alpha, p = softmax(x, s0_ref[x, rows, :], tab_ref[0:tq, x * hw:(x + 1) * hw], key0)
            accumulate(x, alpha, p, vt)

    for x in range(2):
        m_ref[x] = jnp.full((1, hw), MASK_VALUE, F32)
        acc_ref[x] = jnp.zeros((va, hw), F32)
        s0_ref[x] = scores(0, x)

    def body(j, carry):
        full_tile(j)
        return carry

    lax.fori_loop(0, jd, body, 0)
    own = (t0 - jd * tk) // tq
    for r in range(sub_tiles - 1):
        @pl.when(r < own)
        def _():
            sub_tile(r, tabs_ref)
    sub_tile(own, tabd_ref)
    o_slc = [acc_ref[x, 0:NSA_DV, :] / (acc_ref[x, NSA_DV:NSA_DV + 1, :] + SOFTMAX_EPS)
             for x in range(2)]

    gates = _sigmoid(g_ref[0]).T
    z = z_ref[0]
    for h in range(heads):
        x, lanes = h // 2, slice((h % 2) * tq, (h % 2 + 1) * tq)
        c = NSA_DK + 3 * h
        o = (gates[c:c + 1] * o_cmp[x][:, lanes] + gates[c + 1:c + 2] * o_slc[x][:, lanes]
             + gates[c + 2:c + 3] * o_win[x][:, lanes])
        zh = z[:, h * NSA_DV:(h + 1) * NSA_DV]
        o_ref[0, :, h * NSA_DV:(h + 1) * NSA_DV] = (o.T * zh.astype(F32)).astype(BF16)


def _nsa_attn(hb, hf, hg, k_cmp, v_cmp, tables, tq, tk):
    ovt, onehot, tab_sel, tab_diag, tab_win, tab_cmp, slope_rows = tables
    b, s, _ = hb.shape
    ncp = k_cmp.shape[1]
    n_s = s // SLC_LEN
    assert n_s <= HEAD_PAD - NSA_DK and n_s % 8 == 0, "selection blocks must fit the spare rows"
    assert tq & (tq - 1) == 0 and tq % LANES == 0 and tk % tq == 0 and s % tk == 0
    kern = functools.partial(_nsa_attn_kernel, tq=tq, tk=tk, n_s=n_s, top_n=min(SLC_TOPN, n_s))
    qw = NSA_HEADS * NSA_DK
    const2 = lambda a: pl.BlockSpec(a.shape, lambda bi, i: (0, 0), pipeline_mode=pl.Buffered(1))
    return pl.pallas_call(
        kern,
        grid=(b, s // tq),
        in_specs=[
            pl.BlockSpec((1, tq, qw), lambda bi, i: (bi, i, B_QNSA // qw)),
            pl.BlockSpec((1, tq, HEAD_PAD), lambda bi, i: (bi, i, F_KC // HEAD_PAD)),
            pl.BlockSpec((1, tq, NSA_WIDTH), lambda bi, i: (bi, i, G_ZNSA // NSA_WIDTH)),
            pl.BlockSpec((1, ncp, HEAD_PAD), lambda bi, i: (bi, 0, 0)),
            pl.BlockSpec((1, ncp, NSA_DV), lambda bi, i: (bi, 0, 0)),
            pl.BlockSpec((1, s, HEAD_PAD), lambda bi, i: (bi, 0, B_KS // HEAD_PAD)),
            pl.BlockSpec((1, s, NSA_DV), lambda bi, i: (bi, 0, B_VS // NSA_DV)),
            pl.BlockSpec((1, s, HEAD_PAD), lambda bi, i: (bi, 0, B_KW // HEAD_PAD)),
            pl.BlockSpec((1, s, NSA_DV), lambda bi, i: (bi, 0, B_VW // NSA_DV)),
            const2(ovt), const2(onehot), const2(tab_sel), const2(tab_diag), const2(tab_win),
            const2(tab_cmp), const2(slope_rows),
        ],
        out_specs=pl.BlockSpec((1, tq, NSA_WIDTH), lambda bi, i: (bi, i, 0)),
        out_shape=jax.ShapeDtypeStruct((b, s, NSA_WIDTH), BF16),
        scratch_shapes=[
            pltpu.VMEM((s, HEAD_PAD), BF16),
            pltpu.VMEM((s + WIN, HEAD_PAD), BF16),
            pltpu.VMEM((s // tq, NSA_DV + SUM_ROWS, tq), BF16),
            pltpu.VMEM(((s + WIN) // LANES, NSA_DV + SUM_ROWS, LANES), BF16),
            pltpu.VMEM((NSA_DV, ncp), BF16),
            pltpu.VMEM((2, 1, 2 * tq), F32),
            pltpu.VMEM((2, NSA_DV + SUM_ROWS, 2 * tq), F32),
            pltpu.VMEM((2, tk, 2 * tq), F32),
            pltpu.VMEM((2, ncp, 2 * tq), F32),
            pltpu.VMEM((2, WIN + tq, 2 * tq), F32),
            pltpu.VMEM((2, HEAD_PAD, 2 * tq), BF16),
            pltpu.VMEM((2, HEAD_PAD, 2 * tq), BF16),
        ],
        compiler_params=_params("arbitrary", "arbitrary"),
        name="nsa_attn",
    )(hb, hf, hg, k_cmp, v_cmp, hb, hb, hb, hb, ovt, onehot, tab_sel, tab_diag, tab_win, tab_cmp,
      slope_rows)


def _mem_attn_kernel(q_ref, k_ref, v_ref, z_ref, o_ref):
    q = q_ref[0]
    k = k_ref[0]
    v = v_ref[0]
    z = z_ref[0]
    for h in range(MEM_HEADS):
        sl = slice(h * MEM_DH, (h + 1) * MEM_DH)
        s = lax.dot_general(q[:, sl], k[:, sl], _NT, preferred_element_type=F32)
        p = jnp.exp(s - jnp.max(s, axis=-1, keepdims=True))
        o = jnp.dot(p.astype(BF16), v[:, sl], preferred_element_type=F32)
        o = o / jnp.sum(p, axis=-1, keepdims=True)
        o_ref[0, :, sl] = (o * z[:, sl].astype(F32)).astype(BF16)


def _mem_attn(hb, mem_kv, hg, tq):
    b, s, _ = hb.shape
    mlen = mem_kv.shape[1]
    return pl.pallas_call(
        _mem_attn_kernel,
        grid=(b, s // tq),
        in_specs=[
            pl.BlockSpec((1, tq, MEM_WIDTH), lambda bi, i: (bi, i, B_QMEM // MEM_WIDTH)),
            pl.BlockSpec((1, mlen, MEM_WIDTH), lambda bi, i: (bi, 0, 0)),
            pl.BlockSpec((1, mlen, MEM_WIDTH), lambda bi, i: (bi, 0, 1)),
            pl.BlockSpec((1, tq, MEM_WIDTH), lambda bi, i: (bi, i, G_ZMEM // MEM_WIDTH)),
        ],
        out_specs=pl.BlockSpec((1, tq, MEM_WIDTH), lambda bi, i: (bi, i, 0)),
        out_shape=jax.ShapeDtypeStruct((b, s, MEM_WIDTH), BF16),
        compiler_params=_params("arbitrary", "arbitrary"),
        name="mem_attn",
    )(hb, mem_kv, mem_kv, hg)


def _out_proj_kernel(x_ref, a_ref, n_ref, m_ref, w_ref, g_ref, o_ref, *, final_norm):
    y = x_ref[...]
    y = y + jnp.dot(a_ref[...], w_ref[0:MLA_WIDTH, :], preferred_element_type=F32)
    y = y + jnp.dot(n_ref[...], w_ref[MLA_WIDTH:MLA_WIDTH + NSA_WIDTH, :], preferred_element_type=F32)
    y = y + jnp.dot(m_ref[...], w_ref[MLA_WIDTH + NSA_WIDTH:, :], preferred_element_type=F32)
    if final_norm:
        y = _rmsnorm(y, g_ref[...])
    o_ref[...] = y


def _out_proj(x, o_mla, o_nsa, o_mem, w_out, g, final_norm, tm):
    m, d = x.shape
    kern = functools.partial(_out_proj_kernel, final_norm=final_norm)
    row = lambda width: pl.BlockSpec((tm, width), lambda i: (i, 0))
    return pl.pallas_call(
        kern,
        grid=(m // tm,),
        in_specs=[row(d), row(MLA_WIDTH), row(NSA_WIDTH), row(MEM_WIDTH),
                  pl.BlockSpec(w_out.shape, lambda i: (0, 0)),
                  pl.BlockSpec((1, d), lambda i: (0, 0))],
        out_specs=row(d),
        out_shape=jax.ShapeDtypeStruct((m, d), F32),
        compiler_params=_params("arbitrary"),
        name="out_proj",
    )(x, o_mla, o_nsa, o_mem, w_out, g.reshape(1, d))


def _pad_cols(w, width):
    return jnp.pad(w, ((0, 0), (0, width - w.shape[1])))


def _w_in_moves():
    names = ("c_q", "c_kv", "k_rope", "z_mla", "q_nsa", "k_c", "v_c", "k_s", "v_s", "k_w", "v_w",
             "g_nsa", "z_nsa", "q_mem", "z_mem")
    src, off = {}, 0
    for name, n in zip(names, IN_SPLITS):
        src[name] = (off, n)
        off += n
    dst = {"c_q": (0, F_CQ), "c_kv": (0, F_CKV), "k_c": (0, F_KC), "g_nsa": (0, F_GATE),
           "v_c": (0, F_VC), "k_rope": (0, F_KROPE),
           "z_mla": (1, G_ZMLA), "z_nsa": (1, G_ZNSA), "z_mem": (1, G_ZMEM),
           "q_nsa": (2, B_QNSA), "q_mem": (2, B_QMEM), "k_s": (2, B_KS), "k_w": (2, B_KW),
           "v_s": (2, B_VS), "v_w": (2, B_VW)}
    return [(src[n][0], src[n][1], dst[n][0], dst[n][1]) for n in dst]


def _w_prep_kernel(wt_ref, wf_ref, wg_ref, wb_ref):
    outs = (wf_ref, wg_ref, wb_ref)
    for o_ref in outs:
        o_ref[...] = jnp.zeros(o_ref.shape, BF16)
    for s0, width, which, d0 in _w_in_moves():
        rows = -(-width // LANES) * LANES
        start = min(s0, wt_ref.shape[0] - rows)
        slab = wt_ref[start:start + rows, :].T
        outs[which][:, d0:d0 + width] = slab[:, s0 - start:s0 - start + width].astype(BF16)


def _layout_w_in(w_in, tr=256):
    d, n = w_in.shape
    widths = (F_WIDTH, G_WIDTH, B_WIDTH)
    wf, wg, wb = pl.pallas_call(
        _w_prep_kernel,
        grid=(d // tr,),
        in_specs=[pl.BlockSpec((n, tr), lambda i: (0, i))],
        out_specs=[pl.BlockSpec((tr, w), lambda i: (i, 0)) for w in widths],
        out_shape=[jax.ShapeDtypeStruct((d, w), BF16) for w in widths],
        compiler_params=_params("arbitrary"),
        name="w_in_layout",
    )(w_in.T)
    scale_b = jnp.ones((B_WIDTH,), F32)
    scale_b = scale_b.at[B_QNSA:B_QNSA + NSA_HEADS * NSA_DK].set(LOG2E * NSA_DK ** -0.5)
    scale_b = scale_b.at[B_QMEM:B_QMEM + MEM_WIDTH].set(MEM_DH ** -0.5)
    return wf, wg, wb, scale_b


def _rope_tables(seq):
    pos = jnp.arange(seq, dtype=F32)
    inv_freq = ROPE_THETA ** (-jnp.arange(0, MLA_ROPE, 2, dtype=F32) / MLA_ROPE)
    ang = pos[:, None] * inv_freq[None, :]
    zeros = jnp.zeros((seq, LANES - MLA_ROPE), F32)
    cos2 = jnp.concatenate([jnp.cos(ang), jnp.cos(ang), zeros], axis=1)
    sin2 = jnp.concatenate([jnp.sin(ang), jnp.sin(ang), zeros], axis=1)
    return cos2, sin2, jnp.cos(ang).T, jnp.sin(ang).T


def _nsa_tables(seq, tq, tk):
    chunks = seq // CMP_STRIDE
    n_s = seq // SLC_LEN
    c_start = jnp.arange(chunks) * CMP_STRIDE
    s_start = jnp.arange(n_s) * SLC_LEN
    overlap_t = ((c_start[None, :] < s_start[:, None] + SLC_LEN)
                 & (c_start[None, :] + CMP_LEN > s_start[:, None])
                 & (jnp.arange(chunks)[None, :] < chunks - 1))
    key_block = jnp.arange(seq) // SLC_LEN
    onehot = jnp.arange(HEAD_PAD)[None, :] == (NSA_DK + key_block)[:, None]
    slope = jnp.repeat(jnp.array([_alibi_slope(h) for h in range(NSA_HEADS)], F32) * LOG2E, tq)
    q_lane = jnp.tile(jnp.arange(tq), NSA_HEADS)
    tab_sel = jnp.arange(tk, dtype=F32)[:, None] * slope[None, :]
    rel = q_lane[None, :] - jnp.arange(WIN + tq)[:, None] + WIN
    tab_win = jnp.where((rel >= 0) & (rel < WIN), -slope[None, :] * rel.astype(F32), MASK_VALUE)
    slope_rows = jnp.broadcast_to(slope[None, :], (8, NSA_HEADS * tq))
    tab_diag = jnp.where(jnp.arange(tq)[:, None] <= q_lane[None, :], tab_sel[0:tq], MASK_VALUE)
    d_blk = jnp.arange(2 * chunks)[:, None] - chunks
    seen = d_blk * CMP_STRIDE + (CMP_LEN - 1) <= q_lane[None, :]
    dist = (q_lane[None, :] - d_blk * CMP_STRIDE).astype(F32) - (CMP_LEN - 1) / 2.0
    tab_cmp = jnp.where(seen, -slope[None, :] * dist, MASK_VALUE)
    return overlap_t.astype(BF16), onehot.astype(BF16), tab_sel, tab_diag, tab_win, tab_cmp, slope_rows


def _layer(x2, mem2, batch, seq, tables, norm_g, w_in, q_norm_g, w_uq, kv_norm_g, w_ukv,
           cmp_pe_k, cmp_pe_v, cmp_w1k, cmp_w2k, cmp_w1v, cmp_w2v, mem_norm_g, w_mem_kv, w_out,
           final_g, final_norm):
    rope, nsa_tables = tables
    d = x2.shape[1]
    wf, wg, wb, scale_b = _layout_w_in(w_in)
    hf, hg, hb = _in_proj(x2, norm_g, wf, wg, wb, scale_b, PROJ_ROWS)

    wq = jnp.pad(w_uq.reshape(MLA_Q_RANK, MLA_HEADS, MLA_NOPE + MLA_ROPE),
                 ((0, 0), (0, 0), (0, HEAD_PAD - MLA_NOPE - MLA_ROPE)))
    wqt = wq.reshape(MLA_Q_RANK, MLA_HEADS * HEAD_PAD).T.astype(BF16)
    wkv = w_ukv.reshape(MLA_KV_RANK, MLA_HEADS, MLA_NOPE + MLA_V)
    wk = wkv[:, :, :MLA_NOPE].reshape(MLA_KV_RANK, MLA_HEADS * MLA_NOPE).astype(BF16)
    wvt = wkv[:, :, MLA_NOPE:].reshape(MLA_KV_RANK, MLA_WIDTH).T.astype(BF16)
    qt, k, kpe, vt = _mla_up(hf, q_norm_g, kv_norm_g, wqt, wk, wvt, rope, seq, MLA_TILE)
    hf3 = hf.reshape(batch, seq, F_WIDTH)
    hg3 = hg.reshape(batch, seq, G_WIDTH)
    hb3 = hb.reshape(batch, seq, B_WIDTH)
    o_mla = _mla_attn(qt, k.reshape(batch, seq, -1), kpe.reshape(batch, seq, -1), vt, hg3,
                      MLA_TILE, MLA_HEADS_PER_STEP)

    pad_k = HEAD_PAD - NSA_DK
    k_cmp, v_cmp = _compress(
        hf3, _pad_cols(cmp_pe_k, HEAD_PAD), cmp_pe_v,
        jnp.pad(cmp_w1k.reshape(CMP_LEN, NSA_DK, NSA_DK), ((0, 0), (0, pad_k), (0, pad_k))).astype(BF16),
        jnp.pad(cmp_w2k, ((0, pad_k), (0, pad_k))).astype(BF16),
        cmp_w1v.reshape(CMP_LEN, NSA_DV, NSA_DV).astype(BF16), cmp_w2v.astype(BF16))
    o_nsa = _nsa_attn(hb3, hf3, hg3, k_cmp, v_cmp, nsa_tables, NSA_TQ, NSA_TK)

    mem_kv = _norm_proj(mem2, mem_norm_g, w_mem_kv.astype(BF16), jnp.ones((2 * MEM_WIDTH,), F32),
                        BF16, mem2.shape[0] // batch, MEM_WIDTH, "mem_kv_proj")
    o_mem = _mem_attn(hb3, mem_kv.reshape(batch, -1, 2 * MEM_WIDTH), hg3, PROJ_ROWS)

    return _out_proj(x2, o_mla.reshape(-1, MLA_WIDTH), o_nsa.reshape(-1, NSA_WIDTH),
                     o_mem.reshape(-1, MEM_WIDTH), w_out.astype(BF16), final_g, final_norm,
                     PROJ_ROWS)


def kernel(x, mem, norm_g, w_in, q_norm_g, w_uq, kv_norm_g, w_ukv, cmp_pe_k, cmp_pe_v,
           cmp_w1k, cmp_w2k, cmp_w1v, cmp_w2v, mem_norm_g, w_mem_kv, w_out, final_norm_g):
    batch, seq, d = x.shape
    depth = norm_g.shape[0]
    tables = (_rope_tables(seq), _nsa_tables(seq, NSA_TQ, NSA_TK))
    x2 = x.reshape(batch * seq, d)
    mem2 = mem.reshape(batch * mem.shape[1], d)
    for l in range(depth):
        x2 = _layer(x2, mem2, batch, seq, tables, norm_g[l], w_in[l], q_norm_g[l], w_uq[l],
                    kv_norm_g[l], w_ukv[l], cmp_pe_k[l], cmp_pe_v[l], cmp_w1k[l], cmp_w2k[l],
                    cmp_w1v[l], cmp_w2v[l], mem_norm_g[l], w_mem_kv[l], w_out[l],
                    final_norm_g, l == depth - 1)
    return x2.reshape(batch, seq, d)
```

```python
import functools

import jax
import jax.numpy as jnp
from jax import lax
from jax.experimental import pallas as pl
from jax.experimental.pallas import tpu as pltpu

F32 = jnp.float32
BF16 = jnp.bfloat16

EPS = 1e-6
NEG_BIG = 1e9
MASK_VALUE = -1e30
SOFTMAX_EPS = 1e-20
LOG2E = 1.4426950408889634

MLA_HEADS = 8
MLA_NOPE = 128
MLA_ROPE = 64
MLA_V = 128
MLA_Q_RANK = 512
MLA_KV_RANK = 512
ROPE_THETA = 10000.0

NSA_HEADS = 4
NSA_DK = 192
NSA_DV = 128
CMP_LEN = 32
CMP_STRIDE = 16
SLC_LEN = 64
SLC_TOPN = 16
WIN = 512

MEM_HEADS = 4
MEM_DH = 128

MLA_WIDTH = MLA_HEADS * MLA_V
NSA_WIDTH = NSA_HEADS * NSA_DV
MEM_WIDTH = MEM_HEADS * MEM_DH

IN_SPLITS = (
    MLA_Q_RANK, MLA_KV_RANK, MLA_ROPE, MLA_WIDTH,
    NSA_HEADS * NSA_DK, NSA_DK, NSA_DV, NSA_DK, NSA_DV,
    NSA_DK, NSA_DV, 3 * NSA_HEADS, NSA_WIDTH,
    MEM_WIDTH, MEM_WIDTH,
)

LANES = 128
HEAD_PAD = 256
VMEM_LIMIT = 56 * 1024 * 1024
SUM_ROWS = 16
PROJ_ROWS = 512
MLA_TILE = 512
MLA_HEADS_PER_STEP = 4
NSA_TQ = 256
NSA_TK = 512
NSA_STREAMS = 4

F_CQ, F_CKV, F_KC, F_VC, F_KROPE = 0, 512, 1024, 1280, 1408
F_GATE = F_KC + NSA_DK
F_WIDTH = 1536
G_ZMLA, G_ZNSA, G_ZMEM = 0, MLA_WIDTH, MLA_WIDTH + NSA_WIDTH
G_WIDTH = MLA_WIDTH + NSA_WIDTH + MEM_WIDTH
B_QNSA, B_KS, B_QMEM, B_KW, B_VS, B_VW = 0, 768, 1024, 1536, 1792, 1920
B_WIDTH = 2048

_NT = (((1,), (1,)), ((), ()))


def _params(*sem):
    return pltpu.CompilerParams(dimension_semantics=sem, vmem_limit_bytes=VMEM_LIMIT)


def _sigmoid(x):
    return 1.0 / (1.0 + jnp.exp(-x))


def _silu(x):
    return x * _sigmoid(x)


def _rmsnorm(x, g):
    ms = jnp.mean(x * x, axis=-1, keepdims=True)
    return (x * lax.rsqrt(ms + EPS)) * g


def _norm_proj_kernel(x_ref, g_ref, w_ref, cs_ref, o_ref, xn_ref):
    @pl.when(pl.program_id(1) == 0)
    def _():
        xn_ref[...] = _rmsnorm(x_ref[...], g_ref[...]).astype(BF16)

    acc = jnp.dot(xn_ref[...], w_ref[...], preferred_element_type=F32)
    o_ref[...] = (acc * cs_ref[...]).astype(o_ref.dtype)


def _norm_proj(x, g, w, colscale, out_dtype, tm, tn, name):
    m, k = x.shape
    n = w.shape[1]
    return pl.pallas_call(
        _norm_proj_kernel,
        grid=(m // tm, n // tn),
        in_specs=[
            pl.BlockSpec((tm, k), lambda i, j: (i, 0)),
            pl.BlockSpec((1, k), lambda i, j: (0, 0)),
            pl.BlockSpec((k, tn), lambda i, j: (0, j)),
            pl.BlockSpec((1, tn), lambda i, j: (0, j)),
        ],
        out_specs=pl.BlockSpec((tm, tn), lambda i, j: (i, j)),
        out_shape=jax.ShapeDtypeStruct((m, n), out_dtype),
        scratch_shapes=[pltpu.VMEM((tm, k), BF16)],
        compiler_params=_params("arbitrary", "arbitrary"),
        name=name,
    )(x, g.reshape(1, k), w, colscale.reshape(1, n))


def _in_proj_kernel(x_ref, g_ref, wf_ref, wg_ref, wb_ref, cs_ref, of_ref, og_ref, ob_ref, xn_ref):
    j = pl.program_id(1)

    @pl.when(j == 0)
    def _():
        xn_ref[...] = _rmsnorm(x_ref[...], g_ref[...]).astype(BF16)
        of_ref[...] = jnp.dot(xn_ref[...], wf_ref[...], preferred_element_type=F32)

    @pl.when(j == 1)
    def _():
        og_ref[...] = _silu(jnp.dot(xn_ref[...], wg_ref[...], preferred_element_type=F32)).astype(BF16)

    @pl.when(j == 2)
    def _():
        acc = jnp.dot(xn_ref[...], wb_ref[...], preferred_element_type=F32)
        ob_ref[...] = (acc * cs_ref[...]).astype(BF16)


def _in_proj(x, g, wf, wg, wb, scale_b, tm):
    m, k = x.shape
    whole = lambda a: pl.BlockSpec(a.shape, lambda i, j: (0, 0), pipeline_mode=pl.Buffered(1))
    rows = lambda a: pl.BlockSpec((tm, a.shape[1]), lambda i, j: (i, 0))
    cs = scale_b.reshape(1, -1)
    return pl.pallas_call(
        _in_proj_kernel,
        grid=(m // tm, 3),
        in_specs=[rows(x), pl.BlockSpec((1, k), lambda i, j: (0, 0)),
                  whole(wf), whole(wg), whole(wb), whole(cs)],
        out_specs=[rows(wf), rows(wg), rows(wb)],
        out_shape=[jax.ShapeDtypeStruct((m, wf.shape[1]), F32),
                   jax.ShapeDtypeStruct((m, wg.shape[1]), BF16),
                   jax.ShapeDtypeStruct((m, wb.shape[1]), BF16)],
        scratch_shapes=[pltpu.VMEM((tm, k), BF16)],
        compiler_params=_params("arbitrary", "arbitrary"),
        name="in_proj",
    )(x, g.reshape(1, k), wf, wg, wb, cs)


def _rope_half(hi, cos2, sin2):
    up = pltpu.roll(hi, 32, axis=1)
    down = pltpu.roll(hi, 96, axis=1)
    return hi * cos2 + (up - down) * sin2


def _mla_up_kernel(cq_ref, ckv_ref, kr_ref, gq_ref, gkv_ref, wqt_ref, wk_ref, wvt_ref,
                   cos_ref, sin_ref, cost_ref, sint_ref, qt_ref, k_ref, kpe_ref, vt_ref):
    scale = LOG2E * (MLA_NOPE + MLA_ROPE) ** -0.5
    half = MLA_ROPE // 2
    cos_t = cost_ref[...]
    sin_t = sint_ref[...]

    cqn = _rmsnorm(cq_ref[...], gq_ref[...]).astype(BF16)
    yt = lax.dot_general(wqt_ref[...], cqn, _NT, preferred_element_type=F32)
    for h in range(MLA_HEADS):
        r = h * HEAD_PAD
        x1 = yt[r + MLA_NOPE:r + MLA_NOPE + half]
        x2 = yt[r + MLA_NOPE + half:r + MLA_NOPE + MLA_ROPE]
        qt_ref[0, r:r + MLA_NOPE, :] = (yt[r:r + MLA_NOPE] * scale).astype(BF16)
        qt_ref[0, r + MLA_NOPE:r + MLA_NOPE + half, :] = (
            (x1 * cos_t - x2 * sin_t) * scale).astype(BF16)
        qt_ref[0, r + MLA_NOPE + half:r + MLA_NOPE + MLA_ROPE, :] = (
            (x1 * sin_t + x2 * cos_t) * scale).astype(BF16)
        qt_ref[0, r + MLA_NOPE + MLA_ROPE:r + HEAD_PAD, :] = jnp.zeros(
            (HEAD_PAD - MLA_NOPE - MLA_ROPE, yt.shape[1]), BF16)

    ckn = _rmsnorm(ckv_ref[...], gkv_ref[...]).astype(BF16)
    kn = jnp.dot(ckn, wk_ref[...], preferred_element_type=F32)
    k_pe = _rope_half(kr_ref[...], cos_ref[...], sin_ref[...]).astype(BF16)
    k_ref[...] = kn.astype(BF16)
    kpe_ref[...] = k_pe
    vt = lax.dot_general(wvt_ref[...], ckn, _NT, preferred_element_type=F32)
    ones = jnp.ones((SUM_ROWS, vt.shape[1]), BF16)
    for h in range(MLA_HEADS):
        r = h * (MLA_V + SUM_ROWS)
        vt_ref[0, r:r + MLA_V, :] = vt[h * MLA_V:(h + 1) * MLA_V].astype(BF16)
        vt_ref[0, r + MLA_V:r + MLA_V + SUM_ROWS, :] = ones


def _mla_up(hf, gq, gkv, wqt, wk, wvt, rope, seq, tm):
    cos2, sin2, cos_t, sin_t = rope
    m = hf.shape[0]
    steps_per_seq = seq // tm
    hq = MLA_HEADS * HEAD_PAD
    half = MLA_ROPE // 2
    const = lambda i: (0, 0)
    return pl.pallas_call(
        _mla_up_kernel,
        grid=(m // tm,),
        in_specs=[
            pl.BlockSpec((tm, MLA_Q_RANK), lambda i: (i, F_CQ // MLA_Q_RANK)),
            pl.BlockSpec((tm, MLA_KV_RANK), lambda i: (i, F_CKV // MLA_KV_RANK)),
            pl.BlockSpec((tm, LANES), lambda i: (i, F_KROPE // LANES)),
            pl.BlockSpec((1, MLA_Q_RANK), const),
            pl.BlockSpec((1, MLA_KV_RANK), const),
            pl.BlockSpec((hq, MLA_Q_RANK), const),
            pl.BlockSpec((MLA_KV_RANK, MLA_HEADS * MLA_NOPE), const),
            pl.BlockSpec((MLA_WIDTH, MLA_KV_RANK), const),
            pl.BlockSpec((tm, LANES), lambda i: (i % steps_per_seq, 0)),
            pl.BlockSpec((tm, LANES), lambda i: (i % steps_per_seq, 0)),
            pl.BlockSpec((half, tm), lambda i: (0, i % steps_per_seq)),
            pl.BlockSpec((half, tm), lambda i: (0, i % steps_per_seq)),
        ],
        out_specs=[
            pl.BlockSpec((1, hq, tm), lambda i: (i, 0, 0)),
            pl.BlockSpec((tm, MLA_HEADS * MLA_NOPE), lambda i: (i, 0)),
            pl.BlockSpec((tm, LANES), lambda i: (i, 0)),
            pl.BlockSpec((1, MLA_HEADS * (MLA_V + SUM_ROWS), tm), lambda i: (i, 0, 0)),
        ],
        out_shape=[
            jax.ShapeDtypeStruct((m // tm, hq, tm), BF16),
            jax.ShapeDtypeStruct((m, MLA_HEADS * MLA_NOPE), BF16),
            jax.ShapeDtypeStruct((m, LANES), BF16),
            jax.ShapeDtypeStruct((m // tm, MLA_HEADS * (MLA_V + SUM_ROWS), tm), BF16),
        ],
        compiler_params=_params("arbitrary"),
        name="mla_up",
    )(hf, hf, hf, gq.reshape(1, -1), gkv.reshape(1, -1), wqt, wk, wvt, cos2, sin2, cos_t, sin_t)


def _mla_attn_kernel(qt_ref, k_ref, kpe_ref, vt_ref, z_ref, o_ref, m_ref, acc_ref, s_ref, *,
                     tile_len, hps):
    t = tile_len
    va = MLA_V + SUM_ROWS
    qi = pl.program_id(2)

    def scores(j, h):
        kb = pl.multiple_of(j * t, t)
        k = jnp.concatenate([k_ref[0, pl.ds(kb, t), h * MLA_NOPE:(h + 1) * MLA_NOPE],
                             kpe_ref[0, pl.ds(kb, t), :]], axis=1)
        q_t = qt_ref[0, h * HEAD_PAD:(h + 1) * HEAD_PAD, :]
        return jnp.dot(k, q_t, preferred_element_type=F32)

    def softmax(h, s, diagonal):
        if diagonal:
            kpos = lax.broadcasted_iota(jnp.int32, (t, t), 0)
            qpos = lax.broadcasted_iota(jnp.int32, (t, t), 1)
            s = jnp.where(kpos <= qpos, s, MASK_VALUE)
        m = m_ref[h]
        m_new = jnp.maximum(m, jnp.max(s, axis=0, keepdims=True))
        alpha = jnp.exp2(m - m_new)
        p = jnp.exp2(s - m_new)
        m_ref[h] = m_new
        return alpha, p.astype(BF16)

    def accumulate(j, h, alpha, p):
        vt = vt_ref[j, h * va:(h + 1) * va, :]
        acc_ref[h] = alpha * acc_ref[h] + jnp.dot(vt, p, preferred_element_type=F32)

    def tile(j, diagonal):
        for h in range(hps):
            s_ref[h] = scores(j, h)
        for h in range(hps):
            alpha, p = softmax(h, s_ref[h], diagonal)
            accumulate(j, h, alpha, p)

    for h in range(hps):
        m_ref[h] = jnp.full((1, t), MASK_VALUE, F32)
        acc_ref[h] = jnp.zeros((va, t), F32)

    def body(j, carry):
        tile(j, False)
        return carry

    lax.fori_loop(0, qi, body, 0)

    tile(qi, True)
    for h in range(hps):
        l = acc_ref[h, MLA_V:MLA_V + 1, :]
        o = (acc_ref[h, 0:MLA_V, :] / (l + SOFTMAX_EPS)).T
        cols = slice(h * MLA_V, (h + 1) * MLA_V)
        o_ref[0, :, cols] = (o * z_ref[0, :, cols].astype(F32)).astype(BF16)


def _mla_attn(qt, k, kpe, vt, hg, tile_len, hps):
    b, s, _ = k.shape
    t = tile_len
    tiles = s // t
    kern = functools.partial(_mla_attn_kernel, tile_len=t, hps=hps)
    qw, vw, va = hps * HEAD_PAD, hps * MLA_V, MLA_V + SUM_ROWS
    return pl.pallas_call(
        kern,
        grid=(b, MLA_HEADS // hps, tiles),
        in_specs=[
            pl.BlockSpec((1, qw, t), lambda bi, h, i: (bi * tiles + i, h, 0)),
            pl.BlockSpec((1, s, hps * MLA_NOPE), lambda bi, h, i: (bi, 0, h)),
            pl.BlockSpec((1, s, LANES), lambda bi, h, i: (bi, 0, 0)),
            pl.BlockSpec((tiles, hps * va, t), lambda bi, h, i: (bi, h, 0)),
            pl.BlockSpec((1, t, vw), lambda bi, h, i: (bi, i, G_ZMLA // vw + h)),
        ],
        out_specs=pl.BlockSpec((1, t, vw), lambda bi, h, i: (bi, i, h)),
        out_shape=jax.ShapeDtypeStruct((b, s, MLA_WIDTH), BF16),
        scratch_shapes=[
            pltpu.VMEM((hps, 1, t), F32),
            pltpu.VMEM((hps, va, t), F32),
            pltpu.VMEM((hps, t, t), F32),
        ],
        compiler_params=_params("arbitrary", "arbitrary", "arbitrary"),
        name="mla_attn",
    )(qt, k, kpe, vt, hg)


def _compress_one(x_refs, pe_ref, w1_ref, w2_ref, o_ref):
    chunks = x_refs[0].shape[1] // CMP_STRIDE
    a = b = None
    for i in range(CMP_STRIDE):
        for c, x_ref in enumerate(x_refs):
            lanes = slice(c * LANES, (c + 1) * LANES)
            x = x_ref[0, pl.ds(i, chunks, stride=CMP_STRIDE), :]
            ai = jnp.dot((x + pe_ref[i:i + 1, lanes]).astype(BF16), w1_ref[i, lanes, :],
                         preferred_element_type=F32)
            bi = jnp.dot((x + pe_ref[CMP_STRIDE + i:CMP_STRIDE + i + 1, lanes]).astype(BF16),
                         w1_ref[CMP_STRIDE + i, lanes, :], preferred_element_type=F32)
            a = ai if a is None else a + ai
            b = bi if b is None else b + bi
    h1 = a + pltpu.roll(b, chunks - 1, axis=0)
    o_ref[0] = jnp.dot(_silu(h1).astype(BF16), w2_ref[...], preferred_element_type=F32).astype(BF16)


def _compress_kernel(xk0_ref, xk1_ref, xv_ref, pek_ref, pev_ref, w1k_ref, w2k_ref, w1v_ref, w2v_ref,
                     ok_ref, ov_ref):
    _compress_one((xk0_ref, xk1_ref), pek_ref, w1k_ref, w2k_ref, ok_ref)
    _compress_one((xv_ref,), pev_ref, w1v_ref, w2v_ref, ov_ref)


def _compress(hf, pek, pev, w1k, w2k, w1v, w2v):
    b, s, _ = hf.shape
    chunks = s // CMP_STRIDE
    full = lambda a: pl.BlockSpec(a.shape, lambda bi: (0,) * a.ndim)
    return pl.pallas_call(
        _compress_kernel,
        grid=(b,),
        in_specs=[pl.BlockSpec((1, s, LANES), lambda bi: (bi, 0, F_KC // LANES)),
                  pl.BlockSpec((1, s, LANES), lambda bi: (bi, 0, F_KC // LANES + 1)),
                  pl.BlockSpec((1, s, NSA_DV), lambda bi: (bi, 0, F_VC // NSA_DV)),
                  full(pek), full(pev), full(w1k), full(w2k), full(w1v), full(w2v)],
        out_specs=[
            pl.BlockSpec((1, chunks, HEAD_PAD), lambda bi: (bi, 0, 0)),
            pl.BlockSpec((1, chunks, NSA_DV), lambda bi: (bi, 0, 0)),
        ],
        out_shape=[
            jax.ShapeDtypeStruct((b, chunks, HEAD_PAD), BF16),
            jax.ShapeDtypeStruct((b, chunks, NSA_DV), BF16),
        ],
        compiler_params=_params("arbitrary"),
        name="nsa_compress",
    )(hf, hf, hf, pek, pev, w1k, w2k, w1v, w2v)


def _alibi_slope(h):
    return 2.0 ** (-8.0 * (h + 1) / NSA_HEADS)


def _nsa_attn_kernel(q_ref, g_ref, z_ref, kc_ref, vc_ref, ks_ref, vs_ref, kw_ref, vw_ref,
                     ovt_ref, oh_ref, tabs_ref, tabd_ref, tabw_ref, tabc_ref, slope_ref, o_ref,
                     kaug_ref, kwp_ref, vst_ref, vwt_ref, vct_ref, m_ref, acc_ref, s0_ref,
                     sc_ref, sw_ref, qwin_ref, qsel_ref,
                     *, tq, tk, n_s, top_n, n_str):
    qi = pl.program_id(1)
    t0 = qi * tq
    heads = NSA_HEADS
    seq = ks_ref.shape[1]
    ncp = kc_ref.shape[1]
    hps = heads // n_str
    hw = hps * tq
    aug = HEAD_PAD - NSA_DK
    wlen = WIN + tq
    va = NSA_DV + SUM_ROWS
    sub_tiles = tk // tq

    @pl.when(qi == 0)
    def _():
        kaug_ref[...] = ks_ref[0] + oh_ref[...]
        col = lax.broadcasted_iota(jnp.int32, (WIN, HEAD_PAD), 1)
        kwp_ref[0:WIN, :] = jnp.where(col == NSA_DK, 1.0, 0.0).astype(BF16)
        kwp_ref[WIN:, :] = kw_ref[0]
        for i in range(seq // tq):
            vst_ref[i, 0:NSA_DV, :] = vs_ref[0, i * tq:(i + 1) * tq, :].astype(F32).T.astype(BF16)
            vst_ref[i, NSA_DV:va, :] = jnp.ones((SUM_ROWS, tq), BF16)
        for i in range(WIN // LANES):
            vwt_ref[i] = jnp.zeros((va, LANES), BF16)
        for i in range(seq // LANES):
            vwt_ref[WIN // LANES + i, 0:NSA_DV, :] = (
                vw_ref[0, i * LANES:(i + 1) * LANES, :].astype(F32).T.astype(BF16))
            vwt_ref[WIN // LANES + i, NSA_DV:va, :] = jnp.ones((SUM_ROWS, LANES), BF16)
        vct_ref[...] = vc_ref[0].astype(F32).T.astype(BF16)

    q_t = q_ref[0].astype(F32).T
    flag_rows = jnp.where(lax.broadcasted_iota(jnp.int32, (aug, tq), 0) == 0,
                          MASK_VALUE, 0.0).astype(BF16)
    head_lanes = [(h // hps, slice((h % hps) * tq, (h % hps + 1) * tq)) for h in range(heads)]
    for h, (x, lanes) in enumerate(head_lanes):
        rows = q_t[h * NSA_DK:(h + 1) * NSA_DK].astype(BF16)
        qwin_ref[x, 0:NSA_DK, lanes] = rows
        qsel_ref[x, 0:NSA_DK, lanes] = rows
        qwin_ref[x, NSA_DK:HEAD_PAD, lanes] = flag_rows

    kc = kc_ref[0]
    kwb = kwp_ref[pl.ds(pl.multiple_of(t0, LANES), wlen), :]
    for x in range(n_str):
        sc_ref[x] = jnp.dot(kc, qwin_ref[x], preferred_element_type=F32)
    for x in range(n_str):
        sw_ref[x] = (jnp.dot(kwb, qwin_ref[x], preferred_element_type=F32)
                     + tabw_ref[:, x * hw:(x + 1) * hw])

    first_blk = qi * (tq // CMP_STRIDE)
    tab_c = tabc_ref[pl.ds(pl.multiple_of(ncp - first_blk, CMP_STRIDE), ncp), :]
    o_cmp, p_sum = [], None
    for x in range(n_str):
        p_x = []
        for hl in range(hps):
            h = hps * x + hl
            s = sc_ref[x, :, hl * tq:(hl + 1) * tq] + tab_c[:, h * tq:(h + 1) * tq]
            m = jnp.max(s, axis=0, keepdims=True)
            e = jnp.exp2(s - m)
            inv = jnp.where(m > 0.5 * MASK_VALUE,
                            1.0 / (jnp.sum(e, axis=0, keepdims=True) + SOFTMAX_EPS), 0.0)
            p = e * inv
            p_sum = p if p_sum is None else p_sum + p
            p_x.append(p.astype(BF16))
        o_cmp.append(jnp.dot(vct_ref[...], jnp.concatenate(p_x, axis=1),
                             preferred_element_type=F32))
    p_hi = p_sum.astype(BF16)
    p_lo = (p_sum - p_hi.astype(F32)).astype(BF16)
    ovt = ovt_ref[...]
    imp = (jnp.dot(ovt, p_hi, preferred_element_type=F32)
           + jnp.dot(ovt, p_lo, preferred_element_type=F32))

    tile0 = qi * (tq // LANES)
    vwb = jnp.concatenate([vwt_ref[tile0 + r] for r in range(wlen // LANES)], axis=1)
    o_win = []
    for x in range(n_str):
        s = sw_ref[x]
        p = jnp.exp2(s - jnp.max(s, axis=0, keepdims=True))
        pv = jnp.dot(vwb, p.astype(BF16), preferred_element_type=F32)
        o_win.append(pv[0:NSA_DV] / (pv[NSA_DV:NSA_DV + 1] + SOFTMAX_EPS))

    blk = lax.broadcasted_iota(jnp.int32, (n_s, tq), 0)
    cur = (t0 + lax.broadcasted_iota(jnp.int32, (n_s, tq), 1)) // SLC_LEN
    forced = (blk == 0) | (blk == cur) | (blk == cur - 1)
    imp = jnp.where(forced, NEG_BIG, imp)
    imp = jnp.where(blk > cur, -NEG_BIG, imp)
    sub = lax.broadcasted_iota(jnp.int32, (8, tq), 0)
    groups = [imp[8 * g:8 * g + 8] for g in range(n_s // 8)]
    ranks = [jnp.zeros((8, tq), F32) for _ in groups]
    for jp in range(n_s):
        row = imp[jp:jp + 1, :]
        for g, grp in enumerate(groups):
            ge = jnp.where(row >= grp, 1.0, 0.0)
            gt = jnp.where(row > grp, 1.0, 0.0)
            if 8 * g > jp:
                beats = ge
            elif 8 * g + 8 <= jp:
                beats = gt
            else:
                beats = jnp.where(sub + 8 * g > jp, ge, gt)
            ranks[g] = ranks[g] + beats
    sel_rows = jnp.where(jnp.concatenate(ranks, axis=0) < top_n, 0.0, MASK_VALUE)
    if n_s < aug:
        sel_rows = jnp.concatenate([sel_rows, jnp.zeros((aug - n_s, tq), F32)], axis=0)
    for x, lanes in head_lanes:
        qsel_ref[x, NSA_DK:HEAD_PAD, lanes] = sel_rows.astype(BF16)

    jd = t0 // tk

    def scores(j, x):
        kb = j * tk if isinstance(j, int) else pl.multiple_of(j * tk, tk)
        return jnp.dot(kaug_ref[pl.ds(kb, tk), :], qsel_ref[x], preferred_element_type=F32)

    def softmax(x, s, tab, key0):
        u = s + tab
        off = slope_ref[0:1, x * hw:(x + 1) * hw] * (key0 - t0).astype(F32)
        m_old = m_ref[x]
        m_new = jnp.maximum(m_old, jnp.max(u, axis=0, keepdims=True) + off)
        alpha = jnp.exp2(m_old - m_new)
        p = jnp.exp2(u - (m_new - off))
        m_ref[x] = m_new
        return alpha, p.astype(BF16)

    def accumulate(x, alpha, p, vt):
        acc_ref[x] = alpha * acc_ref[x] + jnp.dot(vt, p, preferred_element_type=F32)

    def full_tile(j):
        vt = jnp.concatenate([vst_ref[j * sub_tiles + r] for r in range(sub_tiles)], axis=1)
        for x in range(n_str):
            alpha, p = softmax(x, s0_ref[x], tabs_ref[:, x * hw:(x + 1) * hw], j * tk)
            accumulate(x, alpha, p, vt)
            s0_ref[x] = scores(j + 1, x)

    def sub_tile(r, tab_ref):
        key0 = jd * tk + r * tq
        rows = pl.ds(r * tq if isinstance(r, int) else pl.multiple_of(r * tq, tq), tq)
        vt = vst_ref[jd * sub_tiles + r]
        for x in range(n_str):
            alpha, p = softmax(x, s0_ref[x, rows, :], tab_ref[0:tq, x * hw:(x + 1) * hw], key0)
            accumulate(x, alpha, p, vt)

    for x in range(n_str):
        m_ref[x] = jnp.full((1, hw), MASK_VALUE, F32)
        acc_ref[x] = jnp.zeros((va, hw), F32)
        s0_ref[x] = scores(0, x)

    def body(j, carry):
        full_tile(j)
        return carry

    lax.fori_loop(0, jd, body, 0)
    own = (t0 - jd * tk) // tq
    for r in range(sub_tiles - 1):
        @pl.when(r < own)
        def _():
            sub_tile(r, tabs_ref)
    sub_tile(own, tabd_ref)
    o_slc = [acc_ref[x, 0:NSA_DV, :] / (acc_ref[x, NSA_DV:NSA_DV + 1, :] + SOFTMAX_EPS)
             for x in range(n_str)]

    gates = _sigmoid(g_ref[0]).T
    z = z_ref[0]
    for h in range(heads):
        x, lanes = head_lanes[h]
        c = NSA_DK + 3 * h
        o = (gates[c:c + 1] * o_cmp[x][:, lanes] + gates[c + 1:c + 2] * o_slc[x][:, lanes]
             + gates[c + 2:c + 3] * o_win[x][:, lanes])
        zh = z[:, h * NSA_DV:(h + 1) * NSA_DV]
        o_ref[0, :, h * NSA_DV:(h + 1) * NSA_DV] = (o.T * zh.astype(F32)).astype(BF16)


def _nsa_attn(hb, hf, hg, k_cmp, v_cmp, tables, tq, tk, n_str):
    ovt, onehot, tab_sel, tab_diag, tab_win, tab_cmp, slope_rows = tables
    b, s, _ = hb.shape
    ncp = k_cmp.shape[1]
    n_s = s // SLC_LEN
    hw = NSA_HEADS // n_str * tq
    assert n_s <= HEAD_PAD - NSA_DK and n_s % 8 == 0, "selection blocks must fit the spare rows"
    assert tq & (tq - 1) == 0 and tq % LANES == 0 and tk % tq == 0 and s % tk == 0
    kern = functools.partial(_nsa_attn_kernel, tq=tq, tk=tk, n_s=n_s, top_n=min(SLC_TOPN, n_s),
                             n_str=n_str)
    qw = NSA_HEADS * NSA_DK
    const2 = lambda a: pl.BlockSpec(a.shape, lambda bi, i: (0, 0), pipeline_mode=pl.Buffered(1))
    return pl.pallas_call(
        kern,
        grid=(b, s // tq),
        in_specs=[
            pl.BlockSpec((1, tq, qw), lambda bi, i: (bi, i, B_QNSA // qw)),
            pl.BlockSpec((1, tq, HEAD_PAD), lambda bi, i: (bi, i, F_KC // HEAD_PAD)),
            pl.BlockSpec((1, tq, NSA_WIDTH), lambda bi, i: (bi, i, G_ZNSA // NSA_WIDTH)),
            pl.BlockSpec((1, ncp, HEAD_PAD), lambda bi, i: (bi, 0, 0)),
            pl.BlockSpec((1, ncp, NSA_DV), lambda bi, i: (bi, 0, 0)),
            pl.BlockSpec((1, s, HEAD_PAD), lambda bi, i: (bi, 0, B_KS // HEAD_PAD)),
            pl.BlockSpec((1, s, NSA_DV), lambda bi, i: (bi, 0, B_VS // NSA_DV)),
            pl.BlockSpec((1, s, HEAD_PAD), lambda bi, i: (bi, 0, B_KW // HEAD_PAD)),
            pl.BlockSpec((1, s, NSA_DV), lambda bi, i: (bi, 0, B_VW // NSA_DV)),
            const2(ovt), const2(onehot), const2(tab_sel), const2(tab_diag), const2(tab_win),
            const2(tab_cmp), const2(slope_rows),
        ],
        out_specs=pl.BlockSpec((1, tq, NSA_WIDTH), lambda bi, i: (bi, i, 0)),
        out_shape=jax.ShapeDtypeStruct((b, s, NSA_WIDTH), BF16),
        scratch_shapes=[
            pltpu.VMEM((s, HEAD_PAD), BF16),
            pltpu.VMEM((s + WIN, HEAD_PAD), BF16),
            pltpu.VMEM((s // tq, NSA_DV + SUM_ROWS, tq), BF16),
            pltpu.VMEM(((s + WIN) // LANES, NSA_DV + SUM_ROWS, LANES), BF16),
            pltpu.VMEM((NSA_DV, ncp), BF16),
            pltpu.VMEM((n_str, 1, hw), F32),
            pltpu.VMEM((n_str, NSA_DV + SUM_ROWS, hw), F32),
            pltpu.VMEM((n_str, tk, hw), F32),
            pltpu.VMEM((n_str, ncp, hw), F32),
            pltpu.VMEM((n_str, WIN + tq, hw), F32),
            pltpu.VMEM((n_str, HEAD_PAD, hw), BF16),
            pltpu.VMEM((n_str, HEAD_PAD, hw), BF16),
        ],
        compiler_params=_params("arbitrary", "arbitrary"),
        name="nsa_attn",
    )(hb, hf, hg, k_cmp, v_cmp, hb, hb, hb, hb, ovt, onehot, tab_sel, tab_diag, tab_win, tab_cmp,
      slope_rows)


def _mem_attn_kernel(q_ref, k_ref, v_ref, z_ref, o_ref):
    q = q_ref[0]
    k = k_ref[0]
    v = v_ref[0]
    z = z_ref[0]
    for h in range(MEM_HEADS):
        sl = slice(h * MEM_DH, (h + 1) * MEM_DH)
        s = lax.dot_general(q[:, sl], k[:, sl], _NT, preferred_element_type=F32)
        p = jnp.exp(s - jnp.max(s, axis=-1, keepdims=True))
        o = jnp.dot(p.astype(BF16), v[:, sl], preferred_element_type=F32)
        o = o / jnp.sum(p, axis=-1, keepdims=True)
        o_ref[0, :, sl] = (o * z[:, sl].astype(F32)).astype(BF16)


def _mem_attn(hb, mem_kv, hg, tq):
    b, s, _ = hb.shape
    mlen = mem_kv.shape[1]
    return pl.pallas_call(
        _mem_attn_kernel,
        grid=(b, s // tq),
        in_specs=[
            pl.BlockSpec((1, tq, MEM_WIDTH), lambda bi, i: (bi, i, B_QMEM // MEM_WIDTH)),
            pl.BlockSpec((1, mlen, MEM_WIDTH), lambda bi, i: (bi, 0, 0)),
            pl.BlockSpec((1, mlen, MEM_WIDTH), lambda bi, i: (bi, 0, 1)),
            pl.BlockSpec((1, tq, MEM_WIDTH), lambda bi, i: (bi, i, G_ZMEM // MEM_WIDTH)),
        ],
        out_specs=pl.BlockSpec((1, tq, MEM_WIDTH), lambda bi, i: (bi, i, 0)),
        out_shape=jax.ShapeDtypeStruct((b, s, MEM_WIDTH), BF16),
        compiler_params=_params("arbitrary", "arbitrary"),
        name="mem_attn",
    )(hb, mem_kv, mem_kv, hg)


def _out_proj_kernel(x_ref, a_ref, n_ref, m_ref, w_ref, g_ref, o_ref, *, final_norm):
    y = x_ref[...]
    y = y + jnp.dot(a_ref[...], w_ref[0:MLA_WIDTH, :], preferred_element_type=F32)
    y = y + jnp.dot(n_ref[...], w_ref[MLA_WIDTH:MLA_WIDTH + NSA_WIDTH, :], preferred_element_type=F32)
    y = y + jnp.dot(m_ref[...], w_ref[MLA_WIDTH + NSA_WIDTH:, :], preferred_element_type=F32)
    if final_norm:
        y = _rmsnorm(y, g_ref[...])
    o_ref[...] = y


def _out_proj(x, o_mla, o_nsa, o_mem, w_out, g, final_norm, tm):
    m, d = x.shape
    kern = functools.partial(_out_proj_kernel, final_norm=final_norm)
    row = lambda width: pl.BlockSpec((tm, width), lambda i: (i, 0))
    return pl.pallas_call(
        kern,
        grid=(m // tm,),
        in_specs=[row(d), row(MLA_WIDTH), row(NSA_WIDTH), row(MEM_WIDTH),
                  pl.BlockSpec(w_out.shape, lambda i: (0, 0)),
                  pl.BlockSpec((1, d), lambda i: (0, 0))],
        out_specs=row(d),
        out_shape=jax.ShapeDtypeStruct((m, d), F32),
        compiler_params=_params("arbitrary"),
        name="out_proj",
    )(x, o_mla, o_nsa, o_mem, w_out, g.reshape(1, d))


def _pad_cols(w, width):
    return jnp.pad(w, ((0, 0), (0, width - w.shape[1])))


def _w_in_moves():
    names = ("c_q", "c_kv", "k_rope", "z_mla", "q_nsa", "k_c", "v_c", "k_s", "v_s", "k_w", "v_w",
             "g_nsa", "z_nsa", "q_mem", "z_mem")
    src, off = {}, 0
    for name, n in zip(names, IN_SPLITS):
        src[name] = (off, n)
        off += n
    dst = {"c_q": (0, F_CQ), "c_kv": (0, F_CKV), "k_c": (0, F_KC), "g_nsa": (0, F_GATE),
           "v_c": (0, F_VC), "k_rope": (0, F_KROPE),
           "z_mla": (1, G_ZMLA), "z_nsa": (1, G_ZNSA), "z_mem": (1, G_ZMEM),
           "q_nsa": (2, B_QNSA), "q_mem": (2, B_QMEM), "k_s": (2, B_KS), "k_w": (2, B_KW),
           "v_s": (2, B_VS), "v_w": (2, B_VW)}
    return [(src[n][0], src[n][1], dst[n][0], dst[n][1]) for n in dst]


def _w_prep_kernel(wt_ref, wf_ref, wg_ref, wb_ref):
    outs = (wf_ref, wg_ref, wb_ref)
    for o_ref in outs:
        o_ref[...] = jnp.zeros(o_ref.shape, BF16)
    for s0, width, which, d0 in _w_in_moves():
        rows = -(-width // LANES) * LANES
        start = min(s0, wt_ref.shape[0] - rows)
        slab = wt_ref[start:start + rows, :].T
        outs[which][:, d0:d0 + width] = slab[:, s0 - start:s0 - start + width].astype(BF16)


def _layout_w_in(w_in, tr=256):
    d, n = w_in.shape
    widths = (F_WIDTH, G_WIDTH, B_WIDTH)
    wf, wg, wb = pl.pallas_call(
        _w_prep_kernel,
        grid=(d // tr,),
        in_specs=[pl.BlockSpec((n, tr), lambda i: (0, i))],
        out_specs=[pl.BlockSpec((tr, w), lambda i: (i, 0)) for w in widths],
        out_shape=[jax.ShapeDtypeStruct((d, w), BF16) for w in widths],
        compiler_params=_params("arbitrary"),
        name="w_in_layout",
    )(w_in.T)
    scale_b = jnp.ones((B_WIDTH,), F32)
    scale_b = scale_b.at[B_QNSA:B_QNSA + NSA_HEADS * NSA_DK].set(LOG2E * NSA_DK ** -0.5)
    scale_b = scale_b.at[B_QMEM:B_QMEM + MEM_WIDTH].set(MEM_DH ** -0.5)
    return wf, wg, wb, scale_b


def _rope_tables(seq):
    pos = jnp.arange(seq, dtype=F32)
    inv_freq = ROPE_THETA ** (-jnp.arange(0, MLA_ROPE, 2, dtype=F32) / MLA_ROPE)
    ang = pos[:, None] * inv_freq[None, :]
    zeros = jnp.zeros((seq, LANES - MLA_ROPE), F32)
    cos2 = jnp.concatenate([jnp.cos(ang), jnp.cos(ang), zeros], axis=1)
    sin2 = jnp.concatenate([jnp.sin(ang), jnp.sin(ang), zeros], axis=1)
    return cos2, sin2, jnp.cos(ang).T, jnp.sin(ang).T


def _nsa_tables(seq, tq, tk):
    chunks = seq // CMP_STRIDE
    n_s = seq // SLC_LEN
    c_start = jnp.arange(chunks) * CMP_STRIDE
    s_start = jnp.arange(n_s) * SLC_LEN
    overlap_t = ((c_start[None, :] < s_start[:, None] + SLC_LEN)
                 & (c_start[None, :] + CMP_LEN > s_start[:, None])
                 & (jnp.arange(chunks)[None, :] < chunks - 1))
    key_block = jnp.arange(seq) // SLC_LEN
    onehot = jnp.arange(HEAD_PAD)[None, :] == (NSA_DK + key_block)[:, None]
    slope = jnp.repeat(jnp.array([_alibi_slope(h) for h in range(NSA_HEADS)], F32) * LOG2E, tq)
    q_lane = jnp.tile(jnp.arange(tq), NSA_HEADS)
    tab_sel = jnp.arange(tk, dtype=F32)[:, None] * slope[None, :]
    rel = q_lane[None, :] - jnp.arange(WIN + tq)[:, None] + WIN
    tab_win = jnp.where((rel >= 0) & (rel < WIN), -slope[None, :] * rel.astype(F32), MASK_VALUE)
    slope_rows = jnp.broadcast_to(slope[None, :], (8, NSA_HEADS * tq))
    tab_diag = jnp.where(jnp.arange(tq)[:, None] <= q_lane[None, :], tab_sel[0:tq], MASK_VALUE)
    d_blk = jnp.arange(2 * chunks)[:, None] - chunks
    seen = d_blk * CMP_STRIDE + (CMP_LEN - 1) <= q_lane[None, :]
    dist = (q_lane[None, :] - d_blk * CMP_STRIDE).astype(F32) - (CMP_LEN - 1) / 2.0
    tab_cmp = jnp.where(seen, -slope[None, :] * dist, MASK_VALUE)
    return overlap_t.astype(BF16), onehot.astype(BF16), tab_sel, tab_diag, tab_win, tab_cmp, slope_rows


def _layer(x2, mem2, batch, seq, tables, norm_g, w_in, q_norm_g, w_uq, kv_norm_g, w_ukv,
           cmp_pe_k, cmp_pe_v, cmp_w1k, cmp_w2k, cmp_w1v, cmp_w2v, mem_norm_g, w_mem_kv, w_out,
           final_g, final_norm):
    rope, nsa_tables = tables
    d = x2.shape[1]
    wf, wg, wb, scale_b = _layout_w_in(w_in)
    hf, hg, hb = _in_proj(x2, norm_g, wf, wg, wb, scale_b, PROJ_ROWS)

    wq = jnp.pad(w_uq.reshape(MLA_Q_RANK, MLA_HEADS, MLA_NOPE + MLA_ROPE),
                 ((0, 0), (0, 0), (0, HEAD_PAD - MLA_NOPE - MLA_ROPE)))
    wqt = wq.reshape(MLA_Q_RANK, MLA_HEADS * HEAD_PAD).T.astype(BF16)
    wkv = w_ukv.reshape(MLA_KV_RANK, MLA_HEADS, MLA_NOPE + MLA_V)
    wk = wkv[:, :, :MLA_NOPE].reshape(MLA_KV_RANK, MLA_HEADS * MLA_NOPE).astype(BF16)
    wvt = wkv[:, :, MLA_NOPE:].reshape(MLA_KV_RANK, MLA_WIDTH).T.astype(BF16)
    qt, k, kpe, vt = _mla_up(hf, q_norm_g, kv_norm_g, wqt, wk, wvt, rope, seq, MLA_TILE)
    hf3 = hf.reshape(batch, seq, F_WIDTH)
    hg3 = hg.reshape(batch, seq, G_WIDTH)
    hb3 = hb.reshape(batch, seq, B_WIDTH)
    o_mla = _mla_attn(qt, k.reshape(batch, seq, -1), kpe.reshape(batch, seq, -1), vt, hg3,
                      MLA_TILE, MLA_HEADS_PER_STEP)

    pad_k = HEAD_PAD - NSA_DK
    k_cmp, v_cmp = _compress(
        hf3, _pad_cols(cmp_pe_k, HEAD_PAD), cmp_pe_v,
        jnp.pad(cmp_w1k.reshape(CMP_LEN, NSA_DK, NSA_DK), ((0, 0), (0, pad_k), (0, pad_k))).astype(BF16),
        jnp.pad(cmp_w2k, ((0, pad_k), (0, pad_k))).astype(BF16),
        cmp_w1v.reshape(CMP_LEN, NSA_DV, NSA_DV).astype(BF16), cmp_w2v.astype(BF16))
    o_nsa = _nsa_attn(hb3, hf3, hg3, k_cmp, v_cmp, nsa_tables, NSA_TQ, NSA_TK, NSA_STREAMS)

    mem_kv = _norm_proj(mem2, mem_norm_g, w_mem_kv.astype(BF16), jnp.ones((2 * MEM_WIDTH,), F32),
                        BF16, mem2.shape[0] // batch, MEM_WIDTH, "mem_kv_proj")
    o_mem = _mem_attn(hb3, mem_kv.reshape(batch, -1, 2 * MEM_WIDTH), hg3, PROJ_ROWS)

    return _out_proj(x2, o_mla.reshape(-1, MLA_WIDTH), o_nsa.reshape(-1, NSA_WIDTH),
                     o_mem.reshape(-1, MEM_WIDTH), w_out.astype(BF16), final_g, final_norm,
                     PROJ_ROWS)


def kernel(x, mem, norm_g, w_in, q_norm_g, w_uq, kv_norm_g, w_ukv, cmp_pe_k, cmp_pe_v,
           cmp_w1k, cmp_w2k, cmp_w1v, cmp_w2v, mem_norm_g, w_mem_kv, w_out, final_norm_g):
    batch, seq, d = x.shape
    depth = norm_g.shape[0]
    tables = (_rope_tables(seq), _nsa_tables(seq, NSA_TQ, NSA_TK))
    x2 = x.reshape(batch * seq, d)
    mem2 = mem.reshape(batch * mem.shape[1], d)
    for l in range(depth):
        x2 = _layer(x2, mem2, batch, seq, tables, norm_g[l], w_in[l], q_norm_g[l], w_uq[l],
                    kv_norm_g[l], w_ukv[l], cmp_pe_k[l], cmp_pe_v[l], cmp_w1k[l], cmp_w2k[l],
                    cmp_w1v[l], cmp_w2v[l], mem_norm_g[l], w_mem_kv[l], w_out[l],
                    final_norm_g, l == depth - 1)
    return x2.reshape(batch, seq, d)
```

```python
import functools

import jax
import jax.numpy as jnp
from jax import lax
from jax.experimental import pallas as pl
from jax.experimental.pallas import tpu as pltpu

F32 = jnp.float32
BF16 = jnp.bfloat16

EPS = 1e-6
NEG_BIG = 1e9
MASK_VALUE = -1e30
SOFTMAX_EPS = 1e-20
LOG2E = 1.4426950408889634

MLA_HEADS = 8
MLA_NOPE = 128
MLA_ROPE = 64
MLA_V = 128
MLA_Q_RANK = 512
MLA_KV_RANK = 512
ROPE_THETA = 10000.0

NSA_HEADS = 4
NSA_DK = 192
NSA_DV = 128
CMP_LEN = 32
CMP_STRIDE = 16
SLC_LEN = 64
SLC_TOPN = 16
WIN = 512

MEM_HEADS = 4
MEM_DH = 128

MLA_WIDTH = MLA_HEADS * MLA_V
NSA_WIDTH = NSA_HEADS * NSA_DV
MEM_WIDTH = MEM_HEADS * MEM_DH

IN_SPLITS = (
    MLA_Q_RANK, MLA_KV_RANK, MLA_ROPE, MLA_WIDTH,
    NSA_HEADS * NSA_DK, NSA_DK, NSA_DV, NSA_DK, NSA_DV,
    NSA_DK, NSA_DV, 3 * NSA_HEADS, NSA_WIDTH,
    MEM_WIDTH, MEM_WIDTH,
)

LANES = 128
HEAD_PAD = 256
VMEM_LIMIT = 56 * 1024 * 1024
SUM_ROWS = 16
PROJ_ROWS = 512
MLA_TILE = 512
MLA_HEADS_PER_STEP = 4
NSA_TQ = 256
NSA_TK = 512
NSA_STREAMS = 4

F_CQ, F_CKV, F_KC, F_VC, F_KROPE = 0, 512, 1024, 1280, 1408
F_GATE = F_KC + NSA_DK
F_WIDTH = 1536
G_ZMLA, G_ZNSA, G_ZMEM = 0, MLA_WIDTH, MLA_WIDTH + NSA_WIDTH
G_WIDTH = MLA_WIDTH + NSA_WIDTH + MEM_WIDTH
B_QNSA, B_KS, B_QMEM, B_KW, B_VS, B_VW = 0, 768, 1024, 1536, 1792, 1920
B_WIDTH = 2048

_NT = (((1,), (1,)), ((), ()))


def _params(*sem):
    return pltpu.CompilerParams(dimension_semantics=sem, vmem_limit_bytes=VMEM_LIMIT)


def _sigmoid(x):
    return 1.0 / (1.0 + jnp.exp(-x))


def _silu(x):
    return x * _sigmoid(x)


def _rmsnorm(x, g):
    ms = jnp.mean(x * x, axis=-1, keepdims=True)
    return (x * lax.rsqrt(ms + EPS)) * g


def _norm_proj_kernel(x_ref, g_ref, w_ref, cs_ref, o_ref, xn_ref):
    @pl.when(pl.program_id(1) == 0)
    def _():
        xn_ref[...] = _rmsnorm(x_ref[...], g_ref[...]).astype(BF16)

    acc = jnp.dot(xn_ref[...], w_ref[...], preferred_element_type=F32)
    o_ref[...] = (acc * cs_ref[...]).astype(o_ref.dtype)


def _norm_proj(x, g, w, colscale, out_dtype, tm, tn, name):
    m, k = x.shape
    n = w.shape[1]
    return pl.pallas_call(
        _norm_proj_kernel,
        grid=(m // tm, n // tn),
        in_specs=[
            pl.BlockSpec((tm, k), lambda i, j: (i, 0)),
            pl.BlockSpec((1, k), lambda i, j: (0, 0)),
            pl.BlockSpec((k, tn), lambda i, j: (0, j)),
            pl.BlockSpec((1, tn), lambda i, j: (0, j)),
        ],
        out_specs=pl.BlockSpec((tm, tn), lambda i, j: (i, j)),
        out_shape=jax.ShapeDtypeStruct((m, n), out_dtype),
        scratch_shapes=[pltpu.VMEM((tm, k), BF16)],
        compiler_params=_params("arbitrary", "arbitrary"),
        name=name,
    )(x, g.reshape(1, k), w, colscale.reshape(1, n))


def _in_proj_kernel(x_ref, g_ref, wf_ref, wg_ref, wb_ref, cs_ref, of_ref, og_ref, ob_ref, xn_ref):
    j = pl.program_id(1)

    @pl.when(j == 0)
    def _():
        xn_ref[...] = _rmsnorm(x_ref[...], g_ref[...]).astype(BF16)
        of_ref[...] = jnp.dot(xn_ref[...], wf_ref[...], preferred_element_type=F32)

    @pl.when(j == 1)
    def _():
        og_ref[...] = _silu(jnp.dot(xn_ref[...], wg_ref[...], preferred_element_type=F32)).astype(BF16)

    @pl.when(j == 2)
    def _():
        acc = jnp.dot(xn_ref[...], wb_ref[...], preferred_element_type=F32)
        ob_ref[...] = (acc * cs_ref[...]).astype(BF16)


def _in_proj(x, g, wf, wg, wb, scale_b, tm):
    m, k = x.shape
    whole = lambda a: pl.BlockSpec(a.shape, lambda i, j: (0, 0), pipeline_mode=pl.Buffered(1))
    rows = lambda a: pl.BlockSpec((tm, a.shape[1]), lambda i, j: (i, 0))
    cs = scale_b.reshape(1, -1)
    return pl.pallas_call(
        _in_proj_kernel,
        grid=(m // tm, 3),
        in_specs=[rows(x), pl.BlockSpec((1, k), lambda i, j: (0, 0)),
                  whole(wf), whole(wg), whole(wb), whole(cs)],
        out_specs=[rows(wf), rows(wg), rows(wb)],
        out_shape=[jax.ShapeDtypeStruct((m, wf.shape[1]), F32),
                   jax.ShapeDtypeStruct((m, wg.shape[1]), BF16),
                   jax.ShapeDtypeStruct((m, wb.shape[1]), BF16)],
        scratch_shapes=[pltpu.VMEM((tm, k), BF16)],
        compiler_params=_params("arbitrary", "arbitrary"),
        name="in_proj",
    )(x, g.reshape(1, k), wf, wg, wb, cs)


def _rope_half(hi, cos2, sin2):
    up = pltpu.roll(hi, 32, axis=1)
    down = pltpu.roll(hi, 96, axis=1)
    return hi * cos2 + (up - down) * sin2


def _mla_up_kernel(cq_ref, ckv_ref, kr_ref, gq_ref, gkv_ref, wqt_ref, wk_ref, wvt_ref,
                   cos_ref, sin_ref, cost_ref, sint_ref, qt_ref, k_ref, kpe_ref, vt_ref):
    scale = LOG2E * (MLA_NOPE + MLA_ROPE) ** -0.5
    half = MLA_ROPE // 2
    cos_t = cost_ref[...]
    sin_t = sint_ref[...]

    cqn = _rmsnorm(cq_ref[...], gq_ref[...]).astype(BF16)
    yt = lax.dot_general(wqt_ref[...], cqn, _NT, preferred_element_type=F32)
    for h in range(MLA_HEADS):
        r = h * HEAD_PAD
        x1 = yt[r + MLA_NOPE:r + MLA_NOPE + half]
        x2 = yt[r + MLA_NOPE + half:r + MLA_NOPE + MLA_ROPE]
        qt_ref[0, r:r + MLA_NOPE, :] = (yt[r:r + MLA_NOPE] * scale).astype(BF16)
        qt_ref[0, r + MLA_NOPE:r + MLA_NOPE + half, :] = (
            (x1 * cos_t - x2 * sin_t) * scale).astype(BF16)
        qt_ref[0, r + MLA_NOPE + half:r + MLA_NOPE + MLA_ROPE, :] = (
            (x1 * sin_t + x2 * cos_t) * scale).astype(BF16)
        qt_ref[0, r + MLA_NOPE + MLA_ROPE:r + HEAD_PAD, :] = jnp.zeros(
            (HEAD_PAD - MLA_NOPE - MLA_ROPE, yt.shape[1]), BF16)

    ckn = _rmsnorm(ckv_ref[...], gkv_ref[...]).astype(BF16)
    kn = jnp.dot(ckn, wk_ref[...], preferred_element_type=F32)
    k_pe = _rope_half(kr_ref[...], cos_ref[...], sin_ref[...]).astype(BF16)
    k_ref[...] = kn.astype(BF16)
    kpe_ref[...] = k_pe
    vt = lax.dot_general(wvt_ref[...], ckn, _NT, preferred_element_type=F32)
    ones = jnp.ones((SUM_ROWS, vt.shape[1]), BF16)
    for h in range(MLA_HEADS):
        r = h * (MLA_V + SUM_ROWS)
        vt_ref[0, r:r + MLA_V, :] = vt[h * MLA_V:(h + 1) * MLA_V].astype(BF16)
        vt_ref[0, r + MLA_V:r + MLA_V + SUM_ROWS, :] = ones


def _mla_up(hf, gq, gkv, wqt, wk, wvt, rope, seq, tm):
    cos2, sin2, cos_t, sin_t = rope
    m = hf.shape[0]
    steps_per_seq = seq // tm
    hq = MLA_HEADS * HEAD_PAD
    half = MLA_ROPE // 2
    const = lambda i: (0, 0)
    return pl.pallas_call(
        _mla_up_kernel,
        grid=(m // tm,),
        in_specs=[
            pl.BlockSpec((tm, MLA_Q_RANK), lambda i: (i, F_CQ // MLA_Q_RANK)),
            pl.BlockSpec((tm, MLA_KV_RANK), lambda i: (i, F_CKV // MLA_KV_RANK)),
            pl.BlockSpec((tm, LANES), lambda i: (i, F_KROPE // LANES)),
            pl.BlockSpec((1, MLA_Q_RANK), const),
            pl.BlockSpec((1, MLA_KV_RANK), const),
            pl.BlockSpec((hq, MLA_Q_RANK), const),
            pl.BlockSpec((MLA_KV_RANK, MLA_HEADS * MLA_NOPE), const),
            pl.BlockSpec((MLA_WIDTH, MLA_KV_RANK), const),
            pl.BlockSpec((tm, LANES), lambda i: (i % steps_per_seq, 0)),
            pl.BlockSpec((tm, LANES), lambda i: (i % steps_per_seq, 0)),
            pl.BlockSpec((half, tm), lambda i: (0, i % steps_per_seq)),
            pl.BlockSpec((half, tm), lambda i: (0, i % steps_per_seq)),
        ],
        out_specs=[
            pl.BlockSpec((1, hq, tm), lambda i: (i, 0, 0)),
            pl.BlockSpec((tm, MLA_HEADS * MLA_NOPE), lambda i: (i, 0)),
            pl.BlockSpec((tm, LANES), lambda i: (i, 0)),
            pl.BlockSpec((1, MLA_HEADS * (MLA_V + SUM_ROWS), tm), lambda i: (i, 0, 0)),
        ],
        out_shape=[
            jax.ShapeDtypeStruct((m // tm, hq, tm), BF16),
            jax.ShapeDtypeStruct((m, MLA_HEADS * MLA_NOPE), BF16),
            jax.ShapeDtypeStruct((m, LANES), BF16),
            jax.ShapeDtypeStruct((m // tm, MLA_HEADS * (MLA_V + SUM_ROWS), tm), BF16),
        ],
        compiler_params=_params("arbitrary"),
        name="mla_up",
    )(hf, hf, hf, gq.reshape(1, -1), gkv.reshape(1, -1), wqt, wk, wvt, cos2, sin2, cos_t, sin_t)


def _mla_attn_kernel(qt_ref, k_ref, kpe_ref, vt_ref, z_ref, o_ref, m_ref, acc_ref, s_ref, *,
                     tile_len, hps):
    t = tile_len
    va = MLA_V + SUM_ROWS
    qi = pl.program_id(2)

    def scores(j, h):
        kb = pl.multiple_of(j * t, t)
        k = jnp.concatenate([k_ref[0, pl.ds(kb, t), h * MLA_NOPE:(h + 1) * MLA_NOPE],
                             kpe_ref[0, pl.ds(kb, t), :]], axis=1)
        q_t = qt_ref[0, h * HEAD_PAD:(h + 1) * HEAD_PAD, :]
        return jnp.dot(k, q_t, preferred_element_type=F32)

    def softmax(h, s, diagonal):
        if diagonal:
            kpos = lax.broadcasted_iota(jnp.int32, (t, t), 0)
            qpos = lax.broadcasted_iota(jnp.int32, (t, t), 1)
            s = jnp.where(kpos <= qpos, s, MASK_VALUE)
        m = m_ref[h]
        m_new = jnp.maximum(m, jnp.max(s, axis=0, keepdims=True))
        alpha = jnp.exp2(m - m_new)
        p = jnp.exp2(s - m_new)
        m_ref[h] = m_new
        return alpha, p.astype(BF16)

    def accumulate(j, h, alpha, p):
        vt = vt_ref[j, h * va:(h + 1) * va, :]
        acc_ref[h] = alpha * acc_ref[h] + jnp.dot(vt, p, preferred_element_type=F32)

    def tile(j, diagonal):
        for h in range(hps):
            s_ref[h] = scores(j, h)
        for h in range(hps):
            alpha, p = softmax(h, s_ref[h], diagonal)
            accumulate(j, h, alpha, p)

    for h in range(hps):
        m_ref[h] = jnp.full((1, t), MASK_VALUE, F32)
        acc_ref[h] = jnp.zeros((va, t), F32)

    def body(j, carry):
        tile(j, False)
        return carry

    lax.fori_loop(0, qi, body, 0)

    tile(qi, True)
    for h in range(hps):
        l = acc_ref[h, MLA_V:MLA_V + 1, :]
        o = (acc_ref[h, 0:MLA_V, :] / (l + SOFTMAX_EPS)).T
        cols = slice(h * MLA_V, (h + 1) * MLA_V)
        o_ref[0, :, cols] = (o * z_ref[0, :, cols].astype(F32)).astype(BF16)


def _mla_attn(qt, k, kpe, vt, hg, tile_len, hps):
    b, s, _ = k.shape
    t = tile_len
    tiles = s // t
    kern = functools.partial(_mla_attn_kernel, tile_len=t, hps=hps)
    qw, vw, va = hps * HEAD_PAD, hps * MLA_V, MLA_V + SUM_ROWS
    return pl.pallas_call(
        kern,
        grid=(b, MLA_HEADS // hps, tiles),
        in_specs=[
            pl.BlockSpec((1, qw, t), lambda bi, h, i: (bi * tiles + i, h, 0)),
            pl.BlockSpec((1, s, hps * MLA_NOPE), lambda bi, h, i: (bi, 0, h)),
            pl.BlockSpec((1, s, LANES), lambda bi, h, i: (bi, 0, 0)),
            pl.BlockSpec((tiles, hps * va, t), lambda bi, h, i: (bi, h, 0)),
            pl.BlockSpec((1, t, vw), lambda bi, h, i: (bi, i, G_ZMLA // vw + h)),
        ],
        out_specs=pl.BlockSpec((1, t, vw), lambda bi, h, i: (bi, i, h)),
        out_shape=jax.ShapeDtypeStruct((b, s, MLA_WIDTH), BF16),
        scratch_shapes=[
            pltpu.VMEM((hps, 1, t), F32),
            pltpu.VMEM((hps, va, t), F32),
            pltpu.VMEM((hps, t, t), F32),
        ],
        compiler_params=_params("arbitrary", "arbitrary", "arbitrary"),
        name="mla_attn",
    )(qt, k, kpe, vt, hg)


def _compress_one(x_refs, pe_ref, w1_ref, w2_ref, o_ref):
    chunks = x_refs[0].shape[1] // CMP_STRIDE
    a = b = None
    for i in range(CMP_STRIDE):
        for c, x_ref in enumerate(x_refs):
            lanes = slice(c * LANES, (c + 1) * LANES)
            x = x_ref[0, pl.ds(i, chunks, stride=CMP_STRIDE), :]
            ai = jnp.dot((x + pe_ref[i:i + 1, lanes]).astype(BF16), w1_ref[i, lanes, :],
                         preferred_element_type=F32)
            bi = jnp.dot((x + pe_ref[CMP_STRIDE + i:CMP_STRIDE + i + 1, lanes]).astype(BF16),
                         w1_ref[CMP_STRIDE + i, lanes, :], preferred_element_type=F32)
            a = ai if a is None else a + ai
            b = bi if b is None else b + bi
    h1 = a + pltpu.roll(b, chunks - 1, axis=0)
    o_ref[0] = jnp.dot(_silu(h1).astype(BF16), w2_ref[...], preferred_element_type=F32).astype(BF16)


def _compress_kernel(xk0_ref, xk1_ref, xv_ref, pek_ref, pev_ref, w1k_ref, w2k_ref, w1v_ref, w2v_ref,
                     ok_ref, ov_ref):
    _compress_one((xk0_ref, xk1_ref), pek_ref, w1k_ref, w2k_ref, ok_ref)
    _compress_one((xv_ref,), pev_ref, w1v_ref, w2v_ref, ov_ref)


def _compress(hf, pek, pev, w1k, w2k, w1v, w2v):
    b, s, _ = hf.shape
    chunks = s // CMP_STRIDE
    full = lambda a: pl.BlockSpec(a.shape, lambda bi: (0,) * a.ndim)
    return pl.pallas_call(
        _compress_kernel,
        grid=(b,),
        in_specs=[pl.BlockSpec((1, s, LANES), lambda bi: (bi, 0, F_KC // LANES)),
                  pl.BlockSpec((1, s, LANES), lambda bi: (bi, 0, F_KC // LANES + 1)),
                  pl.BlockSpec((1, s, NSA_DV), lambda bi: (bi, 0, F_VC // NSA_DV)),
                  full(pek), full(pev), full(w1k), full(w2k), full(w1v), full(w2v)],
        out_specs=[
            pl.BlockSpec((1, chunks, HEAD_PAD), lambda bi: (bi, 0, 0)),
            pl.BlockSpec((1, chunks, NSA_DV), lambda bi: (bi, 0, 0)),
        ],
        out_shape=[
            jax.ShapeDtypeStruct((b, chunks, HEAD_PAD), BF16),
            jax.ShapeDtypeStruct((b, chunks, NSA_DV), BF16),
        ],
        compiler_params=_params("arbitrary"),
        name="nsa_compress",
    )(hf, hf, hf, pek, pev, w1k, w2k, w1v, w2v)


def _alibi_slope(h):
    return 2.0 ** (-8.0 * (h + 1) / NSA_HEADS)


def _nsa_attn_kernel(q_ref, g_ref, z_ref, kc_ref, vc_ref, ks_ref, vs_ref, kw_ref, vw_ref,
                     ovt_ref, oh_ref, tabs_ref, tabd_ref, tabw_ref, tabc_ref, slope_ref, o_ref,
                     kaug_ref, kwp_ref, vst_ref, vwt_ref, vct_ref, m_ref, acc_ref, s0_ref,
                     sc_ref, sw_ref, qwin_ref, qsel_ref,
                     *, tq, tk, n_s, top_n, n_str):
    qi = pl.program_id(1)
    t0 = qi * tq
    heads = NSA_HEADS
    seq = ks_ref.shape[1]
    ncp = kc_ref.shape[1]
    hps = heads // n_str
    hw = hps * tq
    aug = HEAD_PAD - NSA_DK
    wlen = WIN + tq
    va = NSA_DV + SUM_ROWS
    sub_tiles = tk // tq

    @pl.when(qi == 0)
    def _():
        kaug_ref[...] = ks_ref[0] + oh_ref[...]
        col = lax.broadcasted_iota(jnp.int32, (WIN, HEAD_PAD), 1)
        kwp_ref[0:WIN, :] = jnp.where(col == NSA_DK, 1.0, 0.0).astype(BF16)
        kwp_ref[WIN:, :] = kw_ref[0]
        for i in range(seq // tq):
            vst_ref[i, 0:NSA_DV, :] = vs_ref[0, i * tq:(i + 1) * tq, :].astype(F32).T.astype(BF16)
            vst_ref[i, NSA_DV:va, :] = jnp.ones((SUM_ROWS, tq), BF16)
        for i in range(WIN // LANES):
            vwt_ref[i] = jnp.zeros((va, LANES), BF16)
        for i in range(seq // LANES):
            vwt_ref[WIN // LANES + i, 0:NSA_DV, :] = (
                vw_ref[0, i * LANES:(i + 1) * LANES, :].astype(F32).T.astype(BF16))
            vwt_ref[WIN // LANES + i, NSA_DV:va, :] = jnp.ones((SUM_ROWS, LANES), BF16)
        vct_ref[...] = vc_ref[0].astype(F32).T.astype(BF16)

    q_t = q_ref[0].astype(F32).T
    flag_rows = jnp.where(lax.broadcasted_iota(jnp.int32, (aug, tq), 0) == 0,
                          MASK_VALUE, 0.0).astype(BF16)
    head_lanes = [(h // hps, slice((h % hps) * tq, (h % hps + 1) * tq)) for h in range(heads)]
    for h, (x, lanes) in enumerate(head_lanes):
        rows = q_t[h * NSA_DK:(h + 1) * NSA_DK].astype(BF16)
        qwin_ref[x, 0:NSA_DK, lanes] = rows
        qsel_ref[x, 0:NSA_DK, lanes] = rows
        qwin_ref[x, NSA_DK:HEAD_PAD, lanes] = flag_rows

    kc = kc_ref[0]
    kwb = kwp_ref[pl.ds(pl.multiple_of(t0, LANES), wlen), :]
    for x in range(n_str):
        sc_ref[x] = jnp.dot(kc, qwin_ref[x], preferred_element_type=F32)
    for x in range(n_str):
        sw_ref[x] = (jnp.dot(kwb, qwin_ref[x], preferred_element_type=F32)
                     + tabw_ref[:, x * hw:(x + 1) * hw])

    first_blk = qi * (tq // CMP_STRIDE)
    tab_c = tabc_ref[pl.ds(pl.multiple_of(ncp - first_blk, CMP_STRIDE), ncp), :]
    o_cmp, p_sum = [], None
    for x in range(n_str):
        p_x = []
        for hl in range(hps):
            h = hps * x + hl
            s = sc_ref[x, :, hl * tq:(hl + 1) * tq] + tab_c[:, h * tq:(h + 1) * tq]
            m = jnp.max(s, axis=0, keepdims=True)
            e = jnp.exp2(s - m)
            inv = jnp.where(m > 0.5 * MASK_VALUE,
                            1.0 / (jnp.sum(e, axis=0, keepdims=True) + SOFTMAX_EPS), 0.0)
            p = e * inv
            p_sum = p if p_sum is None else p_sum + p
            p_x.append(p.astype(BF16))
        o_cmp.append(jnp.dot(vct_ref[...], jnp.concatenate(p_x, axis=1),
                             preferred_element_type=F32))
    p_hi = p_sum.astype(BF16)
    p_lo = (p_sum - p_hi.astype(F32)).astype(BF16)
    ovt = ovt_ref[...]
    imp = (jnp.dot(ovt, p_hi, preferred_element_type=F32)
           + jnp.dot(ovt, p_lo, preferred_element_type=F32))

    tile0 = qi * (tq // LANES)
    vwb = jnp.concatenate([vwt_ref[tile0 + r] for r in range(wlen // LANES)], axis=1)
    o_win = []
    for x in range(n_str):
        s = sw_ref[x]
        p = jnp.exp2(s - jnp.max(s, axis=0, keepdims=True))
        pv = jnp.dot(vwb, p.astype(BF16), preferred_element_type=F32)
        o_win.append(pv[0:NSA_DV] / (pv[NSA_DV:NSA_DV + 1] + SOFTMAX_EPS))

    blk = lax.broadcasted_iota(jnp.int32, (n_s, tq), 0)
    cur = (t0 + lax.broadcasted_iota(jnp.int32, (n_s, tq), 1)) // SLC_LEN
    forced = (blk == 0) | (blk == cur) | (blk == cur - 1)
    imp = jnp.where(forced, NEG_BIG, imp)
    imp = jnp.where(blk > cur, -NEG_BIG, imp)
    sub = lax.broadcasted_iota(jnp.int32, (8, tq), 0)
    groups = [imp[8 * g:8 * g + 8] for g in range(n_s // 8)]
    ranks = [jnp.zeros((8, tq), F32) for _ in groups]
    for jp in range(n_s):
        row = imp[jp:jp + 1, :]
        for g, grp in enumerate(groups):
            ge = jnp.where(row >= grp, 1.0, 0.0)
            gt = jnp.where(row > grp, 1.0, 0.0)
            if 8 * g > jp:
                beats = ge
            elif 8 * g + 8 <= jp:
                beats = gt
            else:
                beats = jnp.where(sub + 8 * g > jp, ge, gt)
            ranks[g] = ranks[g] + beats
    sel_rows = jnp.where(jnp.concatenate(ranks, axis=0) < top_n, 0.0, MASK_VALUE)
    if n_s < aug:
        sel_rows = jnp.concatenate([sel_rows, jnp.zeros((aug - n_s, tq), F32)], axis=0)
    for x, lanes in head_lanes:
        qsel_ref[x, NSA_DK:HEAD_PAD, lanes] = sel_rows.astype(BF16)

    jd = t0 // tk

    def scores(j, x):
        kb = j * tk if isinstance(j, int) else pl.multiple_of(j * tk, tk)
        return jnp.dot(kaug_ref[pl.ds(kb, tk), :], qsel_ref[x], preferred_element_type=F32)

    def softmax(x, s, tab, key0):
        u = s + tab
        off = slope_ref[0:1, x * hw:(x + 1) * hw] * (key0 - t0).astype(F32)
        m_old = m_ref[x]
        m_new = jnp.maximum(m_old, jnp.max(u, axis=0, keepdims=True) + off)
        alpha = jnp.exp2(m_old - m_new)
        p = jnp.exp2(u - (m_new - off))
        m_ref[x] = m_new
        return alpha, p.astype(BF16)

    def accumulate(x, alpha, p, vt):
        acc_ref[x] = alpha * acc_ref[x] + jnp.dot(vt, p, preferred_element_type=F32)

    ahead = n_str // 2

    def full_tile(j):
        vt = jnp.concatenate([vst_ref[j * sub_tiles + r] for r in range(sub_tiles)], axis=1)
        for x in range(ahead, n_str):
            s0_ref[x] = scores(j, x)
        for x in range(n_str):
            alpha, p = softmax(x, s0_ref[x], tabs_ref[:, x * hw:(x + 1) * hw], j * tk)
            accumulate(x, alpha, p, vt)
            if x < ahead:
                s0_ref[x] = scores(j + 1, x)

    def sub_tile(r, tab_ref):
        key0 = jd * tk + r * tq
        rows = pl.ds(r * tq if isinstance(r, int) else pl.multiple_of(r * tq, tq), tq)
        vt = vst_ref[jd * sub_tiles + r]
        for x in range(n_str):
            alpha, p = softmax(x, s0_ref[x, rows, :], tab_ref[0:tq, x * hw:(x + 1) * hw], key0)
            accumulate(x, alpha, p, vt)

    for x in range(n_str):
        m_ref[x] = jnp.full((1, hw), MASK_VALUE, F32)
        acc_ref[x] = jnp.zeros((va, hw), F32)
    for x in range(ahead):
        s0_ref[x] = scores(0, x)

    def body(j, carry):
        full_tile(j)
        return carry

    lax.fori_loop(0, jd, body, 0)
    for x in range(ahead, n_str):
        s0_ref[x] = scores(jd, x)
    own = (t0 - jd * tk) // tq
    for r in range(sub_tiles - 1):
        @pl.when(r < own)
        def _():
            sub_tile(r, tabs_ref)
    sub_tile(own, tabd_ref)
    o_slc = [acc_ref[x, 0:NSA_DV, :] / (acc_ref[x, NSA_DV:NSA_DV + 1, :] + SOFTMAX_EPS)
             for x in range(n_str)]

    gates = _sigmoid(g_ref[0]).T
    z = z_ref[0]
    for h in range(heads):
        x, lanes = head_lanes[h]
        c = NSA_DK + 3 * h
        o = (gates[c:c + 1] * o_cmp[x][:, lanes] + gates[c + 1:c + 2] * o_slc[x][:, lanes]
             + gates[c + 2:c + 3] * o_win[x][:, lanes])
        zh = z[:, h * NSA_DV:(h + 1) * NSA_DV]
        o_ref[0, :, h * NSA_DV:(h + 1) * NSA_DV] = (o.T * zh.astype(F32)).astype(BF16)


def _nsa_attn(hb, hf, hg, k_cmp, v_cmp, tables, tq, tk, n_str):
    ovt, onehot, tab_sel, tab_diag, tab_win, tab_cmp, slope_rows = tables
    b, s, _ = hb.shape
    ncp = k_cmp.shape[1]
    n_s = s // SLC_LEN
    hw = NSA_HEADS // n_str * tq
    assert n_s <= HEAD_PAD - NSA_DK and n_s % 8 == 0, "selection blocks must fit the spare rows"
    assert tq & (tq - 1) == 0 and tq % LANES == 0 and tk % tq == 0 and s % tk == 0
    kern = functools.partial(_nsa_attn_kernel, tq=tq, tk=tk, n_s=n_s, top_n=min(SLC_TOPN, n_s),
                             n_str=n_str)
    qw = NSA_HEADS * NSA_DK
    const2 = lambda a: pl.BlockSpec(a.shape, lambda bi, i: (0, 0), pipeline_mode=pl.Buffered(1))
    return pl.pallas_call(
        kern,
        grid=(b, s // tq),
        in_specs=[
            pl.BlockSpec((1, tq, qw), lambda bi, i: (bi, i, B_QNSA // qw)),
            pl.BlockSpec((1, tq, HEAD_PAD), lambda bi, i: (bi, i, F_KC // HEAD_PAD)),
            pl.BlockSpec((1, tq, NSA_WIDTH), lambda bi, i: (bi, i, G_ZNSA // NSA_WIDTH)),
            pl.BlockSpec((1, ncp, HEAD_PAD), lambda bi, i: (bi, 0, 0)),
            pl.BlockSpec((1, ncp, NSA_DV), lambda bi, i: (bi, 0, 0)),
            pl.BlockSpec((1, s, HEAD_PAD), lambda bi, i: (bi, 0, B_KS // HEAD_PAD)),
            pl.BlockSpec((1, s, NSA_DV), lambda bi, i: (bi, 0, B_VS // NSA_DV)),
            pl.BlockSpec((1, s, HEAD_PAD), lambda bi, i: (bi, 0, B_KW // HEAD_PAD)),
            pl.BlockSpec((1, s, NSA_DV), lambda bi, i: (bi, 0, B_VW // NSA_DV)),
            const2(ovt), const2(onehot), const2(tab_sel), const2(tab_diag), const2(tab_win),
            const2(tab_cmp), const2(slope_rows),
        ],
        out_specs=pl.BlockSpec((1, tq, NSA_WIDTH), lambda bi, i: (bi, i, 0)),
        out_shape=jax.ShapeDtypeStruct((b, s, NSA_WIDTH), BF16),
        scratch_shapes=[
            pltpu.VMEM((s, HEAD_PAD), BF16),
            pltpu.VMEM((s + WIN, HEAD_PAD), BF16),
            pltpu.VMEM((s // tq, NSA_DV + SUM_ROWS, tq), BF16),
            pltpu.VMEM(((s + WIN) // LANES, NSA_DV + SUM_ROWS, LANES), BF16),
            pltpu.VMEM((NSA_DV, ncp), BF16),
            pltpu.VMEM((n_str, 1, hw), F32),
            pltpu.VMEM((n_str, NSA_DV + SUM_ROWS, hw), F32),
            pltpu.VMEM((n_str, tk, hw), F32),
            pltpu.VMEM((n_str, ncp, hw), F32),
            pltpu.VMEM((n_str, WIN + tq, hw), F32),
            pltpu.VMEM((n_str, HEAD_PAD, hw), BF16),
            pltpu.VMEM((n_str, HEAD_PAD, hw), BF16),
        ],
        compiler_params=_params("arbitrary", "arbitrary"),
        name="nsa_attn",
    )(hb, hf, hg, k_cmp, v_cmp, hb, hb, hb, hb, ovt, onehot, tab_sel, tab_diag, tab_win, tab_cmp,
      slope_rows)


def _mem_attn_kernel(q_ref, k_ref, v_ref, z_ref, o_ref):
    q = q_ref[0]
    k = k_ref[0]
    v = v_ref[0]
    z = z_ref[0]
    for h in range(MEM_HEADS):
        sl = slice(h * MEM_DH, (h + 1) * MEM_DH)
        s = lax.dot_general(q[:, sl], k[:, sl], _NT, preferred_element_type=F32)
        p = jnp.exp(s - jnp.max(s, axis=-1, keepdims=True))
        o = jnp.dot(p.astype(BF16), v[:, sl], preferred_element_type=F32)
        o = o / jnp.sum(p, axis=-1, keepdims=True)
        o_ref[0, :, sl] = (o * z[:, sl].astype(F32)).astype(BF16)


def _mem_attn(hb, mem_kv, hg, tq):
    b, s, _ = hb.shape
    mlen = mem_kv.shape[1]
    return pl.pallas_call(
        _mem_attn_kernel,
        grid=(b, s // tq),
        in_specs=[
            pl.BlockSpec((1, tq, MEM_WIDTH), lambda bi, i: (bi, i, B_QMEM // MEM_WIDTH)),
            pl.BlockSpec((1, mlen, MEM_WIDTH), lambda bi, i: (bi, 0, 0)),
            pl.BlockSpec((1, mlen, MEM_WIDTH), lambda bi, i: (bi, 0, 1)),
            pl.BlockSpec((1, tq, MEM_WIDTH), lambda bi, i: (bi, i, G_ZMEM // MEM_WIDTH)),
        ],
        out_specs=pl.BlockSpec((1, tq, MEM_WIDTH), lambda bi, i: (bi, i, 0)),
        out_shape=jax.ShapeDtypeStruct((b, s, MEM_WIDTH), BF16),
        compiler_params=_params("arbitrary", "arbitrary"),
        name="mem_attn",
    )(hb, mem_kv, mem_kv, hg)


def _out_proj_kernel(x_ref, a_ref, n_ref, m_ref, w_ref, g_ref, o_ref, *, final_norm):
    y = x_ref[...]
    y = y + jnp.dot(a_ref[...], w_ref[0:MLA_WIDTH, :], preferred_element_type=F32)
    y = y + jnp.dot(n_ref[...], w_ref[MLA_WIDTH:MLA_WIDTH + NSA_WIDTH, :], preferred_element_type=F32)
    y = y + jnp.dot(m_ref[...], w_ref[MLA_WIDTH + NSA_WIDTH:, :], preferred_element_type=F32)
    if final_norm:
        y = _rmsnorm(y, g_ref[...])
    o_ref[...] = y


def _out_proj(x, o_mla, o_nsa, o_mem, w_out, g, final_norm, tm):
    m, d = x.shape
    kern = functools.partial(_out_proj_kernel, final_norm=final_norm)
    row = lambda width: pl.BlockSpec((tm, width), lambda i: (i, 0))
    return pl.pallas_call(
        kern,
        grid=(m // tm,),
        in_specs=[row(d), row(MLA_WIDTH), row(NSA_WIDTH), row(MEM_WIDTH),
                  pl.BlockSpec(w_out.shape, lambda i: (0, 0)),
                  pl.BlockSpec((1, d), lambda i: (0, 0))],
        out_specs=row(d),
        out_shape=jax.ShapeDtypeStruct((m, d), F32),
        compiler_params=_params("arbitrary"),
        name="out_proj",
    )(x, o_mla, o_nsa, o_mem, w_out, g.reshape(1, d))


def _pad_cols(w, width):
    return jnp.pad(w, ((0, 0), (0, width - w.shape[1])))


def _w_in_moves():
    names = ("c_q", "c_kv", "k_rope", "z_mla", "q_nsa", "k_c", "v_c", "k_s", "v_s", "k_w", "v_w",
             "g_nsa", "z_nsa", "q_mem", "z_mem")
    src, off = {}, 0
    for name, n in zip(names, IN_SPLITS):
        src[name] = (off, n)
        off += n
    dst = {"c_q": (0, F_CQ), "c_kv": (0, F_CKV), "k_c": (0, F_KC), "g_nsa": (0, F_GATE),
           "v_c": (0, F_VC), "k_rope": (0, F_KROPE),
           "z_mla": (1, G_ZMLA), "z_nsa": (1, G_ZNSA), "z_mem": (1, G_ZMEM),
           "q_nsa": (2, B_QNSA), "q_mem": (2, B_QMEM), "k_s": (2, B_KS), "k_w": (2, B_KW),
           "v_s": (2, B_VS), "v_w": (2, B_VW)}
    return [(src[n][0], src[n][1], dst[n][0], dst[n][1]) for n in dst]


def _w_prep_kernel(wt_ref, wf_ref, wg_ref, wb_ref):
    outs = (wf_ref, wg_ref, wb_ref)
    for o_ref in outs:
        o_ref[...] = jnp.zeros(o_ref.shape, BF16)
    for s0, width, which, d0 in _w_in_moves():
        rows = -(-width // LANES) * LANES
        start = min(s0, wt_ref.shape[0] - rows)
        slab = wt_ref[start:start + rows, :].T
        outs[which][:, d0:d0 + width] = slab[:, s0 - start:s0 - start + width].astype(BF16)


def _layout_w_in(w_in, tr=256):
    d, n = w_in.shape
    widths = (F_WIDTH, G_WIDTH, B_WIDTH)
    wf, wg, wb = pl.pallas_call(
        _w_prep_kernel,
        grid=(d // tr,),
        in_specs=[pl.BlockSpec((n, tr), lambda i: (0, i))],
        out_specs=[pl.BlockSpec((tr, w), lambda i: (i, 0)) for w in widths],
        out_shape=[jax.ShapeDtypeStruct((d, w), BF16) for w in widths],
        compiler_params=_params("arbitrary"),
        name="w_in_layout",
    )(w_in.T)
    scale_b = jnp.ones((B_WIDTH,), F32)
    scale_b = scale_b.at[B_QNSA:B_QNSA + NSA_HEADS * NSA_DK].set(LOG2E * NSA_DK ** -0.5)
    scale_b = scale_b.at[B_QMEM:B_QMEM + MEM_WIDTH].set(MEM_DH ** -0.5)
    return wf, wg, wb, scale_b


def _rope_tables(seq):
    pos = jnp.arange(seq, dtype=F32)
    inv_freq = ROPE_THETA ** (-jnp.arange(0, MLA_ROPE, 2, dtype=F32) / MLA_ROPE)
    ang = pos[:, None] * inv_freq[None, :]
    zeros = jnp.zeros((seq, LANES - MLA_ROPE), F32)
    cos2 = jnp.concatenate([jnp.cos(ang), jnp.cos(ang), zeros], axis=1)
    sin2 = jnp.concatenate([jnp.sin(ang), jnp.sin(ang), zeros], axis=1)
    return cos2, sin2, jnp.cos(ang).T, jnp.sin(ang).T


def _nsa_tables(seq, tq, tk):
    chunks = seq // CMP_STRIDE
    n_s = seq // SLC_LEN
    c_start = jnp.arange(chunks) * CMP_STRIDE
    s_start = jnp.arange(n_s) * SLC_LEN
    overlap_t = ((c_start[None, :] < s_start[:, None] + SLC_LEN)
                 & (c_start[None, :] + CMP_LEN > s_start[:, None])
                 & (jnp.arange(chunks)[None, :] < chunks - 1))
    key_block = jnp.arange(seq) // SLC_LEN
    onehot = jnp.arange(HEAD_PAD)[None, :] == (NSA_DK + key_block)[:, None]
    slope = jnp.repeat(jnp.array([_alibi_slope(h) for h in range(NSA_HEADS)], F32) * LOG2E, tq)
    q_lane = jnp.tile(jnp.arange(tq), NSA_HEADS)
    tab_sel = jnp.arange(tk, dtype=F32)[:, None] * slope[None, :]
    rel = q_lane[None, :] - jnp.arange(WIN + tq)[:, None] + WIN
    tab_win = jnp.where((rel >= 0) & (rel < WIN), -slope[None, :] * rel.astype(F32), MASK_VALUE)
    slope_rows = jnp.broadcast_to(slope[None, :], (8, NSA_HEADS * tq))
    tab_diag = jnp.where(jnp.arange(tq)[:, None] <= q_lane[None, :], tab_sel[0:tq], MASK_VALUE)
    d_blk = jnp.arange(2 * chunks)[:, None] - chunks
    seen = d_blk * CMP_STRIDE + (CMP_LEN - 1) <= q_lane[None, :]
    dist = (q_lane[None, :] - d_blk * CMP_STRIDE).astype(F32) - (CMP_LEN - 1) / 2.0
    tab_cmp = jnp.where(seen, -slope[None, :] * dist, MASK_VALUE)
    return overlap_t.astype(BF16), onehot.astype(BF16), tab_sel, tab_diag, tab_win, tab_cmp, slope_rows


def _layer(x2, mem2, batch, seq, tables, norm_g, w_in, q_norm_g, w_uq, kv_norm_g, w_ukv,
           cmp_pe_k, cmp_pe_v, cmp_w1k, cmp_w2k, cmp_w1v, cmp_w2v, mem_norm_g, w_mem_kv, w_out,
           final_g, final_norm):
    rope, nsa_tables = tables
    d = x2.shape[1]
    wf, wg, wb, scale_b = _layout_w_in(w_in)
    hf, hg, hb = _in_proj(x2, norm_g, wf, wg, wb, scale_b, PROJ_ROWS)

    wq = jnp.pad(w_uq.reshape(MLA_Q_RANK, MLA_HEADS, MLA_NOPE + MLA_ROPE),
                 ((0, 0), (0, 0), (0, HEAD_PAD - MLA_NOPE - MLA_ROPE)))
    wqt = wq.reshape(MLA_Q_RANK, MLA_HEADS * HEAD_PAD).T.astype(BF16)
    wkv = w_ukv.reshape(MLA_KV_RANK, MLA_HEADS, MLA_NOPE + MLA_V)
    wk = wkv[:, :, :MLA_NOPE].reshape(MLA_KV_RANK, MLA_HEADS * MLA_NOPE).astype(BF16)
    wvt = wkv[:, :, MLA_NOPE:].reshape(MLA_KV_RANK, MLA_WIDTH).T.astype(BF16)
    qt, k, kpe, vt = _mla_up(hf, q_norm_g, kv_norm_g, wqt, wk, wvt, rope, seq, MLA_TILE)
    hf3 = hf.reshape(batch, seq, F_WIDTH)
    hg3 = hg.reshape(batch, seq, G_WIDTH)
    hb3 = hb.reshape(batch, seq, B_WIDTH)
    o_mla = _mla_attn(qt, k.reshape(batch, seq, -1), kpe.reshape(batch, seq, -1), vt, hg3,
                      MLA_TILE, MLA_HEADS_PER_STEP)

    pad_k = HEAD_PAD - NSA_DK
    k_cmp, v_cmp = _compress(
        hf3, _pad_cols(cmp_pe_k, HEAD_PAD), cmp_pe_v,
        jnp.pad(cmp_w1k.reshape(CMP_LEN, NSA_DK, NSA_DK), ((0, 0), (0, pad_k), (0, pad_k))).astype(BF16),
        jnp.pad(cmp_w2k, ((0, pad_k), (0, pad_k))).astype(BF16),
        cmp_w1v.reshape(CMP_LEN, NSA_DV, NSA_DV).astype(BF16), cmp_w2v.astype(BF16))
    o_nsa = _nsa_attn(hb3, hf3, hg3, k_cmp, v_cmp, nsa_tables, NSA_TQ, NSA_TK, NSA_STREAMS)

    mem_kv = _norm_proj(mem2, mem_norm_g, w_mem_kv.astype(BF16), jnp.ones((2 * MEM_WIDTH,), F32),
                        BF16, mem2.shape[0] // batch, MEM_WIDTH, "mem_kv_proj")
    o_mem = _mem_attn(hb3, mem_kv.reshape(batch, -1, 2 * MEM_WIDTH), hg3, PROJ_ROWS)

    return _out_proj(x2, o_mla.reshape(-1, MLA_WIDTH), o_nsa.reshape(-1, NSA_WIDTH),
                     o_mem.reshape(-1, MEM_WIDTH), w_out.astype(BF16), final_g, final_norm,
                     PROJ_ROWS)


def kernel(x, mem, norm_g, w_in, q_norm_g, w_uq, kv_norm_g, w_ukv, cmp_pe_k, cmp_pe_v,
           cmp_w1k, cmp_w2k, cmp_w1v, cmp_w2v, mem_norm_g, w_mem_kv, w_out, final_norm_g):
    batch, seq, d = x.shape
    depth = norm_g.shape[0]
    tables = (_rope_tables(seq), _nsa_tables(seq, NSA_TQ, NSA_TK))
    x2 = x.reshape(batch * seq, d)
    mem2 = mem.reshape(batch * mem.shape[1], d)
    for l in range(depth):
        x2 = _layer(x2, mem2, batch, seq, tables, norm_g[l], w_in[l], q_norm_g[l], w_uq[l],
                    kv_norm_g[l], w_ukv[l], cmp_pe_k[l], cmp_pe_v[l], cmp_w1k[l], cmp_w2k[l],
                    cmp_w1v[l], cmp_w2v[l], mem_norm_g[l], w_mem_kv[l], w_out[l],
                    final_norm_g, l == depth - 1)
    return x2.reshape(batch, seq, d)
```

```python
import functools

import jax
import jax.numpy as jnp
from jax import lax
from jax.experimental import pallas as pl
from jax.experimental.pallas import tpu as pltpu

F32 = jnp.float32
BF16 = jnp.bfloat16

EPS = 1e-6
NEG_BIG = 1e9
MASK_VALUE = -1e30
SOFTMAX_EPS = 1e-20
LOG2E = 1.4426950408889634

MLA_HEADS = 8
MLA_NOPE = 128
MLA_ROPE = 64
MLA_V = 128
MLA_Q_RANK = 512
MLA_KV_RANK = 512
ROPE_THETA = 10000.0

NSA_HEADS = 4
NSA_DK = 192
NSA_DV = 128
CMP_LEN = 32
CMP_STRIDE = 16
SLC_LEN = 64
SLC_TOPN = 16
WIN = 512

MEM_HEADS = 4
MEM_DH = 128

MLA_WIDTH = MLA_HEADS * MLA_V
NSA_WIDTH = NSA_HEADS * NSA_DV
MEM_WIDTH = MEM_HEADS * MEM_DH

IN_SPLITS = (
    MLA_Q_RANK, MLA_KV_RANK, MLA_ROPE, MLA_WIDTH,
    NSA_HEADS * NSA_DK, NSA_DK, NSA_DV, NSA_DK, NSA_DV,
    NSA_DK, NSA_DV, 3 * NSA_HEADS, NSA_WIDTH,
    MEM_WIDTH, MEM_WIDTH,
)

LANES = 128
HEAD_PAD = 256
VMEM_LIMIT = 56 * 1024 * 1024
SUM_ROWS = 16
PROJ_ROWS = 512
MLA_TILE = 512
MLA_HEADS_PER_STEP = 4
NSA_TQ = 256
NSA_TK = 512
NSA_STREAMS = 4

F_CQ, F_CKV, F_KC, F_VC, F_KROPE = 0, 512, 1024, 1280, 1408
F_GATE = F_KC + NSA_DK
F_WIDTH = 1536
G_ZMLA, G_ZNSA, G_ZMEM = 0, MLA_WIDTH, MLA_WIDTH + NSA_WIDTH
G_WIDTH = MLA_WIDTH + NSA_WIDTH + MEM_WIDTH
B_QNSA, B_KS, B_QMEM, B_KW, B_VS, B_VW = 0, 768, 1024, 1536, 1792, 1920
B_WIDTH = 2048

_NT = (((1,), (1,)), ((), ()))


def _params(*sem):
    return pltpu.CompilerParams(dimension_semantics=sem, vmem_limit_bytes=VMEM_LIMIT)


def _sigmoid(x):
    return 1.0 / (1.0 + jnp.exp(-x))


def _silu(x):
    return x * _sigmoid(x)


def _rmsnorm(x, g):
    ms = jnp.mean(x * x, axis=-1, keepdims=True)
    return (x * lax.rsqrt(ms + EPS)) * g


def _norm_proj_kernel(x_ref, g_ref, w_ref, cs_ref, o_ref, xn_ref):
    @pl.when(pl.program_id(1) == 0)
    def _():
        xn_ref[...] = _rmsnorm(x_ref[...], g_ref[...]).astype(BF16)

    acc = jnp.dot(xn_ref[...], w_ref[...], preferred_element_type=F32)
    o_ref[...] = (acc * cs_ref[...]).astype(o_ref.dtype)


def _norm_proj(x, g, w, colscale, out_dtype, tm, tn, name):
    m, k = x.shape
    n = w.shape[1]
    return pl.pallas_call(
        _norm_proj_kernel,
        grid=(m // tm, n // tn),
        in_specs=[
            pl.BlockSpec((tm, k), lambda i, j: (i, 0)),
            pl.BlockSpec((1, k), lambda i, j: (0, 0)),
            pl.BlockSpec((k, tn), lambda i, j: (0, j)),
            pl.BlockSpec((1, tn), lambda i, j: (0, j)),
        ],
        out_specs=pl.BlockSpec((tm, tn), lambda i, j: (i, j)),
        out_shape=jax.ShapeDtypeStruct((m, n), out_dtype),
        scratch_shapes=[pltpu.VMEM((tm, k), BF16)],
        compiler_params=_params("arbitrary", "arbitrary"),
        name=name,
    )(x, g.reshape(1, k), w, colscale.reshape(1, n))


def _in_proj_kernel(x_ref, g_ref, wf_ref, wg_ref, wb_ref, cs_ref, of_ref, og_ref, ob_ref, xn_ref):
    xn_ref[...] = _rmsnorm(x_ref[...], g_ref[...]).astype(BF16)
    of_ref[...] = jnp.dot(xn_ref[...], wf_ref[...], preferred_element_type=F32)
    og_ref[...] = _silu(jnp.dot(xn_ref[...], wg_ref[...], preferred_element_type=F32)).astype(BF16)
    acc = jnp.dot(xn_ref[...], wb_ref[...], preferred_element_type=F32)
    ob_ref[...] = (acc * cs_ref[...]).astype(BF16)


def _in_proj(x, g, wf, wg, wb, scale_b, tm):
    m, k = x.shape
    whole = lambda a: pl.BlockSpec(a.shape, lambda i: (0, 0), pipeline_mode=pl.Buffered(1))
    rows = lambda a: pl.BlockSpec((tm, a.shape[1]), lambda i: (i, 0))
    cs = scale_b.reshape(1, -1)
    return pl.pallas_call(
        _in_proj_kernel,
        grid=(m // tm,),
        in_specs=[rows(x), pl.BlockSpec((1, k), lambda i: (0, 0)),
                  whole(wf), whole(wg), whole(wb), whole(cs)],
        out_specs=[rows(wf), rows(wg), rows(wb)],
        out_shape=[jax.ShapeDtypeStruct((m, wf.shape[1]), F32),
                   jax.ShapeDtypeStruct((m, wg.shape[1]), BF16),
                   jax.ShapeDtypeStruct((m, wb.shape[1]), BF16)],
        scratch_shapes=[pltpu.VMEM((tm, k), BF16)],
        compiler_params=_params("arbitrary"),
        name="in_proj",
    )(x, g.reshape(1, k), wf, wg, wb, cs)


def _rope_half(hi, cos2, sin2):
    up = pltpu.roll(hi, 32, axis=1)
    down = pltpu.roll(hi, 96, axis=1)
    return hi * cos2 + (up - down) * sin2


def _mla_up_kernel(cq_ref, ckv_ref, kr_ref, gq_ref, gkv_ref, wqt_ref, wk_ref, wvt_ref,
                   cos_ref, sin_ref, cost_ref, sint_ref, qt_ref, k_ref, kpe_ref, vt_ref):
    scale = LOG2E * (MLA_NOPE + MLA_ROPE) ** -0.5
    half = MLA_ROPE // 2
    cos_t = cost_ref[...]
    sin_t = sint_ref[...]

    cqn = _rmsnorm(cq_ref[...], gq_ref[...]).astype(BF16)
    yt = lax.dot_general(wqt_ref[...], cqn, _NT, preferred_element_type=F32)
    for h in range(MLA_HEADS):
        r = h * HEAD_PAD
        x1 = yt[r + MLA_NOPE:r + MLA_NOPE + half]
        x2 = yt[r + MLA_NOPE + half:r + MLA_NOPE + MLA_ROPE]
        qt_ref[0, r:r + MLA_NOPE, :] = (yt[r:r + MLA_NOPE] * scale).astype(BF16)
        qt_ref[0, r + MLA_NOPE:r + MLA_NOPE + half, :] = (
            (x1 * cos_t - x2 * sin_t) * scale).astype(BF16)
        qt_ref[0, r + MLA_NOPE + half:r + MLA_NOPE + MLA_ROPE, :] = (
            (x1 * sin_t + x2 * cos_t) * scale).astype(BF16)
        qt_ref[0, r + MLA_NOPE + MLA_ROPE:r + HEAD_PAD, :] = jnp.zeros(
            (HEAD_PAD - MLA_NOPE - MLA_ROPE, yt.shape[1]), BF16)

    ckn = _rmsnorm(ckv_ref[...], gkv_ref[...]).astype(BF16)
    kn = jnp.dot(ckn, wk_ref[...], preferred_element_type=F32)
    k_pe = _rope_half(kr_ref[...], cos_ref[...], sin_ref[...]).astype(BF16)
    k_ref[...] = kn.astype(BF16)
    kpe_ref[...] = k_pe
    vt = lax.dot_general(wvt_ref[...], ckn, _NT, preferred_element_type=F32)
    ones = jnp.ones((SUM_ROWS, vt.shape[1]), BF16)
    for h in range(MLA_HEADS):
        r = h * (MLA_V + SUM_ROWS)
        vt_ref[0, r:r + MLA_V, :] = vt[h * MLA_V:(h + 1) * MLA_V].astype(BF16)
        vt_ref[0, r + MLA_V:r + MLA_V + SUM_ROWS, :] = ones


def _mla_up(hf, gq, gkv, wqt, wk, wvt, rope, seq, tm):
    cos2, sin2, cos_t, sin_t = rope
    m = hf.shape[0]
    steps_per_seq = seq // tm
    hq = MLA_HEADS * HEAD_PAD
    half = MLA_ROPE // 2
    const = lambda i: (0, 0)
    return pl.pallas_call(
        _mla_up_kernel,
        grid=(m // tm,),
        in_specs=[
            pl.BlockSpec((tm, MLA_Q_RANK), lambda i: (i, F_CQ // MLA_Q_RANK)),
            pl.BlockSpec((tm, MLA_KV_RANK), lambda i: (i, F_CKV // MLA_KV_RANK)),
            pl.BlockSpec((tm, LANES), lambda i: (i, F_KROPE // LANES)),
            pl.BlockSpec((1, MLA_Q_RANK), const),
            pl.BlockSpec((1, MLA_KV_RANK), const),
            pl.BlockSpec((hq, MLA_Q_RANK), const),
            pl.BlockSpec((MLA_KV_RANK, MLA_HEADS * MLA_NOPE), const),
            pl.BlockSpec((MLA_WIDTH, MLA_KV_RANK), const),
            pl.BlockSpec((tm, LANES), lambda i: (i % steps_per_seq, 0)),
            pl.BlockSpec((tm, LANES), lambda i: (i % steps_per_seq, 0)),
            pl.BlockSpec((half, tm), lambda i: (0, i % steps_per_seq)),
            pl.BlockSpec((half, tm), lambda i: (0, i % steps_per_seq)),
        ],
        out_specs=[
            pl.BlockSpec((1, hq, tm), lambda i: (i, 0, 0)),
            pl.BlockSpec((tm, MLA_HEADS * MLA_NOPE), lambda i: (i, 0)),
            pl.BlockSpec((tm, LANES), lambda i: (i, 0)),
            pl.BlockSpec((1, MLA_HEADS * (MLA_V + SUM_ROWS), tm), lambda i: (i, 0, 0)),
        ],
        out_shape=[
            jax.ShapeDtypeStruct((m // tm, hq, tm), BF16),
            jax.ShapeDtypeStruct((m, MLA_HEADS * MLA_NOPE), BF16),
            jax.ShapeDtypeStruct((m, LANES), BF16),
            jax.ShapeDtypeStruct((m // tm, MLA_HEADS * (MLA_V + SUM_ROWS), tm), BF16),
        ],
        compiler_params=_params("arbitrary"),
        name="mla_up",
    )(hf, hf, hf, gq.reshape(1, -1), gkv.reshape(1, -1), wqt, wk, wvt, cos2, sin2, cos_t, sin_t)


def _mla_attn_kernel(qt_ref, k_ref, kpe_ref, vt_ref, z_ref, o_ref, m_ref, acc_ref, s_ref, *,
                     tile_len, hps):
    t = tile_len
    va = MLA_V + SUM_ROWS
    qi = pl.program_id(2)

    def scores(j, h):
        kb = pl.multiple_of(j * t, t)
        k = jnp.concatenate([k_ref[0, pl.ds(kb, t), h * MLA_NOPE:(h + 1) * MLA_NOPE],
                             kpe_ref[0, pl.ds(kb, t), :]], axis=1)
        q_t = qt_ref[0, h * HEAD_PAD:(h + 1) * HEAD_PAD, :]
        return jnp.dot(k, q_t, preferred_element_type=F32)

    def softmax(h, s, diagonal):
        if diagonal:
            kpos = lax.broadcasted_iota(jnp.int32, (t, t), 0)
            qpos = lax.broadcasted_iota(jnp.int32, (t, t), 1)
            s = jnp.where(kpos <= qpos, s, MASK_VALUE)
        m = m_ref[h]
        m_new = jnp.maximum(m, jnp.max(s, axis=0, keepdims=True))
        alpha = jnp.exp2(m - m_new)
        p = jnp.exp2(s - m_new)
        m_ref[h] = m_new
        return alpha, p.astype(BF16)

    def accumulate(j, h, alpha, p):
        vt = vt_ref[j, h * va:(h + 1) * va, :]
        acc_ref[h] = alpha * acc_ref[h] + jnp.dot(vt, p, preferred_element_type=F32)

    def tile(j, diagonal):
        for h in range(hps):
            s_ref[h] = scores(j, h)
        for h in range(hps):
            alpha, p = softmax(h, s_ref[h], diagonal)
            accumulate(j, h, alpha, p)

    for h in range(hps):
        m_ref[h] = jnp.full((1, t), MASK_VALUE, F32)
        acc_ref[h] = jnp.zeros((va, t), F32)

    def body(j, carry):
        tile(j, False)
        return carry

    lax.fori_loop(0, qi, body, 0)

    tile(qi, True)
    for h in range(hps):
        l = acc_ref[h, MLA_V:MLA_V + 1, :]
        o = (acc_ref[h, 0:MLA_V, :] / (l + SOFTMAX_EPS)).T
        cols = slice(h * MLA_V, (h + 1) * MLA_V)
        o_ref[0, :, cols] = (o * z_ref[0, :, cols].astype(F32)).astype(BF16)


def _mla_attn(qt, k, kpe, vt, hg, tile_len, hps):
    b, s, _ = k.shape
    t = tile_len
    tiles = s // t
    kern = functools.partial(_mla_attn_kernel, tile_len=t, hps=hps)
    qw, vw, va = hps * HEAD_PAD, hps * MLA_V, MLA_V + SUM_ROWS
    return pl.pallas_call(
        kern,
        grid=(b, MLA_HEADS // hps, tiles),
        in_specs=[
            pl.BlockSpec((1, qw, t), lambda bi, h, i: (bi * tiles + i, h, 0)),
            pl.BlockSpec((1, s, hps * MLA_NOPE), lambda bi, h, i: (bi, 0, h)),
            pl.BlockSpec((1, s, LANES), lambda bi, h, i: (bi, 0, 0)),
            pl.BlockSpec((tiles, hps * va, t), lambda bi, h, i: (bi, h, 0)),
            pl.BlockSpec((1, t, vw), lambda bi, h, i: (bi, i, G_ZMLA // vw + h)),
        ],
        out_specs=pl.BlockSpec((1, t, vw), lambda bi, h, i: (bi, i, h)),
        out_shape=jax.ShapeDtypeStruct((b, s, MLA_WIDTH), BF16),
        scratch_shapes=[
            pltpu.VMEM((hps, 1, t), F32),
            pltpu.VMEM((hps, va, t), F32),
            pltpu.VMEM((hps, t, t), F32),
        ],
        compiler_params=_params("arbitrary", "arbitrary", "arbitrary"),
        name="mla_attn",
    )(qt, k, kpe, vt, hg)


def _compress_one(x_refs, pe_ref, w1_ref, w2_ref, o_ref):
    chunks = x_refs[0].shape[1] // CMP_STRIDE
    a = b = None
    for i in range(CMP_STRIDE):
        for c, x_ref in enumerate(x_refs):
            lanes = slice(c * LANES, (c + 1) * LANES)
            x = x_ref[0, pl.ds(i, chunks, stride=CMP_STRIDE), :]
            ai = jnp.dot((x + pe_ref[i:i + 1, lanes]).astype(BF16), w1_ref[i, lanes, :],
                         preferred_element_type=F32)
            bi = jnp.dot((x + pe_ref[CMP_STRIDE + i:CMP_STRIDE + i + 1, lanes]).astype(BF16),
                         w1_ref[CMP_STRIDE + i, lanes, :], preferred_element_type=F32)
            a = ai if a is None else a + ai
            b = bi if b is None else b + bi
    h1 = a + pltpu.roll(b, chunks - 1, axis=0)
    o_ref[0] = jnp.dot(_silu(h1).astype(BF16), w2_ref[...], preferred_element_type=F32).astype(BF16)


def _compress_kernel(xk0_ref, xk1_ref, xv_ref, pek_ref, pev_ref, w1k_ref, w2k_ref, w1v_ref, w2v_ref,
                     ok_ref, ov_ref):
    _compress_one((xk0_ref, xk1_ref), pek_ref, w1k_ref, w2k_ref, ok_ref)
    _compress_one((xv_ref,), pev_ref, w1v_ref, w2v_ref, ov_ref)


def _compress(hf, pek, pev, w1k, w2k, w1v, w2v):
    b, s, _ = hf.shape
    chunks = s // CMP_STRIDE
    full = lambda a: pl.BlockSpec(a.shape, lambda bi: (0,) * a.ndim)
    return pl.pallas_call(
        _compress_kernel,
        grid=(b,),
        in_specs=[pl.BlockSpec((1, s, LANES), lambda bi: (bi, 0, F_KC // LANES)),
                  pl.BlockSpec((1, s, LANES), lambda bi: (bi, 0, F_KC // LANES + 1)),
                  pl.BlockSpec((1, s, NSA_DV), lambda bi: (bi, 0, F_VC // NSA_DV)),
                  full(pek), full(pev), full(w1k), full(w2k), full(w1v), full(w2v)],
        out_specs=[
            pl.BlockSpec((1, chunks, HEAD_PAD), lambda bi: (bi, 0, 0)),
            pl.BlockSpec((1, chunks, NSA_DV), lambda bi: (bi, 0, 0)),
        ],
        out_shape=[
            jax.ShapeDtypeStruct((b, chunks, HEAD_PAD), BF16),
            jax.ShapeDtypeStruct((b, chunks, NSA_DV), BF16),
        ],
        compiler_params=_params("arbitrary"),
        name="nsa_compress",
    )(hf, hf, hf, pek, pev, w1k, w2k, w1v, w2v)


def _alibi_slope(h):
    return 2.0 ** (-8.0 * (h + 1) / NSA_HEADS)


def _nsa_attn_kernel(q_ref, g_ref, z_ref, kc_ref, vc_ref, ks_ref, vs_ref, kw_ref, vw_ref,
                     ovt_ref, oh_ref, tabs_ref, tabd_ref, tabw_ref, tabc_ref, slope_ref, o_ref,
                     kaug_ref, kwp_ref, vst_ref, vwt_ref, vct_ref, m_ref, acc_ref, s0_ref,
                     sc_ref, sw_ref, qwin_ref, qsel_ref,
                     *, tq, tk, n_s, top_n, n_str):
    qi = pl.program_id(1)
    t0 = qi * tq
    heads = NSA_HEADS
    seq = ks_ref.shape[1]
    ncp = kc_ref.shape[1]
    hps = heads // n_str
    hw = hps * tq
    aug = HEAD_PAD - NSA_DK
    wlen = WIN + tq
    va = NSA_DV + SUM_ROWS
    sub_tiles = tk // tq

    @pl.when(qi == 0)
    def _():
        kaug_ref[...] = ks_ref[0] + oh_ref[...]
        col = lax.broadcasted_iota(jnp.int32, (WIN, HEAD_PAD), 1)
        kwp_ref[0:WIN, :] = jnp.where(col == NSA_DK, 1.0, 0.0).astype(BF16)
        kwp_ref[WIN:, :] = kw_ref[0]
        for i in range(seq // tq):
            vst_ref[i, 0:NSA_DV, :] = vs_ref[0, i * tq:(i + 1) * tq, :].astype(F32).T.astype(BF16)
            vst_ref[i, NSA_DV:va, :] = jnp.ones((SUM_ROWS, tq), BF16)
        for i in range(WIN // LANES):
            vwt_ref[i] = jnp.zeros((va, LANES), BF16)
        for i in range(seq // LANES):
            vwt_ref[WIN // LANES + i, 0:NSA_DV, :] = (
                vw_ref[0, i * LANES:(i + 1) * LANES, :].astype(F32).T.astype(BF16))
            vwt_ref[WIN // LANES + i, NSA_DV:va, :] = jnp.ones((SUM_ROWS, LANES), BF16)
        vct_ref[...] = vc_ref[0].astype(F32).T.astype(BF16)

    q_t = q_ref[0].astype(F32).T
    flag_rows = jnp.where(lax.broadcasted_iota(jnp.int32, (aug, tq), 0) == 0,
                          MASK_VALUE, 0.0).astype(BF16)
    head_lanes = [(h // hps, slice((h % hps) * tq, (h % hps + 1) * tq)) for h in range(heads)]
    for h, (x, lanes) in enumerate(head_lanes):
        rows = q_t[h * NSA_DK:(h + 1) * NSA_DK].astype(BF16)
        qwin_ref[x, 0:NSA_DK, lanes] = rows
        qsel_ref[x, 0:NSA_DK, lanes] = rows
        qwin_ref[x, NSA_DK:HEAD_PAD, lanes] = flag_rows

    kc = kc_ref[0]
    kwb = kwp_ref[pl.ds(pl.multiple_of(t0, LANES), wlen), :]
    for x in range(n_str):
        sc_ref[x] = jnp.dot(kc, qwin_ref[x], preferred_element_type=F32)
    for x in range(n_str):
        sw_ref[x] = (jnp.dot(kwb, qwin_ref[x], preferred_element_type=F32)
                     + tabw_ref[:, x * hw:(x + 1) * hw])

    first_blk = qi * (tq // CMP_STRIDE)
    tab_c = tabc_ref[pl.ds(pl.multiple_of(ncp - first_blk, CMP_STRIDE), ncp), :]
    o_cmp, p_sum = [], None
    for x in range(n_str):
        p_x = []
        for hl in range(hps):
            h = hps * x + hl
            s = sc_ref[x, :, hl * tq:(hl + 1) * tq] + tab_c[:, h * tq:(h + 1) * tq]
            m = jnp.max(s, axis=0, keepdims=True)
            e = jnp.exp2(s - m)
            inv = jnp.where(m > 0.5 * MASK_VALUE,
                            1.0 / (jnp.sum(e, axis=0, keepdims=True) + SOFTMAX_EPS), 0.0)
            p = e * inv
            p_sum = p if p_sum is None else p_sum + p
            p_x.append(p.astype(BF16))
        o_cmp.append(jnp.dot(vct_ref[...], jnp.concatenate(p_x, axis=1),
                             preferred_element_type=F32))
    p_hi = p_sum.astype(BF16)
    p_lo = (p_sum - p_hi.astype(F32)).astype(BF16)
    ovt = ovt_ref[...]
    imp = (jnp.dot(ovt, p_hi, preferred_element_type=F32)
           + jnp.dot(ovt, p_lo, preferred_element_type=F32))

    tile0 = qi * (tq // LANES)
    vwb = jnp.concatenate([vwt_ref[tile0 + r] for r in range(wlen // LANES)], axis=1)
    o_win = []
    for x in range(n_str):
        s = sw_ref[x]
        p = jnp.exp2(s - jnp.max(s, axis=0, keepdims=True))
        pv = jnp.dot(vwb, p.astype(BF16), preferred_element_type=F32)
        o_win.append(pv[0:NSA_DV] / (pv[NSA_DV:NSA_DV + 1] + SOFTMAX_EPS))

    blk = lax.broadcasted_iota(jnp.int32, (n_s, tq), 0)
    cur = (t0 + lax.broadcasted_iota(jnp.int32, (n_s, tq), 1)) // SLC_LEN
    forced = (blk == 0) | (blk == cur) | (blk == cur - 1)
    imp = jnp.where(forced, NEG_BIG, imp)
    imp = jnp.where(blk > cur, -NEG_BIG, imp)
    sub = lax.broadcasted_iota(jnp.int32, (8, tq), 0)
    groups = [imp[8 * g:8 * g + 8] for g in range(n_s // 8)]
    ranks = [jnp.zeros((8, tq), F32) for _ in groups]
    for jp in range(n_s):
        row = imp[jp:jp + 1, :]
        for g, grp in enumerate(groups):
            ge = jnp.where(row >= grp, 1.0, 0.0)
            gt = jnp.where(row > grp, 1.0, 0.0)
            if 8 * g > jp:
                beats = ge
            elif 8 * g + 8 <= jp:
                beats = gt
            else:
                beats = jnp.where(sub + 8 * g > jp, ge, gt)
            ranks[g] = ranks[g] + beats
    sel_rows = jnp.where(jnp.concatenate(ranks, axis=0) < top_n, 0.0, MASK_VALUE)
    if n_s < aug:
        sel_rows = jnp.concatenate([sel_rows, jnp.zeros((aug - n_s, tq), F32)], axis=0)
    for x, lanes in head_lanes:
        qsel_ref[x, NSA_DK:HEAD_PAD, lanes] = sel_rows.astype(BF16)

    jd = t0 // tk

    def scores(j, x):
        kb = j * tk if isinstance(j, int) else pl.multiple_of(j * tk, tk)
        return jnp.dot(kaug_ref[pl.ds(kb, tk), :], qsel_ref[x], preferred_element_type=F32)

    def softmax(x, s, tab, key0):
        u = s + tab
        off = slope_ref[0:1, x * hw:(x + 1) * hw] * (key0 - t0).astype(F32)
        m_old = m_ref[x]
        m_new = jnp.maximum(m_old, jnp.max(u, axis=0, keepdims=True) + off)
        alpha = jnp.exp2(m_old - m_new)
        p = jnp.exp2(u - (m_new - off))
        m_ref[x] = m_new
        return alpha, p.astype(BF16)

    def accumulate(x, alpha, p, vt):
        acc_ref[x] = alpha * acc_ref[x] + jnp.dot(vt, p, preferred_element_type=F32)

    ahead = n_str // 2

    def full_tile(j):
        vt = jnp.concatenate([vst_ref[j * sub_tiles + r] for r in range(sub_tiles)], axis=1)
        for x in range(ahead, n_str):
            s0_ref[x] = scores(j, x)
        for x in range(n_str):
            alpha, p = softmax(x, s0_ref[x], tabs_ref[:, x * hw:(x + 1) * hw], j * tk)
            accumulate(x, alpha, p, vt)
            if x < ahead:
                s0_ref[x] = scores(j + 1, x)

    def sub_tile(r, tab_ref):
        key0 = jd * tk + r * tq
        rows = pl.ds(r * tq if isinstance(r, int) else pl.multiple_of(r * tq, tq), tq)
        vt = vst_ref[jd * sub_tiles + r]
        for x in range(n_str):
            alpha, p = softmax(x, s0_ref[x, rows, :], tab_ref[0:tq, x * hw:(x + 1) * hw], key0)
            accumulate(x, alpha, p, vt)

    for x in range(n_str):
        m_ref[x] = jnp.full((1, hw), MASK_VALUE, F32)
        acc_ref[x] = jnp.zeros((va, hw), F32)
    for x in range(ahead):
        s0_ref[x] = scores(0, x)

    def body(j, carry):
        full_tile(j)
        return carry

    lax.fori_loop(0, jd, body, 0)
    for x in range(ahead, n_str):
        s0_ref[x] = scores(jd, x)
    own = (t0 - jd * tk) // tq
    for r in range(sub_tiles - 1):
        @pl.when(r < own)
        def _():
            sub_tile(r, tabs_ref)
    sub_tile(own, tabd_ref)
    o_slc = [acc_ref[x, 0:NSA_DV, :] / (acc_ref[x, NSA_DV:NSA_DV + 1, :] + SOFTMAX_EPS)
             for x in range(n_str)]

    gates = _sigmoid(g_ref[0]).T
    z = z_ref[0]
    for h in range(heads):
        x, lanes = head_lanes[h]
        c = NSA_DK + 3 * h
        o = (gates[c:c + 1] * o_cmp[x][:, lanes] + gates[c + 1:c + 2] * o_slc[x][:, lanes]
             + gates[c + 2:c + 3] * o_win[x][:, lanes])
        zh = z[:, h * NSA_DV:(h + 1) * NSA_DV]
        o_ref[0, :, h * NSA_DV:(h + 1) * NSA_DV] = (o.T * zh.astype(F32)).astype(BF16)


def _nsa_attn(hb, hf, hg, k_cmp, v_cmp, tables, tq, tk, n_str):
    ovt, onehot, tab_sel, tab_diag, tab_win, tab_cmp, slope_rows = tables
    b, s, _ = hb.shape
    ncp = k_cmp.shape[1]
    n_s = s // SLC_LEN
    hw = NSA_HEADS // n_str * tq
    assert n_s <= HEAD_PAD - NSA_DK and n_s % 8 == 0, "selection blocks must fit the spare rows"
    assert tq & (tq - 1) == 0 and tq % LANES == 0 and tk % tq == 0 and s % tk == 0
    kern = functools.partial(_nsa_attn_kernel, tq=tq, tk=tk, n_s=n_s, top_n=min(SLC_TOPN, n_s),
                             n_str=n_str)
    qw = NSA_HEADS * NSA_DK
    const2 = lambda a: pl.BlockSpec(a.shape, lambda bi, i: (0, 0), pipeline_mode=pl.Buffered(1))
    return pl.pallas_call(
        kern,
        grid=(b, s // tq),
        in_specs=[
            pl.BlockSpec((1, tq, qw), lambda bi, i: (bi, i, B_QNSA // qw)),
            pl.BlockSpec((1, tq, HEAD_PAD), lambda bi, i: (bi, i, F_KC // HEAD_PAD)),
            pl.BlockSpec((1, tq, NSA_WIDTH), lambda bi, i: (bi, i, G_ZNSA // NSA_WIDTH)),
            pl.BlockSpec((1, ncp, HEAD_PAD), lambda bi, i: (bi, 0, 0)),
            pl.BlockSpec((1, ncp, NSA_DV), lambda bi, i: (bi, 0, 0)),
            pl.BlockSpec((1, s, HEAD_PAD), lambda bi, i: (bi, 0, B_KS // HEAD_PAD)),
            pl.BlockSpec((1, s, NSA_DV), lambda bi, i: (bi, 0, B_VS // NSA_DV)),
            pl.BlockSpec((1, s, HEAD_PAD), lambda bi, i: (bi, 0, B_KW // HEAD_PAD)),
            pl.BlockSpec((1, s, NSA_DV), lambda bi, i: (bi, 0, B_VW // NSA_DV)),
            const2(ovt), const2(onehot), const2(tab_sel), const2(tab_diag), const2(tab_win),
            const2(tab_cmp), const2(slope_rows),
        ],
        out_specs=pl.BlockSpec((1, tq, NSA_WIDTH), lambda bi, i: (bi, i, 0)),
        out_shape=jax.ShapeDtypeStruct((b, s, NSA_WIDTH), BF16),
        scratch_shapes=[
            pltpu.VMEM((s, HEAD_PAD), BF16),
            pltpu.VMEM((s + WIN, HEAD_PAD), BF16),
            pltpu.VMEM((s // tq, NSA_DV + SUM_ROWS, tq), BF16),
            pltpu.VMEM(((s + WIN) // LANES, NSA_DV + SUM_ROWS, LANES), BF16),
            pltpu.VMEM((NSA_DV, ncp), BF16),
            pltpu.VMEM((n_str, 1, hw), F32),
            pltpu.VMEM((n_str, NSA_DV + SUM_ROWS, hw), F32),
            pltpu.VMEM((n_str, tk, hw), F32),
            pltpu.VMEM((n_str, ncp, hw), F32),
            pltpu.VMEM((n_str, WIN + tq, hw), F32),
            pltpu.VMEM((n_str, HEAD_PAD, hw), BF16),
            pltpu.VMEM((n_str, HEAD_PAD, hw), BF16),
        ],
        compiler_params=_params("arbitrary", "arbitrary"),
        name="nsa_attn",
    )(hb, hf, hg, k_cmp, v_cmp, hb, hb, hb, hb, ovt, onehot, tab_sel, tab_diag, tab_win, tab_cmp,
      slope_rows)


def _mem_attn_kernel(q_ref, k_ref, v_ref, z_ref, o_ref):
    q = q_ref[0]
    k = k_ref[0]
    v = v_ref[0]
    z = z_ref[0]
    for h in range(MEM_HEADS):
        sl = slice(h * MEM_DH, (h + 1) * MEM_DH)
        s = lax.dot_general(q[:, sl], k[:, sl], _NT, preferred_element_type=F32)
        p = jnp.exp(s - jnp.max(s, axis=-1, keepdims=True))
        o = jnp.dot(p.astype(BF16), v[:, sl], preferred_element_type=F32)
        o = o / jnp.sum(p, axis=-1, keepdims=True)
        o_ref[0, :, sl] = (o * z[:, sl].astype(F32)).astype(BF16)


def _mem_attn(hb, mem_kv, hg, tq):
    b, s, _ = hb.shape
    mlen = mem_kv.shape[1]
    return pl.pallas_call(
        _mem_attn_kernel,
        grid=(b, s // tq),
        in_specs=[
            pl.BlockSpec((1, tq, MEM_WIDTH), lambda bi, i: (bi, i, B_QMEM // MEM_WIDTH)),
            pl.BlockSpec((1, mlen, MEM_WIDTH), lambda bi, i: (bi, 0, 0)),
            pl.BlockSpec((1, mlen, MEM_WIDTH), lambda bi, i: (bi, 0, 1)),
            pl.BlockSpec((1, tq, MEM_WIDTH), lambda bi, i: (bi, i, G_ZMEM // MEM_WIDTH)),
        ],
        out_specs=pl.BlockSpec((1, tq, MEM_WIDTH), lambda bi, i: (bi, i, 0)),
        out_shape=jax.ShapeDtypeStruct((b, s, MEM_WIDTH), BF16),
        compiler_params=_params("arbitrary", "arbitrary"),
        name="mem_attn",
    )(hb, mem_kv, mem_kv, hg)


def _out_proj_kernel(x_ref, a_ref, n_ref, m_ref, w_ref, g_ref, o_ref, *, final_norm):
    y = x_ref[...]
    y = y + jnp.dot(a_ref[...], w_ref[0:MLA_WIDTH, :], preferred_element_type=F32)
    y = y + jnp.dot(n_ref[...], w_ref[MLA_WIDTH:MLA_WIDTH + NSA_WIDTH, :], preferred_element_type=F32)
    y = y + jnp.dot(m_ref[...], w_ref[MLA_WIDTH + NSA_WIDTH:, :], preferred_element_type=F32)
    if final_norm:
        y = _rmsnorm(y, g_ref[...])
    o_ref[...] = y


def _out_proj(x, o_mla, o_nsa, o_mem, w_out, g, final_norm, tm):
    m, d = x.shape
    kern = functools.partial(_out_proj_kernel, final_norm=final_norm)
    row = lambda width: pl.BlockSpec((tm, width), lambda i: (i, 0))
    return pl.pallas_call(
        kern,
        grid=(m // tm,),
        in_specs=[row(d), row(MLA_WIDTH), row(NSA_WIDTH), row(MEM_WIDTH),
                  pl.BlockSpec(w_out.shape, lambda i: (0, 0)),
                  pl.BlockSpec((1, d), lambda i: (0, 0))],
        out_specs=row(d),
        out_shape=jax.ShapeDtypeStruct((m, d), F32),
        compiler_params=_params("arbitrary"),
        name="out_proj",
    )(x, o_mla, o_nsa, o_mem, w_out, g.reshape(1, d))


def _pad_cols(w, width):
    return jnp.pad(w, ((0, 0), (0, width - w.shape[1])))


def _w_in_moves():
    names = ("c_q", "c_kv", "k_rope", "z_mla", "q_nsa", "k_c", "v_c", "k_s", "v_s", "k_w", "v_w",
             "g_nsa", "z_nsa", "q_mem", "z_mem")
    src, off = {}, 0
    for name, n in zip(names, IN_SPLITS):
        src[name] = (off, n)
        off += n
    dst = {"c_q": (0, F_CQ), "c_kv": (0, F_CKV), "k_c": (0, F_KC), "g_nsa": (0, F_GATE),
           "v_c": (0, F_VC), "k_rope": (0, F_KROPE),
           "z_mla": (1, G_ZMLA), "z_nsa": (1, G_ZNSA), "z_mem": (1, G_ZMEM),
           "q_nsa": (2, B_QNSA), "q_mem": (2, B_QMEM), "k_s": (2, B_KS), "k_w": (2, B_KW),
           "v_s": (2, B_VS), "v_w": (2, B_VW)}
    return [(src[n][0], src[n][1], dst[n][0], dst[n][1]) for n in dst]


def _w_prep_kernel(wt_ref, wf_ref, wg_ref, wb_ref):
    outs = (wf_ref, wg_ref, wb_ref)
    for o_ref in outs:
        o_ref[...] = jnp.zeros(o_ref.shape, BF16)
    for s0, width, which, d0 in _w_in_moves():
        rows = -(-width // LANES) * LANES
        start = min(s0, wt_ref.shape[0] - rows)
        slab = wt_ref[start:start + rows, :].T
        outs[which][:, d0:d0 + width] = slab[:, s0 - start:s0 - start + width].astype(BF16)


def _layout_w_in(w_in, tr=256):
    d, n = w_in.shape
    widths = (F_WIDTH, G_WIDTH, B_WIDTH)
    wf, wg, wb = pl.pallas_call(
        _w_prep_kernel,
        grid=(d // tr,),
        in_specs=[pl.BlockSpec((n, tr), lambda i: (0, i))],
        out_specs=[pl.BlockSpec((tr, w), lambda i: (i, 0)) for w in widths],
        out_shape=[jax.ShapeDtypeStruct((d, w), BF16) for w in widths],
        compiler_params=_params("arbitrary"),
        name="w_in_layout",
    )(w_in.T)
    scale_b = jnp.ones((B_WIDTH,), F32)
    scale_b = scale_b.at[B_QNSA:B_QNSA + NSA_HEADS * NSA_DK].set(LOG2E * NSA_DK ** -0.5)
    scale_b = scale_b.at[B_QMEM:B_QMEM + MEM_WIDTH].set(MEM_DH ** -0.5)
    return wf, wg, wb, scale_b


def _rope_tables(seq):
    pos = jnp.arange(seq, dtype=F32)
    inv_freq = ROPE_THETA ** (-jnp.arange(0, MLA_ROPE, 2, dtype=F32) / MLA_ROPE)
    ang = pos[:, None] * inv_freq[None, :]
    zeros = jnp.zeros((seq, LANES - MLA_ROPE), F32)
    cos2 = jnp.concatenate([jnp.cos(ang), jnp.cos(ang), zeros], axis=1)
    sin2 = jnp.concatenate([jnp.sin(ang), jnp.sin(ang), zeros], axis=1)
    return cos2, sin2, jnp.cos(ang).T, jnp.sin(ang).T


def _nsa_tables(seq, tq, tk):
    chunks = seq // CMP_STRIDE
    n_s = seq // SLC_LEN
    c_start = jnp.arange(chunks) * CMP_STRIDE
    s_start = jnp.arange(n_s) * SLC_LEN
    overlap_t = ((c_start[None, :] < s_start[:, None] + SLC_LEN)
                 & (c_start[None, :] + CMP_LEN > s_start[:, None])
                 & (jnp.arange(chunks)[None, :] < chunks - 1))
    key_block = jnp.arange(seq) // SLC_LEN
    onehot = jnp.arange(HEAD_PAD)[None, :] == (NSA_DK + key_block)[:, None]
    slope = jnp.repeat(jnp.array([_alibi_slope(h) for h in range(NSA_HEADS)], F32) * LOG2E, tq)
    q_lane = jnp.tile(jnp.arange(tq), NSA_HEADS)
    tab_sel = jnp.arange(tk, dtype=F32)[:, None] * slope[None, :]
    rel = q_lane[None, :] - jnp.arange(WIN + tq)[:, None] + WIN
    tab_win = jnp.where((rel >= 0) & (rel < WIN), -slope[None, :] * rel.astype(F32), MASK_VALUE)
    slope_rows = jnp.broadcast_to(slope[None, :], (8, NSA_HEADS * tq))
    tab_diag = jnp.where(jnp.arange(tq)[:, None] <= q_lane[None, :], tab_sel[0:tq], MASK_VALUE)
    d_blk = jnp.arange(2 * chunks)[:, None] - chunks
    seen = d_blk * CMP_STRIDE + (CMP_LEN - 1) <= q_lane[None, :]
    dist = (q_lane[None, :] - d_blk * CMP_STRIDE).astype(F32) - (CMP_LEN - 1) / 2.0
    tab_cmp = jnp.where(seen, -slope[None, :] * dist, MASK_VALUE)
    return overlap_t.astype(BF16), onehot.astype(BF16), tab_sel, tab_diag, tab_win, tab_cmp, slope_rows


def _layer(x2, mem2, batch, seq, tables, norm_g, w_in, q_norm_g, w_uq, kv_norm_g, w_ukv,
           cmp_pe_k, cmp_pe_v, cmp_w1k, cmp_w2k, cmp_w1v, cmp_w2v, mem_norm_g, w_mem_kv, w_out,
           final_g, final_norm):
    rope, nsa_tables = tables
    d = x2.shape[1]
    wf, wg, wb, scale_b = _layout_w_in(w_in)
    hf, hg, hb = _in_proj(x2, norm_g, wf, wg, wb, scale_b, PROJ_ROWS)

    wq = jnp.pad(w_uq.reshape(MLA_Q_RANK, MLA_HEADS, MLA_NOPE + MLA_ROPE),
                 ((0, 0), (0, 0), (0, HEAD_PAD - MLA_NOPE - MLA_ROPE)))
    wqt = wq.reshape(MLA_Q_RANK, MLA_HEADS * HEAD_PAD).T.astype(BF16)
    wkv = w_ukv.reshape(MLA_KV_RANK, MLA_HEADS, MLA_NOPE + MLA_V)
    wk = wkv[:, :, :MLA_NOPE].reshape(MLA_KV_RANK, MLA_HEADS * MLA_NOPE).astype(BF16)
    wvt = wkv[:, :, MLA_NOPE:].reshape(MLA_KV_RANK, MLA_WIDTH).T.astype(BF16)
    qt, k, kpe, vt = _mla_up(hf, q_norm_g, kv_norm_g, wqt, wk, wvt, rope, seq, MLA_TILE)
    hf3 = hf.reshape(batch, seq, F_WIDTH)
    hg3 = hg.reshape(batch, seq, G_WIDTH)
    hb3 = hb.reshape(batch, seq, B_WIDTH)
    o_mla = _mla_attn(qt, k.reshape(batch, seq, -1), kpe.reshape(batch, seq, -1), vt, hg3,
                      MLA_TILE, MLA_HEADS_PER_STEP)

    pad_k = HEAD_PAD - NSA_DK
    k_cmp, v_cmp = _compress(
        hf3, _pad_cols(cmp_pe_k, HEAD_PAD), cmp_pe_v,
        jnp.pad(cmp_w1k.reshape(CMP_LEN, NSA_DK, NSA_DK), ((0, 0), (0, pad_k), (0, pad_k))).astype(BF16),
        jnp.pad(cmp_w2k, ((0, pad_k), (0, pad_k))).astype(BF16),
        cmp_w1v.reshape(CMP_LEN, NSA_DV, NSA_DV).astype(BF16), cmp_w2v.astype(BF16))
    o_nsa = _nsa_attn(hb3, hf3, hg3, k_cmp, v_cmp, nsa_tables, NSA_TQ, NSA_TK, NSA_STREAMS)

    mem_kv = _norm_proj(mem2, mem_norm_g, w_mem_kv.astype(BF16), jnp.ones((2 * MEM_WIDTH,), F32),
                        BF16, mem2.shape[0] // batch, MEM_WIDTH, "mem_kv_proj")
    o_mem = _mem_attn(hb3, mem_kv.reshape(batch, -1, 2 * MEM_WIDTH), hg3, PROJ_ROWS)

    return _out_proj(x2, o_mla.reshape(-1, MLA_WIDTH), o_nsa.reshape(-1, NSA_WIDTH),
                     o_mem.reshape(-1, MEM_WIDTH), w_out.astype(BF16), final_g, final_norm,
                     PROJ_ROWS)


def kernel(x, mem, norm_g, w_in, q_norm_g, w_uq, kv_norm_g, w_ukv, cmp_pe_k, cmp_pe_v,
           cmp_w1k, cmp_w2k, cmp_w1v, cmp_w2v, mem_norm_g, w_mem_kv, w_out, final_norm_g):
    batch, seq, d = x.shape
    depth = norm_g.shape[0]
    tables = (_rope_tables(seq), _nsa_tables(seq, NSA_TQ, NSA_TK))
    x2 = x.reshape(batch * seq, d)
    mem2 = mem.reshape(batch * mem.shape[1], d)
    for l in range(depth):
        x2 = _layer(x2, mem2, batch, seq, tables, norm_g[l], w_in[l], q_norm_g[l], w_uq[l],
                    kv_norm_g[l], w_ukv[l], cmp_pe_k[l], cmp_pe_v[l], cmp_w1k[l], cmp_w2k[l],
                    cmp_w1v[l], cmp_w2v[l], mem_norm_g[l], w_mem_kv[l], w_out[l],
                    final_norm_g, l == depth - 1)
    return x2.reshape(batch, seq, d)
```

```python
import functools

import jax
import jax.numpy as jnp
from jax import lax
from jax.experimental import pallas as pl
from jax.experimental.pallas import tpu as pltpu

F32 = jnp.float32
BF16 = jnp.bfloat16

EPS = 1e-6
NEG_BIG = 1e9
MASK_VALUE = -1e30
SOFTMAX_EPS = 1e-20
LOG2E = 1.4426950408889634

MLA_HEADS = 8
MLA_NOPE = 128
MLA_ROPE = 64
MLA_V = 128
MLA_Q_RANK = 512
MLA_KV_RANK = 512
ROPE_THETA = 10000.0

NSA_HEADS = 4
NSA_DK = 192
NSA_DV = 128
CMP_LEN = 32
CMP_STRIDE = 16
SLC_LEN = 64
SLC_TOPN = 16
WIN = 512

MEM_HEADS = 4
MEM_DH = 128

MLA_WIDTH = MLA_HEADS * MLA_V
NSA_WIDTH = NSA_HEADS * NSA_DV
MEM_WIDTH = MEM_HEADS * MEM_DH

IN_SPLITS = (
    MLA_Q_RANK, MLA_KV_RANK, MLA_ROPE, MLA_WIDTH,
    NSA_HEADS * NSA_DK, NSA_DK, NSA_DV, NSA_DK, NSA_DV,
    NSA_DK, NSA_DV, 3 * NSA_HEADS, NSA_WIDTH,
    MEM_WIDTH, MEM_WIDTH,
)

LANES = 128
HEAD_PAD = 256
VMEM_LIMIT = 56 * 1024 * 1024
SUM_ROWS = 16
PROJ_ROWS = 512
MLA_TILE = 512
MLA_HEADS_PER_STEP = 4
NSA_TQ = 256
NSA_TK = 1024
NSA_STREAMS = 4

F_CQ, F_CKV, F_KC, F_VC, F_KROPE = 0, 512, 1024, 1280, 1408
F_GATE = F_KC + NSA_DK
F_WIDTH = 1536
G_ZMLA, G_ZNSA, G_ZMEM = 0, MLA_WIDTH, MLA_WIDTH + NSA_WIDTH
G_WIDTH = MLA_WIDTH + NSA_WIDTH + MEM_WIDTH
B_QNSA, B_KS, B_QMEM, B_KW, B_VS, B_VW = 0, 768, 1024, 1536, 1792, 1920
B_WIDTH = 2048

_NT = (((1,), (1,)), ((), ()))


def _params(*sem):
    return pltpu.CompilerParams(dimension_semantics=sem, vmem_limit_bytes=VMEM_LIMIT)


def _sigmoid(x):
    return 1.0 / (1.0 + jnp.exp(-x))


def _silu(x):
    return x * _sigmoid(x)


def _rmsnorm(x, g):
    ms = jnp.mean(x * x, axis=-1, keepdims=True)
    return (x * lax.rsqrt(ms + EPS)) * g


def _norm_proj_kernel(x_ref, g_ref, w_ref, cs_ref, o_ref, xn_ref):
    @pl.when(pl.program_id(1) == 0)
    def _():
        xn_ref[...] = _rmsnorm(x_ref[...], g_ref[...]).astype(BF16)

    acc = jnp.dot(xn_ref[...], w_ref[...], preferred_element_type=F32)
    o_ref[...] = (acc * cs_ref[...]).astype(o_ref.dtype)


def _norm_proj(x, g, w, colscale, out_dtype, tm, tn, name):
    m, k = x.shape
    n = w.shape[1]
    return pl.pallas_call(
        _norm_proj_kernel,
        grid=(m // tm, n // tn),
        in_specs=[
            pl.BlockSpec((tm, k), lambda i, j: (i, 0)),
            pl.BlockSpec((1, k), lambda i, j: (0, 0)),
            pl.BlockSpec((k, tn), lambda i, j: (0, j)),
            pl.BlockSpec((1, tn), lambda i, j: (0, j)),
        ],
        out_specs=pl.BlockSpec((tm, tn), lambda i, j: (i, j)),
        out_shape=jax.ShapeDtypeStruct((m, n), out_dtype),
        scratch_shapes=[pltpu.VMEM((tm, k), BF16)],
        compiler_params=_params("arbitrary", "arbitrary"),
        name=name,
    )(x, g.reshape(1, k), w, colscale.reshape(1, n))


def _in_proj_kernel(x_ref, g_ref, wf_ref, wg_ref, wb_ref, cs_ref, of_ref, og_ref, ob_ref, xn_ref):
    half = x_ref.shape[0] // 2
    for r in range(2):
        rows = slice(r * half, (r + 1) * half)
        xn_ref[rows, :] = _rmsnorm(x_ref[rows, :], g_ref[...]).astype(BF16)
        xn = xn_ref[rows, :]
        of_ref[rows, :] = jnp.dot(xn, wf_ref[...], preferred_element_type=F32)
        og_ref[rows, :] = _silu(jnp.dot(xn, wg_ref[...], preferred_element_type=F32)).astype(BF16)
        acc = jnp.dot(xn, wb_ref[...], preferred_element_type=F32)
        ob_ref[rows, :] = (acc * cs_ref[...]).astype(BF16)


def _in_proj(x, g, wf, wg, wb, scale_b, tm):
    m, k = x.shape
    whole = lambda a: pl.BlockSpec(a.shape, lambda i: (0, 0), pipeline_mode=pl.Buffered(1))
    rows = lambda a: pl.BlockSpec((tm, a.shape[1]), lambda i: (i, 0))
    cs = scale_b.reshape(1, -1)
    return pl.pallas_call(
        _in_proj_kernel,
        grid=(m // tm,),
        in_specs=[rows(x), pl.BlockSpec((1, k), lambda i: (0, 0)),
                  whole(wf), whole(wg), whole(wb), whole(cs)],
        out_specs=[rows(wf), rows(wg), rows(wb)],
        out_shape=[jax.ShapeDtypeStruct((m, wf.shape[1]), F32),
                   jax.ShapeDtypeStruct((m, wg.shape[1]), BF16),
                   jax.ShapeDtypeStruct((m, wb.shape[1]), BF16)],
        scratch_shapes=[pltpu.VMEM((tm, k), BF16)],
        compiler_params=_params("arbitrary"),
        name="in_proj",
    )(x, g.reshape(1, k), wf, wg, wb, cs)


def _rope_half(hi, cos2, sin2):
    up = pltpu.roll(hi, 32, axis=1)
    down = pltpu.roll(hi, 96, axis=1)
    return hi * cos2 + (up - down) * sin2


def _mla_up_kernel(cq_ref, ckv_ref, kr_ref, gq_ref, gkv_ref, wqt_ref, wk_ref, wvt_ref,
                   cos_ref, sin_ref, cost_ref, sint_ref, qt_ref, k_ref, kpe_ref, vt_ref):
    scale = LOG2E * (MLA_NOPE + MLA_ROPE) ** -0.5
    half = MLA_ROPE // 2
    cos_t = cost_ref[...]
    sin_t = sint_ref[...]

    cqn = _rmsnorm(cq_ref[...], gq_ref[...]).astype(BF16)
    yt = lax.dot_general(wqt_ref[...], cqn, _NT, preferred_element_type=F32)
    for h in range(MLA_HEADS):
        r = h * HEAD_PAD
        x1 = yt[r + MLA_NOPE:r + MLA_NOPE + half]
        x2 = yt[r + MLA_NOPE + half:r + MLA_NOPE + MLA_ROPE]
        qt_ref[0, r:r + MLA_NOPE, :] = (yt[r:r + MLA_NOPE] * scale).astype(BF16)
        qt_ref[0, r + MLA_NOPE:r + MLA_NOPE + half, :] = (
            (x1 * cos_t - x2 * sin_t) * scale).astype(BF16)
        qt_ref[0, r + MLA_NOPE + half:r + MLA_NOPE + MLA_ROPE, :] = (
            (x1 * sin_t + x2 * cos_t) * scale).astype(BF16)
        qt_ref[0, r + MLA_NOPE + MLA_ROPE:r + HEAD_PAD, :] = jnp.zeros(
            (HEAD_PAD - MLA_NOPE - MLA_ROPE, yt.shape[1]), BF16)

    ckn = _rmsnorm(ckv_ref[...], gkv_ref[...]).astype(BF16)
    kn = jnp.dot(ckn, wk_ref[...], preferred_element_type=F32)
    k_pe = _rope_half(kr_ref[...], cos_ref[...], sin_ref[...]).astype(BF16)
    k_ref[...] = kn.astype(BF16)
    kpe_ref[...] = k_pe
    vt = lax.dot_general(wvt_ref[...], ckn, _NT, preferred_element_type=F32)
    ones = jnp.ones((SUM_ROWS, vt.shape[1]), BF16)
    for h in range(MLA_HEADS):
        r = h * (MLA_V + SUM_ROWS)
        vt_ref[0, r:r + MLA_V, :] = vt[h * MLA_V:(h + 1) * MLA_V].astype(BF16)
        vt_ref[0, r + MLA_V:r + MLA_V + SUM_ROWS, :] = ones


def _mla_up(hf, gq, gkv, wqt, wk, wvt, rope, seq, tm):
    cos2, sin2, cos_t, sin_t = rope
    m = hf.shape[0]
    steps_per_seq = seq // tm
    hq = MLA_HEADS * HEAD_PAD
    half = MLA_ROPE // 2
    const = lambda i: (0, 0)
    return pl.pallas_call(
        _mla_up_kernel,
        grid=(m // tm,),
        in_specs=[
            pl.BlockSpec((tm, MLA_Q_RANK), lambda i: (i, F_CQ // MLA_Q_RANK)),
            pl.BlockSpec((tm, MLA_KV_RANK), lambda i: (i, F_CKV // MLA_KV_RANK)),
            pl.BlockSpec((tm, LANES), lambda i: (i, F_KROPE // LANES)),
            pl.BlockSpec((1, MLA_Q_RANK), const),
            pl.BlockSpec((1, MLA_KV_RANK), const),
            pl.BlockSpec((hq, MLA_Q_RANK), const),
            pl.BlockSpec((MLA_KV_RANK, MLA_HEADS * MLA_NOPE), const),
            pl.BlockSpec((MLA_WIDTH, MLA_KV_RANK), const),
            pl.BlockSpec((tm, LANES), lambda i: (i % steps_per_seq, 0)),
            pl.BlockSpec((tm, LANES), lambda i: (i % steps_per_seq, 0)),
            pl.BlockSpec((half, tm), lambda i: (0, i % steps_per_seq)),
            pl.BlockSpec((half, tm), lambda i: (0, i % steps_per_seq)),
        ],
        out_specs=[
            pl.BlockSpec((1, hq, tm), lambda i: (i, 0, 0)),
            pl.BlockSpec((tm, MLA_HEADS * MLA_NOPE), lambda i: (i, 0)),
            pl.BlockSpec((tm, LANES), lambda i: (i, 0)),
            pl.BlockSpec((1, MLA_HEADS * (MLA_V + SUM_ROWS), tm), lambda i: (i, 0, 0)),
        ],
        out_shape=[
            jax.ShapeDtypeStruct((m // tm, hq, tm), BF16),
            jax.ShapeDtypeStruct((m, MLA_HEADS * MLA_NOPE), BF16),
            jax.ShapeDtypeStruct((m, LANES), BF16),
            jax.ShapeDtypeStruct((m // tm, MLA_HEADS * (MLA_V + SUM_ROWS), tm), BF16),
        ],
        compiler_params=_params("arbitrary"),
        name="mla_up",
    )(hf, hf, hf, gq.reshape(1, -1), gkv.reshape(1, -1), wqt, wk, wvt, cos2, sin2, cos_t, sin_t)


def _mla_attn_kernel(qt_ref, k_ref, kpe_ref, vt_ref, z_ref, o_ref, m_ref, acc_ref, s_ref, *,
                     tile_len, hps):
    t = tile_len
    va = MLA_V + SUM_ROWS
    qi = pl.program_id(2)

    def scores(j, h):
        kb = pl.multiple_of(j * t, t)
        k = jnp.concatenate([k_ref[0, pl.ds(kb, t), h * MLA_NOPE:(h + 1) * MLA_NOPE],
                             kpe_ref[0, pl.ds(kb, t), :]], axis=1)
        q_t = qt_ref[0, h * HEAD_PAD:(h + 1) * HEAD_PAD, :]
        return jnp.dot(k, q_t, preferred_element_type=F32)

    def softmax(h, s, visible):
        if visible is not None:
            s = jnp.where(visible, s, MASK_VALUE)
        m = m_ref[h]
        m_new = jnp.maximum(m, jnp.max(s, axis=0, keepdims=True))
        alpha = jnp.exp2(m - m_new)
        p = jnp.exp2(s - m_new)
        m_ref[h] = m_new
        return alpha, p.astype(BF16)

    def accumulate(j, h, alpha, p):
        vt = vt_ref[j, h * va:(h + 1) * va, :]
        acc_ref[h] = alpha * acc_ref[h] + jnp.dot(vt, p, preferred_element_type=F32)

    def tile(j, diagonal):
        for h in range(hps):
            s_ref[h] = scores(j, h)
        visible = None
        if diagonal:
            visible = (lax.broadcasted_iota(jnp.int32, (t, t), 0)
                       <= lax.broadcasted_iota(jnp.int32, (t, t), 1))
        for h in range(hps):
            alpha, p = softmax(h, s_ref[h], visible)
            accumulate(j, h, alpha, p)

    for h in range(hps):
        m_ref[h] = jnp.full((1, t), MASK_VALUE, F32)
        acc_ref[h] = jnp.zeros((va, t), F32)

    def body(j, carry):
        tile(j, False)
        return carry

    lax.fori_loop(0, qi, body, 0)

    tile(qi, True)
    for h in range(hps):
        l = acc_ref[h, MLA_V:MLA_V + 1, :]
        o = (acc_ref[h, 0:MLA_V, :] / (l + SOFTMAX_EPS)).T
        cols = slice(h * MLA_V, (h + 1) * MLA_V)
        o_ref[0, :, cols] = (o * z_ref[0, :, cols].astype(F32)).astype(BF16)


def _mla_attn(qt, k, kpe, vt, hg, tile_len, hps):
    b, s, _ = k.shape
    t = tile_len
    tiles = s // t
    kern = functools.partial(_mla_attn_kernel, tile_len=t, hps=hps)
    qw, vw, va = hps * HEAD_PAD, hps * MLA_V, MLA_V + SUM_ROWS
    return pl.pallas_call(
        kern,
        grid=(b, MLA_HEADS // hps, tiles),
        in_specs=[
            pl.BlockSpec((1, qw, t), lambda bi, h, i: (bi * tiles + i, h, 0)),
            pl.BlockSpec((1, s, hps * MLA_NOPE), lambda bi, h, i: (bi, 0, h)),
            pl.BlockSpec((1, s, LANES), lambda bi, h, i: (bi, 0, 0)),
            pl.BlockSpec((tiles, hps * va, t), lambda bi, h, i: (bi, h, 0)),
            pl.BlockSpec((1, t, vw), lambda bi, h, i: (bi, i, G_ZMLA // vw + h)),
        ],
        out_specs=pl.BlockSpec((1, t, vw), lambda bi, h, i: (bi, i, h)),
        out_shape=jax.ShapeDtypeStruct((b, s, MLA_WIDTH), BF16),
        scratch_shapes=[
            pltpu.VMEM((hps, 1, t), F32),
            pltpu.VMEM((hps, va, t), F32),
            pltpu.VMEM((hps, t, t), F32),
        ],
        compiler_params=_params("arbitrary", "arbitrary", "arbitrary"),
        name="mla_attn",
    )(qt, k, kpe, vt, hg)


def _compress_one(x_refs, pe_ref, w1_ref, w2_ref, o_ref):
    chunks = x_refs[0].shape[1] // CMP_STRIDE
    a = b = None
    for i in range(CMP_STRIDE):
        for c, x_ref in enumerate(x_refs):
            lanes = slice(c * LANES, (c + 1) * LANES)
            x = x_ref[0, pl.ds(i, chunks, stride=CMP_STRIDE), :]
            ai = jnp.dot((x + pe_ref[i:i + 1, lanes]).astype(BF16), w1_ref[i, lanes, :],
                         preferred_element_type=F32)
            bi = jnp.dot((x + pe_ref[CMP_STRIDE + i:CMP_STRIDE + i + 1, lanes]).astype(BF16),
                         w1_ref[CMP_STRIDE + i, lanes, :], preferred_element_type=F32)
            a = ai if a is None else a + ai
            b = bi if b is None else b + bi
    h1 = a + pltpu.roll(b, chunks - 1, axis=0)
    o_ref[0] = jnp.dot(_silu(h1).astype(BF16), w2_ref[...], preferred_element_type=F32).astype(BF16)


def _compress_kernel(xk0_ref, xk1_ref, xv_ref, pek_ref, pev_ref, w1k_ref, w2k_ref, w1v_ref, w2v_ref,
                     ok_ref, ov_ref):
    _compress_one((xk0_ref, xk1_ref), pek_ref, w1k_ref, w2k_ref, ok_ref)
    _compress_one((xv_ref,), pev_ref, w1v_ref, w2v_ref, ov_ref)


def _compress(hf, pek, pev, w1k, w2k, w1v, w2v):
    b, s, _ = hf.shape
    chunks = s // CMP_STRIDE
    full = lambda a: pl.BlockSpec(a.shape, lambda bi: (0,) * a.ndim)
    return pl.pallas_call(
        _compress_kernel,
        grid=(b,),
        in_specs=[pl.BlockSpec((1, s, LANES), lambda bi: (bi, 0, F_KC // LANES)),
                  pl.BlockSpec((1, s, LANES), lambda bi: (bi, 0, F_KC // LANES + 1)),
                  pl.BlockSpec((1, s, NSA_DV), lambda bi: (bi, 0, F_VC // NSA_DV)),
                  full(pek), full(pev), full(w1k), full(w2k), full(w1v), full(w2v)],
        out_specs=[
            pl.BlockSpec((1, chunks, HEAD_PAD), lambda bi: (bi, 0, 0)),
            pl.BlockSpec((1, chunks, NSA_DV), lambda bi: (bi, 0, 0)),
        ],
        out_shape=[
            jax.ShapeDtypeStruct((b, chunks, HEAD_PAD), BF16),
            jax.ShapeDtypeStruct((b, chunks, NSA_DV), BF16),
        ],
        compiler_params=_params("arbitrary"),
        name="nsa_compress",
    )(hf, hf, hf, pek, pev, w1k, w2k, w1v, w2v)


def _alibi_slope(h):
    return 2.0 ** (-8.0 * (h + 1) / NSA_HEADS)


def _nsa_attn_kernel(q_ref, g_ref, z_ref, kc_ref, vc_ref, ks_ref, vs_ref, kw_ref, vw_ref,
                     ovt_ref, oh_ref, tabs_ref, tabd_ref, tabw_ref, tabc_ref, slope_ref, o_ref,
                     kaug_ref, kwp_ref, vst_ref, vwt_ref, vct_ref, m_ref, acc_ref, s0_ref,
                     sc_ref, sw_ref, qwin_ref, qsel_ref,
                     *, tq, tk, n_s, top_n, n_str):
    qi = pl.program_id(1)
    t0 = qi * tq
    heads = NSA_HEADS
    seq = ks_ref.shape[1]
    ncp = kc_ref.shape[1]
    hps = heads // n_str
    hw = hps * tq
    aug = HEAD_PAD - NSA_DK
    wlen = WIN + tq
    va = NSA_DV + SUM_ROWS
    sub_tiles = tk // tq

    @pl.when(qi == 0)
    def _():
        kaug_ref[...] = ks_ref[0] + oh_ref[...]
        col = lax.broadcasted_iota(jnp.int32, (WIN, HEAD_PAD), 1)
        kwp_ref[0:WIN, :] = jnp.where(col == NSA_DK, 1.0, 0.0).astype(BF16)
        kwp_ref[WIN:, :] = kw_ref[0]
        for i in range(seq // tq):
            vst_ref[i, 0:NSA_DV, :] = vs_ref[0, i * tq:(i + 1) * tq, :].astype(F32).T.astype(BF16)
            vst_ref[i, NSA_DV:va, :] = jnp.ones((SUM_ROWS, tq), BF16)
        for i in range(WIN // LANES):
            vwt_ref[i] = jnp.zeros((va, LANES), BF16)
        for i in range(seq // LANES):
            vwt_ref[WIN // LANES + i, 0:NSA_DV, :] = (
                vw_ref[0, i * LANES:(i + 1) * LANES, :].astype(F32).T.astype(BF16))
            vwt_ref[WIN // LANES + i, NSA_DV:va, :] = jnp.ones((SUM_ROWS, LANES), BF16)
        vct_ref[...] = vc_ref[0].astype(F32).T.astype(BF16)

    q_t = q_ref[0].astype(F32).T
    flag_rows = jnp.where(lax.broadcasted_iota(jnp.int32, (aug, tq), 0) == 0,
                          MASK_VALUE, 0.0).astype(BF16)
    head_lanes = [(h // hps, slice((h % hps) * tq, (h % hps + 1) * tq)) for h in range(heads)]
    for h, (x, lanes) in enumerate(head_lanes):
        rows = q_t[h * NSA_DK:(h + 1) * NSA_DK].astype(BF16)
        qwin_ref[x, 0:NSA_DK, lanes] = rows
        qsel_ref[x, 0:NSA_DK, lanes] = rows
        qwin_ref[x, NSA_DK:HEAD_PAD, lanes] = flag_rows

    kc = kc_ref[0]
    kwb = kwp_ref[pl.ds(pl.multiple_of(t0, LANES), wlen), :]
    for x in range(n_str):
        sc_ref[x] = jnp.dot(kc, qwin_ref[x], preferred_element_type=F32)
    for x in range(n_str):
        sw_ref[x] = (jnp.dot(kwb, qwin_ref[x], preferred_element_type=F32)
                     + tabw_ref[:, x * hw:(x + 1) * hw])

    first_blk = qi * (tq // CMP_STRIDE)
    tab_c = tabc_ref[pl.ds(pl.multiple_of(ncp - first_blk, CMP_STRIDE), ncp), :]
    o_cmp, p_sum = [], None
    for x in range(n_str):
        p_x = []
        for hl in range(hps):
            h = hps * x + hl
            s = sc_ref[x, :, hl * tq:(hl + 1) * tq] + tab_c[:, h * tq:(h + 1) * tq]
            m = jnp.max(s, axis=0, keepdims=True)
            e = jnp.exp2(s - m)
            inv = jnp.where(m > 0.5 * MASK_VALUE,
                            1.0 / (jnp.sum(e, axis=0, keepdims=True) + SOFTMAX_EPS), 0.0)
            p = e * inv
            p_sum = p if p_sum is None else p_sum + p
            p_x.append(p.astype(BF16))
        o_cmp.append(jnp.dot(vct_ref[...], jnp.concatenate(p_x, axis=1),
                             preferred_element_type=F32))
    p_hi = p_sum.astype(BF16)
    p_lo = (p_sum - p_hi.astype(F32)).astype(BF16)
    ovt = ovt_ref[...]
    imp = (jnp.dot(ovt, p_hi, preferred_element_type=F32)
           + jnp.dot(ovt, p_lo, preferred_element_type=F32))

    tile0 = qi * (tq // LANES)
    vwb = jnp.concatenate([vwt_ref[tile0 + r] for r in range(wlen // LANES)], axis=1)
    o_win = []
    for x in range(n_str):
        s = sw_ref[x]
        p = jnp.exp2(s - jnp.max(s, axis=0, keepdims=True))
        pv = jnp.dot(vwb, p.astype(BF16), preferred_element_type=F32)
        o_win.append(pv[0:NSA_DV] / (pv[NSA_DV:NSA_DV + 1] + SOFTMAX_EPS))

    blk = lax.broadcasted_iota(jnp.int32, (n_s, tq), 0)
    cur = (t0 + lax.broadcasted_iota(jnp.int32, (n_s, tq), 1)) // SLC_LEN
    forced = (blk == 0) | (blk == cur) | (blk == cur - 1)
    imp = jnp.where(forced, NEG_BIG, imp)
    imp = jnp.where(blk > cur, -NEG_BIG, imp)
    sub = lax.broadcasted_iota(jnp.int32, (8, tq), 0)
    groups = [imp[8 * g:8 * g + 8] for g in range(n_s // 8)]
    ranks = [jnp.zeros((8, tq), F32) for _ in groups]
    for jp in range(n_s):
        row = imp[jp:jp + 1, :]
        for g, grp in enumerate(groups):
            ge = jnp.where(row >= grp, 1.0, 0.0)
            gt = jnp.where(row > grp, 1.0, 0.0)
            if 8 * g > jp:
                beats = ge
            elif 8 * g + 8 <= jp:
                beats = gt
            else:
                beats = jnp.where(sub + 8 * g > jp, ge, gt)
            ranks[g] = ranks[g] + beats
    sel_rows = jnp.where(jnp.concatenate(ranks, axis=0) < top_n, 0.0, MASK_VALUE)
    if n_s < aug:
        sel_rows = jnp.concatenate([sel_rows, jnp.zeros((aug - n_s, tq), F32)], axis=0)
    for x, lanes in head_lanes:
        qsel_ref[x, NSA_DK:HEAD_PAD, lanes] = sel_rows.astype(BF16)

    jd = t0 // tk

    def scores(j, x):
        kb = j * tk if isinstance(j, int) else pl.multiple_of(j * tk, tk)
        return jnp.dot(kaug_ref[pl.ds(kb, tk), :], qsel_ref[x], preferred_element_type=F32)

    def softmax(x, s, tab, key0):
        u = s + tab
        off = slope_ref[0:1, x * hw:(x + 1) * hw] * (key0 - t0).astype(F32)
        m_old = m_ref[x]
        m_new = jnp.maximum(m_old, jnp.max(u, axis=0, keepdims=True) + off)
        alpha = jnp.exp2(m_old - m_new)
        p = jnp.exp2(u - (m_new - off))
        m_ref[x] = m_new
        return alpha, p.astype(BF16)

    def accumulate(x, alpha, p, vt):
        acc_ref[x] = alpha * acc_ref[x] + jnp.dot(vt, p, preferred_element_type=F32)

    ahead = n_str // 2

    def full_tile(j):
        vt = jnp.concatenate([vst_ref[j * sub_tiles + r] for r in range(sub_tiles)], axis=1)
        for x in range(ahead, n_str):
            s0_ref[x] = scores(j, x)
        for x in range(n_str):
            alpha, p = softmax(x, s0_ref[x], tabs_ref[:, x * hw:(x + 1) * hw], j * tk)
            accumulate(x, alpha, p, vt)
            if x < ahead:
                s0_ref[x] = scores(j + 1, x)

    def sub_tile(r, tab_ref):
        key0 = jd * tk + r * tq
        rows = pl.ds(r * tq if isinstance(r, int) else pl.multiple_of(r * tq, tq), tq)
        vt = vst_ref[jd * sub_tiles + r]
        for x in range(n_str):
            alpha, p = softmax(x, s0_ref[x, rows, :], tab_ref[0:tq, x * hw:(x + 1) * hw], key0)
            accumulate(x, alpha, p, vt)

    for x in range(n_str):
        m_ref[x] = jnp.full((1, hw), MASK_VALUE, F32)
        acc_ref[x] = jnp.zeros((va, hw), F32)
    for x in range(ahead):
        s0_ref[x] = scores(0, x)

    def body(j, carry):
        full_tile(j)
        return carry

    lax.fori_loop(0, jd, body, 0)
    for x in range(ahead, n_str):
        s0_ref[x] = scores(jd, x)
    own = (t0 - jd * tk) // tq
    for r in range(sub_tiles - 1):
        @pl.when(r < own)
        def _():
            sub_tile(r, tabs_ref)
    sub_tile(own, tabd_ref)
    o_slc = [acc_ref[x, 0:NSA_DV, :] / (acc_ref[x, NSA_DV:NSA_DV + 1, :] + SOFTMAX_EPS)
             for x in range(n_str)]

    gates = _sigmoid(g_ref[0]).T
    z = z_ref[0]
    for h in range(heads):
        x, lanes = head_lanes[h]
        c = NSA_DK + 3 * h
        o = (gates[c:c + 1] * o_cmp[x][:, lanes] + gates[c + 1:c + 2] * o_slc[x][:, lanes]
             + gates[c + 2:c + 3] * o_win[x][:, lanes])
        zh = z[:, h * NSA_DV:(h + 1) * NSA_DV]
        o_ref[0, :, h * NSA_DV:(h + 1) * NSA_DV] = (o.T * zh.astype(F32)).astype(BF16)


def _nsa_attn(hb, hf, hg, k_cmp, v_cmp, tables, tq, tk, n_str):
    ovt, onehot, tab_sel, tab_diag, tab_win, tab_cmp, slope_rows = tables
    b, s, _ = hb.shape
    ncp = k_cmp.shape[1]
    n_s = s // SLC_LEN
    hw = NSA_HEADS // n_str * tq
    assert n_s <= HEAD_PAD - NSA_DK and n_s % 8 == 0, "selection blocks must fit the spare rows"
    assert tq & (tq - 1) == 0 and tq % LANES == 0 and tk % tq == 0 and s % tk == 0
    kern = functools.partial(_nsa_attn_kernel, tq=tq, tk=tk, n_s=n_s, top_n=min(SLC_TOPN, n_s),
                             n_str=n_str)
    qw = NSA_HEADS * NSA_DK
    const2 = lambda a: pl.BlockSpec(a.shape, lambda bi, i: (0, 0), pipeline_mode=pl.Buffered(1))
    return pl.pallas_call(
        kern,
        grid=(b, s // tq),
        in_specs=[
            pl.BlockSpec((1, tq, qw), lambda bi, i: (bi, i, B_QNSA // qw)),
            pl.BlockSpec((1, tq, HEAD_PAD), lambda bi, i: (bi, i, F_KC // HEAD_PAD)),
            pl.BlockSpec((1, tq, NSA_WIDTH), lambda bi, i: (bi, i, G_ZNSA // NSA_WIDTH)),
            pl.BlockSpec((1, ncp, HEAD_PAD), lambda bi, i: (bi, 0, 0)),
            pl.BlockSpec((1, ncp, NSA_DV), lambda bi, i: (bi, 0, 0)),
            pl.BlockSpec((1, s, HEAD_PAD), lambda bi, i: (bi, 0, B_KS // HEAD_PAD)),
            pl.BlockSpec((1, s, NSA_DV), lambda bi, i: (bi, 0, B_VS // NSA_DV)),
            pl.BlockSpec((1, s, HEAD_PAD), lambda bi, i: (bi, 0, B_KW // HEAD_PAD)),
            pl.BlockSpec((1, s, NSA_DV), lambda bi, i: (bi, 0, B_VW // NSA_DV)),
            const2(ovt), const2(onehot), const2(tab_sel), const2(tab_diag), const2(tab_win),
            const2(tab_cmp), const2(slope_rows),
        ],
        out_specs=pl.BlockSpec((1, tq, NSA_WIDTH), lambda bi, i: (bi, i, 0)),
        out_shape=jax.ShapeDtypeStruct((b, s, NSA_WIDTH), BF16),
        scratch_shapes=[
            pltpu.VMEM((s, HEAD_PAD), BF16),
            pltpu.VMEM((s + WIN, HEAD_PAD), BF16),
            pltpu.VMEM((s // tq, NSA_DV + SUM_ROWS, tq), BF16),
            pltpu.VMEM(((s + WIN) // LANES, NSA_DV + SUM_ROWS, LANES), BF16),
            pltpu.VMEM((NSA_DV, ncp), BF16),
            pltpu.VMEM((n_str, 1, hw), F32),
            pltpu.VMEM((n_str, NSA_DV + SUM_ROWS, hw), F32),
            pltpu.VMEM((n_str, tk, hw), F32),
            pltpu.VMEM((n_str, ncp, hw), F32),
            pltpu.VMEM((n_str, WIN + tq, hw), F32),
            pltpu.VMEM((n_str, HEAD_PAD, hw), BF16),
            pltpu.VMEM((n_str, HEAD_PAD, hw), BF16),
        ],
        compiler_params=_params("arbitrary", "arbitrary"),
        name="nsa_attn",
    )(hb, hf, hg, k_cmp, v_cmp, hb, hb, hb, hb, ovt, onehot, tab_sel, tab_diag, tab_win, tab_cmp,
      slope_rows)


def _mem_attn_kernel(q_ref, k_ref, v_ref, z_ref, o_ref):
    q = q_ref[0]
    k = k_ref[0]
    v = v_ref[0]
    z = z_ref[0]
    heads = [slice(h * MEM_DH, (h + 1) * MEM_DH) for h in range(MEM_HEADS)]
    scores = [lax.dot_general(q[:, sl], k[:, sl], _NT, preferred_element_type=F32) for sl in heads]
    for sl, s in zip(heads, scores):
        p = jnp.exp(s - jnp.max(s, axis=-1, keepdims=True))
        o = jnp.dot(p.astype(BF16), v[:, sl], preferred_element_type=F32)
        o = o / jnp.sum(p, axis=-1, keepdims=True)
        o_ref[0, :, sl] = (o * z[:, sl].astype(F32)).astype(BF16)


def _mem_attn(hb, mem_kv, hg, tq):
    b, s, _ = hb.shape
    mlen = mem_kv.shape[1]
    return pl.pallas_call(
        _mem_attn_kernel,
        grid=(b, s // tq),
        in_specs=[
            pl.BlockSpec((1, tq, MEM_WIDTH), lambda bi, i: (bi, i, B_QMEM // MEM_WIDTH)),
            pl.BlockSpec((1, mlen, MEM_WIDTH), lambda bi, i: (bi, 0, 0)),
            pl.BlockSpec((1, mlen, MEM_WIDTH), lambda bi, i: (bi, 0, 1)),
            pl.BlockSpec((1, tq, MEM_WIDTH), lambda bi, i: (bi, i, G_ZMEM // MEM_WIDTH)),
        ],
        out_specs=pl.BlockSpec((1, tq, MEM_WIDTH), lambda bi, i: (bi, i, 0)),
        out_shape=jax.ShapeDtypeStruct((b, s, MEM_WIDTH), BF16),
        compiler_params=_params("arbitrary", "arbitrary"),
        name="mem_attn",
    )(hb, mem_kv, mem_kv, hg)


def _out_proj_kernel(x_ref, a_ref, n_ref, m_ref, w_ref, g_ref, o_ref, *, final_norm):
    half = x_ref.shape[0] // 2
    for r in range(2):
        rows = slice(r * half, (r + 1) * half)
        y = x_ref[rows, :]
        y = y + jnp.dot(a_ref[rows, :], w_ref[0:MLA_WIDTH, :], preferred_element_type=F32)
        y = y + jnp.dot(n_ref[rows, :], w_ref[MLA_WIDTH:MLA_WIDTH + NSA_WIDTH, :],
                        preferred_element_type=F32)
        y = y + jnp.dot(m_ref[rows, :], w_ref[MLA_WIDTH + NSA_WIDTH:, :], preferred_element_type=F32)
        if final_norm:
            y = _rmsnorm(y, g_ref[...])
        o_ref[rows, :] = y


def _out_proj(x, o_mla, o_nsa, o_mem, w_out, g, final_norm, tm):
    m, d = x.shape
    kern = functools.partial(_out_proj_kernel, final_norm=final_norm)
    row = lambda width: pl.BlockSpec((tm, width), lambda i: (i, 0))
    return pl.pallas_call(
        kern,
        grid=(m // tm,),
        in_specs=[row(d), row(MLA_WIDTH), row(NSA_WIDTH), row(MEM_WIDTH),
                  pl.BlockSpec(w_out.shape, lambda i: (0, 0)),
                  pl.BlockSpec((1, d), lambda i: (0, 0))],
        out_specs=row(d),
        out_shape=jax.ShapeDtypeStruct((m, d), F32),
        compiler_params=_params("arbitrary"),
        name="out_proj",
    )(x, o_mla, o_nsa, o_mem, w_out, g.reshape(1, d))


def _pad_cols(w, width):
    return jnp.pad(w, ((0, 0), (0, width - w.shape[1])))


def _w_in_moves():
    names = ("c_q", "c_kv", "k_rope", "z_mla", "q_nsa", "k_c", "v_c", "k_s", "v_s", "k_w", "v_w",
             "g_nsa", "z_nsa", "q_mem", "z_mem")
    src, off = {}, 0
    for name, n in zip(names, IN_SPLITS):
        src[name] = (off, n)
        off += n
    dst = {"c_q": (0, F_CQ), "c_kv": (0, F_CKV), "k_c": (0, F_KC), "g_nsa": (0, F_GATE),
           "v_c": (0, F_VC), "k_rope": (0, F_KROPE),
           "z_mla": (1, G_ZMLA), "z_nsa": (1, G_ZNSA), "z_mem": (1, G_ZMEM),
           "q_nsa": (2, B_QNSA), "q_mem": (2, B_QMEM), "k_s": (2, B_KS), "k_w": (2, B_KW),
           "v_s": (2, B_VS), "v_w": (2, B_VW)}
    return [(src[n][0], src[n][1], dst[n][0], dst[n][1]) for n in dst]


def _w_prep_kernel(wt_ref, wf_ref, wg_ref, wb_ref):
    outs = (wf_ref, wg_ref, wb_ref)
    for o_ref in outs:
        o_ref[...] = jnp.zeros(o_ref.shape, BF16)
    for s0, width, which, d0 in _w_in_moves():
        rows = -(-width // LANES) * LANES
        start = min(s0, wt_ref.shape[0] - rows)
        slab = wt_ref[start:start + rows, :].T
        outs[which][:, d0:d0 + width] = slab[:, s0 - start:s0 - start + width].astype(BF16)


def _layout_w_in(w_in, tr=256):
    d, n = w_in.shape
    widths = (F_WIDTH, G_WIDTH, B_WIDTH)
    wf, wg, wb = pl.pallas_call(
        _w_prep_kernel,
        grid=(d // tr,),
        in_specs=[pl.BlockSpec((n, tr), lambda i: (0, i))],
        out_specs=[pl.BlockSpec((tr, w), lambda i: (i, 0)) for w in widths],
        out_shape=[jax.ShapeDtypeStruct((d, w), BF16) for w in widths],
        compiler_params=_params("arbitrary"),
        name="w_in_layout",
    )(w_in.T)
    scale_b = jnp.ones((B_WIDTH,), F32)
    scale_b = scale_b.at[B_QNSA:B_QNSA + NSA_HEADS * NSA_DK].set(LOG2E * NSA_DK ** -0.5)
    scale_b = scale_b.at[B_QMEM:B_QMEM + MEM_WIDTH].set(MEM_DH ** -0.5)
    return wf, wg, wb, scale_b


def _rope_tables(seq):
    pos = jnp.arange(seq, dtype=F32)
    inv_freq = ROPE_THETA ** (-jnp.arange(0, MLA_ROPE, 2, dtype=F32) / MLA_ROPE)
    ang = pos[:, None] * inv_freq[None, :]
    zeros = jnp.zeros((seq, LANES - MLA_ROPE), F32)
    cos2 = jnp.concatenate([jnp.cos(ang), jnp.cos(ang), zeros], axis=1)
    sin2 = jnp.concatenate([jnp.sin(ang), jnp.sin(ang), zeros], axis=1)
    return cos2, sin2, jnp.cos(ang).T, jnp.sin(ang).T


def _nsa_tables(seq, tq, tk):
    chunks = seq // CMP_STRIDE
    n_s = seq // SLC_LEN
    c_start = jnp.arange(chunks) * CMP_STRIDE
    s_start = jnp.arange(n_s) * SLC_LEN
    overlap_t = ((c_start[None, :] < s_start[:, None] + SLC_LEN)
                 & (c_start[None, :] + CMP_LEN > s_start[:, None])
                 & (jnp.arange(chunks)[None, :] < chunks - 1))
    key_block = jnp.arange(seq) // SLC_LEN
    onehot = jnp.arange(HEAD_PAD)[None, :] == (NSA_DK + key_block)[:, None]
    slope = jnp.repeat(jnp.array([_alibi_slope(h) for h in range(NSA_HEADS)], F32) * LOG2E, tq)
    q_lane = jnp.tile(jnp.arange(tq), NSA_HEADS)
    tab_sel = jnp.arange(tk, dtype=F32)[:, None] * slope[None, :]
    rel = q_lane[None, :] - jnp.arange(WIN + tq)[:, None] + WIN
    tab_win = jnp.where((rel >= 0) & (rel < WIN), -slope[None, :] * rel.astype(F32), MASK_VALUE)
    slope_rows = jnp.broadcast_to(slope[None, :], (8, NSA_HEADS * tq))
    tab_diag = jnp.where(jnp.arange(tq)[:, None] <= q_lane[None, :], tab_sel[0:tq], MASK_VALUE)
    d_blk = jnp.arange(2 * chunks)[:, None] - chunks
    seen = d_blk * CMP_STRIDE + (CMP_LEN - 1) <= q_lane[None, :]
    dist = (q_lane[None, :] - d_blk * CMP_STRIDE).astype(F32) - (CMP_LEN - 1) / 2.0
    tab_cmp = jnp.where(seen, -slope[None, :] * dist, MASK_VALUE)
    return overlap_t.astype(BF16), onehot.astype(BF16), tab_sel, tab_diag, tab_win, tab_cmp, slope_rows


def _layer(x2, mem2, batch, seq, tables, norm_g, w_in, q_norm_g, w_uq, kv_norm_g, w_ukv,
           cmp_pe_k, cmp_pe_v, cmp_w1k, cmp_w2k, cmp_w1v, cmp_w2v, mem_norm_g, w_mem_kv, w_out,
           final_g, final_norm):
    rope, nsa_tables = tables
    d = x2.shape[1]
    wf, wg, wb, scale_b = _layout_w_in(w_in)
    hf, hg, hb = _in_proj(x2, norm_g, wf, wg, wb, scale_b, PROJ_ROWS)

    wq = jnp.pad(w_uq.reshape(MLA_Q_RANK, MLA_HEADS, MLA_NOPE + MLA_ROPE),
                 ((0, 0), (0, 0), (0, HEAD_PAD - MLA_NOPE - MLA_ROPE)))
    wqt = wq.reshape(MLA_Q_RANK, MLA_HEADS * HEAD_PAD).T.astype(BF16)
    wkv = w_ukv.reshape(MLA_KV_RANK, MLA_HEADS, MLA_NOPE + MLA_V)
    wk = wkv[:, :, :MLA_NOPE].reshape(MLA_KV_RANK, MLA_HEADS * MLA_NOPE).astype(BF16)
    wvt = wkv[:, :, MLA_NOPE:].reshape(MLA_KV_RANK, MLA_WIDTH).T.astype(BF16)
    qt, k, kpe, vt = _mla_up(hf, q_norm_g, kv_norm_g, wqt, wk, wvt, rope, seq, MLA_TILE)
    hf3 = hf.reshape(batch, seq, F_WIDTH)
    hg3 = hg.reshape(batch, seq, G_WIDTH)
    hb3 = hb.reshape(batch, seq, B_WIDTH)
    o_mla = _mla_attn(qt, k.reshape(batch, seq, -1), kpe.reshape(batch, seq, -1), vt, hg3,
                      MLA_TILE, MLA_HEADS_PER_STEP)

    pad_k = HEAD_PAD - NSA_DK
    k_cmp, v_cmp = _compress(
        hf3, _pad_cols(cmp_pe_k, HEAD_PAD), cmp_pe_v,
        jnp.pad(cmp_w1k.reshape(CMP_LEN, NSA_DK, NSA_DK), ((0, 0), (0, pad_k), (0, pad_k))).astype(BF16),
        jnp.pad(cmp_w2k, ((0, pad_k), (0, pad_k))).astype(BF16),
        cmp_w1v.reshape(CMP_LEN, NSA_DV, NSA_DV).astype(BF16), cmp_w2v.astype(BF16))
    o_nsa = _nsa_attn(hb3, hf3, hg3, k_cmp, v_cmp, nsa_tables, NSA_TQ, NSA_TK, NSA_STREAMS)

    mem_kv = _norm_proj(mem2, mem_norm_g, w_mem_kv.astype(BF16), jnp.ones((2 * MEM_WIDTH,), F32),
                        BF16, mem2.shape[0] // batch, MEM_WIDTH, "mem_kv_proj")
    o_mem = _mem_attn(hb3, mem_kv.reshape(batch, -1, 2 * MEM_WIDTH), hg3, PROJ_ROWS)

    return _out_proj(x2, o_mla.reshape(-1, MLA_WIDTH), o_nsa.reshape(-1, NSA_WIDTH),
                     o_mem.reshape(-1, MEM_WIDTH), w_out.astype(BF16), final_g, final_norm,
                     PROJ_ROWS)


def kernel(x, mem, norm_g, w_in, q_norm_g, w_uq, kv_norm_g, w_ukv, cmp_pe_k, cmp_pe_v,
           cmp_w1k, cmp_w2k, cmp_w1v, cmp_w2v, mem_norm_g, w_mem_kv, w_out, final_norm_g):
    batch, seq, d = x.shape
    depth = norm_g.shape[0]
    tables = (_rope_tables(seq), _nsa_tables(seq, NSA_TQ, NSA_TK))
    x2 = x.reshape(batch * seq, d)
    mem2 = mem.reshape(batch * mem.shape[1], d)
    for l in range(depth):
        x2 = _layer(x2, mem2, batch, seq, tables, norm_g[l], w_in[l], q_norm_g[l], w_uq[l],
                    kv_norm_g[l], w_ukv[l], cmp_pe_k[l], cmp_pe_v[l], cmp_w1k[l], cmp_w2k[l],
                    cmp_w1v[l], cmp_w2v[l], mem_norm_g[l], w_mem_kv[l], w_out[l],
                    final_norm_g, l == depth - 1)
    return x2.reshape(batch, seq, d)
```
